```python
import math
import jax
import jax.numpy as jnp
from jax import lax

D_MODEL = 1024
BATCH = 8
SEQ = 2048
DEPTH = 1

CTX_LEN = 256
GRID_W = 64
EPS = 1e-6
N_MOD = 6
HEAD_DIM = 64
ATTN_HEADS = 8
ATTN_KV_HEADS = 2
ATTN_WIDTH = ATTN_HEADS * HEAD_DIM
KV_WIDTH = ATTN_KV_HEADS * HEAD_DIM
WINDOW = 128
BLOCK = 128
ATTN_SCALE = HEAD_DIM ** -0.5
ROPE_THETA = 10000.0
ROT_PAIRS = HEAD_DIM // 4
SSM_WIDTH = D_MODEL - ATTN_WIDTH
SSM_GROUP = 16
SSM_GROUPS = SSM_WIDTH // SSM_GROUP
SSM_STATE = 64
DT_MIN = 0.001
DT_MAX = 0.1
IN_WIDTH = ATTN_WIDTH + 2 * KV_WIDTH + SSM_WIDTH
N_EXPERTS = 64
EXPERT_DIM = 128
SHARED_DIM = 128
TOP_K = 8
N_EXPERT_GROUPS = 8
TOPK_GROUPS = 4
ROUTED_SCALE = 2.5

kernel_name = 'hybrid_dit_swa_s5_moe_prefix'


def _rmsnorm(t, g):
    tf = t.astype(jnp.float32)
    tf = tf * lax.rsqrt(jnp.mean(tf * tf, axis=-1, keepdims=True) + EPS)
    return tf.astype(t.dtype) * g


def _ada(cond, w, b):
    return jax.nn.silu(cond) @ w + b


def _modulate(h, shift, scale):
    return h * (1.0 + scale[:, None, :]) + shift[:, None, :]


def _split_groups(p):
    B, L, _ = p.shape
    q = p[..., :ATTN_WIDTH].reshape(B, L, ATTN_HEADS, HEAD_DIM)
    k = p[..., ATTN_WIDTH:ATTN_WIDTH + KV_WIDTH].reshape(B, L, ATTN_KV_HEADS, HEAD_DIM)
    v = p[..., ATTN_WIDTH + KV_WIDTH:ATTN_WIDTH + 2 * KV_WIDTH].reshape(B, L, ATTN_KV_HEADS, HEAD_DIM)
    u = p[..., ATTN_WIDTH + 2 * KV_WIDTH:]
    return q, k, v, u


def _rope_tables(rows):
    row = jnp.repeat(jnp.arange(rows, dtype=jnp.float32), GRID_W)
    col = (jnp.arange(rows * GRID_W) % GRID_W).astype(jnp.float32)
    inv = ROPE_THETA ** (-jnp.arange(ROT_PAIRS, dtype=jnp.float32) / ROT_PAIRS)
    ang = jnp.stack([row[:, None] * inv, col[:, None] * inv], axis=1)
    return jnp.cos(ang), jnp.sin(ang)


def _axial_rope(t, cos, sin):
    B, L, H, _ = t.shape
    tf = t.astype(jnp.float32).reshape(B, L, H, 2, 2, ROT_PAIRS)
    x1, x2 = tf[..., 0, :], tf[..., 1, :]
    cs, sn = cos[None, :, None], sin[None, :, None]
    out = jnp.stack([x1 * cs - x2 * sn, x1 * sn + x2 * cs], axis=-2)
    return out.reshape(t.shape).astype(t.dtype)


def _window_attention(q, k, v, k_ctx, v_ctx, sink):
    B, L, H, hd = q.shape
    KVH = k.shape[2]
    G = H // KVH
    nb = L // BLOCK
    C = k_ctx.shape[1]
    qb = q.reshape(B, nb, BLOCK, KVH, G, hd)
    pad = ((0, 0), (BLOCK, BLOCK), (0, 0), (0, 0))

    def bands(t):
        tp = jnp.pad(t, pad).reshape(B, nb + 2, BLOCK, KVH, hd)
        return jnp.concatenate([tp[:, :-2], tp[:, 1:-1], tp[:, 2:]], axis=2)

    kb, vb = bands(k), bands(v)
    q_pos = jnp.arange(L).reshape(nb, BLOCK)
    kp = jnp.arange(-BLOCK, L + BLOCK).reshape(nb + 2, BLOCK)
    k_pos = jnp.concatenate([kp[:-2], kp[1:-1], kp[2:]], axis=1)
    mask = ((jnp.abs(q_pos[:, :, None] - k_pos[:, None, :]) <= WINDOW)
            & (k_pos[:, None, :] >= 0) & (k_pos[:, None, :] < L))
    s_loc = jnp.einsum('bnqkgd,bnjkd->bkgnqj', qb, kb).astype(jnp.float32) * ATTN_SCALE
    s_loc = jnp.where(mask, s_loc, -jnp.inf)
    s_ctx = jnp.einsum('bnqkgd,bckd->bkgnqc', qb, k_ctx).astype(jnp.float32) * ATTN_SCALE
    s_sink = jnp.broadcast_to(sink.astype(jnp.float32).reshape(1, KVH, G, 1, 1, 1), s_loc.shape[:-1] + (1,))
    p = jax.nn.softmax(jnp.concatenate([s_loc, s_ctx, s_sink], axis=-1), axis=-1).astype(v.dtype)
    nk = 3 * BLOCK
    o = (jnp.einsum('bkgnqj,bnjkd->bnqkgd', p[..., :nk], vb)
         + jnp.einsum('bkgnqc,bckd->bnqkgd', p[..., nk:nk + C], v_ctx))
    return o.reshape(B, L, H * hd)


def _context_attention(q, k, v, sink):
    B, C, H, hd = q.shape
    KVH = k.shape[2]
    G = H // KVH
    qg = q.reshape(B, C, KVH, G, hd)
    s = jnp.einsum('bqkgd,bjkd->bkgqj', qg, k).astype(jnp.float32) * ATTN_SCALE
    s_sink = jnp.broadcast_to(sink.astype(jnp.float32).reshape(1, KVH, G, 1, 1), s.shape[:-1] + (1,))
    p = jax.nn.softmax(jnp.concatenate([s, s_sink], axis=-1), axis=-1).astype(v.dtype)
    o = jnp.einsum('bkgqj,bjkd->bqkgd', p[..., :C], v)
    return o.reshape(B, C, H * hd)


def _discretize(a_re, a_im, log_dt, b_re, b_im):
    dt = jnp.exp(log_dt)[:, None]
    mag = jnp.exp(dt * a_re)
    abar_re = mag * jnp.cos(dt * a_im)
    abar_im = mag * jnp.sin(dt * a_im)
    den = a_re * a_re + a_im * a_im
    num_re = abar_re - 1.0
    coef_re = (num_re * a_re + abar_im * a_im) / den
    coef_im = (abar_im * a_re - num_re * a_im) / den
    bb_re = coef_re[..., None] * b_re - coef_im[..., None] * b_im
    bb_im = coef_re[..., None] * b_im + coef_im[..., None] * b_re
    return abar_re, abar_im, bb_re, bb_im


def _complex_combine(e1, e2):
    a1r, a1i, b1r, b1i = e1
    a2r, a2i, b2r, b2i = e2
    return (a2r * a1r - a2i * a1i,
            a2r * a1i + a2i * a1r,
            a2r * b1r - a2i * b1i + b2r,
            a2r * b1i + a2i * b1r + b2i)


def _ssm_states(u, a_re, a_im, log_dt, b_re, b_im, h0, reverse):
    abar_re, abar_im, bb_re, bb_im = _discretize(a_re, a_im, log_dt, b_re, b_im)
    if reverse:
        u = jnp.flip(u, axis=1)
    x_re = jnp.einsum('blgp,gnp->blgn', u, bb_re)
    x_im = jnp.einsum('blgp,gnp->blgn', u, bb_im)
    if h0 is not None:
        h0_re, h0_im = h0
        x_re = x_re.at[:, 0].add(abar_re * h0_re - abar_im * h0_im)
        x_im = x_im.at[:, 0].add(abar_re * h0_im + abar_im * h0_re)
    L = u.shape[1]
    a_re_seq = jnp.broadcast_to(abar_re[None, None], (1, L) + abar_re.shape)
    a_im_seq = jnp.broadcast_to(abar_im[None, None], (1, L) + abar_im.shape)
    _, _, h_re, h_im = lax.associative_scan(_complex_combine, (a_re_seq, a_im_seq, x_re, x_im), axis=1)
    final = (h_re[:, -1], h_im[:, -1])
    if reverse:
        h_re, h_im = jnp.flip(h_re, axis=1), jnp.flip(h_im, axis=1)
    return h_re, h_im, final


def _bidirectional_scan(u_lat, u_ctx, a_re, a_im, log_dt, b_re, b_im):
    B, L, _ = u_lat.shape
    ul = u_lat.astype(jnp.float32).reshape(B, L, SSM_GROUPS, SSM_GROUP)
    uc = u_ctx.astype(jnp.float32).reshape(B, u_ctx.shape[1], SSM_GROUPS, SSM_GROUP)
    lat_states, ctx_states = [], []
    for d, rev in enumerate((False, True)):
        prm = (a_re[d].astype(jnp.float32), a_im[d].astype(jnp.float32), log_dt[d].astype(jnp.float32),
               b_re[d].astype(jnp.float32), b_im[d].astype(jnp.float32))
        c_re_s, c_im_s, c_final = _ssm_states(uc, *prm, None, rev)
        l_re_s, l_im_s, _ = _ssm_states(ul, *prm, c_final, rev)
        lat_states.append((l_re_s, l_im_s))
        ctx_states.append((c_re_s, c_im_s))
    return lat_states, ctx_states


def _readout_dir(states, c_re, c_im):
    h_re, h_im = states
    return (jnp.einsum('blgn,gpn->blgp', h_re, c_re.astype(jnp.float32))
            - jnp.einsum('blgn,gpn->blgp', h_im, c_im.astype(jnp.float32)))


def _ssm_readout(states, u, c_re, c_im, d_skip, w_glu, b_glu):
    B, L, _ = u.shape
    y = _readout_dir(states[0], c_re[0], c_im[0]) + _readout_dir(states[1], c_re[1], c_im[1])
    y = y.reshape(B, L, SSM_WIDTH) + d_skip.astype(jnp.float32) * u.astype(jnp.float32)
    y = jax.nn.gelu(y).astype(u.dtype)
    return y * jax.nn.sigmoid(y @ w_glu + b_glu)


def _merge_groups(attn_o, ssm_o, g_attn, g_ssm, w_out):
    return jnp.concatenate([_rmsnorm(attn_o, g_attn), _rmsnorm(ssm_o, g_ssm)], axis=-1) @ w_out


def _moe(h, w_router, router_bias, w_gate_e, w_up_e, w_down_e, w_gate_s, w_up_s, w_down_s):
    scores = jax.nn.sigmoid((h @ w_router).astype(jnp.float32))
    biased = scores + router_bias.astype(jnp.float32)
    grouped = biased.reshape(biased.shape[:-1] + (N_EXPERT_GROUPS, N_EXPERTS // N_EXPERT_GROUPS))
    group_score = lax.top_k(grouped, 2)[0].sum(axis=-1)
    _, group_idx = lax.top_k(group_score, TOPK_GROUPS)
    group_mask = jax.nn.one_hot(group_idx, N_EXPERT_GROUPS, dtype=jnp.float32).sum(axis=-2)
    expert_mask = jnp.repeat(group_mask, N_EXPERTS // N_EXPERT_GROUPS, axis=-1) > 0
    _, idx = lax.top_k(jnp.where(expert_mask, biased, -jnp.inf), TOP_K)
    sel = jnp.take_along_axis(scores, idx, axis=-1)
    weights = sel / jnp.sum(sel, axis=-1, keepdims=True) * ROUTED_SCALE
    gates = jnp.sum(jax.nn.one_hot(idx, N_EXPERTS, dtype=jnp.float32) * weights[..., None], axis=-2).astype(h.dtype)
    hid = jax.nn.silu(jnp.einsum('bld,edf->blef', h, w_gate_e)) * jnp.einsum('bld,edf->blef', h, w_up_e)
    routed = jnp.einsum('blef,efd->bld', hid * gates[..., None], w_down_e)
    shared = (jax.nn.silu(h @ w_gate_s) * (h @ w_up_s)) @ w_down_s
    return routed + shared


def setup_inputs(seed: int = 0) -> dict:
    key = jax.random.key(seed)
    ks = jax.random.split(key, 32)
    G, N, P, E = SSM_GROUPS, SSM_STATE, SSM_GROUP, N_EXPERTS

    def nrm(i, shape, s):
        return s * jax.random.normal(ks[i], shape, jnp.float32)

    n_idx = jnp.arange(N, dtype=jnp.float32)
    return {
        'x': nrm(0, (BATCH, SEQ, D_MODEL), 1.0),
        'c': nrm(1, (BATCH, D_MODEL), 1.0),
        'ctx': nrm(2, (BATCH, CTX_LEN, D_MODEL), 1.0),
        'c_ctx': nrm(3, (D_MODEL,), 1.0),
        'w_ada': nrm(4, (DEPTH, D_MODEL, N_MOD * D_MODEL), 0.02),
        'b_ada': nrm(5, (DEPTH, N_MOD * D_MODEL), 0.01),
        'norm_mix': 1.0 + nrm(6, (DEPTH, D_MODEL), 0.02),
        'norm_ffn': 1.0 + nrm(7, (DEPTH, D_MODEL), 0.02),
        'w_in': nrm(11, (DEPTH, D_MODEL, IN_WIDTH), D_MODEL ** -0.5),
        'attn_sink': nrm(12, (DEPTH, ATTN_HEADS), 0.5),
        'ssm_a_re': -0.5 + nrm(8, (DEPTH, 2, G, N), 0.01),
        'ssm_a_im': math.pi * n_idx + nrm(9, (DEPTH, 2, G, N), 0.01),
        'ssm_log_dt': jax.random.uniform(ks[10], (DEPTH, 2, G), jnp.float32, math.log(DT_MIN), math.log(DT_MAX)),
        'ssm_b_re': nrm(13, (DEPTH, 2, G, N, P), (2 * P) ** -0.5),
        'ssm_b_im': nrm(14, (DEPTH, 2, G, N, P), (2 * P) ** -0.5),
        'ssm_c_re': nrm(15, (DEPTH, 2, G, P, N), (2 * N) ** -0.5),
        'ssm_c_im': nrm(16, (DEPTH, 2, G, P, N), (2 * N) ** -0.5),
        'ssm_d': nrm(17, (DEPTH, SSM_WIDTH), 1.0),
        'w_glu': nrm(18, (DEPTH, SSM_WIDTH, SSM_WIDTH), SSM_WIDTH ** -0.5),
        'b_glu': nrm(19, (DEPTH, SSM_WIDTH), 0.01),
        'norm_attn_out': 1.0 + nrm(20, (DEPTH, ATTN_WIDTH), 0.02),
        'norm_ssm_out': 1.0 + nrm(21, (DEPTH, SSM_WIDTH), 0.02),
        'w_out': nrm(22, (DEPTH, D_MODEL, D_MODEL), D_MODEL ** -0.5),
        'w_router': nrm(23, (DEPTH, D_MODEL, E), D_MODEL ** -0.5),
        'router_bias': nrm(24, (DEPTH, E), 0.01),
        'w_gate_e': nrm(25, (DEPTH, E, D_MODEL, EXPERT_DIM), D_MODEL ** -0.5),
        'w_up_e': nrm(26, (DEPTH, E, D_MODEL, EXPERT_DIM), D_MODEL ** -0.5),
        'w_down_e': nrm(27, (DEPTH, E, EXPERT_DIM, D_MODEL), EXPERT_DIM ** -0.5),
        'w_gate_s': nrm(28, (DEPTH, D_MODEL, SHARED_DIM), D_MODEL ** -0.5),
        'w_up_s': nrm(29, (DEPTH, D_MODEL, SHARED_DIM), D_MODEL ** -0.5),
        'w_down_s': nrm(30, (DEPTH, SHARED_DIM, D_MODEL), SHARED_DIM ** -0.5),
        'norm_final': 1.0 + nrm(31, (D_MODEL,), 0.02),
    }


def reference(x, c, ctx, c_ctx, w_ada, b_ada, norm_mix, norm_ffn, w_in, attn_sink,
              ssm_a_re, ssm_a_im, ssm_log_dt, ssm_b_re, ssm_b_im, ssm_c_re, ssm_c_im, ssm_d,
              w_glu, b_glu, norm_attn_out, norm_ssm_out, w_out, w_router, router_bias,
              w_gate_e, w_up_e, w_down_e, w_gate_s, w_up_s, w_down_s, norm_final):
    rows = x.shape[1] // GRID_W
    cos, sin = _rope_tables(rows)
    ctx_h = ctx
    for layer in range(DEPTH):
        mod = _ada(c, w_ada[layer], b_ada[layer])
        mod_c = _ada(c_ctx[None, :], w_ada[layer], b_ada[layer])
        sh1, sc1, g1, sh2, sc2, g2 = jnp.split(mod, N_MOD, axis=-1)
        sh1c, sc1c, g1c, sh2c, sc2c, g2c = jnp.split(mod_c, N_MOD, axis=-1)

        h = _modulate(_rmsnorm(x, norm_mix[layer]), sh1, sc1)
        hc = _modulate(_rmsnorm(ctx_h, norm_mix[layer]), sh1c, sc1c)
        q, k, v, u = _split_groups(h @ w_in[layer])
        qc, kc, vc, uc = _split_groups(hc @ w_in[layer])
        q = _axial_rope(q, cos, sin)
        k = _axial_rope(k, cos, sin)
        attn_o = _window_attention(q, k, v, kc, vc, attn_sink[layer])
        lat_states, ctx_states = _bidirectional_scan(u, uc, ssm_a_re[layer], ssm_a_im[layer], ssm_log_dt[layer],
                                                     ssm_b_re[layer], ssm_b_im[layer])
        ssm_o = _ssm_readout(lat_states, u, ssm_c_re[layer], ssm_c_im[layer], ssm_d[layer], w_glu[layer], b_glu[layer])
        x = x + g1[:, None, :] * _merge_groups(attn_o, ssm_o, norm_attn_out[layer], norm_ssm_out[layer], w_out[layer])

        h2 = _modulate(_rmsnorm(x, norm_ffn[layer]), sh2, sc2)
        x = x + g2[:, None, :] * _moe(h2, w_router[layer], router_bias[layer], w_gate_e[layer], w_up_e[layer],
                                      w_down_e[layer], w_gate_s[layer], w_up_s[layer], w_down_s[layer])

        if layer + 1 < DEPTH:
            attn_c = _context_attention(qc, kc, vc, attn_sink[layer])
            ssm_c = _ssm_readout(ctx_states, uc, ssm_c_re[layer], ssm_c_im[layer], ssm_d[layer], w_glu[layer], b_glu[layer])
            ctx_h = ctx_h + g1c[:, None, :] * _merge_groups(attn_c, ssm_c, norm_attn_out[layer], norm_ssm_out[layer], w_out[layer])
            h2c = _modulate(_rmsnorm(ctx_h, norm_ffn[layer]), sh2c, sc2c)
            ctx_h = ctx_h + g2c[:, None, :] * _moe(h2c, w_router[layer], router_bias[layer], w_gate_e[layer], w_up_e[layer],
                                                  w_down_e[layer], w_gate_s[layer], w_up_s[layer], w_down_s[layer])
    return _rmsnorm(x, norm_final)
```

```python
import functools
import math

import jax
import jax.numpy as jnp
from jax import lax
from jax.experimental import pallas as pl
from jax.experimental.pallas import tpu as pltpu

D_MODEL = 1024
EPS = 1e-6
N_MOD = 6
HEAD_DIM = 64
ATTN_HEADS = 8
ATTN_KV_HEADS = 2
ATTN_WIDTH = ATTN_HEADS * HEAD_DIM
KV_WIDTH = ATTN_KV_HEADS * HEAD_DIM
WINDOW = 128
ATTN_SCALE = HEAD_DIM ** -0.5
ROPE_THETA = 10000.0
ROT_PAIRS = HEAD_DIM // 4
GRID_W = 64
SSM_WIDTH = D_MODEL - ATTN_WIDTH
SSM_GROUP = 16
SSM_GROUPS = SSM_WIDTH // SSM_GROUP
SSM_STATE = 64
N_EXPERTS = 64
EXPERT_DIM = 128
TOP_K = 8
N_EXPERT_GROUPS = 8
TOPK_GROUPS = 4
ROUTED_SCALE = 2.5

CHUNK = 16
CHUNK_W = CHUNK * SSM_GROUP
LANES = 128
MOE_TILE = 1024
EXPERTS_PER_STEP = 8
VMEM_LIMIT = 56 * 1024 * 1024

MXU = jnp.bfloat16
F32 = jnp.float32
HIGHEST = lax.Precision.HIGHEST


def _sigmoid(x):
    return 1.0 / (1.0 + jnp.exp(-x))


def _cparams(sem):
    return pltpu.CompilerParams(dimension_semantics=sem, vmem_limit_bytes=VMEM_LIMIT)


def _ada_kernel(c_ref, w_ref, b_ref, o_ref):
    cv = c_ref[...]
    s = cv * _sigmoid(cv)
    o_ref[...] = jnp.dot(s, w_ref[...], precision=HIGHEST, preferred_element_type=F32) + b_ref[...]


def _ada(c_all, w, b):
    rows, d = c_all.shape
    n = w.shape[1]
    tn = 1024
    return pl.pallas_call(
        _ada_kernel,
        grid=(n // tn,),
        in_specs=[pl.BlockSpec((rows, d), lambda j: (0, 0)),
                  pl.BlockSpec((d, tn), lambda j: (0, j)),
                  pl.BlockSpec((1, tn), lambda j: (0, j))],
        out_specs=pl.BlockSpec((rows, tn), lambda j: (0, j)),
        out_shape=jax.ShapeDtypeStruct((rows, n), F32),
        compiler_params=_cparams(("arbitrary",)),
    )(c_all, w, b.reshape(1, n))


def _ssm_prep_kernel(are_ref, aim_ref, ldt_ref, btr_ref, bti_ref, cr_ref, ci_ref,
                     kc_ref, pb_ref, cp_ref, q_ref):
    ar, ai = are_ref[...], aim_ref[...]
    dt = jnp.exp(ldt_ref[...])
    mag = jnp.exp(dt * ar)
    abr = mag * jnp.cos(dt * ai)
    abi = mag * jnp.sin(dt * ai)
    den = ar * ar + ai * ai
    nr = abr - 1.0
    cor = (nr * ar + abi * ai) / den
    coi = (abi * ar - nr * ai) / den
    btr, bti = btr_ref[...], bti_ref[...]
    bbr = cor[:, None, :] * btr - coi[:, None, :] * bti
    bbi = cor[:, None, :] * bti + coi[:, None, :] * btr
    cr, ci = cr_ref[...], ci_ref[...]
    pr, pi = jnp.ones_like(abr), jnp.zeros_like(abr)
    cps = []
    for k in range(CHUNK + 1):
        cpr = cr * pr[:, None, :] - ci * pi[:, None, :]
        cpi = cr * pi[:, None, :] + ci * pr[:, None, :]
        cpk = jnp.concatenate([cpr, -cpi], axis=-1)
        cps.append(cpk)
        cp_ref[:, k] = cpk
        if k < CHUNK:
            pb_ref[:, k] = jnp.concatenate([bbr * pr[:, None, :] - bbi * pi[:, None, :],
                                            bbr * pi[:, None, :] + bbi * pr[:, None, :]], axis=-1)
        if k < CHUNK:
            pr, pi = pr * abr - pi * abi, pr * abi + pi * abr
    qr, qi = pr, pi
    bbcat = jnp.concatenate([bbr, bbi], axis=-1)
    lagk = []
    for d in range(2):
        stack = jnp.concatenate([cps[k][d] for k in range(CHUNK)], axis=0)
        lagk.append(lax.dot_general(stack, bbcat[d], (((1,), (1,)), ((), ())),
                                    precision=HIGHEST, preferred_element_type=F32))
    kf, kb = lagk
    p = SSM_GROUP
    blocks = [kb[(CHUNK - 1 - j) * p:(CHUNK - j) * p] for j in range(CHUNK - 1)]
    blocks.append(kf[:p] + kb[:p])
    blocks += [kf[k * p:(k + 1) * p] for k in range(1, CHUNK)]
    kc_ref[...] = jnp.concatenate(blocks, axis=0)
    for j in range(8):
        q_ref[:, j] = jnp.concatenate([qr, qi], axis=-1)
        qr, qi = qr * qr - qi * qi, 2.0 * qr * qi


def _ssm_prep(a_re, a_im, log_dt, b_re, b_im, c_re, c_im):
    g, n, p = SSM_GROUPS, SSM_STATE, SSM_GROUP
    are = jnp.transpose(a_re, (1, 0, 2))
    aim = jnp.transpose(a_im, (1, 0, 2))
    ldt = jnp.transpose(log_dt, (1, 0))[..., None]
    btr = jnp.transpose(b_re, (1, 0, 3, 2))
    bti = jnp.transpose(b_im, (1, 0, 3, 2))
    cr = jnp.transpose(c_re, (1, 0, 2, 3))
    ci = jnp.transpose(c_im, (1, 0, 2, 3))
    nlag = 2 * CHUNK - 1
    vec = pl.BlockSpec((None, 2, n), lambda i: (i, 0, 0))
    mat = pl.BlockSpec((None, 2, p, n), lambda i: (i, 0, 0, 0))
    return pl.pallas_call(
        _ssm_prep_kernel,
        grid=(g,),
        in_specs=[vec, vec, pl.BlockSpec((None, 2, 1), lambda i: (i, 0, 0)), mat, mat, mat, mat],
        out_specs=[pl.BlockSpec((None, nlag * p, p), lambda i: (i, 0, 0)),
                   pl.BlockSpec((None, 2, CHUNK, p, 2 * n), lambda i: (i, 0, 0, 0, 0)),
                   pl.BlockSpec((None, 2, CHUNK + 1, p, 2 * n), lambda i: (i, 0, 0, 0, 0)),
                   pl.BlockSpec((None, 2, 8, 2 * n), lambda i: (i, 0, 0, 0))],
        out_shape=[jax.ShapeDtypeStruct((g, nlag * p, p), F32),
                   jax.ShapeDtypeStruct((g, 2, CHUNK, p, 2 * n), F32),
                   jax.ShapeDtypeStruct((g, 2, CHUNK + 1, p, 2 * n), F32),
                   jax.ShapeDtypeStruct((g, 2, 8, 2 * n), F32)],
        compiler_params=_cparams(("arbitrary",)),
    )(are, aim, ldt, btr, bti, cr, ci)


def _ssm_operators(kc, pb, cp, q, d_skip):
    g, p, n = SSM_GROUPS, SSM_GROUP, SSM_STATE
    t = jnp.arange(CHUNK)
    lag = t[:, None] - t[None, :] + CHUNK - 1
    kc5 = kc.reshape(g, 2 * CHUNK - 1, p, p)[:, lag]
    toep = jnp.transpose(kc5, (0, 1, 3, 2, 4)).reshape(g, CHUNK_W, CHUNK_W)
    wst_f = jnp.transpose(pb[:, 0, ::-1].reshape(g, CHUNK_W, 2 * n), (0, 2, 1))
    wst_b = jnp.transpose(pb[:, 1].reshape(g, CHUNK_W, 2 * n), (0, 2, 1))
    l1 = jnp.concatenate([toep, wst_f, wst_b], axis=1).astype(MXU)
    v_f = cp[:, 0, 1:].reshape(g, CHUNK_W, 2 * n)
    v_b = cp[:, 1, 1:][:, ::-1].reshape(g, CHUNK_W, 2 * n)
    vcat = jnp.concatenate([v_f, v_b], axis=2).astype(MXU)
    qre = q[:, :, :7, :n, None]
    qim = q[:, :, :7, n:, None]
    dsk = jnp.tile(d_skip.reshape(g, 1, p), (1, CHUNK, 1)).reshape(g, CHUNK_W, 1)
    return l1, vcat, qre, qim, dsk


def _proj_kernel(x_ref, mod_ref, nw_ref, wqkv_ref, wut_ref, cos_ref, s1_ref, s2_ref,
                 q_ref, k_ref, v_ref, ut_ref, *, rope):
    nch = x_ref.shape[0]
    d = D_MODEL
    sh = mod_ref[0:1, :]
    sc = mod_ref[1:2, :]
    nw = nw_ref[...]

    def normed(xb):
        ms = jnp.mean(xb * xb, axis=-1, keepdims=True)
        hn = xb * lax.rsqrt(ms + EPS) * nw
        return (hn * (1.0 + sc) + sh).astype(MXU)

    def rot(tile, t):
        if not rope:
            return tile
        cs = cos_ref[:, t * LANES:(t + 1) * LANES]
        a1 = s1_ref[:, t * LANES:(t + 1) * LANES]
        a2 = s2_ref[:, t * LANES:(t + 1) * LANES]
        return tile * cs + pltpu.roll(tile, LANES - ROT_PAIRS, 1) * a1 + pltpu.roll(tile, ROT_PAIRS, 1) * a2

    tg = 4
    for t0 in range(0, CHUNK, tg):
        hcat = jnp.concatenate([normed(x_ref[:, t * d:(t + 1) * d]) for t in range(t0, t0 + tg)], axis=0)
        qkv = jnp.dot(hcat, wqkv_ref[...], preferred_element_type=F32)
        pt = lax.dot_general(wut_ref[...], hcat, (((1,), (1,)), ((), ())), preferred_element_type=F32)
        for i, t in enumerate(range(t0, t0 + tg)):
            rows = qkv[i * nch:(i + 1) * nch]
            for j in range(ATTN_WIDTH // LANES):
                qj = rot(rows[:, j * LANES:(j + 1) * LANES], t) * ATTN_SCALE
                q_ref[:, t * ATTN_WIDTH + j * LANES:t * ATTN_WIDTH + (j + 1) * LANES] = qj.astype(MXU)
            k_ref[:, t * KV_WIDTH:(t + 1) * KV_WIDTH] = rot(rows[:, ATTN_WIDTH:ATTN_WIDTH + KV_WIDTH], t).astype(MXU)
            v_ref[:, t * KV_WIDTH:(t + 1) * KV_WIDTH] = rows[:, ATTN_WIDTH + KV_WIDTH:].astype(MXU)
            ut_ref[:, t * SSM_GROUP:(t + 1) * SSM_GROUP, :] = (
                pt[:, i * nch:(i + 1) * nch].reshape(SSM_GROUPS, SSM_GROUP, nch))


def _proj(x, mod3, mod_row0, nw, wqkv, wut, cos_t, s1_t, s2_t, rope):
    nb, ntok, d = x.shape
    nch = ntok // CHUNK
    const2 = lambda b: (0, 0)
    chunked = lambda width: pl.BlockSpec((None, nch, CHUNK * width), lambda b: (b, 0, 0))
    tables = [t.reshape(-1, CHUNK * LANES) for t in (cos_t, s1_t, s2_t)]
    q, k, v, ut = pl.pallas_call(
        functools.partial(_proj_kernel, rope=rope),
        grid=(nb,),
        in_specs=[chunked(d),
                  pl.BlockSpec((None, N_MOD, d), lambda b: (b + mod_row0, 0, 0)),
                  pl.BlockSpec((1, d), const2),
                  pl.BlockSpec(wqkv.shape, const2),
                  pl.BlockSpec(wut.shape, const2)] + [pl.BlockSpec(t.shape, const2) for t in tables],
        out_specs=[chunked(ATTN_WIDTH), chunked(KV_WIDTH), chunked(KV_WIDTH),
                   pl.BlockSpec((SSM_GROUPS, CHUNK_W, nch), lambda b: (0, 0, b))],
        out_shape=[jax.ShapeDtypeStruct((nb, nch, CHUNK * ATTN_WIDTH), MXU),
                   jax.ShapeDtypeStruct((nb, nch, CHUNK * KV_WIDTH), MXU),
                   jax.ShapeDtypeStruct((nb, nch, CHUNK * KV_WIDTH), MXU),
                   jax.ShapeDtypeStruct((SSM_GROUPS, CHUNK_W, nb * nch), F32)],
        compiler_params=_cparams(("arbitrary",)),
    )(x.reshape(nb, nch, CHUNK * d), mod3, nw, wqkv, wut, *tables)
    return (q.reshape(nb, ntok, ATTN_WIDTH), k.reshape(nb, ntok, KV_WIDTH), v.reshape(nb, ntok, KV_WIDTH), ut)


def _attn_kernel(sink_ref, q_ref, k_ref, v_ref, kc_ref, vc_ref, o_ref):
    blk = q_ref.shape[0]
    seq = k_ref.shape[0]
    nwin = 3 * blk
    gq = ATTN_HEADS // ATTN_KV_HEADS
    i = pl.program_id(1)
    start = pl.multiple_of(jnp.clip((i - 1) * blk, 0, seq - nwin), blk)
    kw = k_ref[pl.ds(start, nwin), :]
    vw = v_ref[pl.ds(start, nwin), :]
    kc = kc_ref[...]
    vc = vc_ref[...]
    q = q_ref[...]
    rows = gq * blk
    qpos = i * blk + lax.broadcasted_iota(jnp.int32, (rows, nwin), 0) % blk
    kpos = start + lax.broadcasted_iota(jnp.int32, (rows, nwin), 1)
    band = jnp.abs(qpos - kpos) <= WINDOW
    lane_q = lax.broadcasted_iota(jnp.int32, (blk, LANES), 1)
    nt = (((1,), (1,)), ((), ()))
    outs = []
    for kh in range(ATTN_KV_HEADS):
        own = (lane_q // HEAD_DIM) == kh
        qs = jnp.concatenate(
            [jnp.where(own, q[:, j * LANES:(j + 1) * LANES], jnp.zeros((), MXU)) for j in range(gq)], axis=0)
        sink = jnp.concatenate(
            [jnp.full((blk, 1), sink_ref[kh * gq + j], F32) for j in range(gq)], axis=0)
        s_loc = lax.dot_general(qs, kw, nt, preferred_element_type=F32)
        s_ctx = lax.dot_general(qs, kc, nt, preferred_element_type=F32)
        s_loc = jnp.where(band, s_loc, -jnp.inf)
        m = jnp.maximum(jnp.maximum(jnp.max(s_loc, axis=-1, keepdims=True),
                                    jnp.max(s_ctx, axis=-1, keepdims=True)), sink)
        p_loc = jnp.exp(s_loc - m)
        p_ctx = jnp.exp(s_ctx - m)
        den = (jnp.sum(p_loc, axis=-1, keepdims=True) + jnp.sum(p_ctx, axis=-1, keepdims=True)
               + jnp.exp(sink - m))
        o = (jnp.dot(p_loc.astype(MXU), vw, preferred_element_type=F32)
             + jnp.dot(p_ctx.astype(MXU), vc, preferred_element_type=F32)) / den
        outs.append(o)
    first = lax.broadcasted_iota(jnp.int32, (blk, LANES), 1) < HEAD_DIM
    for j in range(gq):
        o_ref[:, j * LANES:(j + 1) * LANES] = jnp.where(
            first, outs[0][j * blk:(j + 1) * blk], outs[1][j * blk:(j + 1) * blk])


def _attention(sink, q, k, v, kc, vc):
    nb, seq, _ = q.shape
    nctx = kc.shape[1]
    blk = WINDOW
    return pl.pallas_call(
        _attn_kernel,
        grid=(nb, seq // blk),
        in_specs=[pl.BlockSpec(memory_space=pltpu.SMEM),
                  pl.BlockSpec((None, blk, ATTN_WIDTH), lambda b, i: (b, i, 0)),
                  pl.BlockSpec((None, seq, KV_WIDTH), lambda b, i: (b, 0, 0)),
                  pl.BlockSpec((None, seq, KV_WIDTH), lambda b, i: (b, 0, 0)),
                  pl.BlockSpec((None, nctx, KV_WIDTH), lambda b, i: (b, 0, 0)),
                  pl.BlockSpec((None, nctx, KV_WIDTH), lambda b, i: (b, 0, 0))],
        out_specs=pl.BlockSpec((None, blk, ATTN_WIDTH), lambda b, i: (b, i, 0)),
        out_shape=jax.ShapeDtypeStruct((nb, seq, ATTN_WIDTH), F32),
        compiler_params=_cparams(("arbitrary", "arbitrary")),
    )(sink, q, k, v, kc, vc)


def _chunk_scan(xr, xi, qre_ref, qim_ref, d, nsteps, seg, fwd):
    width = xr.shape[1]
    pos = lax.broadcasted_iota(jnp.int32, xr.shape, 1) % seg
    for j in range(nsteps):
        s = 1 << j
        if fwd:
            sr, si, ok = pltpu.roll(xr, s, 1), pltpu.roll(xi, s, 1), pos >= s
        else:
            sr, si, ok = pltpu.roll(xr, width - s, 1), pltpu.roll(xi, width - s, 1), pos < seg - s
        sr = jnp.where(ok, sr, 0.0)
        si = jnp.where(ok, si, 0.0)
        qr = qre_ref[d, j]
        qi = qim_ref[d, j]
        xr, xi = xr + qr * sr - qi * si, xi + qr * si + qi * sr
    return xr, xi


def _ssm_kernel(ul_ref, uc_ref, l1_ref, v_ref, qre_ref, qim_ref, dsk_ref, y_ref, *, nb):
    n = SSM_STATE
    ul = ul_ref[...]
    width = ul.shape[1]
    seg_l = width // nb
    seg_c = uc_ref.shape[1] // nb
    r = jnp.dot(l1_ref[...], ul.astype(MXU), preferred_element_type=F32)
    rc = jnp.dot(l1_ref[CHUNK_W:, :], uc_ref[...].astype(MXU), preferred_element_type=F32)
    lane = lax.broadcasted_iota(jnp.int32, (n, width), 1)
    hin = []
    for d, fwd in enumerate((True, False)):
        base = CHUNK_W + 2 * n * d
        xr, xi = r[base:base + n], r[base + n:base + 2 * n]
        cr, ci = rc[2 * n * d:2 * n * d + n], rc[2 * n * d + n:2 * n * d + 2 * n]
        cr, ci = _chunk_scan(cr, ci, qre_ref, qim_ref, d, int(math.log2(seg_c)), seg_c, fwd)
        injr = jnp.zeros((n, width), F32)
        inji = jnp.zeros((n, width), F32)
        for b in range(nb):
            src = b * seg_c + (seg_c - 1 if fwd else 0)
            dst = b * seg_l + (0 if fwd else seg_l - 1)
            injr = jnp.where(lane == dst, cr[:, src:src + 1], injr)
            inji = jnp.where(lane == dst, ci[:, src:src + 1], inji)
        q0r, q0i = qre_ref[d, 0], qim_ref[d, 0]
        xr, xi = xr + q0r * injr - q0i * inji, xi + q0r * inji + q0i * injr
        sr, si = _chunk_scan(xr, xi, qre_ref, qim_ref, d, int(math.log2(seg_l)), seg_l, fwd)
        edge = (lane % seg_l) == (0 if fwd else seg_l - 1)
        shift = 1 if fwd else width - 1
        hin.append(jnp.where(edge, injr, pltpu.roll(sr, shift, 1)))
        hin.append(jnp.where(edge, inji, pltpu.roll(si, shift, 1)))
    hcat = jnp.concatenate(hin, axis=0).astype(MXU)
    y = r[:CHUNK_W] + jnp.dot(v_ref[...], hcat, preferred_element_type=F32) + dsk_ref[...] * ul
    y_ref[...] = 0.5 * y * (1.0 + jnp.tanh(math.sqrt(2.0 / math.pi) * (y + 0.044715 * (y * y * y))))


def _ssm(ul, uc, l1, vcat, qre, qim, dsk, nb):
    g, _, width = ul.shape
    wc = uc.shape[2]
    n = SSM_STATE
    return pl.pallas_call(
        functools.partial(_ssm_kernel, nb=nb),
        grid=(g,),
        in_specs=[pl.BlockSpec((None, CHUNK_W, width), lambda i: (i, 0, 0)),
                  pl.BlockSpec((None, CHUNK_W, wc), lambda i: (i, 0, 0)),
                  pl.BlockSpec((None, 2 * CHUNK_W, CHUNK_W), lambda i: (i, 0, 0)),
                  pl.BlockSpec((None, CHUNK_W, CHUNK_W), lambda i: (i, 0, 0)),
                  pl.BlockSpec((None, 2, 7, n, 1), lambda i: (i, 0, 0, 0, 0)),
                  pl.BlockSpec((None, 2, 7, n, 1), lambda i: (i, 0, 0, 0, 0)),
                  pl.BlockSpec((None, CHUNK_W, 1), lambda i: (i, 0, 0))],
        out_specs=pl.BlockSpec((None, CHUNK_W, width), lambda i: (i, 0, 0)),
        out_shape=jax.ShapeDtypeStruct((g, CHUNK_W, width), F32),
        compiler_params=_cparams(("arbitrary",)),
    )(ul, uc, l1, vcat, qre, qim, dsk)


def _route(logits_t, bias):
    ng, ne = N_EXPERT_GROUPS, N_EXPERTS // N_EXPERT_GROUPS
    t = logits_t.shape[1]
    scores = _sigmoid(logits_t).reshape(ng, ne, t)
    biased = scores + bias.reshape(ng, ne, 1)
    iw = lax.broadcasted_iota(jnp.int32, (ng, ne, t), 1)
    ig = lax.broadcasted_iota(jnp.int32, (ng, ne, t), 0)
    neg = -jnp.inf
    m1 = jnp.max(biased, axis=1, keepdims=True)
    i1 = jnp.min(jnp.where(biased == m1, iw, ne), axis=1, keepdims=True)
    m2 = jnp.max(jnp.where(iw == i1, neg, biased), axis=1, keepdims=True)
    gscore = jnp.broadcast_to(m1 + m2, (ng, ne, t))
    gsel = jnp.zeros((ng, ne, t), F32)
    cur = gscore
    for _ in range(TOPK_GROUPS):
        m = jnp.max(cur, axis=0, keepdims=True)
        gi = jnp.min(jnp.where(cur == m, ig, ng), axis=0, keepdims=True)
        pick = ig == gi
        gsel = jnp.where(pick, 1.0, gsel)
        cur = jnp.where(pick, neg, cur)
    cur = jnp.where(gsel > 0.0, biased, neg)
    flat = ig * ne + iw
    chosen = jnp.zeros((ng, ne, t), F32)
    for _ in range(TOP_K):
        m = jnp.max(jnp.max(cur, axis=1, keepdims=True), axis=0, keepdims=True)
        fi = jnp.min(jnp.min(jnp.where(cur == m, flat, N_EXPERTS), axis=1, keepdims=True), axis=0, keepdims=True)
        pick = flat == fi
        chosen = jnp.where(pick, 1.0, chosen)
        cur = jnp.where(pick, neg, cur)
    sel = jnp.where(chosen > 0.0, scores, 0.0)
    tot = jnp.sum(jnp.sum(sel, axis=1, keepdims=True), axis=0, keepdims=True)
    return (sel / tot * ROUTED_SCALE).reshape(N_EXPERTS, t)


def _mix_kernel(yt_ref, attn_ref, x_ref, mod_ref, wglut_ref, bglu_ref, gssm_ref, wouts_ref,
                gattn_ref, wouta_ref, nffn_ref, wrt_ref, rbias_ref,
                x1_ref, h2_ref, gates_ref):
    y = yt_ref[...].reshape(SSM_WIDTH, yt_ref.shape[-1])
    glu = jnp.dot(wglut_ref[...], y.astype(MXU), preferred_element_type=F32) + bglu_ref[...]
    z = y * _sigmoid(glu)
    zn = z * lax.rsqrt(jnp.mean(z * z, axis=0, keepdims=True) + EPS) * gssm_ref[...]
    o_s = lax.dot_general(zn.astype(MXU), wouts_ref[...], (((0,), (0,)), ((), ())),
                          preferred_element_type=F32)
    a = attn_ref[...]
    an = a * lax.rsqrt(jnp.mean(a * a, axis=-1, keepdims=True) + EPS) * gattn_ref[...]
    o_a = jnp.dot(an.astype(MXU), wouta_ref[...], preferred_element_type=F32)
    g1 = mod_ref[2:3, :]
    sh2 = mod_ref[3:4, :]
    sc2 = mod_ref[4:5, :]
    x1 = x_ref[...] + g1 * (o_s + o_a)
    x1_ref[...] = x1
    h2 = x1 * lax.rsqrt(jnp.mean(x1 * x1, axis=-1, keepdims=True) + EPS) * nffn_ref[...]
    h2 = h2 * (1.0 + sc2) + sh2
    h2_ref[...] = h2.astype(MXU)
    logits_t = lax.dot_general(wrt_ref[...], h2, (((1,), (1,)), ((), ())),
                               precision=HIGHEST, preferred_element_type=F32)
    gates_t = _route(logits_t, rbias_ref[...])
    t = gates_t.shape[1]
    padded = jnp.concatenate([gates_t, jnp.zeros((LANES - N_EXPERTS, t), F32)], axis=0)
    gates_ref[...] = padded.T


def _mix(yt, attn, x, mod3, wglut, bglu, gssm, wouts, gattn, wouta, nffn, wrt, rbias):
    nb, seq, d = x.shape
    nch = seq // CHUNK
    yt4 = yt.reshape(SSM_GROUPS, CHUNK, SSM_GROUP, nb * nch)
    attn4 = attn.reshape(nb, nch, CHUNK * ATTN_WIDTH)
    x4 = x.reshape(nb, nch, CHUNK * d)
    c2 = lambda b, t: (0, 0)
    perm = lambda last: pl.BlockSpec((None, None, nch, last), lambda b, t: (b, t, 0, 0))
    return pl.pallas_call(
        _mix_kernel,
        grid=(nb, CHUNK),
        in_specs=[pl.BlockSpec((SSM_GROUPS, None, SSM_GROUP, nch), lambda b, t: (0, t, 0, b)),
                  pl.BlockSpec((None, nch, ATTN_WIDTH), lambda b, t: (b, 0, t)),
                  pl.BlockSpec((None, nch, d), lambda b, t: (b, 0, t)),
                  pl.BlockSpec((None, N_MOD, d), lambda b, t: (b, 0, 0)),
                  pl.BlockSpec(wglut.shape, c2), pl.BlockSpec(bglu.shape, c2), pl.BlockSpec(gssm.shape, c2),
                  pl.BlockSpec(wouts.shape, c2), pl.BlockSpec(gattn.shape, c2), pl.BlockSpec(wouta.shape, c2),
                  pl.BlockSpec(nffn.shape, c2), pl.BlockSpec(wrt.shape, c2), pl.BlockSpec(rbias.shape, c2)],
        out_specs=[perm(d), perm(d), perm(LANES)],
        out_shape=[jax.ShapeDtypeStruct((nb, CHUNK, nch, d), F32),
                   jax.ShapeDtypeStruct((nb, CHUNK, nch, d), MXU),
                   jax.ShapeDtypeStruct((nb, CHUNK, nch, LANES), F32)],
        compiler_params=_cparams(("arbitrary", "arbitrary")),
    )(yt4, attn4, x4, mod3, wglut, bglu, gssm, wouts, gattn, wouta, nffn, wrt, rbias)


def _moe_kernel(h2_ref, gates_ref, x1_ref, mod_ref, wcat_ref, wd_ref, wscat_ref, wsd_ref, nfin_ref,
                o_ref, acc_ref, hid_ref):
    s = pl.program_id(1)
    h2 = h2_ref[...]
    f = EXPERT_DIM

    def glu(gu):
        g = gu[:, :f]
        return g * _sigmoid(g) * gu[:, f:]

    @pl.when(s == 0)
    def _():
        hs = glu(jnp.dot(h2, wscat_ref[...], preferred_element_type=F32))
        acc_ref[...] = jnp.dot(hs.astype(MXU), wsd_ref[...], preferred_element_type=F32)

    gsh = pltpu.roll(gates_ref[...], (LANES - EXPERTS_PER_STEP * s) % LANES, 1)
    for e in range(EXPERTS_PER_STEP):
        hid = glu(jnp.dot(h2, wcat_ref[e], preferred_element_type=F32)) * gsh[:, e:e + 1]
        hid_ref[:, e * f:(e + 1) * f] = hid.astype(MXU)
    acc_ref[...] += jnp.dot(hid_ref[...], wd_ref[...], preferred_element_type=F32)

    @pl.when(s == pl.num_programs(1) - 1)
    def _():
        g2 = mod_ref[5:6, :]
        x2 = x1_ref[...] + g2 * acc_ref[...]
        out = x2 * lax.rsqrt(jnp.mean(x2 * x2, axis=-1, keepdims=True) + EPS) * nfin_ref[...]
        nch, d = o_ref.shape[0], x2.shape[1]
        for tt in range(o_ref.shape[1] // d):
            o_ref[:, tt * d:(tt + 1) * d] = out[tt * nch:(tt + 1) * nch, :]


def _moe(h2p, gatesp, x1p, mod3, wcat, wd, wscat, wsd, nfin):
    nb, _, nch, d = x1p.shape
    seq = nch * CHUNK
    tper = MOE_TILE // nch
    tiles_per_b = CHUNK // tper
    ntiles = nb * tiles_per_b
    nsteps = N_EXPERTS // EXPERTS_PER_STEP
    h2f = h2p.reshape(ntiles, MOE_TILE, d)
    gf = gatesp.reshape(ntiles, MOE_TILE, LANES)
    x1f = x1p.reshape(ntiles, MOE_TILE, d)
    c2 = lambda i, s: (0, 0)
    tok = lambda last: pl.BlockSpec((None, MOE_TILE, last), lambda i, s: (i, 0, 0))
    out = pl.pallas_call(
        _moe_kernel,
        grid=(ntiles, nsteps),
        in_specs=[tok(d), tok(LANES), tok(d),
                  pl.BlockSpec((None, N_MOD, d), lambda i, s: (i // tiles_per_b, 0, 0)),
                  pl.BlockSpec((EXPERTS_PER_STEP, d, 2 * EXPERT_DIM), lambda i, s: (s, 0, 0)),
                  pl.BlockSpec((EXPERTS_PER_STEP * EXPERT_DIM, d), lambda i, s: (s, 0)),
                  pl.BlockSpec(wscat.shape, c2), pl.BlockSpec(wsd.shape, c2), pl.BlockSpec(nfin.shape, c2)],
        out_specs=pl.BlockSpec((None, nch, tper * d), lambda i, s: (i // tiles_per_b, 0, i % tiles_per_b)),
        out_shape=jax.ShapeDtypeStruct((nb, nch, CHUNK * d), F32),
        scratch_shapes=[pltpu.VMEM((MOE_TILE, d), F32),
                        pltpu.VMEM((MOE_TILE, EXPERTS_PER_STEP * EXPERT_DIM), MXU)],
        compiler_params=_cparams(("arbitrary", "arbitrary")),
    )(h2f, gf, x1f, mod3, wcat, wd, wscat, wsd, nfin)
    return out.reshape(nb, seq, d)


def _rope_tables(seq):
    pos = jnp.arange(seq)
    row = (pos // GRID_W).astype(F32)
    col = (pos % GRID_W).astype(F32)
    inv = ROPE_THETA ** (-jnp.arange(ROT_PAIRS, dtype=F32) / ROT_PAIRS)
    ar, ac = row[:, None] * inv, col[:, None] * inv
    zero = jnp.zeros_like(ar)
    rep = LANES // HEAD_DIM
    cos_t = jnp.tile(jnp.concatenate([jnp.cos(ar), jnp.cos(ar), jnp.cos(ac), jnp.cos(ac)], axis=1), (1, rep))
    s1_t = jnp.tile(jnp.concatenate([-jnp.sin(ar), zero, -jnp.sin(ac), zero], axis=1), (1, rep))
    s2_t = jnp.tile(jnp.concatenate([zero, jnp.sin(ar), zero, jnp.sin(ac)], axis=1), (1, rep))
    return cos_t, s1_t, s2_t


def kernel(x, c, ctx, c_ctx, w_ada, b_ada, norm_mix, norm_ffn, w_in, attn_sink, ssm_a_re, ssm_a_im, ssm_log_dt, ssm_b_re, ssm_b_im, ssm_c_re, ssm_c_im, ssm_d, w_glu, b_glu, norm_attn_out, norm_ssm_out, w_out, w_router, router_bias, w_gate_e, w_up_e, w_down_e, w_gate_s, w_up_s, w_down_s, norm_final):
    nb, seq, d = x.shape
    nctx = ctx.shape[1]
    layer = 0

    pad = jnp.zeros((16 - nb - 1, d), F32)
    c_all = jnp.concatenate([c, c_ctx[None, :], pad], axis=0)
    mod3 = _ada(c_all, w_ada[layer], b_ada[layer]).reshape(16, N_MOD, d)

    gq = ATTN_HEADS // ATTN_KV_HEADS
    heads = jnp.arange(ATTN_HEADS).reshape(ATTN_KV_HEADS, gq).T.reshape(-1)
    perm_q = (heads[:, None] * HEAD_DIM + jnp.arange(HEAD_DIM)[None, :]).reshape(-1)
    w_in0 = w_in[layer]
    wqkv = jnp.concatenate([w_in0[:, :ATTN_WIDTH][:, perm_q], w_in0[:, ATTN_WIDTH:ATTN_WIDTH + 2 * KV_WIDTH]],
                           axis=1).astype(MXU)
    wut = w_in0[:, ATTN_WIDTH + 2 * KV_WIDTH:].T.astype(MXU)
    nw = norm_mix[layer].reshape(1, d)
    cos_t, s1_t, s2_t = _rope_tables(seq)

    q, k, v, ul = _proj(x, mod3, 0, nw, wqkv, wut, cos_t, s1_t, s2_t, True)
    _, kc, vc, uc = _proj(ctx.reshape(1, nb * nctx, d), mod3, nb, nw, wqkv, wut, cos_t, s1_t, s2_t, False)
    kc = kc.reshape(nb, nctx, KV_WIDTH)
    vc = vc.reshape(nb, nctx, KV_WIDTH)

    attn = _attention(attn_sink[layer], q, k, v, kc, vc)

    kcomb, pb, cp, qpow = _ssm_prep(ssm_a_re[layer], ssm_a_im[layer], ssm_log_dt[layer], ssm_b_re[layer],
                                    ssm_b_im[layer], ssm_c_re[layer], ssm_c_im[layer])
    l1, vcat, qre, qim, dsk = _ssm_operators(kcomb, pb, cp, qpow, ssm_d[layer])
    yt = _ssm(ul, uc, l1, vcat, qre, qim, dsk, nb)

    w_out0 = w_out[layer]
    x1p, h2p, gatesp = _mix(
        yt, attn, x, mod3,
        w_glu[layer].T.astype(MXU), b_glu[layer].reshape(SSM_WIDTH, 1), norm_ssm_out[layer].reshape(SSM_WIDTH, 1),
        w_out0[ATTN_WIDTH:].astype(MXU), norm_attn_out[layer][perm_q].reshape(1, ATTN_WIDTH),
        w_out0[:ATTN_WIDTH][perm_q].astype(MXU), norm_ffn[layer].reshape(1, d),
        w_router[layer].T, router_bias[layer].reshape(N_EXPERTS, 1))

    wcat = jnp.concatenate([w_gate_e[layer], w_up_e[layer]], axis=-1).astype(MXU)
    wd = w_down_e[layer].astype(MXU).reshape(N_EXPERTS * EXPERT_DIM, d)
    wscat = jnp.concatenate([w_gate_s[layer], w_up_s[layer]], axis=-1).astype(MXU)
    wsd = w_down_s[layer].astype(MXU)
    return _moe(h2p, gatesp, x1p, mod3, wcat, wd, wscat, wsd, norm_final.reshape(1, d))
```

```python
import functools
import math

import jax
import jax.numpy as jnp
from jax import lax
from jax.experimental import pallas as pl
from jax.experimental.pallas import tpu as pltpu

D_MODEL = 1024
EPS = 1e-6
N_MOD = 6
HEAD_DIM = 64
ATTN_HEADS = 8
ATTN_KV_HEADS = 2
ATTN_WIDTH = ATTN_HEADS * HEAD_DIM
KV_WIDTH = ATTN_KV_HEADS * HEAD_DIM
WINDOW = 128
ATTN_SCALE = HEAD_DIM ** -0.5
ROPE_THETA = 10000.0
ROT_PAIRS = HEAD_DIM // 4
GRID_W = 64
SSM_WIDTH = D_MODEL - ATTN_WIDTH
SSM_GROUP = 16
SSM_GROUPS = SSM_WIDTH // SSM_GROUP
SSM_STATE = 64
N_EXPERTS = 64
EXPERT_DIM = 128
TOP_K = 8
N_EXPERT_GROUPS = 8
TOPK_GROUPS = 4
ROUTED_SCALE = 2.5

CHUNK = 16
CHUNK_W = CHUNK * SSM_GROUP
LANES = 128
ATTN_QBLOCKS = 2
MIX_POSITIONS = 4
MOE_TILE = 1024
EXPERTS_PER_STEP = 8
VMEM_LIMIT = 56 * 1024 * 1024

MXU = jnp.bfloat16
F32 = jnp.float32
HIGHEST = lax.Precision.HIGHEST


def _sigmoid(x):
    return 1.0 / (1.0 + jnp.exp(-x))


def _cparams(sem):
    return pltpu.CompilerParams(dimension_semantics=sem, vmem_limit_bytes=VMEM_LIMIT)


def _slab_specs(rows, width, batch_of, rowblock_of):
    return [pl.BlockSpec((None, rows, LANES), lambda *g, j=j: (batch_of(*g), rowblock_of(*g), j))
            for j in range(width // LANES)]


def _ada_kernel(c_ref, w_ref, b_ref, o_ref):
    cv = c_ref[...]
    s = cv * _sigmoid(cv)
    o_ref[...] = jnp.dot(s, w_ref[...], precision=HIGHEST, preferred_element_type=F32) + b_ref[...]


def _ada(c_all, w, b):
    rows, d = c_all.shape
    n = w.shape[1]
    tn = 1024
    return pl.pallas_call(
        _ada_kernel,
        grid=(n // tn,),
        in_specs=[pl.BlockSpec((rows, d), lambda j: (0, 0)),
                  pl.BlockSpec((d, tn), lambda j: (0, j)),
                  pl.BlockSpec((1, tn), lambda j: (0, j))],
        out_specs=pl.BlockSpec((rows, tn), lambda j: (0, j)),
        out_shape=jax.ShapeDtypeStruct((rows, n), F32),
        compiler_params=_cparams(("arbitrary",)),
    )(c_all, w, b.reshape(1, n))


def _ssm_prep_kernel(are_ref, aim_ref, ldt_ref, btr_ref, bti_ref, cr_ref, ci_ref,
                     kc_ref, pb_ref, cp_ref, q_ref):
    ar, ai = are_ref[...], aim_ref[...]
    dt = jnp.exp(ldt_ref[...])
    mag = jnp.exp(dt * ar)
    abr = mag * jnp.cos(dt * ai)
    abi = mag * jnp.sin(dt * ai)
    den = ar * ar + ai * ai
    nr = abr - 1.0
    cor = (nr * ar + abi * ai) / den
    coi = (abi * ar - nr * ai) / den
    btr, bti = btr_ref[...], bti_ref[...]
    bbr = cor[:, None, :] * btr - coi[:, None, :] * bti
    bbi = cor[:, None, :] * bti + coi[:, None, :] * btr
    cr, ci = cr_ref[...], ci_ref[...]
    pr, pi = jnp.ones_like(abr), jnp.zeros_like(abr)
    cps = []
    for k in range(CHUNK + 1):
        cpr = cr * pr[:, None, :] - ci * pi[:, None, :]
        cpi = cr * pi[:, None, :] + ci * pr[:, None, :]
        cpk = jnp.concatenate([cpr, -cpi], axis=-1)
        cps.append(cpk)
        cp_ref[:, k] = cpk
        if k < CHUNK:
            pb_ref[:, k] = jnp.concatenate([bbr * pr[:, None, :] - bbi * pi[:, None, :],
                                            bbr * pi[:, None, :] + bbi * pr[:, None, :]], axis=-1)
        if k < CHUNK:
            pr, pi = pr * abr - pi * abi, pr * abi + pi * abr
    qr, qi = pr, pi
    bbcat = jnp.concatenate([bbr, bbi], axis=-1)
    lagk = []
    for d in range(2):
        stack = jnp.concatenate([cps[k][d] for k in range(CHUNK)], axis=0)
        lagk.append(lax.dot_general(stack, bbcat[d], (((1,), (1,)), ((), ())),
                                    precision=HIGHEST, preferred_element_type=F32))
    kf, kb = lagk
    p = SSM_GROUP
    blocks = [kb[(CHUNK - 1 - j) * p:(CHUNK - j) * p] for j in range(CHUNK - 1)]
    blocks.append(kf[:p] + kb[:p])
    blocks += [kf[k * p:(k + 1) * p] for k in range(1, CHUNK)]
    kc_ref[...] = jnp.concatenate(blocks, axis=0)
    for j in range(8):
        q_ref[:, j] = jnp.concatenate([qr, qi], axis=-1)
        qr, qi = qr * qr - qi * qi, 2.0 * qr * qi


def _ssm_prep(a_re, a_im, log_dt, b_re, b_im, c_re, c_im):
    g, n, p = SSM_GROUPS, SSM_STATE, SSM_GROUP
    are = jnp.transpose(a_re, (1, 0, 2))
    aim = jnp.transpose(a_im, (1, 0, 2))
    ldt = jnp.transpose(log_dt, (1, 0))[..., None]
    btr = jnp.transpose(b_re, (1, 0, 3, 2))
    bti = jnp.transpose(b_im, (1, 0, 3, 2))
    cr = jnp.transpose(c_re, (1, 0, 2, 3))
    ci = jnp.transpose(c_im, (1, 0, 2, 3))
    nlag = 2 * CHUNK - 1
    vec = pl.BlockSpec((None, 2, n), lambda i: (i, 0, 0))
    mat = pl.BlockSpec((None, 2, p, n), lambda i: (i, 0, 0, 0))
    return pl.pallas_call(
        _ssm_prep_kernel,
        grid=(g,),
        in_specs=[vec, vec, pl.BlockSpec((None, 2, 1), lambda i: (i, 0, 0)), mat, mat, mat, mat],
        out_specs=[pl.BlockSpec((None, nlag * p, p), lambda i: (i, 0, 0)),
                   pl.BlockSpec((None, 2, CHUNK, p, 2 * n), lambda i: (i, 0, 0, 0, 0)),
                   pl.BlockSpec((None, 2, CHUNK + 1, p, 2 * n), lambda i: (i, 0, 0, 0, 0)),
                   pl.BlockSpec((None, 2, 8, 2 * n), lambda i: (i, 0, 0, 0))],
        out_shape=[jax.ShapeDtypeStruct((g, nlag * p, p), F32),
                   jax.ShapeDtypeStruct((g, 2, CHUNK, p, 2 * n), F32),
                   jax.ShapeDtypeStruct((g, 2, CHUNK + 1, p, 2 * n), F32),
                   jax.ShapeDtypeStruct((g, 2, 8, 2 * n), F32)],
        compiler_params=_cparams(("arbitrary",)),
    )(are, aim, ldt, btr, bti, cr, ci)


def _ssm_operators(kc, pb, cp, q, d_skip):
    g, p, n = SSM_GROUPS, SSM_GROUP, SSM_STATE
    t = jnp.arange(CHUNK)
    lag = t[:, None] - t[None, :] + CHUNK - 1
    kc5 = kc.reshape(g, 2 * CHUNK - 1, p, p)[:, lag]
    toep = jnp.transpose(kc5, (0, 1, 3, 2, 4)).reshape(g, CHUNK_W, CHUNK_W)
    wst_f = jnp.transpose(pb[:, 0, ::-1].reshape(g, CHUNK_W, 2 * n), (0, 2, 1))
    wst_b = jnp.transpose(pb[:, 1].reshape(g, CHUNK_W, 2 * n), (0, 2, 1))
    l1 = jnp.concatenate([toep, wst_f, wst_b], axis=1).astype(MXU)
    v_f = cp[:, 0, 1:].reshape(g, CHUNK_W, 2 * n)
    v_b = cp[:, 1, 1:][:, ::-1].reshape(g, CHUNK_W, 2 * n)
    vcat = jnp.concatenate([v_f, v_b], axis=2).astype(MXU)
    qre = q[:, :, :7, :n, None]
    qim = q[:, :, :7, n:, None]
    dsk = jnp.tile(d_skip.reshape(g, 1, p), (1, CHUNK, 1)).reshape(g, CHUNK_W, 1)
    return l1, vcat, qre, qim, dsk


def _rows(slabs, idx):
    return jnp.concatenate([s[idx, :] for s in slabs], axis=1)


def _proj_kernel(*refs, rope):
    nslab = D_MODEL // LANES
    xs = refs[:nslab]
    mod_ref, nw_ref, wqkv_ref, wut_ref, cos_ref, s1_ref, s2_ref, q_ref, k_ref, v_ref, ut_ref = refs[nslab:]
    ntok = xs[0].shape[0]
    nch = ntok // CHUNK
    sh = mod_ref[0:1, :]
    sc = mod_ref[1:2, :]
    nw = nw_ref[...]

    def normed(xb):
        ms = jnp.mean(xb * xb, axis=-1, keepdims=True)
        hn = xb * lax.rsqrt(ms + EPS) * nw
        return (hn * (1.0 + sc) + sh).astype(MXU)

    rb = 256

    def rot(tile, r0):
        if not rope:
            return tile
        cs = cos_ref[pl.ds(r0, rb), :]
        a1 = s1_ref[pl.ds(r0, rb), :]
        a2 = s2_ref[pl.ds(r0, rb), :]
        return tile * cs + pltpu.roll(tile, LANES - ROT_PAIRS, 1) * a1 + pltpu.roll(tile, ROT_PAIRS, 1) * a2

    def body(r, carry):
        r0 = pl.multiple_of(r * rb, rb)
        hb = normed(_rows(xs, pl.ds(r0, rb)))
        qkv = jnp.dot(hb, wqkv_ref[...], preferred_element_type=F32)
        for j in range(ATTN_WIDTH // LANES):
            qj = rot(qkv[:, j * LANES:(j + 1) * LANES], r0) * ATTN_SCALE
            q_ref[pl.ds(r0, rb), j * LANES:(j + 1) * LANES] = qj.astype(MXU)
        k_ref[pl.ds(r0, rb), :] = rot(qkv[:, ATTN_WIDTH:ATTN_WIDTH + KV_WIDTH], r0).astype(MXU)
        v_ref[pl.ds(r0, rb), :] = qkv[:, ATTN_WIDTH + KV_WIDTH:].astype(MXU)
        return carry

    lax.fori_loop(0, ntok // rb, body, 0)

    tg = 4
    for t0 in range(0, CHUNK, tg):
        hcat = jnp.concatenate(
            [normed(_rows(xs, pl.ds(t, nch, stride=CHUNK))) for t in range(t0, t0 + tg)], axis=0)
        pt = lax.dot_general(wut_ref[...], hcat, (((1,), (1,)), ((), ())), preferred_element_type=F32)
        for i, t in enumerate(range(t0, t0 + tg)):
            ut_ref[:, t * SSM_GROUP:(t + 1) * SSM_GROUP, :] = (
                pt[:, i * nch:(i + 1) * nch].reshape(SSM_GROUPS, SSM_GROUP, nch))


def _proj(x, mod3, mod_row0, nw, wqkv, wut, cos_t, s1_t, s2_t, rope):
    nb, ntok, d = x.shape
    nch = ntok // CHUNK
    const2 = lambda b: (0, 0)
    tok = lambda width: pl.BlockSpec((None, ntok, width), lambda b: (b, 0, 0))
    tables = (cos_t, s1_t, s2_t)
    return pl.pallas_call(
        functools.partial(_proj_kernel, rope=rope),
        grid=(nb,),
        in_specs=_slab_specs(ntok, d, lambda b: b, lambda b: 0)
        + [pl.BlockSpec((None, N_MOD, d), lambda b: (b + mod_row0, 0, 0)),
           pl.BlockSpec((1, d), const2),
           pl.BlockSpec(wqkv.shape, const2),
           pl.BlockSpec(wut.shape, const2)] + [pl.BlockSpec(t.shape, const2) for t in tables],
        out_specs=[tok(ATTN_WIDTH), tok(KV_WIDTH), tok(KV_WIDTH),
                   pl.BlockSpec((SSM_GROUPS, CHUNK_W, nch), lambda b: (0, 0, b))],
        out_shape=[jax.ShapeDtypeStruct((nb, ntok, ATTN_WIDTH), MXU),
                   jax.ShapeDtypeStruct((nb, ntok, KV_WIDTH), MXU),
                   jax.ShapeDtypeStruct((nb, ntok, KV_WIDTH), MXU),
                   jax.ShapeDtypeStruct((SSM_GROUPS, CHUNK_W, nb * nch), F32)],
        compiler_params=_cparams(("arbitrary",)),
    )(*([x] * (d // LANES)), mod3, nw, wqkv, wut, *tables)


def _attn_kernel(sink_ref, q_ref, k_ref, v_ref, kc_ref, vc_ref, o_ref):
    for sub in range(q_ref.shape[0] // WINDOW):
        _attn_block(sink_ref, q_ref, k_ref, v_ref, kc_ref, vc_ref, o_ref, sub)


def _attn_block(sink_ref, q_ref, k_ref, v_ref, kc_ref, vc_ref, o_ref, sub):
    blk = WINDOW
    seq = k_ref.shape[0]
    nwin = 3 * blk
    gq = ATTN_HEADS // ATTN_KV_HEADS
    i = pl.program_id(1) * (q_ref.shape[0] // blk) + sub
    start = pl.multiple_of(jnp.clip((i - 1) * blk, 0, seq - nwin), blk)
    kw = k_ref[pl.ds(start, nwin), :]
    vw = v_ref[pl.ds(start, nwin), :]
    kc = kc_ref[...]
    vc = vc_ref[...]
    q = q_ref[sub * blk:(sub + 1) * blk, :]
    rows = gq * blk
    qpos = i * blk + lax.broadcasted_iota(jnp.int32, (rows, nwin), 0) % blk
    kpos = start + lax.broadcasted_iota(jnp.int32, (rows, nwin), 1)
    band = jnp.abs(qpos - kpos) <= WINDOW
    lane_q = lax.broadcasted_iota(jnp.int32, (blk, LANES), 1)
    nt = (((1,), (1,)), ((), ()))
    outs = []
    for kh in range(ATTN_KV_HEADS):
        own = (lane_q // HEAD_DIM) == kh
        qs = jnp.concatenate(
            [jnp.where(own, q[:, j * LANES:(j + 1) * LANES], jnp.zeros((), MXU)) for j in range(gq)], axis=0)
        sink = jnp.concatenate(
            [jnp.full((blk, 1), sink_ref[kh * gq + j], F32) for j in range(gq)], axis=0)
        s_loc = lax.dot_general(qs, kw, nt, preferred_element_type=F32)
        s_ctx = lax.dot_general(qs, kc, nt, preferred_element_type=F32)
        s_loc = jnp.where(band, s_loc, -jnp.inf)
        m = jnp.maximum(jnp.maximum(jnp.max(s_loc, axis=-1, keepdims=True),
                                    jnp.max(s_ctx, axis=-1, keepdims=True)), sink)
        p_loc = jnp.exp(s_loc - m)
        p_ctx = jnp.exp(s_ctx - m)
        den = (jnp.sum(p_loc, axis=-1, keepdims=True) + jnp.sum(p_ctx, axis=-1, keepdims=True)
               + jnp.exp(sink - m))
        o = (jnp.dot(p_loc.astype(MXU), vw, preferred_element_type=F32)
             + jnp.dot(p_ctx.astype(MXU), vc, preferred_element_type=F32)) / den
        outs.append(o)
    first = lax.broadcasted_iota(jnp.int32, (blk, LANES), 1) < HEAD_DIM
    for j in range(gq):
        o_ref[sub * blk:(sub + 1) * blk, j * LANES:(j + 1) * LANES] = jnp.where(
            first, outs[0][j * blk:(j + 1) * blk], outs[1][j * blk:(j + 1) * blk])


def _attention(sink, q, k, v, kc, vc):
    nb, seq, _ = q.shape
    nctx = kc.shape[1]
    blk = ATTN_QBLOCKS * WINDOW
    return pl.pallas_call(
        _attn_kernel,
        grid=(nb, seq // blk),
        in_specs=[pl.BlockSpec(memory_space=pltpu.SMEM),
                  pl.BlockSpec((None, blk, ATTN_WIDTH), lambda b, i: (b, i, 0)),
                  pl.BlockSpec((None, seq, KV_WIDTH), lambda b, i: (b, 0, 0)),
                  pl.BlockSpec((None, seq, KV_WIDTH), lambda b, i: (b, 0, 0)),
                  pl.BlockSpec((None, nctx, KV_WIDTH), lambda b, i: (b, 0, 0)),
                  pl.BlockSpec((None, nctx, KV_WIDTH), lambda b, i: (b, 0, 0))],
        out_specs=pl.BlockSpec((None, blk, ATTN_WIDTH), lambda b, i: (b, i, 0)),
        out_shape=jax.ShapeDtypeStruct((nb, seq, ATTN_WIDTH), F32),
        compiler_params=_cparams(("arbitrary", "arbitrary")),
    )(sink, q, k, v, kc, vc)


def _chunk_scan(xr, xi, qre_ref, qim_ref, d, nsteps, seg, fwd):
    width = xr.shape[1]
    pos = lax.broadcasted_iota(jnp.int32, xr.shape, 1) % seg
    for j in range(nsteps):
        s = 1 << j
        if fwd:
            sr, si, ok = pltpu.roll(xr, s, 1), pltpu.roll(xi, s, 1), pos >= s
        else:
            sr, si, ok = pltpu.roll(xr, width - s, 1), pltpu.roll(xi, width - s, 1), pos < seg - s
        sr = jnp.where(ok, sr, 0.0)
        si = jnp.where(ok, si, 0.0)
        qr = qre_ref[d, j]
        qi = qim_ref[d, j]
        xr, xi = xr + qr * sr - qi * si, xi + qr * si + qi * sr
    return xr, xi


def _ssm_kernel(ul_ref, uc_ref, l1_ref, v_ref, qre_ref, qim_ref, dsk_ref, y_ref, *, nb):
    n = SSM_STATE
    ul = ul_ref[...]
    width = ul.shape[1]
    seg_l = width // nb
    seg_c = uc_ref.shape[1] // nb
    r = jnp.dot(l1_ref[...], ul.astype(MXU), preferred_element_type=F32)
    rc = jnp.dot(l1_ref[CHUNK_W:, :], uc_ref[...].astype(MXU), preferred_element_type=F32)
    lane = lax.broadcasted_iota(jnp.int32, (n, width), 1)
    hin = []
    for d, fwd in enumerate((True, False)):
        base = CHUNK_W + 2 * n * d
        xr, xi = r[base:base + n], r[base + n:base + 2 * n]
        cr, ci = rc[2 * n * d:2 * n * d + n], rc[2 * n * d + n:2 * n * d + 2 * n]
        cr, ci = _chunk_scan(cr, ci, qre_ref, qim_ref, d, int(math.log2(seg_c)), seg_c, fwd)
        injr = jnp.zeros((n, width), F32)
        inji = jnp.zeros((n, width), F32)
        for b in range(nb):
            src = b * seg_c + (seg_c - 1 if fwd else 0)
            dst = b * seg_l + (0 if fwd else seg_l - 1)
            injr = jnp.where(lane == dst, cr[:, src:src + 1], injr)
            inji = jnp.where(lane == dst, ci[:, src:src + 1], inji)
        q0r, q0i = qre_ref[d, 0], qim_ref[d, 0]
        xr, xi = xr + q0r * injr - q0i * inji, xi + q0r * inji + q0i * injr
        sr, si = _chunk_scan(xr, xi, qre_ref, qim_ref, d, int(math.log2(seg_l)), seg_l, fwd)
        edge = (lane % seg_l) == (0 if fwd else seg_l - 1)
        shift = 1 if fwd else width - 1
        hin.append(jnp.where(edge, injr, pltpu.roll(sr, shift, 1)))
        hin.append(jnp.where(edge, inji, pltpu.roll(si, shift, 1)))
    hcat = jnp.concatenate(hin, axis=0).astype(MXU)
    y = r[:CHUNK_W] + jnp.dot(v_ref[...], hcat, preferred_element_type=F32) + dsk_ref[...] * ul
    y_ref[...] = 0.5 * y * (1.0 + jnp.tanh(math.sqrt(2.0 / math.pi) * (y + 0.044715 * (y * y * y))))


def _ssm(ul, uc, l1, vcat, qre, qim, dsk, nb):
    g, _, width = ul.shape
    wc = uc.shape[2]
    n = SSM_STATE
    return pl.pallas_call(
        functools.partial(_ssm_kernel, nb=nb),
        grid=(g,),
        in_specs=[pl.BlockSpec((None, CHUNK_W, width), lambda i: (i, 0, 0)),
                  pl.BlockSpec((None, CHUNK_W, wc), lambda i: (i, 0, 0)),
                  pl.BlockSpec((None, 2 * CHUNK_W, CHUNK_W), lambda i: (i, 0, 0)),
                  pl.BlockSpec((None, CHUNK_W, CHUNK_W), lambda i: (i, 0, 0)),
                  pl.BlockSpec((None, 2, 7, n, 1), lambda i: (i, 0, 0, 0, 0)),
                  pl.BlockSpec((None, 2, 7, n, 1), lambda i: (i, 0, 0, 0, 0)),
                  pl.BlockSpec((None, CHUNK_W, 1), lambda i: (i, 0, 0))],
        out_specs=pl.BlockSpec((None, CHUNK_W, width), lambda i: (i, 0, 0)),
        out_shape=jax.ShapeDtypeStruct((g, CHUNK_W, width), F32),
        compiler_params=_cparams(("arbitrary",)),
    )(ul, uc, l1, vcat, qre, qim, dsk)


def _route(logits_t, bias):
    ng, ne = N_EXPERT_GROUPS, N_EXPERTS // N_EXPERT_GROUPS
    t = logits_t.shape[1]
    scores = _sigmoid(logits_t).reshape(ng, ne, t)
    biased = scores + bias.reshape(ng, ne, 1)
    iw = lax.broadcasted_iota(jnp.int32, (ng, ne, t), 1)
    ig = lax.broadcasted_iota(jnp.int32, (ng, ne, t), 0)
    neg = -jnp.inf
    m1 = jnp.max(biased, axis=1, keepdims=True)
    i1 = jnp.min(jnp.where(biased == m1, iw, ne), axis=1, keepdims=True)
    m2 = jnp.max(jnp.where(iw == i1, neg, biased), axis=1, keepdims=True)
    gscore = jnp.broadcast_to(m1 + m2, (ng, ne, t))
    gsel = jnp.zeros((ng, ne, t), F32)
    cur = gscore
    for _ in range(TOPK_GROUPS):
        m = jnp.max(cur, axis=0, keepdims=True)
        gi = jnp.min(jnp.where(cur == m, ig, ng), axis=0, keepdims=True)
        pick = ig == gi
        gsel = jnp.where(pick, 1.0, gsel)
        cur = jnp.where(pick, neg, cur)
    cur = jnp.where(gsel > 0.0, biased, neg)
    flat = ig * ne + iw
    chosen = jnp.zeros((ng, ne, t), F32)
    for _ in range(TOP_K):
        m = jnp.max(jnp.max(cur, axis=1, keepdims=True), axis=0, keepdims=True)
        fi = jnp.min(jnp.min(jnp.where(cur == m, flat, N_EXPERTS), axis=1, keepdims=True), axis=0, keepdims=True)
        pick = flat == fi
        chosen = jnp.where(pick, 1.0, chosen)
        cur = jnp.where(pick, neg, cur)
    sel = jnp.where(chosen > 0.0, scores, 0.0)
    tot = jnp.sum(jnp.sum(sel, axis=1, keepdims=True), axis=0, keepdims=True)
    return (sel / tot * ROUTED_SCALE).reshape(N_EXPERTS, t)


def _mix_kernel(*refs):
    nx, na = D_MODEL // LANES, ATTN_WIDTH // LANES
    yt_ref = refs[0]
    attn_s = refs[1:1 + na]
    x_s = refs[1 + na:1 + na + nx]
    (mod_ref, wglut_ref, bglu_ref, gssm_ref, wouts_ref, gattn_ref, wouta_ref, nffn_ref, wrt_ref,
     rbias_ref) = refs[1 + na + nx:11 + na + nx]
    x1_s = refs[11 + na + nx:11 + na + 2 * nx]
    gates_ref = refs[11 + na + 2 * nx]
    npos, nch = yt_ref.shape[1], yt_ref.shape[3]
    t0 = pl.program_id(1) * npos
    idx = [pl.ds(t0 + i, nch, stride=CHUNK) for i in range(npos)]
    y = jnp.concatenate([yt_ref[:, i].reshape(SSM_WIDTH, nch) for i in range(npos)], axis=1)
    glu = jnp.dot(wglut_ref[...], y.astype(MXU), preferred_element_type=F32) + bglu_ref[...]
    z = y * _sigmoid(glu)
    zn = z * lax.rsqrt(jnp.mean(z * z, axis=0, keepdims=True) + EPS) * gssm_ref[...]
    o_s = lax.dot_general(zn.astype(MXU), wouts_ref[...], (((0,), (0,)), ((), ())),
                          preferred_element_type=F32)
    a = jnp.concatenate([_rows(attn_s, ix) for ix in idx], axis=0)
    an = a * lax.rsqrt(jnp.mean(a * a, axis=-1, keepdims=True) + EPS) * gattn_ref[...]
    o_a = jnp.dot(an.astype(MXU), wouta_ref[...], preferred_element_type=F32)
    g1 = mod_ref[2:3, :]
    x1 = jnp.concatenate([_rows(x_s, ix) for ix in idx], axis=0) + g1 * (o_s + o_a)
    h2 = _ffn_input(x1, mod_ref, nffn_ref)
    logits_t = lax.dot_general(wrt_ref[...], h2, (((1,), (1,)), ((), ())),
                               precision=HIGHEST, preferred_element_type=F32)
    gates_t = _route(logits_t, rbias_ref[...])
    zeros = jnp.zeros((LANES - N_EXPERTS, nch), F32)
    for i, ix in enumerate(idx):
        for j, ref in enumerate(x1_s):
            ref[ix, :] = x1[i * nch:(i + 1) * nch, j * LANES:(j + 1) * LANES]
        gates_ref[ix, :] = jnp.concatenate([gates_t[:, i * nch:(i + 1) * nch], zeros], axis=0).T


def _ffn_input(x1, mod_ref, nffn_ref):
    h2 = x1 * lax.rsqrt(jnp.mean(x1 * x1, axis=-1, keepdims=True) + EPS) * nffn_ref[...]
    return h2 * (1.0 + mod_ref[4:5, :]) + mod_ref[3:4, :]


def _mix(yt, attn, x, mod3, wglut, bglu, gssm, wouts, gattn, wouta, nffn, wrt, rbias):
    nb, seq, d = x.shape
    nch = seq // CHUNK
    yt4 = yt.reshape(SSM_GROUPS, CHUNK, SSM_GROUP, nb * nch)
    c2 = lambda b, t: (0, 0)
    batch, zero = (lambda b, t: b), (lambda b, t: 0)
    slab = pl.BlockSpec((None, seq, LANES), lambda b, t: (b, 0, 0))
    nx = d // LANES
    outs = pl.pallas_call(
        _mix_kernel,
        grid=(nb, CHUNK // MIX_POSITIONS),
        in_specs=[pl.BlockSpec((SSM_GROUPS, MIX_POSITIONS, SSM_GROUP, nch), lambda b, t: (0, t, 0, b))]
        + _slab_specs(seq, ATTN_WIDTH, batch, zero) + _slab_specs(seq, d, batch, zero)
        + [pl.BlockSpec((None, N_MOD, d), lambda b, t: (b, 0, 0)),
           pl.BlockSpec(wglut.shape, c2), pl.BlockSpec(bglu.shape, c2), pl.BlockSpec(gssm.shape, c2),
           pl.BlockSpec(wouts.shape, c2), pl.BlockSpec(gattn.shape, c2), pl.BlockSpec(wouta.shape, c2),
           pl.BlockSpec(nffn.shape, c2), pl.BlockSpec(wrt.shape, c2), pl.BlockSpec(rbias.shape, c2)],
        out_specs=[slab] * (nx + 1),
        out_shape=[jax.ShapeDtypeStruct((nb, seq, LANES), F32)] * (nx + 1),
        compiler_params=_cparams(("arbitrary", "arbitrary")),
    )(yt4, *([attn] * (ATTN_WIDTH // LANES)), *([x] * nx), mod3, wglut, bglu, gssm, wouts, gattn, wouta,
      nffn, wrt, rbias)
    return outs[:nx], outs[nx]


def _moe_kernel(*refs):
    nx = D_MODEL // LANES
    x1_s = refs[:nx]
    (gates_ref, mod_ref, nffn_ref, wcat_ref, wd_ref, wscat_ref, wsd_ref, nfin_ref,
     o_ref, acc_ref, hid_ref, h2_ref) = refs[nx:]
    s = pl.program_id(1)
    f = EXPERT_DIM

    def glu(gu):
        g = gu[:, :f]
        return g * _sigmoid(g) * gu[:, f:]

    @pl.when(s == 0)
    def _():
        h2_ref[...] = _ffn_input(_rows(x1_s, slice(None)), mod_ref, nffn_ref).astype(MXU)
        hs = glu(jnp.dot(h2_ref[...], wscat_ref[...], preferred_element_type=F32))
        acc_ref[...] = jnp.dot(hs.astype(MXU), wsd_ref[...], preferred_element_type=F32)

    h2 = h2_ref[...]
    gsh = pltpu.roll(gates_ref[...], (LANES - EXPERTS_PER_STEP * s) % LANES, 1)
    for e in range(EXPERTS_PER_STEP):
        hid = glu(jnp.dot(h2, wcat_ref[e], preferred_element_type=F32)) * gsh[:, e:e + 1]
        hid_ref[:, e * f:(e + 1) * f] = hid.astype(MXU)
    acc_ref[...] += jnp.dot(hid_ref[...], wd_ref[...], preferred_element_type=F32)

    @pl.when(s == pl.num_programs(1) - 1)
    def _():
        g2 = mod_ref[5:6, :]
        x2 = _rows(x1_s, slice(None)) + g2 * acc_ref[...]
        o_ref[...] = x2 * lax.rsqrt(jnp.mean(x2 * x2, axis=-1, keepdims=True) + EPS) * nfin_ref[...]


def _moe(x1_slabs, gates, mod3, nffn, wcat, wd, wscat, wsd, nfin):
    nb, seq, _ = gates.shape
    d = D_MODEL
    tiles_per_b = seq // MOE_TILE
    nsteps = N_EXPERTS // EXPERTS_PER_STEP
    c2 = lambda i, s: (0, 0)
    batch, rowblock = (lambda i, s: i // tiles_per_b), (lambda i, s: i % tiles_per_b)
    return pl.pallas_call(
        _moe_kernel,
        grid=(nb * tiles_per_b, nsteps),
        in_specs=[pl.BlockSpec((None, MOE_TILE, LANES), lambda i, s: (batch(i, s), rowblock(i, s), 0))] * (d // LANES + 1)
        + [pl.BlockSpec((None, N_MOD, d), lambda i, s: (batch(i, s), 0, 0)),
           pl.BlockSpec(nffn.shape, c2),
           pl.BlockSpec((EXPERTS_PER_STEP, d, 2 * EXPERT_DIM), lambda i, s: (s, 0, 0)),
           pl.BlockSpec((EXPERTS_PER_STEP * EXPERT_DIM, d), lambda i, s: (s, 0)),
           pl.BlockSpec(wscat.shape, c2), pl.BlockSpec(wsd.shape, c2), pl.BlockSpec(nfin.shape, c2)],
        out_specs=pl.BlockSpec((None, MOE_TILE, d), lambda i, s: (batch(i, s), rowblock(i, s), 0)),
        out_shape=jax.ShapeDtypeStruct((nb, seq, d), F32),
        scratch_shapes=[pltpu.VMEM((MOE_TILE, d), F32),
                        pltpu.VMEM((MOE_TILE, EXPERTS_PER_STEP * EXPERT_DIM), MXU),
                        pltpu.VMEM((MOE_TILE, d), MXU)],
        compiler_params=_cparams(("arbitrary", "arbitrary")),
    )(*x1_slabs, gates, mod3, nffn, wcat, wd, wscat, wsd, nfin)


def _rope_tables(seq):
    pos = jnp.arange(seq)
    row = (pos // GRID_W).astype(F32)
    col = (pos % GRID_W).astype(F32)
    inv = ROPE_THETA ** (-jnp.arange(ROT_PAIRS, dtype=F32) / ROT_PAIRS)
    ar, ac = row[:, None] * inv, col[:, None] * inv
    zero = jnp.zeros_like(ar)
    rep = LANES // HEAD_DIM
    cos_t = jnp.tile(jnp.concatenate([jnp.cos(ar), jnp.cos(ar), jnp.cos(ac), jnp.cos(ac)], axis=1), (1, rep))
    s1_t = jnp.tile(jnp.concatenate([-jnp.sin(ar), zero, -jnp.sin(ac), zero], axis=1), (1, rep))
    s2_t = jnp.tile(jnp.concatenate([zero, jnp.sin(ar), zero, jnp.sin(ac)], axis=1), (1, rep))
    return cos_t, s1_t, s2_t


def kernel(x, c, ctx, c_ctx, w_ada, b_ada, norm_mix, norm_ffn, w_in, attn_sink, ssm_a_re, ssm_a_im, ssm_log_dt, ssm_b_re, ssm_b_im, ssm_c_re, ssm_c_im, ssm_d, w_glu, b_glu, norm_attn_out, norm_ssm_out, w_out, w_router, router_bias, w_gate_e, w_up_e, w_down_e, w_gate_s, w_up_s, w_down_s, norm_final):
    nb, seq, d = x.shape
    nctx = ctx.shape[1]
    layer = 0

    pad = jnp.zeros((16 - nb - 1, d), F32)
    c_all = jnp.concatenate([c, c_ctx[None, :], pad], axis=0)
    mod3 = _ada(c_all, w_ada[layer], b_ada[layer]).reshape(16, N_MOD, d)

    gq = ATTN_HEADS // ATTN_KV_HEADS
    heads = jnp.arange(ATTN_HEADS).reshape(ATTN_KV_HEADS, gq).T.reshape(-1)
    perm_q = (heads[:, None] * HEAD_DIM + jnp.arange(HEAD_DIM)[None, :]).reshape(-1)
    w_in0 = w_in[layer]
    wqkv = jnp.concatenate([w_in0[:, :ATTN_WIDTH][:, perm_q], w_in0[:, ATTN_WIDTH:ATTN_WIDTH + 2 * KV_WIDTH]],
                           axis=1).astype(MXU)
    wut = w_in0[:, ATTN_WIDTH + 2 * KV_WIDTH:].T.astype(MXU)
    nw = norm_mix[layer].reshape(1, d)
    cos_t, s1_t, s2_t = _rope_tables(seq)

    q, k, v, ul = _proj(x, mod3, 0, nw, wqkv, wut, cos_t, s1_t, s2_t, True)
    _, kc, vc, uc = _proj(ctx.reshape(1, nb * nctx, d), mod3, nb, nw, wqkv, wut, cos_t, s1_t, s2_t, False)
    kc = kc.reshape(nb, nctx, KV_WIDTH)
    vc = vc.reshape(nb, nctx, KV_WIDTH)

    attn = _attention(attn_sink[layer], q, k, v, kc, vc)

    kcomb, pb, cp, qpow = _ssm_prep(ssm_a_re[layer], ssm_a_im[layer], ssm_log_dt[layer], ssm_b_re[layer],
                                    ssm_b_im[layer], ssm_c_re[layer], ssm_c_im[layer])
    l1, vcat, qre, qim, dsk = _ssm_operators(kcomb, pb, cp, qpow, ssm_d[layer])
    yt = _ssm(ul, uc, l1, vcat, qre, qim, dsk, nb)

    w_out0 = w_out[layer]
    nffn = norm_ffn[layer].reshape(1, d)
    x1_slabs, gates = _mix(
        yt, attn, x, mod3,
        w_glu[layer].T.astype(MXU), b_glu[layer].reshape(SSM_WIDTH, 1), norm_ssm_out[layer].reshape(SSM_WIDTH, 1),
        w_out0[ATTN_WIDTH:].astype(MXU), norm_attn_out[layer][perm_q].reshape(1, ATTN_WIDTH),
        w_out0[:ATTN_WIDTH][perm_q].astype(MXU), nffn,
        w_router[layer].T, router_bias[layer].reshape(N_EXPERTS, 1))

    wcat = jnp.concatenate([w_gate_e[layer], w_up_e[layer]], axis=-1).astype(MXU)
    wd = w_down_e[layer].astype(MXU).reshape(N_EXPERTS * EXPERT_DIM, d)
    wscat = jnp.concatenate([w_gate_s[layer], w_up_s[layer]], axis=-1).astype(MXU)
    wsd = w_down_s[layer].astype(MXU)
    return _moe(x1_slabs, gates, mod3, nffn, wcat, wd, wscat, wsd, norm_final.reshape(1, d))
```

```python
import functools
import math

import jax
import jax.numpy as jnp
from jax import lax
from jax.experimental import pallas as pl
from jax.experimental.pallas import tpu as pltpu

D_MODEL = 1024
EPS = 1e-6
N_MOD = 6
HEAD_DIM = 64
ATTN_HEADS = 8
ATTN_KV_HEADS = 2
ATTN_WIDTH = ATTN_HEADS * HEAD_DIM
KV_WIDTH = ATTN_KV_HEADS * HEAD_DIM
WINDOW = 128
ATTN_SCALE = HEAD_DIM ** -0.5
ROPE_THETA = 10000.0
ROT_PAIRS = HEAD_DIM // 4
GRID_W = 64
SSM_WIDTH = D_MODEL - ATTN_WIDTH
SSM_GROUP = 16
SSM_GROUPS = SSM_WIDTH // SSM_GROUP
SSM_STATE = 64
N_EXPERTS = 64
EXPERT_DIM = 128
TOP_K = 8
N_EXPERT_GROUPS = 8
TOPK_GROUPS = 4
ROUTED_SCALE = 2.5

CHUNK = 16
CHUNK_W = CHUNK * SSM_GROUP
LANES = 128
ATTN_QBLOCKS = 2
MIX_TOKENS = 1024
MIX_ROWS = 512
MOE_TILE = 1024
EXPERTS_PER_STEP = 8
VMEM_LIMIT = 56 * 1024 * 1024

MXU = jnp.bfloat16
F32 = jnp.float32
HIGHEST = lax.Precision.HIGHEST


def _sigmoid(x):
    return 1.0 / (1.0 + jnp.exp(-x))


def _cparams(sem):
    return pltpu.CompilerParams(dimension_semantics=sem, vmem_limit_bytes=VMEM_LIMIT)


def _slab_specs(rows, width, batch_of, rowblock_of):
    return [pl.BlockSpec((None, rows, LANES), lambda *g, j=j: (batch_of(*g), rowblock_of(*g), j))
            for j in range(width // LANES)]


def _ada_kernel(c_ref, w_ref, b_ref, o_ref):
    cv = c_ref[...]
    s = cv * _sigmoid(cv)
    o_ref[...] = jnp.dot(s, w_ref[...], precision=HIGHEST, preferred_element_type=F32) + b_ref[...]


def _ada(c_all, w, b):
    rows, d = c_all.shape
    n = w.shape[1]
    tn = 1024
    return pl.pallas_call(
        _ada_kernel,
        grid=(n // tn,),
        in_specs=[pl.BlockSpec((rows, d), lambda j: (0, 0)),
                  pl.BlockSpec((d, tn), lambda j: (0, j)),
                  pl.BlockSpec((1, tn), lambda j: (0, j))],
        out_specs=pl.BlockSpec((rows, tn), lambda j: (0, j)),
        out_shape=jax.ShapeDtypeStruct((rows, n), F32),
        compiler_params=_cparams(("arbitrary",)),
    )(c_all, w, b.reshape(1, n))


def _ssm_prep_kernel(are_ref, aim_ref, ldt_ref, btr_ref, bti_ref, cr_ref, ci_ref,
                     kc_ref, pb_ref, cp_ref, q_ref):
    ar, ai = are_ref[...], aim_ref[...]
    dt = jnp.exp(ldt_ref[...])
    mag = jnp.exp(dt * ar)
    abr = mag * jnp.cos(dt * ai)
    abi = mag * jnp.sin(dt * ai)
    den = ar * ar + ai * ai
    nr = abr - 1.0
    cor = (nr * ar + abi * ai) / den
    coi = (abi * ar - nr * ai) / den
    btr, bti = btr_ref[...], bti_ref[...]
    bbr = cor[:, None, :] * btr - coi[:, None, :] * bti
    bbi = cor[:, None, :] * bti + coi[:, None, :] * btr
    cr, ci = cr_ref[...], ci_ref[...]
    pr, pi = jnp.ones_like(abr), jnp.zeros_like(abr)
    cps = []
    for k in range(CHUNK + 1):
        cpr = cr * pr[:, None, :] - ci * pi[:, None, :]
        cpi = cr * pi[:, None, :] + ci * pr[:, None, :]
        cpk = jnp.concatenate([cpr, -cpi], axis=-1)
        cps.append(cpk)
        cp_ref[:, k] = cpk
        if k < CHUNK:
            pb_ref[:, k] = jnp.concatenate([bbr * pr[:, None, :] - bbi * pi[:, None, :],
                                            bbr * pi[:, None, :] + bbi * pr[:, None, :]], axis=-1)
        if k < CHUNK:
            pr, pi = pr * abr - pi * abi, pr * abi + pi * abr
    qr, qi = pr, pi
    bbcat = jnp.concatenate([bbr, bbi], axis=-1)
    lagk = []
    for d in range(2):
        stack = jnp.concatenate([cps[k][d] for k in range(CHUNK)], axis=0)
        lagk.append(lax.dot_general(stack, bbcat[d], (((1,), (1,)), ((), ())),
                                    precision=HIGHEST, preferred_element_type=F32))
    kf, kb = lagk
    p = SSM_GROUP
    blocks = [kb[(CHUNK - 1 - j) * p:(CHUNK - j) * p] for j in range(CHUNK - 1)]
    blocks.append(kf[:p] + kb[:p])
    blocks += [kf[k * p:(k + 1) * p] for k in range(1, CHUNK)]
    kc_ref[...] = jnp.concatenate(blocks, axis=0)
    for j in range(8):
        q_ref[:, j] = jnp.concatenate([qr, qi], axis=-1)
        qr, qi = qr * qr - qi * qi, 2.0 * qr * qi


def _ssm_prep(a_re, a_im, log_dt, b_re, b_im, c_re, c_im):
    g, n, p = SSM_GROUPS, SSM_STATE, SSM_GROUP
    are = jnp.transpose(a_re, (1, 0, 2))
    aim = jnp.transpose(a_im, (1, 0, 2))
    ldt = jnp.transpose(log_dt, (1, 0))[..., None]
    btr = jnp.transpose(b_re, (1, 0, 3, 2))
    bti = jnp.transpose(b_im, (1, 0, 3, 2))
    cr = jnp.transpose(c_re, (1, 0, 2, 3))
    ci = jnp.transpose(c_im, (1, 0, 2, 3))
    nlag = 2 * CHUNK - 1
    vec = pl.BlockSpec((None, 2, n), lambda i: (i, 0, 0))
    mat = pl.BlockSpec((None, 2, p, n), lambda i: (i, 0, 0, 0))
    return pl.pallas_call(
        _ssm_prep_kernel,
        grid=(g,),
        in_specs=[vec, vec, pl.BlockSpec((None, 2, 1), lambda i: (i, 0, 0)), mat, mat, mat, mat],
        out_specs=[pl.BlockSpec((None, nlag * p, p), lambda i: (i, 0, 0)),
                   pl.BlockSpec((None, 2, CHUNK, p, 2 * n), lambda i: (i, 0, 0, 0, 0)),
                   pl.BlockSpec((None, 2, CHUNK + 1, p, 2 * n), lambda i: (i, 0, 0, 0, 0)),
                   pl.BlockSpec((None, 2, 8, 2 * n), lambda i: (i, 0, 0, 0))],
        out_shape=[jax.ShapeDtypeStruct((g, nlag * p, p), F32),
                   jax.ShapeDtypeStruct((g, 2, CHUNK, p, 2 * n), F32),
                   jax.ShapeDtypeStruct((g, 2, CHUNK + 1, p, 2 * n), F32),
                   jax.ShapeDtypeStruct((g, 2, 8, 2 * n), F32)],
        compiler_params=_cparams(("arbitrary",)),
    )(are, aim, ldt, btr, bti, cr, ci)


def _ssm_operators(kc, pb, cp, q, d_skip):
    g, p, n = SSM_GROUPS, SSM_GROUP, SSM_STATE
    t = jnp.arange(CHUNK)
    lag = t[:, None] - t[None, :] + CHUNK - 1
    kc5 = kc.reshape(g, 2 * CHUNK - 1, p, p)[:, lag]
    toep = jnp.transpose(kc5, (0, 1, 3, 2, 4)).reshape(g, CHUNK_W, CHUNK_W)
    wst_f = jnp.transpose(pb[:, 0, ::-1].reshape(g, CHUNK_W, 2 * n), (0, 2, 1))
    wst_b = jnp.transpose(pb[:, 1].reshape(g, CHUNK_W, 2 * n), (0, 2, 1))
    l1 = jnp.concatenate([toep, wst_f, wst_b], axis=1).astype(MXU)
    v_f = cp[:, 0, 1:].reshape(g, CHUNK_W, 2 * n)
    v_b = cp[:, 1, 1:][:, ::-1].reshape(g, CHUNK_W, 2 * n)
    vcat = jnp.concatenate([v_f, v_b], axis=2).astype(MXU)
    qre = q[:, :, :7, :n, None]
    qim = q[:, :, :7, n:, None]
    dsk = jnp.tile(d_skip.reshape(g, 1, p), (1, CHUNK, 1)).reshape(g, CHUNK_W, 1)
    return l1, vcat, qre, qim, dsk


def _rows(slabs, idx):
    return jnp.concatenate([s[idx, :] for s in slabs], axis=1)


def _proj_kernel(*refs, rope):
    nslab = D_MODEL // LANES
    xs = refs[:nslab]
    mod_ref, nw_ref, wqkv_ref, wut_ref, cos_ref, s1_ref, s2_ref, q_ref, k_ref, vt_ref, ut_ref = refs[nslab:]
    ntok = xs[0].shape[0]
    nch = ntok // CHUNK
    sh = mod_ref[0:1, :]
    sc = mod_ref[1:2, :]
    nw = nw_ref[...]

    def normed(xb):
        ms = jnp.mean(xb * xb, axis=-1, keepdims=True)
        hn = xb * lax.rsqrt(ms + EPS) * nw
        return (hn * (1.0 + sc) + sh).astype(MXU)

    rb = 256

    def rot(tile, r0):
        if not rope:
            return tile
        cs = cos_ref[pl.ds(r0, rb), :]
        a1 = s1_ref[pl.ds(r0, rb), :]
        a2 = s2_ref[pl.ds(r0, rb), :]
        return tile * cs + pltpu.roll(tile, LANES - ROT_PAIRS, 1) * a1 + pltpu.roll(tile, ROT_PAIRS, 1) * a2

    def body(r, carry):
        r0 = pl.multiple_of(r * rb, rb)
        hb = normed(_rows(xs, pl.ds(r0, rb)))
        qkv = jnp.dot(hb, wqkv_ref[...], preferred_element_type=F32)
        for j in range(ATTN_WIDTH // LANES):
            qj = rot(qkv[:, j * LANES:(j + 1) * LANES], r0) * ATTN_SCALE
            q_ref[pl.ds(r0, rb), j * LANES:(j + 1) * LANES] = qj.astype(MXU)
        k_ref[pl.ds(r0, rb), :] = rot(qkv[:, ATTN_WIDTH:ATTN_WIDTH + KV_WIDTH], r0).astype(MXU)
        vt_ref[:, pl.ds(r0, rb)] = qkv[:, ATTN_WIDTH + KV_WIDTH:].T.astype(MXU)
        return carry

    lax.fori_loop(0, ntok // rb, body, 0)

    tg = 4
    for t0 in range(0, CHUNK, tg):
        hcat = jnp.concatenate(
            [normed(_rows(xs, pl.ds(t, nch, stride=CHUNK))) for t in range(t0, t0 + tg)], axis=0)
        pt = lax.dot_general(wut_ref[...], hcat, (((1,), (1,)), ((), ())), preferred_element_type=F32)
        for i, t in enumerate(range(t0, t0 + tg)):
            ut_ref[:, t * SSM_GROUP:(t + 1) * SSM_GROUP, :] = (
                pt[:, i * nch:(i + 1) * nch].reshape(SSM_GROUPS, SSM_GROUP, nch))


def _proj(x, mod3, mod_row0, nw, wqkv, wut, cos_t, s1_t, s2_t, rope):
    nb, ntok, d = x.shape
    nch = ntok // CHUNK
    const2 = lambda b: (0, 0)
    tok = lambda width: pl.BlockSpec((None, ntok, width), lambda b: (b, 0, 0))
    tables = (cos_t, s1_t, s2_t)
    return pl.pallas_call(
        functools.partial(_proj_kernel, rope=rope),
        grid=(nb,),
        in_specs=_slab_specs(ntok, d, lambda b: b, lambda b: 0)
        + [pl.BlockSpec((None, N_MOD, d), lambda b: (b + mod_row0, 0, 0)),
           pl.BlockSpec((1, d), const2),
           pl.BlockSpec(wqkv.shape, const2),
           pl.BlockSpec(wut.shape, const2)] + [pl.BlockSpec(t.shape, const2) for t in tables],
        out_specs=[tok(ATTN_WIDTH), tok(KV_WIDTH), pl.BlockSpec((None, KV_WIDTH, ntok), lambda b: (b, 0, 0)),
                   pl.BlockSpec((SSM_GROUPS, CHUNK_W, nch), lambda b: (0, 0, b))],
        out_shape=[jax.ShapeDtypeStruct((nb, ntok, ATTN_WIDTH), MXU),
                   jax.ShapeDtypeStruct((nb, ntok, KV_WIDTH), MXU),
                   jax.ShapeDtypeStruct((nb, KV_WIDTH, ntok), MXU),
                   jax.ShapeDtypeStruct((SSM_GROUPS, CHUNK_W, nb * nch), F32)],
        compiler_params=_cparams(("arbitrary",)),
    )(*([x] * (d // LANES)), mod3, nw, wqkv, wut, *tables)


def _attn_kernel(sink_ref, q_ref, k_ref, vt_ref, kc_ref, vct_ref, o_ref):
    for sub in range(q_ref.shape[0] // WINDOW):
        _attn_block(sink_ref, q_ref, k_ref, vt_ref, kc_ref, vct_ref, o_ref, sub)


def _attn_block(sink_ref, q_ref, k_ref, vt_ref, kc_ref, vct_ref, o_ref, sub):
    blk = WINDOW
    seq = k_ref.shape[0]
    nwin = 3 * blk
    gq = ATTN_HEADS // ATTN_KV_HEADS
    i = pl.program_id(1) * (q_ref.shape[0] // blk) + sub
    start = pl.multiple_of(jnp.clip((i - 1) * blk, 0, seq - nwin), blk)
    kw = k_ref[pl.ds(start, nwin), :]
    vtw = vt_ref[:, pl.ds(start, nwin)]
    kc = kc_ref[...]
    vct = vct_ref[...]
    q = q_ref[sub * blk:(sub + 1) * blk, :]
    cols = gq * blk
    kpos = start + lax.broadcasted_iota(jnp.int32, (nwin, cols), 0)
    qpos = i * blk + lax.broadcasted_iota(jnp.int32, (nwin, cols), 1) % blk
    band = jnp.abs(qpos - kpos) <= WINDOW
    lane_q = lax.broadcasted_iota(jnp.int32, (blk, LANES), 1)
    nt = (((1,), (1,)), ((), ()))
    outs = []
    for kh in range(ATTN_KV_HEADS):
        own = (lane_q // HEAD_DIM) == kh
        qs = jnp.concatenate(
            [jnp.where(own, q[:, j * LANES:(j + 1) * LANES], jnp.zeros((), MXU)) for j in range(gq)], axis=0)
        sink = jnp.concatenate(
            [jnp.full((1, blk), sink_ref[kh * gq + j], F32) for j in range(gq)], axis=1)
        s_loc = lax.dot_general(kw, qs, nt, preferred_element_type=F32)
        s_ctx = lax.dot_general(kc, qs, nt, preferred_element_type=F32)
        s_loc = jnp.where(band, s_loc, -jnp.inf)
        m = jnp.maximum(jnp.maximum(jnp.max(s_loc, axis=0, keepdims=True),
                                    jnp.max(s_ctx, axis=0, keepdims=True)), sink)
        p_loc = jnp.exp(s_loc - m)
        p_ctx = jnp.exp(s_ctx - m)
        den = (jnp.sum(p_loc, axis=0, keepdims=True) + jnp.sum(p_ctx, axis=0, keepdims=True)
               + jnp.exp(sink - m))
        ot = (jnp.dot(vtw, p_loc.astype(MXU), preferred_element_type=F32)
              + jnp.dot(vct, p_ctx.astype(MXU), preferred_element_type=F32)) / den
        outs.append(ot)
    top = lax.broadcasted_iota(jnp.int32, (KV_WIDTH, blk), 0) < HEAD_DIM
    for j in range(gq):
        both = jnp.where(top, outs[0][:, j * blk:(j + 1) * blk], outs[1][:, j * blk:(j + 1) * blk])
        o_ref[sub * blk:(sub + 1) * blk, j * LANES:(j + 1) * LANES] = both.T


def _attention(sink, q, k, vt, kc, vct):
    nb, seq, _ = q.shape
    nctx = kc.shape[1]
    blk = ATTN_QBLOCKS * WINDOW
    return pl.pallas_call(
        _attn_kernel,
        grid=(nb, seq // blk),
        in_specs=[pl.BlockSpec(memory_space=pltpu.SMEM),
                  pl.BlockSpec((None, blk, ATTN_WIDTH), lambda b, i: (b, i, 0)),
                  pl.BlockSpec((None, seq, KV_WIDTH), lambda b, i: (b, 0, 0)),
                  pl.BlockSpec((None, KV_WIDTH, seq), lambda b, i: (b, 0, 0)),
                  pl.BlockSpec((None, nctx, KV_WIDTH), lambda b, i: (b, 0, 0)),
                  pl.BlockSpec((None, KV_WIDTH, nctx), lambda b, i: (0, 0, b))],
        out_specs=pl.BlockSpec((None, blk, ATTN_WIDTH), lambda b, i: (b, i, 0)),
        out_shape=jax.ShapeDtypeStruct((nb, seq, ATTN_WIDTH), F32),
        compiler_params=_cparams(("arbitrary", "arbitrary")),
    )(sink, q, k, vt, kc, vct)


def _chunk_scan(xr, xi, qre_ref, qim_ref, d, nsteps, seg, fwd):
    width = xr.shape[1]
    pos = lax.broadcasted_iota(jnp.int32, xr.shape, 1) % seg
    for j in range(nsteps):
        s = 1 << j
        if fwd:
            sr, si, ok = pltpu.roll(xr, s, 1), pltpu.roll(xi, s, 1), pos >= s
        else:
            sr, si, ok = pltpu.roll(xr, width - s, 1), pltpu.roll(xi, width - s, 1), pos < seg - s
        sr = jnp.where(ok, sr, 0.0)
        si = jnp.where(ok, si, 0.0)
        qr = qre_ref[d, j]
        qi = qim_ref[d, j]
        xr, xi = xr + qr * sr - qi * si, xi + qr * si + qi * sr
    return xr, xi


def _ssm_kernel(ul_ref, uc_ref, l1_ref, v_ref, qre_ref, qim_ref, dsk_ref, y_ref, *, nb):
    n = SSM_STATE
    ul = ul_ref[...]
    width = ul.shape[1]
    seg_l = width // nb
    seg_c = uc_ref.shape[1] // nb
    r = jnp.dot(l1_ref[...], ul.astype(MXU), preferred_element_type=F32)
    rc = jnp.dot(l1_ref[CHUNK_W:, :], uc_ref[...].astype(MXU), preferred_element_type=F32)
    lane = lax.broadcasted_iota(jnp.int32, (n, width), 1)
    hin = []
    for d, fwd in enumerate((True, False)):
        base = CHUNK_W + 2 * n * d
        xr, xi = r[base:base + n], r[base + n:base + 2 * n]
        cr, ci = rc[2 * n * d:2 * n * d + n], rc[2 * n * d + n:2 * n * d + 2 * n]
        cr, ci = _chunk_scan(cr, ci, qre_ref, qim_ref, d, int(math.log2(seg_c)), seg_c, fwd)
        injr = jnp.zeros((n, width), F32)
        inji = jnp.zeros((n, width), F32)
        for b in range(nb):
            src = b * seg_c + (seg_c - 1 if fwd else 0)
            dst = b * seg_l + (0 if fwd else seg_l - 1)
            injr = jnp.where(lane == dst, cr[:, src:src + 1], injr)
            inji = jnp.where(lane == dst, ci[:, src:src + 1], inji)
        q0r, q0i = qre_ref[d, 0], qim_ref[d, 0]
        xr, xi = xr + q0r * injr - q0i * inji, xi + q0r * inji + q0i * injr
        sr, si = _chunk_scan(xr, xi, qre_ref, qim_ref, d, int(math.log2(seg_l)), seg_l, fwd)
        edge = (lane % seg_l) == (0 if fwd else seg_l - 1)
        shift = 1 if fwd else width - 1
        hin.append(jnp.where(edge, injr, pltpu.roll(sr, shift, 1)))
        hin.append(jnp.where(edge, inji, pltpu.roll(si, shift, 1)))
    hcat = jnp.concatenate(hin, axis=0).astype(MXU)
    y = r[:CHUNK_W] + jnp.dot(v_ref[...], hcat, preferred_element_type=F32) + dsk_ref[...] * ul
    y_ref[...] = 0.5 * y * (1.0 + jnp.tanh(math.sqrt(2.0 / math.pi) * (y + 0.044715 * (y * y * y))))


def _ssm(ul, uc, l1, vcat, qre, qim, dsk, nb):
    g, _, width = ul.shape
    wc = uc.shape[2]
    n = SSM_STATE
    return pl.pallas_call(
        functools.partial(_ssm_kernel, nb=nb),
        grid=(g,),
        in_specs=[pl.BlockSpec((None, CHUNK_W, width), lambda i: (i, 0, 0)),
                  pl.BlockSpec((None, CHUNK_W, wc), lambda i: (i, 0, 0)),
                  pl.BlockSpec((None, 2 * CHUNK_W, CHUNK_W), lambda i: (i, 0, 0)),
                  pl.BlockSpec((None, CHUNK_W, CHUNK_W), lambda i: (i, 0, 0)),
                  pl.BlockSpec((None, 2, 7, n, 1), lambda i: (i, 0, 0, 0, 0)),
                  pl.BlockSpec((None, 2, 7, n, 1), lambda i: (i, 0, 0, 0, 0)),
                  pl.BlockSpec((None, CHUNK_W, 1), lambda i: (i, 0, 0))],
        out_specs=pl.BlockSpec((None, CHUNK_W, width), lambda i: (i, 0, 0)),
        out_shape=jax.ShapeDtypeStruct((g, CHUNK_W, width), F32),
        compiler_params=_cparams(("arbitrary",)),
    )(ul, uc, l1, vcat, qre, qim, dsk)


def _route(logits_t, bias):
    ng, ne = N_EXPERT_GROUPS, N_EXPERTS // N_EXPERT_GROUPS
    t = logits_t.shape[1]
    scores = _sigmoid(logits_t).reshape(ng, ne, t)
    biased = scores + bias.reshape(ng, ne, 1)
    iw = lax.broadcasted_iota(jnp.int32, (ng, ne, t), 1)
    ig = lax.broadcasted_iota(jnp.int32, (ng, ne, t), 0)
    neg = -jnp.inf
    m1 = jnp.max(biased, axis=1, keepdims=True)
    i1 = jnp.min(jnp.where(biased == m1, iw, ne), axis=1, keepdims=True)
    m2 = jnp.max(jnp.where(iw == i1, neg, biased), axis=1, keepdims=True)
    gscore = jnp.broadcast_to(m1 + m2, (ng, ne, t))
    gsel = jnp.zeros((ng, ne, t), F32)
    cur = gscore
    for _ in range(TOPK_GROUPS):
        m = jnp.max(cur, axis=0, keepdims=True)
        gi = jnp.min(jnp.where(cur == m, ig, ng), axis=0, keepdims=True)
        pick = ig == gi
        gsel = jnp.where(pick, 1.0, gsel)
        cur = jnp.where(pick, neg, cur)
    cur = jnp.where(gsel > 0.0, biased, neg)
    flat = ig * ne + iw
    chosen = jnp.zeros((ng, ne, t), F32)
    for _ in range(TOP_K):
        m = jnp.max(jnp.max(cur, axis=1, keepdims=True), axis=0, keepdims=True)
        fi = jnp.min(jnp.min(jnp.where(cur == m, flat, N_EXPERTS), axis=1, keepdims=True), axis=0, keepdims=True)
        pick = flat == fi
        chosen = jnp.where(pick, 1.0, chosen)
        cur = jnp.where(pick, neg, cur)
    sel = jnp.where(chosen > 0.0, scores, 0.0)
    tot = jnp.sum(jnp.sum(sel, axis=1, keepdims=True), axis=0, keepdims=True)
    return (sel / tot * ROUTED_SCALE).reshape(N_EXPERTS, t)


def _mix_kernel(yt_ref, attn_ref, x_ref, mod_ref, wglu_ref, bglu_ref, gssm_ref, wouts_ref, gattn_ref, wouta_ref,
                nffn_ref, wrt_ref, rbias_ref, x1_ref, gates_ref, *ynat):
    nch = yt_ref.shape[3]
    part = pl.program_id(1)

    @pl.when(part == 0)
    def _():
        for t in range(CHUNK):
            y_t = yt_ref[:, t].reshape(SSM_WIDTH, nch).T
            for j, ref in enumerate(ynat):
                ref[pl.ds(t, nch, stride=CHUNK), :] = y_t[:, j * LANES:(j + 1) * LANES]

    base = part * x_ref.shape[0]
    g1 = mod_ref[2:3, :]
    zeros = jnp.zeros((LANES - N_EXPERTS, LANES), F32)

    def body(r, carry):
        r0 = pl.multiple_of(r * MIX_ROWS, MIX_ROWS)
        y = _rows(ynat, pl.ds(pl.multiple_of(base + r0, MIX_ROWS), MIX_ROWS))
        glu = jnp.dot(y.astype(MXU), wglu_ref[...], preferred_element_type=F32) + bglu_ref[...]
        z = y * _sigmoid(glu)
        zn = z * lax.rsqrt(jnp.mean(z * z, axis=-1, keepdims=True) + EPS) * gssm_ref[...]
        o_s = jnp.dot(zn.astype(MXU), wouts_ref[...], preferred_element_type=F32)
        a = attn_ref[pl.ds(r0, MIX_ROWS), :]
        an = a * lax.rsqrt(jnp.mean(a * a, axis=-1, keepdims=True) + EPS) * gattn_ref[...]
        o_a = jnp.dot(an.astype(MXU), wouta_ref[...], preferred_element_type=F32)
        x1 = x_ref[pl.ds(r0, MIX_ROWS), :] + g1 * (o_s + o_a)
        x1_ref[pl.ds(r0, MIX_ROWS), :] = x1
        h2 = _ffn_input(x1, mod_ref, nffn_ref)
        logits_t = lax.dot_general(wrt_ref[...], h2, (((1,), (1,)), ((), ())),
                                   precision=HIGHEST, preferred_element_type=F32)
        gates_t = _route(logits_t, rbias_ref[...])
        for i in range(MIX_ROWS // LANES):
            piece = jnp.concatenate([gates_t[:, i * LANES:(i + 1) * LANES], zeros], axis=0)
            gates_ref[pl.ds(r0 + i * LANES, LANES), :] = piece.T
        return carry

    lax.fori_loop(0, x_ref.shape[0] // MIX_ROWS, body, 0)


def _ffn_input(x1, mod_ref, nffn_ref):
    h2 = x1 * lax.rsqrt(jnp.mean(x1 * x1, axis=-1, keepdims=True) + EPS) * nffn_ref[...]
    return h2 * (1.0 + mod_ref[4:5, :]) + mod_ref[3:4, :]


def _mix(yt, attn, x, mod3, wglu, bglu, gssm, wouts, gattn, wouta, nffn, wrt, rbias):
    nb, seq, d = x.shape
    nch = seq // CHUNK
    yt4 = yt.reshape(SSM_GROUPS, CHUNK, SSM_GROUP, nb * nch)
    c2 = lambda b, p: (0, 0)
    tok = lambda width: pl.BlockSpec((None, MIX_TOKENS, width), lambda b, p: (b, p, 0))
    return pl.pallas_call(
        _mix_kernel,
        grid=(nb, seq // MIX_TOKENS),
        in_specs=[pl.BlockSpec((SSM_GROUPS, CHUNK, SSM_GROUP, nch), lambda b, p: (0, 0, 0, b)),
                  tok(ATTN_WIDTH), tok(d),
                  pl.BlockSpec((None, N_MOD, d), lambda b, p: (b, 0, 0)),
                  pl.BlockSpec(wglu.shape, c2), pl.BlockSpec(bglu.shape, c2), pl.BlockSpec(gssm.shape, c2),
                  pl.BlockSpec(wouts.shape, c2), pl.BlockSpec(gattn.shape, c2), pl.BlockSpec(wouta.shape, c2),
                  pl.BlockSpec(nffn.shape, c2), pl.BlockSpec(wrt.shape, c2), pl.BlockSpec(rbias.shape, c2)],
        out_specs=[tok(d), tok(LANES)],
        out_shape=[jax.ShapeDtypeStruct((nb, seq, d), F32), jax.ShapeDtypeStruct((nb, seq, LANES), F32)],
        scratch_shapes=[pltpu.VMEM((seq, LANES), F32)] * (SSM_WIDTH // LANES),
        compiler_params=_cparams(("arbitrary", "arbitrary")),
    )(yt4, attn, x, mod3, wglu, bglu, gssm, wouts, gattn, wouta, nffn, wrt, rbias)


def _moe_kernel(x1_ref, gates_ref, mod_ref, nffn_ref, wg_ref, wu_ref, wd_ref, wsg_ref, wsu_ref, wsd_ref, nfin_ref,
                o_ref, acc_ref, hid_ref, h2_ref):
    s = pl.program_id(1)
    f = EXPERT_DIM

    def glu(h, wg, wu):
        gu = jnp.dot(h, jnp.concatenate([wg, wu], axis=1), preferred_element_type=F32)
        g = gu[:, :f]
        return g * _sigmoid(g) * gu[:, f:]

    @pl.when(s == 0)
    def _():
        h2_ref[...] = _ffn_input(x1_ref[...], mod_ref, nffn_ref).astype(MXU)
        hs = glu(h2_ref[...], wsg_ref[...], wsu_ref[...])
        acc_ref[...] = jnp.dot(hs.astype(MXU), wsd_ref[...], preferred_element_type=F32)

    h2 = h2_ref[...]
    gsh = pltpu.roll(gates_ref[...], (LANES - EXPERTS_PER_STEP * s) % LANES, 1)
    for e in range(EXPERTS_PER_STEP):
        hid = glu(h2, wg_ref[e], wu_ref[e]) * gsh[:, e:e + 1]
        hid_ref[:, e * f:(e + 1) * f] = hid.astype(MXU)
    acc_ref[...] += jnp.dot(hid_ref[...], wd_ref[...], preferred_element_type=F32)

    @pl.when(s == pl.num_programs(1) - 1)
    def _():
        g2 = mod_ref[5:6, :]
        x2 = x1_ref[...] + g2 * acc_ref[...]
        o_ref[...] = x2 * lax.rsqrt(jnp.mean(x2 * x2, axis=-1, keepdims=True) + EPS) * nfin_ref[...]


def _moe(x1, gates, mod3, nffn, wg, wu, wd, wsg, wsu, wsd, nfin):
    nb, seq, d = x1.shape
    tiles_per_b = seq // MOE_TILE
    nsteps = N_EXPERTS // EXPERTS_PER_STEP
    c2 = lambda i, s: (0, 0)
    tok = lambda width: pl.BlockSpec((None, MOE_TILE, width), lambda i, s: (i // tiles_per_b, i % tiles_per_b, 0))
    experts = pl.BlockSpec((EXPERTS_PER_STEP, d, EXPERT_DIM), lambda i, s: (s, 0, 0))
    return pl.pallas_call(
        _moe_kernel,
        grid=(nb * tiles_per_b, nsteps),
        in_specs=[tok(d), tok(LANES),
                  pl.BlockSpec((None, N_MOD, d), lambda i, s: (i // tiles_per_b, 0, 0)),
                  pl.BlockSpec(nffn.shape, c2), experts, experts,
                  pl.BlockSpec((EXPERTS_PER_STEP * EXPERT_DIM, d), lambda i, s: (s, 0)),
                  pl.BlockSpec(wsg.shape, c2), pl.BlockSpec(wsu.shape, c2), pl.BlockSpec(wsd.shape, c2),
                  pl.BlockSpec(nfin.shape, c2)],
        out_specs=tok(d),
        out_shape=jax.ShapeDtypeStruct((nb, seq, d), F32),
        scratch_shapes=[pltpu.VMEM((MOE_TILE, d), F32),
                        pltpu.VMEM((MOE_TILE, EXPERTS_PER_STEP * EXPERT_DIM), MXU),
                        pltpu.VMEM((MOE_TILE, d), MXU)],
        compiler_params=_cparams(("arbitrary", "arbitrary")),
    )(x1, gates, mod3, nffn, wg, wu, wd, wsg, wsu, wsd, nfin)


def _rope_tables(seq):
    pos = jnp.arange(seq)
    row = (pos // GRID_W).astype(F32)
    col = (pos % GRID_W).astype(F32)
    inv = ROPE_THETA ** (-jnp.arange(ROT_PAIRS, dtype=F32) / ROT_PAIRS)
    ar, ac = row[:, None] * inv, col[:, None] * inv
    zero = jnp.zeros_like(ar)
    rep = LANES // HEAD_DIM
    cos_t = jnp.tile(jnp.concatenate([jnp.cos(ar), jnp.cos(ar), jnp.cos(ac), jnp.cos(ac)], axis=1), (1, rep))
    s1_t = jnp.tile(jnp.concatenate([-jnp.sin(ar), zero, -jnp.sin(ac), zero], axis=1), (1, rep))
    s2_t = jnp.tile(jnp.concatenate([zero, jnp.sin(ar), zero, jnp.sin(ac)], axis=1), (1, rep))
    return cos_t, s1_t, s2_t


def kernel(x, c, ctx, c_ctx, w_ada, b_ada, norm_mix, norm_ffn, w_in, attn_sink, ssm_a_re, ssm_a_im, ssm_log_dt, ssm_b_re, ssm_b_im, ssm_c_re, ssm_c_im, ssm_d, w_glu, b_glu, norm_attn_out, norm_ssm_out, w_out, w_router, router_bias, w_gate_e, w_up_e, w_down_e, w_gate_s, w_up_s, w_down_s, norm_final):
    nb, seq, d = x.shape
    nctx = ctx.shape[1]
    layer = 0

    pad = jnp.zeros((16 - nb - 1, d), F32)
    c_all = jnp.concatenate([c, c_ctx[None, :], pad], axis=0)
    mod3 = _ada(c_all, w_ada[layer], b_ada[layer]).reshape(16, N_MOD, d)

    gq = ATTN_HEADS // ATTN_KV_HEADS
    heads = jnp.arange(ATTN_HEADS).reshape(ATTN_KV_HEADS, gq).T.reshape(-1)
    perm_q = (heads[:, None] * HEAD_DIM + jnp.arange(HEAD_DIM)[None, :]).reshape(-1)
    w_in0 = w_in[layer]
    wqkv = jnp.concatenate([w_in0[:, :ATTN_WIDTH][:, perm_q], w_in0[:, ATTN_WIDTH:ATTN_WIDTH + 2 * KV_WIDTH]],
                           axis=1).astype(MXU)
    wut = w_in0[:, ATTN_WIDTH + 2 * KV_WIDTH:].T.astype(MXU)
    nw = norm_mix[layer].reshape(1, d)
    cos_t, s1_t, s2_t = _rope_tables(seq)

    q, k, vt, ul = _proj(x, mod3, 0, nw, wqkv, wut, cos_t, s1_t, s2_t, True)
    _, kc, vct, uc = _proj(ctx.reshape(1, nb * nctx, d), mod3, nb, nw, wqkv, wut, cos_t, s1_t, s2_t, False)
    kc = kc.reshape(nb, nctx, KV_WIDTH)

    attn = _attention(attn_sink[layer], q, k, vt, kc, vct)

    kcomb, pb, cp, qpow = _ssm_prep(ssm_a_re[layer], ssm_a_im[layer], ssm_log_dt[layer], ssm_b_re[layer],
                                    ssm_b_im[layer], ssm_c_re[layer], ssm_c_im[layer])
    l1, vcat, qre, qim, dsk = _ssm_operators(kcomb, pb, cp, qpow, ssm_d[layer])
    yt = _ssm(ul, uc, l1, vcat, qre, qim, dsk, nb)

    w_out0 = w_out[layer]
    nffn = norm_ffn[layer].reshape(1, d)
    x1, gates = _mix(
        yt, attn, x, mod3,
        w_glu[layer].astype(MXU), b_glu[layer].reshape(1, SSM_WIDTH), norm_ssm_out[layer].reshape(1, SSM_WIDTH),
        w_out0[ATTN_WIDTH:].astype(MXU), norm_attn_out[layer][perm_q].reshape(1, ATTN_WIDTH),
        w_out0[:ATTN_WIDTH][perm_q].astype(MXU), nffn,
        w_router[layer].T, router_bias[layer].reshape(N_EXPERTS, 1))

    wd = w_down_e[layer].astype(MXU).reshape(N_EXPERTS * EXPERT_DIM, d)
    return _moe(x1, gates, mod3, nffn, w_gate_e[layer].astype(MXU), w_up_e[layer].astype(MXU), wd,
                w_gate_s[layer].astype(MXU), w_up_s[layer].astype(MXU), w_down_s[layer].astype(MXU),
                norm_final.reshape(1, d))
```

```python
import functools
import math

import jax
import jax.numpy as jnp
from jax import lax
from jax.experimental import pallas as pl
from jax.experimental.pallas import tpu as pltpu

D_MODEL = 1024
EPS = 1e-6
N_MOD = 6
HEAD_DIM = 64
ATTN_HEADS = 8
ATTN_KV_HEADS = 2
ATTN_WIDTH = ATTN_HEADS * HEAD_DIM
KV_WIDTH = ATTN_KV_HEADS * HEAD_DIM
WINDOW = 128
ATTN_SCALE = HEAD_DIM ** -0.5
ROPE_THETA = 10000.0
ROT_PAIRS = HEAD_DIM // 4
GRID_W = 64
SSM_WIDTH = D_MODEL - ATTN_WIDTH
SSM_GROUP = 16
SSM_GROUPS = SSM_WIDTH // SSM_GROUP
SSM_STATE = 64
N_EXPERTS = 64
EXPERT_DIM = 128
TOP_K = 8
N_EXPERT_GROUPS = 8
TOPK_GROUPS = 4
ROUTED_SCALE = 2.5

CHUNK = 16
CHUNK_W = CHUNK * SSM_GROUP
LANES = 128
ATTN_QBLOCKS = 2
MIX_TOKENS = 1024
MIX_ROWS = 512
MOE_TILE = 1024
EXPERTS_PER_STEP = 8
VMEM_LIMIT = 56 * 1024 * 1024

MXU = jnp.bfloat16
F32 = jnp.float32
HIGHEST = lax.Precision.HIGHEST


def _sigmoid(x):
    return 1.0 / (1.0 + jnp.exp(-x))


def _cparams(sem):
    return pltpu.CompilerParams(dimension_semantics=sem, vmem_limit_bytes=VMEM_LIMIT)


def _ada_kernel(c_ref, w_ref, b_ref, o_ref):
    cv = c_ref[...]
    s = cv * _sigmoid(cv)
    o_ref[...] = jnp.dot(s, w_ref[...], precision=HIGHEST, preferred_element_type=F32) + b_ref[...]


def _ada(c_all, w, b):
    rows, d = c_all.shape
    n = w.shape[1]
    tn = 1024
    return pl.pallas_call(
        _ada_kernel,
        grid=(n // tn,),
        in_specs=[pl.BlockSpec((rows, d), lambda j: (0, 0)),
                  pl.BlockSpec((d, tn), lambda j: (0, j)),
                  pl.BlockSpec((1, tn), lambda j: (0, j))],
        out_specs=pl.BlockSpec((rows, tn), lambda j: (0, j)),
        out_shape=jax.ShapeDtypeStruct((rows, n), F32),
        compiler_params=_cparams(("arbitrary",)),
    )(c_all, w, b.reshape(1, n))


def _ssm_prep_kernel(are_ref, aim_ref, ldt_ref, btr_ref, bti_ref, cr_ref, ci_ref, dsk_ref,
                     l1_ref, v_ref, qt_ref):
    p, n = SSM_GROUP, SSM_STATE
    ar, ai = are_ref[...], aim_ref[...]
    dt = jnp.exp(ldt_ref[...])
    mag = jnp.exp(dt * ar)
    abr = mag * jnp.cos(dt * ai)
    abi = mag * jnp.sin(dt * ai)
    den = ar * ar + ai * ai
    nr = abr - 1.0
    cor = (nr * ar + abi * ai) / den
    coi = (abi * ar - nr * ai) / den
    btr, bti = btr_ref[...], bti_ref[...]
    bbr = cor[:, None, :] * btr - coi[:, None, :] * bti
    bbi = cor[:, None, :] * bti + coi[:, None, :] * btr
    cr, ci = cr_ref[...], ci_ref[...]
    pr, pi = jnp.ones_like(abr), jnp.zeros_like(abr)
    cps, pbs = [], []
    for k in range(CHUNK + 1):
        cpr = cr * pr[:, None, :] - ci * pi[:, None, :]
        cpi = cr * pi[:, None, :] + ci * pr[:, None, :]
        cps.append(jnp.concatenate([cpr, -cpi], axis=-1))
        if k < CHUNK:
            pbs.append(jnp.concatenate([bbr * pr[:, None, :] - bbi * pi[:, None, :],
                                        bbr * pi[:, None, :] + bbi * pr[:, None, :]], axis=-1))
            pr, pi = pr * abr - pi * abi, pr * abi + pi * abr
    qr, qi = pr, pi

    zero = jnp.zeros((p, 2 * n), F32)
    rows = [jnp.concatenate([zero, cps[k][1]], axis=1) for k in range(CHUNK - 1, 0, -1)]
    rows.append(jnp.concatenate([cps[0][0], cps[0][1]], axis=1))
    rows += [jnp.concatenate([cps[k][0], zero], axis=1) for k in range(1, CHUNK)]
    rows.append(jnp.concatenate([zero, zero], axis=1))
    stack = jnp.concatenate(rows, axis=0)
    bbcat = jnp.concatenate([bbr[0], bbi[0], bbr[1], bbi[1]], axis=-1)
    kwt = lax.dot_general(bbcat, stack, (((1,), (1,)), ((), ())), precision=HIGHEST, preferred_element_type=F32)
    toep_t = jnp.concatenate(
        [kwt[:, (CHUNK - 1 - t) * p:(CHUNK - 1 - t) * p + CHUNK_W] for t in range(CHUNK)], axis=0)
    wst_f = jnp.concatenate([pbs[CHUNK - 1 - t][0] for t in range(CHUNK)], axis=0)
    wst_b = jnp.concatenate([pbs[t][1] for t in range(CHUNK)], axis=0)
    l1_ref[...] = jnp.concatenate([toep_t.T, wst_f.T, wst_b.T], axis=0).astype(MXU)
    v_f = jnp.concatenate([cps[t + 1][0] for t in range(CHUNK)], axis=0)
    v_b = jnp.concatenate([cps[CHUNK - t][1] for t in range(CHUNK)], axis=0)
    v_ref[...] = jnp.concatenate([v_f, v_b], axis=1).astype(MXU)

    qrows = []
    for j in range(8):
        qrows.append(jnp.concatenate([qr, qi], axis=-1))
        qr, qi = qr * qr - qi * qi, 2.0 * qr * qi
    qmat = jnp.concatenate([qrows[j][d:d + 1] for d in range(2) for j in range(8)] + [dsk_ref[...]], axis=0)
    qpad = jnp.concatenate([qmat, jnp.zeros((LANES - qmat.shape[0], 2 * n), F32)], axis=0)
    qt_ref[...] = qpad.T


def _ssm_prep(a_re, a_im, log_dt, b_re, b_im, c_re, c_im, d_skip):
    g, n, p = SSM_GROUPS, SSM_STATE, SSM_GROUP
    are = jnp.transpose(a_re, (1, 0, 2))
    aim = jnp.transpose(a_im, (1, 0, 2))
    ldt = jnp.transpose(log_dt, (1, 0))[..., None]
    btr = jnp.transpose(b_re, (1, 0, 3, 2))
    bti = jnp.transpose(b_im, (1, 0, 3, 2))
    cr = jnp.transpose(c_re, (1, 0, 2, 3))
    ci = jnp.transpose(c_im, (1, 0, 2, 3))
    dsk = jnp.tile(d_skip.reshape(g, 1, p), (1, 1, 2 * n // p))
    vec = pl.BlockSpec((None, 2, n), lambda i: (i, 0, 0))
    mat = pl.BlockSpec((None, 2, p, n), lambda i: (i, 0, 0, 0))
    return pl.pallas_call(
        _ssm_prep_kernel,
        grid=(g,),
        in_specs=[vec, vec, pl.BlockSpec((None, 2, 1), lambda i: (i, 0, 0)), mat, mat, mat, mat,
                  pl.BlockSpec((None, 1, 2 * n), lambda i: (i, 0, 0))],
        out_specs=[pl.BlockSpec((None, 2 * CHUNK_W, CHUNK_W), lambda i: (i, 0, 0)),
                   pl.BlockSpec((None, CHUNK_W, CHUNK_W), lambda i: (i, 0, 0)),
                   pl.BlockSpec((None, LANES, LANES), lambda i: (i, 0, 0))],
        out_shape=[jax.ShapeDtypeStruct((g, 2 * CHUNK_W, CHUNK_W), MXU),
                   jax.ShapeDtypeStruct((g, CHUNK_W, CHUNK_W), MXU),
                   jax.ShapeDtypeStruct((g, LANES, LANES), F32)],
        compiler_params=_cparams(("arbitrary",)),
    )(are, aim, ldt, btr, bti, cr, ci, dsk)


def _rows(slabs, idx):
    return jnp.concatenate([s[idx, :] for s in slabs], axis=1)


def _proj_kernel(x_ref, mod_ref, nw_ref, win_ref, cos_ref, s1_ref, s2_ref, q_ref, k_ref, vt_ref, ut_ref, *u_nat,
                 rope):
    ntok = x_ref.shape[0]
    nch = ntok // CHUNK
    sh = mod_ref[0:1, :]
    sc = mod_ref[1:2, :]
    nw = nw_ref[...]
    rb = 256

    def rot(tile, r0):
        if not rope:
            return tile
        cs = cos_ref[pl.ds(r0, rb), :]
        a1 = s1_ref[pl.ds(r0, rb), :]
        a2 = s2_ref[pl.ds(r0, rb), :]
        return tile * cs + pltpu.roll(tile, LANES - ROT_PAIRS, 1) * a1 + pltpu.roll(tile, ROT_PAIRS, 1) * a2

    def body(r, carry):
        r0 = pl.multiple_of(r * rb, rb)
        xb = x_ref[pl.ds(r0, rb), :]
        hn = xb * lax.rsqrt(jnp.mean(xb * xb, axis=-1, keepdims=True) + EPS) * nw
        hb = (hn * (1.0 + sc) + sh).astype(MXU)
        proj = jnp.dot(hb, win_ref[...], preferred_element_type=F32)
        for j in range(ATTN_WIDTH // LANES):
            qj = rot(proj[:, j * LANES:(j + 1) * LANES], r0) * ATTN_SCALE
            q_ref[pl.ds(r0, rb), j * LANES:(j + 1) * LANES] = qj.astype(MXU)
        k_ref[pl.ds(r0, rb), :] = rot(proj[:, ATTN_WIDTH:ATTN_WIDTH + KV_WIDTH], r0).astype(MXU)
        vt_ref[:, pl.ds(r0, rb)] = proj[:, ATTN_WIDTH + KV_WIDTH:ATTN_WIDTH + 2 * KV_WIDTH].T.astype(MXU)
        u0 = ATTN_WIDTH + 2 * KV_WIDTH
        for j, ref in enumerate(u_nat):
            ref[pl.ds(r0, rb), :] = proj[:, u0 + j * LANES:u0 + (j + 1) * LANES]
        return carry

    lax.fori_loop(0, ntok // rb, body, 0)

    for t in range(CHUNK):
        u_t = _rows(u_nat, pl.ds(t, nch, stride=CHUNK))
        ut_ref[:, t * SSM_GROUP:(t + 1) * SSM_GROUP, :] = u_t.T.reshape(SSM_GROUPS, SSM_GROUP, nch)


def _proj(x, mod3, mod_row0, nw, win, cos_t, s1_t, s2_t, rope):
    nb, ntok, d = x.shape
    nch = ntok // CHUNK
    const2 = lambda b: (0, 0)
    tok = lambda width: pl.BlockSpec((None, ntok, width), lambda b: (b, 0, 0))
    tables = (cos_t, s1_t, s2_t)
    return pl.pallas_call(
        functools.partial(_proj_kernel, rope=rope),
        grid=(nb,),
        in_specs=[tok(d),
                  pl.BlockSpec((None, N_MOD, d), lambda b: (b + mod_row0, 0, 0)),
                  pl.BlockSpec((1, d), const2),
                  pl.BlockSpec(win.shape, const2)] + [pl.BlockSpec(t.shape, const2) for t in tables],
        out_specs=[tok(ATTN_WIDTH), tok(KV_WIDTH), pl.BlockSpec((None, KV_WIDTH, ntok), lambda b: (b, 0, 0)),
                   pl.BlockSpec((SSM_GROUPS, CHUNK_W, nch), lambda b: (0, 0, b))],
        out_shape=[jax.ShapeDtypeStruct((nb, ntok, ATTN_WIDTH), MXU),
                   jax.ShapeDtypeStruct((nb, ntok, KV_WIDTH), MXU),
                   jax.ShapeDtypeStruct((nb, KV_WIDTH, ntok), MXU),
                   jax.ShapeDtypeStruct((SSM_GROUPS, CHUNK_W, nb * nch), F32)],
        scratch_shapes=[pltpu.VMEM((ntok, LANES), F32)] * (SSM_WIDTH // LANES),
        compiler_params=_cparams(("arbitrary",)),
    )(x, mod3, nw, win, *tables)


def _attn_kernel(sink_ref, q_ref, k_ref, vt_ref, kc_ref, vct_ref, o_ref):
    for sub in range(q_ref.shape[0] // WINDOW):
        _attn_block(sink_ref, q_ref, k_ref, vt_ref, kc_ref, vct_ref, o_ref, sub)


def _attn_block(sink_ref, q_ref, k_ref, vt_ref, kc_ref, vct_ref, o_ref, sub):
    blk = WINDOW
    seq = k_ref.shape[0]
    nwin = 3 * blk
    gq = ATTN_HEADS // ATTN_KV_HEADS
    i = pl.program_id(1) * (q_ref.shape[0] // blk) + sub
    start = pl.multiple_of(jnp.clip((i - 1) * blk, 0, seq - nwin), blk)
    kw = k_ref[pl.ds(start, nwin), :]
    vtw = vt_ref[:, pl.ds(start, nwin)]
    kc = kc_ref[...]
    vct = vct_ref[...]
    q = q_ref[sub * blk:(sub + 1) * blk, :]
    cols = gq * blk
    kpos = start + lax.broadcasted_iota(jnp.int32, (nwin, cols), 0)
    qpos = i * blk + lax.broadcasted_iota(jnp.int32, (nwin, cols), 1) % blk
    band = jnp.abs(qpos - kpos) <= WINDOW
    lane_q = lax.broadcasted_iota(jnp.int32, (blk, LANES), 1)
    nt = (((1,), (1,)), ((), ()))
    outs = []
    for kh in range(ATTN_KV_HEADS):
        own = (lane_q // HEAD_DIM) == kh
        qs = jnp.concatenate(
            [jnp.where(own, q[:, j * LANES:(j + 1) * LANES], jnp.zeros((), MXU)) for j in range(gq)], axis=0)
        sink = jnp.concatenate(
            [jnp.full((1, blk), sink_ref[kh * gq + j], F32) for j in range(gq)], axis=1)
        s_loc = lax.dot_general(kw, qs, nt, preferred_element_type=F32)
        s_ctx = lax.dot_general(kc, qs, nt, preferred_element_type=F32)
        s_loc = jnp.where(band, s_loc, -jnp.inf)
        m = jnp.maximum(jnp.maximum(jnp.max(s_loc, axis=0, keepdims=True),
                                    jnp.max(s_ctx, axis=0, keepdims=True)), sink)
        p_loc = jnp.exp(s_loc - m)
        p_ctx = jnp.exp(s_ctx - m)
        den = (jnp.sum(p_loc, axis=0, keepdims=True) + jnp.sum(p_ctx, axis=0, keepdims=True)
               + jnp.exp(sink - m))
        ot = (jnp.dot(vtw, p_loc.astype(MXU), preferred_element_type=F32)
              + jnp.dot(vct, p_ctx.astype(MXU), preferred_element_type=F32)) / den
        outs.append(ot)
    top = lax.broadcasted_iota(jnp.int32, (KV_WIDTH, blk), 0) < HEAD_DIM
    for j in range(gq):
        both = jnp.where(top, outs[0][:, j * blk:(j + 1) * blk], outs[1][:, j * blk:(j + 1) * blk])
        o_ref[sub * blk:(sub + 1) * blk, j * LANES:(j + 1) * LANES] = both.T


def _attention(sink, q, k, vt, kc, vct):
    nb, seq, _ = q.shape
    nctx = kc.shape[1]
    blk = ATTN_QBLOCKS * WINDOW
    return pl.pallas_call(
        _attn_kernel,
        grid=(nb, seq // blk),
        in_specs=[pl.BlockSpec(memory_space=pltpu.SMEM),
                  pl.BlockSpec((None, blk, ATTN_WIDTH), lambda b, i: (b, i, 0)),
                  pl.BlockSpec((None, seq, KV_WIDTH), lambda b, i: (b, 0, 0)),
                  pl.BlockSpec((None, KV_WIDTH, seq), lambda b, i: (b, 0, 0)),
                  pl.BlockSpec((None, nctx, KV_WIDTH), lambda b, i: (b, 0, 0)),
                  pl.BlockSpec((None, KV_WIDTH, nctx), lambda b, i: (0, 0, b))],
        out_specs=pl.BlockSpec((None, blk, ATTN_WIDTH), lambda b, i: (b, i, 0)),
        out_shape=jax.ShapeDtypeStruct((nb, seq, ATTN_WIDTH), F32),
        compiler_params=_cparams(("arbitrary", "arbitrary")),
    )(sink, q, k, vt, kc, vct)


def _chunk_scan(xr, xi, qt_ref, d, nsteps, seg, fwd):
    width = xr.shape[1]
    pos = lax.broadcasted_iota(jnp.int32, xr.shape, 1) % seg
    for j in range(nsteps):
        s = 1 << j
        if fwd:
            sr, si, ok = pltpu.roll(xr, s, 1), pltpu.roll(xi, s, 1), pos >= s
        else:
            sr, si, ok = pltpu.roll(xr, width - s, 1), pltpu.roll(xi, width - s, 1), pos < seg - s
        sr = jnp.where(ok, sr, 0.0)
        si = jnp.where(ok, si, 0.0)
        qr, qi = _chunk_multiplier(qt_ref, d, j)
        xr, xi = xr + qr * sr - qi * si, xi + qr * si + qi * sr
    return xr, xi


def _chunk_multiplier(qt_ref, d, j):
    col = d * 8 + j
    return qt_ref[0:SSM_STATE, col:col + 1], qt_ref[SSM_STATE:2 * SSM_STATE, col:col + 1]


def _ssm_kernel(ul_ref, uc_ref, l1_ref, v_ref, qt_ref, y_ref, *, nb):
    n = SSM_STATE
    ul = ul_ref[...]
    width = ul.shape[1]
    seg_l = width // nb
    seg_c = uc_ref.shape[1] // nb
    r = jnp.dot(l1_ref[...], ul.astype(MXU), preferred_element_type=F32)
    rc = jnp.dot(l1_ref[CHUNK_W:, :], uc_ref[...].astype(MXU), preferred_element_type=F32)
    lane = lax.broadcasted_iota(jnp.int32, (n, width), 1)
    hin = []
    for d, fwd in enumerate((True, False)):
        base = CHUNK_W + 2 * n * d
        xr, xi = r[base:base + n], r[base + n:base + 2 * n]
        cr, ci = rc[2 * n * d:2 * n * d + n], rc[2 * n * d + n:2 * n * d + 2 * n]
        cr, ci = _chunk_scan(cr, ci, qt_ref, d, int(math.log2(seg_c)), seg_c, fwd)
        injr = jnp.zeros((n, width), F32)
        inji = jnp.zeros((n, width), F32)
        for b in range(nb):
            src = b * seg_c + (seg_c - 1 if fwd else 0)
            dst = b * seg_l + (0 if fwd else seg_l - 1)
            injr = jnp.where(lane == dst, cr[:, src:src + 1], injr)
            inji = jnp.where(lane == dst, ci[:, src:src + 1], inji)
        q0r, q0i = _chunk_multiplier(qt_ref, d, 0)
        xr, xi = xr + q0r * injr - q0i * inji, xi + q0r * inji + q0i * injr
        sr, si = _chunk_scan(xr, xi, qt_ref, d, int(math.log2(seg_l)), seg_l, fwd)
        edge = (lane % seg_l) == (0 if fwd else seg_l - 1)
        shift = 1 if fwd else width - 1
        hin.append(jnp.where(edge, injr, pltpu.roll(sr, shift, 1)))
        hin.append(jnp.where(edge, inji, pltpu.roll(si, shift, 1)))
    hcat = jnp.concatenate(hin, axis=0).astype(MXU)
    dsk = jnp.concatenate([qt_ref[:, 16:17]] * (CHUNK_W // LANES), axis=0)
    y = r[:CHUNK_W] + jnp.dot(v_ref[...], hcat, preferred_element_type=F32) + dsk * ul
    y_ref[...] = 0.5 * y * (1.0 + jnp.tanh(math.sqrt(2.0 / math.pi) * (y + 0.044715 * (y * y * y))))


def _ssm(ul, uc, l1, vcat, qt, nb):
    g, _, width = ul.shape
    wc = uc.shape[2]
    return pl.pallas_call(
        functools.partial(_ssm_kernel, nb=nb),
        grid=(g,),
        in_specs=[pl.BlockSpec((None, CHUNK_W, width), lambda i: (i, 0, 0)),
                  pl.BlockSpec((None, CHUNK_W, wc), lambda i: (i, 0, 0)),
                  pl.BlockSpec((None, 2 * CHUNK_W, CHUNK_W), lambda i: (i, 0, 0)),
                  pl.BlockSpec((None, CHUNK_W, CHUNK_W), lambda i: (i, 0, 0)),
                  pl.BlockSpec((None, LANES, LANES), lambda i: (i, 0, 0))],
        out_specs=pl.BlockSpec((None, CHUNK_W, width), lambda i: (i, 0, 0)),
        out_shape=jax.ShapeDtypeStruct((g, CHUNK_W, width), F32),
        compiler_params=_cparams(("arbitrary",)),
    )(ul, uc, l1, vcat, qt)


def _route(logits_t, bias):
    ng, ne = N_EXPERT_GROUPS, N_EXPERTS // N_EXPERT_GROUPS
    t = logits_t.shape[1]
    scores = _sigmoid(logits_t).reshape(ng, ne, t)
    biased = scores + bias.reshape(ng, ne, 1)
    iw = lax.broadcasted_iota(jnp.int32, (ng, ne, t), 1)
    ig = lax.broadcasted_iota(jnp.int32, (ng, ne, t), 0)
    neg = -jnp.inf
    m1 = jnp.max(biased, axis=1, keepdims=True)
    i1 = jnp.min(jnp.where(biased == m1, iw, ne), axis=1, keepdims=True)
    m2 = jnp.max(jnp.where(iw == i1, neg, biased), axis=1, keepdims=True)
    gscore = jnp.broadcast_to(m1 + m2, (ng, ne, t))
    gsel = jnp.zeros((ng, ne, t), F32)
    cur = gscore
    for _ in range(TOPK_GROUPS):
        m = jnp.max(cur, axis=0, keepdims=True)
        gi = jnp.min(jnp.where(cur == m, ig, ng), axis=0, keepdims=True)
        pick = ig == gi
        gsel = jnp.where(pick, 1.0, gsel)
        cur = jnp.where(pick, neg, cur)
    cur = jnp.where(gsel > 0.0, biased, neg)
    flat = ig * ne + iw
    chosen = jnp.zeros((ng, ne, t), F32)
    for _ in range(TOP_K):
        m = jnp.max(jnp.max(cur, axis=1, keepdims=True), axis=0, keepdims=True)
        fi = jnp.min(jnp.min(jnp.where(cur == m, flat, N_EXPERTS), axis=1, keepdims=True), axis=0, keepdims=True)
        pick = flat == fi
        chosen = jnp.where(pick, 1.0, chosen)
        cur = jnp.where(pick, neg, cur)
    sel = jnp.where(chosen > 0.0, scores, 0.0)
    tot = jnp.sum(jnp.sum(sel, axis=1, keepdims=True), axis=0, keepdims=True)
    return (sel / tot * ROUTED_SCALE).reshape(N_EXPERTS, t)


def _mix_kernel(yt_ref, attn_ref, x_ref, mod_ref, wglu_ref, bglu_ref, gssm_ref, wouts_ref, gattn_ref, wouta_ref,
                nffn_ref, wrt_ref, rbias_ref, x1_ref, gates_ref, *ynat):
    nch = yt_ref.shape[3]
    part = pl.program_id(1)

    @pl.when(part == 0)
    def _():
        for t in range(CHUNK):
            y_t = yt_ref[:, t].reshape(SSM_WIDTH, nch).T
            for j, ref in enumerate(ynat):
                ref[pl.ds(t, nch, stride=CHUNK), :] = y_t[:, j * LANES:(j + 1) * LANES]

    base = part * x_ref.shape[0]
    g1 = mod_ref[2:3, :]
    zeros = jnp.zeros((LANES - N_EXPERTS, LANES), F32)

    def body(r, carry):
        r0 = pl.multiple_of(r * MIX_ROWS, MIX_ROWS)
        y = _rows(ynat, pl.ds(pl.multiple_of(base + r0, MIX_ROWS), MIX_ROWS))
        glu = jnp.dot(y.astype(MXU), wglu_ref[...], preferred_element_type=F32) + bglu_ref[...]
        z = y * _sigmoid(glu)
        zn = z * lax.rsqrt(jnp.mean(z * z, axis=-1, keepdims=True) + EPS) * gssm_ref[...]
        o_s = jnp.dot(zn.astype(MXU), wouts_ref[...], preferred_element_type=F32)
        a = attn_ref[pl.ds(r0, MIX_ROWS), :]
        an = a * lax.rsqrt(jnp.mean(a * a, axis=-1, keepdims=True) + EPS) * gattn_ref[...]
        o_a = jnp.dot(an.astype(MXU), wouta_ref[...], preferred_element_type=F32)
        x1 = x_ref[pl.ds(r0, MIX_ROWS), :] + g1 * (o_s + o_a)
        x1_ref[pl.ds(r0, MIX_ROWS), :] = x1
        h2 = _ffn_input(x1, mod_ref, nffn_ref)
        logits_t = lax.dot_general(wrt_ref[...], h2, (((1,), (1,)), ((), ())),
                                   precision=HIGHEST, preferred_element_type=F32)
        gates_t = _route(logits_t, rbias_ref[...])
        for i in range(MIX_ROWS // LANES):
            piece = jnp.concatenate([gates_t[:, i * LANES:(i + 1) * LANES], zeros], axis=0)
            gates_ref[pl.ds(r0 + i * LANES, LANES), :] = piece.T
        return carry

    lax.fori_loop(0, x_ref.shape[0] // MIX_ROWS, body, 0)


def _ffn_input(x1, mod_ref, nffn_ref):
    h2 = x1 * lax.rsqrt(jnp.mean(x1 * x1, axis=-1, keepdims=True) + EPS) * nffn_ref[...]
    return h2 * (1.0 + mod_ref[4:5, :]) + mod_ref[3:4, :]


def _mix(yt, attn, x, mod3, wglu, bglu, gssm, wouts, gattn, wouta, nffn, wrt, rbias):
    nb, seq, d = x.shape
    nch = seq // CHUNK
    yt4 = yt.reshape(SSM_GROUPS, CHUNK, SSM_GROUP, nb * nch)
    c2 = lambda b, p: (0, 0)
    tok = lambda width: pl.BlockSpec((None, MIX_TOKENS, width), lambda b, p: (b, p, 0))
    return pl.pallas_call(
        _mix_kernel,
        grid=(nb, seq // MIX_TOKENS),
        in_specs=[pl.BlockSpec((SSM_GROUPS, CHUNK, SSM_GROUP, nch), lambda b, p: (0, 0, 0, b)),
                  tok(ATTN_WIDTH), tok(d),
                  pl.BlockSpec((None, N_MOD, d), lambda b, p: (b, 0, 0)),
                  pl.BlockSpec(wglu.shape, c2), pl.BlockSpec(bglu.shape, c2), pl.BlockSpec(gssm.shape, c2),
                  pl.BlockSpec(wouts.shape, c2), pl.BlockSpec(gattn.shape, c2), pl.BlockSpec(wouta.shape, c2),
                  pl.BlockSpec(nffn.shape, c2), pl.BlockSpec(wrt.shape, c2), pl.BlockSpec(rbias.shape, c2)],
        out_specs=[tok(d), tok(LANES)],
        out_shape=[jax.ShapeDtypeStruct((nb, seq, d), F32), jax.ShapeDtypeStruct((nb, seq, LANES), F32)],
        scratch_shapes=[pltpu.VMEM((seq, LANES), F32)] * (SSM_WIDTH // LANES),
        compiler_params=_cparams(("arbitrary", "arbitrary")),
    )(yt4, attn, x, mod3, wglu, bglu, gssm, wouts, gattn, wouta, nffn, wrt, rbias)


def _moe_kernel(x1_ref, gates_ref, mod_ref, nffn_ref, wg_ref, wu_ref, wd_ref, wsg_ref, wsu_ref, wsd_ref, nfin_ref,
                o_ref, acc_ref, hid_ref, h2_ref):
    s = pl.program_id(1)
    f = EXPERT_DIM

    def glu(h, wg, wu):
        gu = jnp.dot(h, jnp.concatenate([wg.astype(MXU), wu.astype(MXU)], axis=1), preferred_element_type=F32)
        g = gu[:, :f]
        return g * _sigmoid(g) * gu[:, f:]

    @pl.when(s == 0)
    def _():
        h2_ref[...] = _ffn_input(x1_ref[...], mod_ref, nffn_ref).astype(MXU)
        hs = glu(h2_ref[...], wsg_ref[...], wsu_ref[...])
        acc_ref[...] = jnp.dot(hs.astype(MXU), wsd_ref[...].astype(MXU), preferred_element_type=F32)

    h2 = h2_ref[...]
    gsh = pltpu.roll(gates_ref[...], (LANES - EXPERTS_PER_STEP * s) % LANES, 1)
    for e in range(EXPERTS_PER_STEP):
        hid = glu(h2, wg_ref[e], wu_ref[e]) * gsh[:, e:e + 1]
        hid_ref[:, e * f:(e + 1) * f] = hid.astype(MXU)
    acc_ref[...] += jnp.dot(hid_ref[...], wd_ref[...].astype(MXU), preferred_element_type=F32)

    @pl.when(s == pl.num_programs(1) - 1)
    def _():
        g2 = mod_ref[5:6, :]
        x2 = x1_ref[...] + g2 * acc_ref[...]
        o_ref[...] = x2 * lax.rsqrt(jnp.mean(x2 * x2, axis=-1, keepdims=True) + EPS) * nfin_ref[...]


def _moe(x1, gates, mod3, nffn, wg, wu, wd, wsg, wsu, wsd, nfin):
    nb, seq, d = x1.shape
    tiles_per_b = seq // MOE_TILE
    nsteps = N_EXPERTS // EXPERTS_PER_STEP
    c2 = lambda i, s: (0, 0)
    tok = lambda width: pl.BlockSpec((None, MOE_TILE, width), lambda i, s: (i // tiles_per_b, i % tiles_per_b, 0))
    experts = pl.BlockSpec((EXPERTS_PER_STEP, d, EXPERT_DIM), lambda i, s: (s, 0, 0))
    return pl.pallas_call(
        _moe_kernel,
        grid=(nb * tiles_per_b, nsteps),
        in_specs=[tok(d), tok(LANES),
                  pl.BlockSpec((None, N_MOD, d), lambda i, s: (i // tiles_per_b, 0, 0)),
                  pl.BlockSpec(nffn.shape, c2), experts, experts,
                  pl.BlockSpec((EXPERTS_PER_STEP * EXPERT_DIM, d), lambda i, s: (s, 0)),
                  pl.BlockSpec(wsg.shape, c2), pl.BlockSpec(wsu.shape, c2), pl.BlockSpec(wsd.shape, c2),
                  pl.BlockSpec(nfin.shape, c2)],
        out_specs=tok(d),
        out_shape=jax.ShapeDtypeStruct((nb, seq, d), F32),
        scratch_shapes=[pltpu.VMEM((MOE_TILE, d), F32),
                        pltpu.VMEM((MOE_TILE, EXPERTS_PER_STEP * EXPERT_DIM), MXU),
                        pltpu.VMEM((MOE_TILE, d), MXU)],
        compiler_params=_cparams(("arbitrary", "arbitrary")),
    )(x1, gates, mod3, nffn, wg, wu, wd, wsg, wsu, wsd, nfin)


def _rope_tables(seq):
    pos = jnp.arange(seq)
    row = (pos // GRID_W).astype(F32)
    col = (pos % GRID_W).astype(F32)
    inv = ROPE_THETA ** (-jnp.arange(ROT_PAIRS, dtype=F32) / ROT_PAIRS)
    ar, ac = row[:, None] * inv, col[:, None] * inv
    zero = jnp.zeros_like(ar)
    rep = LANES // HEAD_DIM
    cos_t = jnp.tile(jnp.concatenate([jnp.cos(ar), jnp.cos(ar), jnp.cos(ac), jnp.cos(ac)], axis=1), (1, rep))
    s1_t = jnp.tile(jnp.concatenate([-jnp.sin(ar), zero, -jnp.sin(ac), zero], axis=1), (1, rep))
    s2_t = jnp.tile(jnp.concatenate([zero, jnp.sin(ar), zero, jnp.sin(ac)], axis=1), (1, rep))
    return cos_t, s1_t, s2_t


def kernel(x, c, ctx, c_ctx, w_ada, b_ada, norm_mix, norm_ffn, w_in, attn_sink, ssm_a_re, ssm_a_im, ssm_log_dt, ssm_b_re, ssm_b_im, ssm_c_re, ssm_c_im, ssm_d, w_glu, b_glu, norm_attn_out, norm_ssm_out, w_out, w_router, router_bias, w_gate_e, w_up_e, w_down_e, w_gate_s, w_up_s, w_down_s, norm_final):
    nb, seq, d = x.shape
    nctx = ctx.shape[1]
    layer = 0

    pad = jnp.zeros((16 - nb - 1, d), F32)
    c_all = jnp.concatenate([c, c_ctx[None, :], pad], axis=0)
    mod3 = _ada(c_all, w_ada[layer], b_ada[layer]).reshape(16, N_MOD, d)

    gq = ATTN_HEADS // ATTN_KV_HEADS
    heads = jnp.arange(ATTN_HEADS).reshape(ATTN_KV_HEADS, gq).T.reshape(-1)
    perm_q = (heads[:, None] * HEAD_DIM + jnp.arange(HEAD_DIM)[None, :]).reshape(-1)
    w_in0 = w_in[layer]
    win = jnp.concatenate([w_in0[:, :ATTN_WIDTH][:, perm_q], w_in0[:, ATTN_WIDTH:]], axis=1).astype(MXU)
    nw = norm_mix[layer].reshape(1, d)
    cos_t, s1_t, s2_t = _rope_tables(seq)

    q, k, vt, ul = _proj(x, mod3, 0, nw, win, cos_t, s1_t, s2_t, True)
    _, kc, vct, uc = _proj(ctx.reshape(1, nb * nctx, d), mod3, nb, nw, win, cos_t, s1_t, s2_t, False)
    kc = kc.reshape(nb, nctx, KV_WIDTH)

    attn = _attention(attn_sink[layer], q, k, vt, kc, vct)

    l1, vcat, qt = _ssm_prep(ssm_a_re[layer], ssm_a_im[layer], ssm_log_dt[layer], ssm_b_re[layer],
                             ssm_b_im[layer], ssm_c_re[layer], ssm_c_im[layer], ssm_d[layer])
    yt = _ssm(ul, uc, l1, vcat, qt, nb)

    w_out0 = w_out[layer]
    nffn = norm_ffn[layer].reshape(1, d)
    x1, gates = _mix(
        yt, attn, x, mod3,
        w_glu[layer].astype(MXU), b_glu[layer].reshape(1, SSM_WIDTH), norm_ssm_out[layer].reshape(1, SSM_WIDTH),
        w_out0[ATTN_WIDTH:].astype(MXU), norm_attn_out[layer][perm_q].reshape(1, ATTN_WIDTH),
        w_out0[:ATTN_WIDTH][perm_q].astype(MXU), nffn,
        w_router[layer].T, router_bias[layer].reshape(N_EXPERTS, 1))

    wd = w_down_e[layer].reshape(N_EXPERTS * EXPERT_DIM, d)
    return _moe(x1, gates, mod3, nffn, w_gate_e[layer], w_up_e[layer], wd,
                w_gate_s[layer], w_up_s[layer], w_down_s[layer], norm_final.reshape(1, d))
```

```python
import functools
import math

import jax
import jax.numpy as jnp
from jax import lax
from jax.experimental import pallas as pl
from jax.experimental.pallas import tpu as pltpu

D_MODEL = 1024
EPS = 1e-6
N_MOD = 6
HEAD_DIM = 64
ATTN_HEADS = 8
ATTN_KV_HEADS = 2
ATTN_WIDTH = ATTN_HEADS * HEAD_DIM
KV_WIDTH = ATTN_KV_HEADS * HEAD_DIM
WINDOW = 128
ATTN_SCALE = HEAD_DIM ** -0.5
LOG2E = math.log2(math.e)
ROPE_THETA = 10000.0
ROT_PAIRS = HEAD_DIM // 4
GRID_W = 64
SSM_WIDTH = D_MODEL - ATTN_WIDTH
SSM_GROUP = 16
SSM_GROUPS = SSM_WIDTH // SSM_GROUP
SSM_STATE = 64
N_EXPERTS = 64
EXPERT_DIM = 128
TOP_K = 8
N_EXPERT_GROUPS = 8
TOPK_GROUPS = 4
ROUTED_SCALE = 2.5

CHUNK = 16
CHUNK_W = CHUNK * SSM_GROUP
LANES = 128
ATTN_QBLOCKS = 4
MIX_TOKENS = 1024
MIX_ROWS = 512
MOE_TILE = 1024
EXPERTS_PER_STEP = 8
VMEM_LIMIT = 56 * 1024 * 1024

MXU = jnp.bfloat16
F32 = jnp.float32
HIGHEST = lax.Precision.HIGHEST


def _sigmoid(x):
    return 1.0 / (1.0 + jnp.exp(-x))


def _cparams(sem):
    return pltpu.CompilerParams(dimension_semantics=sem, vmem_limit_bytes=VMEM_LIMIT)


def _ada_kernel(c_ref, w_ref, b_ref, o_ref):
    cv = c_ref[...]
    s = cv * _sigmoid(cv)
    o_ref[...] = jnp.dot(s, w_ref[...], precision=HIGHEST, preferred_element_type=F32) + b_ref[...]


def _ada(c_all, w, b):
    rows, d = c_all.shape
    n = w.shape[1]
    tn = 1024
    return pl.pallas_call(
        _ada_kernel,
        grid=(n // tn,),
        in_specs=[pl.BlockSpec((rows, d), lambda j: (0, 0)),
                  pl.BlockSpec((d, tn), lambda j: (0, j)),
                  pl.BlockSpec((1, tn), lambda j: (0, j))],
        out_specs=pl.BlockSpec((rows, tn), lambda j: (0, j)),
        out_shape=jax.ShapeDtypeStruct((rows, n), F32),
        compiler_params=_cparams(("arbitrary",)),
    )(c_all, w, b.reshape(1, n))


def _ssm_prep_kernel(are_ref, aim_ref, ldt_ref, btr_ref, bti_ref, cr_ref, ci_ref, dsk_ref,
                     l1_ref, v_ref, qt_ref):
    p, n = SSM_GROUP, SSM_STATE
    ar, ai = are_ref[...], aim_ref[...]
    dt = jnp.exp(ldt_ref[...])
    mag = jnp.exp(dt * ar)
    abr = mag * jnp.cos(dt * ai)
    abi = mag * jnp.sin(dt * ai)
    den = ar * ar + ai * ai
    nr = abr - 1.0
    cor = (nr * ar + abi * ai) / den
    coi = (abi * ar - nr * ai) / den
    btr, bti = btr_ref[...], bti_ref[...]
    bbr = cor[:, None, :] * btr - coi[:, None, :] * bti
    bbi = cor[:, None, :] * bti + coi[:, None, :] * btr
    cr, ci = cr_ref[...], ci_ref[...]
    pr, pi = jnp.ones_like(abr), jnp.zeros_like(abr)
    cps, pbs = [], []
    for k in range(CHUNK + 1):
        cpr = cr * pr[:, None, :] - ci * pi[:, None, :]
        cpi = cr * pi[:, None, :] + ci * pr[:, None, :]
        cps.append(jnp.concatenate([cpr, -cpi], axis=-1))
        if k < CHUNK:
            pbs.append(jnp.concatenate([bbr * pr[:, None, :] - bbi * pi[:, None, :],
                                        bbr * pi[:, None, :] + bbi * pr[:, None, :]], axis=-1))
            pr, pi = pr * abr - pi * abi, pr * abi + pi * abr
    qr, qi = pr, pi

    zero = jnp.zeros((p, 2 * n), F32)
    rows = [jnp.concatenate([zero, cps[k][1]], axis=1) for k in range(CHUNK - 1, 0, -1)]
    rows.append(jnp.concatenate([cps[0][0], cps[0][1]], axis=1))
    rows += [jnp.concatenate([cps[k][0], zero], axis=1) for k in range(1, CHUNK)]
    rows.append(jnp.concatenate([zero, zero], axis=1))
    stack = jnp.concatenate(rows, axis=0)
    bbcat = jnp.concatenate([bbr[0], bbi[0], bbr[1], bbi[1]], axis=-1)
    kwt = lax.dot_general(bbcat, stack, (((1,), (1,)), ((), ())), precision=HIGHEST, preferred_element_type=F32)
    toep_t = jnp.concatenate(
        [kwt[:, (CHUNK - 1 - t) * p:(CHUNK - 1 - t) * p + CHUNK_W] for t in range(CHUNK)], axis=0)
    wst_f = jnp.concatenate([pbs[CHUNK - 1 - t][0] for t in range(CHUNK)], axis=0)
    wst_b = jnp.concatenate([pbs[t][1] for t in range(CHUNK)], axis=0)
    l1_ref[...] = jnp.concatenate([toep_t.T, wst_f.T, wst_b.T], axis=0).astype(MXU)
    v_f = jnp.concatenate([cps[t + 1][0] for t in range(CHUNK)], axis=0)
    v_b = jnp.concatenate([cps[CHUNK - t][1] for t in range(CHUNK)], axis=0)
    v_ref[...] = jnp.concatenate([v_f, v_b], axis=1).astype(MXU)

    qrows = []
    for j in range(8):
        qrows.append(jnp.concatenate([qr, qi], axis=-1))
        qr, qi = qr * qr - qi * qi, 2.0 * qr * qi
    qmat = jnp.concatenate([qrows[j][d:d + 1] for d in range(2) for j in range(8)] + [dsk_ref[...]], axis=0)
    qpad = jnp.concatenate([qmat, jnp.zeros((LANES - qmat.shape[0], 2 * n), F32)], axis=0)
    qt_ref[...] = qpad.T


def _ssm_prep(a_re, a_im, log_dt, b_re, b_im, c_re, c_im, d_skip):
    g, n, p = SSM_GROUPS, SSM_STATE, SSM_GROUP
    are = jnp.transpose(a_re, (1, 0, 2))
    aim = jnp.transpose(a_im, (1, 0, 2))
    ldt = jnp.transpose(log_dt, (1, 0))[..., None]
    btr = jnp.transpose(b_re, (1, 0, 3, 2))
    bti = jnp.transpose(b_im, (1, 0, 3, 2))
    cr = jnp.transpose(c_re, (1, 0, 2, 3))
    ci = jnp.transpose(c_im, (1, 0, 2, 3))
    dsk = jnp.tile(d_skip.reshape(g, 1, p), (1, 1, 2 * n // p))
    vec = pl.BlockSpec((None, 2, n), lambda i: (i, 0, 0))
    mat = pl.BlockSpec((None, 2, p, n), lambda i: (i, 0, 0, 0))
    return pl.pallas_call(
        _ssm_prep_kernel,
        grid=(g,),
        in_specs=[vec, vec, pl.BlockSpec((None, 2, 1), lambda i: (i, 0, 0)), mat, mat, mat, mat,
                  pl.BlockSpec((None, 1, 2 * n), lambda i: (i, 0, 0))],
        out_specs=[pl.BlockSpec((None, 2 * CHUNK_W, CHUNK_W), lambda i: (i, 0, 0)),
                   pl.BlockSpec((None, CHUNK_W, CHUNK_W), lambda i: (i, 0, 0)),
                   pl.BlockSpec((None, LANES, LANES), lambda i: (i, 0, 0))],
        out_shape=[jax.ShapeDtypeStruct((g, 2 * CHUNK_W, CHUNK_W), MXU),
                   jax.ShapeDtypeStruct((g, CHUNK_W, CHUNK_W), MXU),
                   jax.ShapeDtypeStruct((g, LANES, LANES), F32)],
        compiler_params=_cparams(("arbitrary",)),
    )(are, aim, ldt, btr, bti, cr, ci, dsk)


def _rows(slabs, idx):
    return jnp.concatenate([s[idx, :] for s in slabs], axis=1)


def _proj_kernel(x_ref, mod_ref, nw_ref, win_ref, cos_ref, s1_ref, s2_ref, q_ref, k_ref, vt_ref, ut_ref, *u_nat,
                 rope):
    ntok = x_ref.shape[0]
    nch = ntok // CHUNK
    sh = mod_ref[0:1, :]
    sc = mod_ref[1:2, :]
    nw = nw_ref[...]
    rb = 256

    def rot(tile, r0):
        if not rope:
            return tile
        cs = cos_ref[pl.ds(r0, rb), :]
        a1 = s1_ref[pl.ds(r0, rb), :]
        a2 = s2_ref[pl.ds(r0, rb), :]
        return tile * cs + pltpu.roll(tile, LANES - ROT_PAIRS, 1) * a1 + pltpu.roll(tile, ROT_PAIRS, 1) * a2

    def body(r, carry):
        r0 = pl.multiple_of(r * rb, rb)
        xb = x_ref[pl.ds(r0, rb), :]
        hn = xb * lax.rsqrt(jnp.mean(xb * xb, axis=-1, keepdims=True) + EPS) * nw
        hb = (hn * (1.0 + sc) + sh).astype(MXU)
        proj = jnp.dot(hb, win_ref[...], preferred_element_type=F32)
        for j in range(ATTN_WIDTH // LANES):
            qj = rot(proj[:, j * LANES:(j + 1) * LANES], r0) * (ATTN_SCALE * LOG2E)
            q_ref[pl.ds(r0, rb), j * LANES:(j + 1) * LANES] = qj.astype(MXU)
        k_ref[pl.ds(r0, rb), :] = rot(proj[:, ATTN_WIDTH:ATTN_WIDTH + KV_WIDTH], r0).astype(MXU)
        vt_ref[:, pl.ds(r0, rb)] = proj[:, ATTN_WIDTH + KV_WIDTH:ATTN_WIDTH + 2 * KV_WIDTH].T.astype(MXU)
        u0 = ATTN_WIDTH + 2 * KV_WIDTH
        for j, ref in enumerate(u_nat):
            ref[pl.ds(r0, rb), :] = proj[:, u0 + j * LANES:u0 + (j + 1) * LANES]
        return carry

    lax.fori_loop(0, ntok // rb, body, 0)

    for t in range(CHUNK):
        u_t = _rows(u_nat, pl.ds(t, nch, stride=CHUNK))
        ut_ref[:, t * SSM_GROUP:(t + 1) * SSM_GROUP, :] = u_t.T.reshape(SSM_GROUPS, SSM_GROUP, nch)


def _proj(x, mod3, mod_row0, nw, win, cos_t, s1_t, s2_t, rope):
    nb, ntok, d = x.shape
    nch = ntok // CHUNK
    const2 = lambda b: (0, 0)
    tok = lambda width: pl.BlockSpec((None, ntok, width), lambda b: (b, 0, 0))
    tables = (cos_t, s1_t, s2_t)
    return pl.pallas_call(
        functools.partial(_proj_kernel, rope=rope),
        grid=(nb,),
        in_specs=[tok(d),
                  pl.BlockSpec((None, N_MOD, d), lambda b: (b + mod_row0, 0, 0)),
                  pl.BlockSpec((1, d), const2),
                  pl.BlockSpec(win.shape, const2)] + [pl.BlockSpec(t.shape, const2) for t in tables],
        out_specs=[tok(ATTN_WIDTH), tok(KV_WIDTH), pl.BlockSpec((None, KV_WIDTH, ntok), lambda b: (b, 0, 0)),
                   pl.BlockSpec((SSM_GROUPS, CHUNK_W, nch), lambda b: (0, 0, b))],
        out_shape=[jax.ShapeDtypeStruct((nb, ntok, ATTN_WIDTH), MXU),
                   jax.ShapeDtypeStruct((nb, ntok, KV_WIDTH), MXU),
                   jax.ShapeDtypeStruct((nb, KV_WIDTH, ntok), MXU),
                   jax.ShapeDtypeStruct((SSM_GROUPS, CHUNK_W, nb * nch), F32)],
        scratch_shapes=[pltpu.VMEM((ntok, LANES), F32)] * (SSM_WIDTH // LANES),
        compiler_params=_cparams(("arbitrary",)),
    )(x, mod3, nw, win, *tables)


def _attn_kernel(sink_ref, q_ref, k_ref, vt_ref, kc_ref, vct_ref, bias_ref, o_ref):
    blk = WINDOW
    seq = k_ref.shape[0]
    nwin = 3 * blk
    gq = ATTN_HEADS // ATTN_KV_HEADS
    nsub = q_ref.shape[0] // blk
    kc = kc_ref[...]
    ones = jnp.ones((16, 1), MXU)
    vct = jnp.concatenate([vct_ref[...], jnp.broadcast_to(ones, (16, kc.shape[0]))], axis=0)
    lane_q = lax.broadcasted_iota(jnp.int32, (blk, LANES), 1)
    nt = (((1,), (1,)), ((), ()))
    chains = [(sub, kh) for sub in range(nsub) for kh in range(ATTN_KV_HEADS)]

    starts, scores = {}, {}
    for sub in range(nsub):
        i = pl.program_id(1) * nsub + sub
        starts[sub] = (i, pl.multiple_of(jnp.clip((i - 1) * blk, 0, seq - nwin), blk))
    for sub, kh in chains:
        i, start = starts[sub]
        q = q_ref[sub * blk:(sub + 1) * blk, :]
        qs = jnp.concatenate(
            [jnp.where((lane_q // HEAD_DIM) == kh, q[:, j * LANES:(j + 1) * LANES], jnp.zeros((), MXU))
             for j in range(gq)], axis=0)
        s_loc = lax.dot_general(k_ref[pl.ds(start, nwin), :], qs, nt, preferred_element_type=F32)
        s_loc = s_loc + bias_ref[(i * blk - start) // blk]
        s_ctx = lax.dot_general(kc, qs, nt, preferred_element_type=F32)
        scores[sub, kh] = (s_loc, s_ctx)

    probs = {}
    for sub, kh in chains:
        s_loc, s_ctx = scores[sub, kh]
        sink = jnp.concatenate(
            [jnp.full((1, blk), sink_ref[kh * gq + j] * LOG2E, F32) for j in range(gq)], axis=1)
        m = jnp.maximum(jnp.maximum(jnp.max(s_loc, axis=0, keepdims=True),
                                    jnp.max(s_ctx, axis=0, keepdims=True)), sink)
        probs[sub, kh] = (jnp.exp2(s_loc - m).astype(MXU), jnp.exp2(s_ctx - m).astype(MXU), jnp.exp2(sink - m))

    outs = {}
    for sub, kh in chains:
        p_loc, p_ctx, p_sink = probs[sub, kh]
        _, start = starts[sub]
        vtw = jnp.concatenate([vt_ref[:, pl.ds(start, nwin)], jnp.broadcast_to(ones, (16, nwin))], axis=0)
        acc = (jnp.dot(vtw, p_loc, preferred_element_type=F32)
               + jnp.dot(vct, p_ctx, preferred_element_type=F32))
        outs[sub, kh] = acc[:KV_WIDTH] / (acc[KV_WIDTH:KV_WIDTH + 1] + p_sink)

    top = lax.broadcasted_iota(jnp.int32, (KV_WIDTH, blk), 0) < HEAD_DIM
    for sub in range(nsub):
        for j in range(gq):
            both = jnp.where(top, outs[sub, 0][:, j * blk:(j + 1) * blk], outs[sub, 1][:, j * blk:(j + 1) * blk])
            o_ref[sub * blk:(sub + 1) * blk, j * LANES:(j + 1) * LANES] = both.T


def _band_bias():
    gq = ATTN_HEADS // ATTN_KV_HEADS
    key = jnp.arange(3 * WINDOW)[None, :, None]
    qry = (jnp.arange(gq * WINDOW) % WINDOW)[None, None, :]
    var = jnp.arange(3)[:, None, None]
    inside = jnp.abs(key - var * WINDOW - qry) <= WINDOW
    return jnp.where(inside, 0.0, -jnp.inf).astype(F32)


def _attention(sink, q, k, vt, kc, vct):
    nb, seq, _ = q.shape
    nctx = kc.shape[1]
    blk = ATTN_QBLOCKS * WINDOW
    bias = _band_bias()
    return pl.pallas_call(
        _attn_kernel,
        grid=(nb, seq // blk),
        in_specs=[pl.BlockSpec(memory_space=pltpu.SMEM),
                  pl.BlockSpec((None, blk, ATTN_WIDTH), lambda b, i: (b, i, 0)),
                  pl.BlockSpec((None, seq, KV_WIDTH), lambda b, i: (b, 0, 0)),
                  pl.BlockSpec((None, KV_WIDTH, seq), lambda b, i: (b, 0, 0)),
                  pl.BlockSpec((None, nctx, KV_WIDTH), lambda b, i: (b, 0, 0)),
                  pl.BlockSpec((None, KV_WIDTH, nctx), lambda b, i: (0, 0, b)),
                  pl.BlockSpec(bias.shape, lambda b, i: (0, 0, 0))],
        out_specs=pl.BlockSpec((None, blk, ATTN_WIDTH), lambda b, i: (b, i, 0)),
        out_shape=jax.ShapeDtypeStruct((nb, seq, ATTN_WIDTH), F32),
        compiler_params=_cparams(("arbitrary", "arbitrary")),
    )(sink, q, k, vt, kc, vct, bias)


def _chunk_scans(xs, qt_ref, nsteps, seg):
    width = xs[0][0].shape[1]
    pos = lax.broadcasted_iota(jnp.int32, xs[0][0].shape, 1) % seg
    for j in range(nsteps):
        s = 1 << j
        nxt = []
        for d, (xr, xi) in enumerate(xs):
            if d == 0:
                sr, si, ok = pltpu.roll(xr, s, 1), pltpu.roll(xi, s, 1), pos >= s
            else:
                sr, si, ok = pltpu.roll(xr, width - s, 1), pltpu.roll(xi, width - s, 1), pos < seg - s
            sr = jnp.where(ok, sr, 0.0)
            si = jnp.where(ok, si, 0.0)
            qr, qi = _chunk_multiplier(qt_ref, d, j)
            nxt.append((xr + qr * sr - qi * si, xi + qr * si + qi * sr))
        xs = nxt
    return xs


def _chunk_multiplier(qt_ref, d, j):
    col = d * 8 + j
    return qt_ref[0:SSM_STATE, col:col + 1], qt_ref[SSM_STATE:2 * SSM_STATE, col:col + 1]


def _ssm_kernel(ul_ref, uc_ref, l1_ref, v_ref, qt_ref, y_ref, *, nb):
    n = SSM_STATE
    ul = ul_ref[...]
    width = ul.shape[1]
    seg_l = width // nb
    seg_c = uc_ref.shape[1] // nb
    r = jnp.dot(l1_ref[...], ul.astype(MXU), preferred_element_type=F32)
    rc = jnp.dot(l1_ref[CHUNK_W:, :], uc_ref[...].astype(MXU), preferred_element_type=F32)
    lane = lax.broadcasted_iota(jnp.int32, (n, width), 1)
    ctx = _chunk_scans([(rc[2 * n * d:2 * n * d + n], rc[2 * n * d + n:2 * n * d + 2 * n]) for d in range(2)],
                       qt_ref, int(math.log2(seg_c)), seg_c)
    xs, inj = [], []
    for d, (cr, ci) in enumerate(ctx):
        fwd = d == 0
        base = CHUNK_W + 2 * n * d
        xr, xi = r[base:base + n], r[base + n:base + 2 * n]
        injr = jnp.zeros((n, width), F32)
        inji = jnp.zeros((n, width), F32)
        for b in range(nb):
            src = b * seg_c + (seg_c - 1 if fwd else 0)
            dst = b * seg_l + (0 if fwd else seg_l - 1)
            injr = jnp.where(lane == dst, cr[:, src:src + 1], injr)
            inji = jnp.where(lane == dst, ci[:, src:src + 1], inji)
        q0r, q0i = _chunk_multiplier(qt_ref, d, 0)
        xs.append((xr + q0r * injr - q0i * inji, xi + q0r * inji + q0i * injr))
        inj.append((injr, inji))
    hin = []
    for d, (sr, si) in enumerate(_chunk_scans(xs, qt_ref, int(math.log2(seg_l)), seg_l)):
        fwd = d == 0
        edge = (lane % seg_l) == (0 if fwd else seg_l - 1)
        shift = 1 if fwd else width - 1
        hin.append(jnp.where(edge, inj[d][0], pltpu.roll(sr, shift, 1)))
        hin.append(jnp.where(edge, inj[d][1], pltpu.roll(si, shift, 1)))
    hcat = jnp.concatenate(hin, axis=0).astype(MXU)
    dsk = jnp.concatenate([qt_ref[:, 16:17]] * (CHUNK_W // LANES), axis=0)
    y = r[:CHUNK_W] + jnp.dot(v_ref[...], hcat, preferred_element_type=F32) + dsk * ul
    y_ref[...] = 0.5 * y * (1.0 + jnp.tanh(math.sqrt(2.0 / math.pi) * (y + 0.044715 * (y * y * y))))


def _ssm(ul, uc, l1, vcat, qt, nb):
    g, _, width = ul.shape
    wc = uc.shape[2]
    return pl.pallas_call(
        functools.partial(_ssm_kernel, nb=nb),
        grid=(g,),
        in_specs=[pl.BlockSpec((None, CHUNK_W, width), lambda i: (i, 0, 0)),
                  pl.BlockSpec((None, CHUNK_W, wc), lambda i: (i, 0, 0)),
                  pl.BlockSpec((None, 2 * CHUNK_W, CHUNK_W), lambda i: (i, 0, 0)),
                  pl.BlockSpec((None, CHUNK_W, CHUNK_W), lambda i: (i, 0, 0)),
                  pl.BlockSpec((None, LANES, LANES), lambda i: (i, 0, 0))],
        out_specs=pl.BlockSpec((None, CHUNK_W, width), lambda i: (i, 0, 0)),
        out_shape=jax.ShapeDtypeStruct((g, CHUNK_W, width), F32),
        compiler_params=_cparams(("arbitrary",)),
    )(ul, uc, l1, vcat, qt)


def _route(logits_t, bias):
    ng, ne = N_EXPERT_GROUPS, N_EXPERTS // N_EXPERT_GROUPS
    t = logits_t.shape[1]
    scores = _sigmoid(logits_t).reshape(ng, ne, t)
    biased = scores + bias.reshape(ng, ne, 1)
    iw = lax.broadcasted_iota(jnp.int32, (ng, ne, t), 1)
    ig = lax.broadcasted_iota(jnp.int32, (ng, ne, t), 0)
    neg = -jnp.inf
    m1 = jnp.max(biased, axis=1, keepdims=True)
    i1 = jnp.min(jnp.where(biased == m1, iw, ne), axis=1, keepdims=True)
    m2 = jnp.max(jnp.where(iw == i1, neg, biased), axis=1, keepdims=True)
    gscore = jnp.broadcast_to(m1 + m2, (ng, ne, t))
    gsel = jnp.zeros((ng, ne, t), F32)
    cur = gscore
    for _ in range(TOPK_GROUPS):
        m = jnp.max(cur, axis=0, keepdims=True)
        gi = jnp.min(jnp.where(cur == m, ig, ng), axis=0, keepdims=True)
        pick = ig == gi
        gsel = jnp.where(pick, 1.0, gsel)
        cur = jnp.where(pick, neg, cur)
    cur = jnp.where(gsel > 0.0, biased, neg)
    flat = ig * ne + iw
    chosen = jnp.zeros((ng, ne, t), F32)
    for _ in range(TOP_K):
        m = jnp.max(jnp.max(cur, axis=1, keepdims=True), axis=0, keepdims=True)
        fi = jnp.min(jnp.min(jnp.where(cur == m, flat, N_EXPERTS), axis=1, keepdims=True), axis=0, keepdims=True)
        pick = flat == fi
        chosen = jnp.where(pick, 1.0, chosen)
        cur = jnp.where(pick, neg, cur)
    sel = jnp.where(chosen > 0.0, scores, 0.0)
    tot = jnp.sum(jnp.sum(sel, axis=1, keepdims=True), axis=0, keepdims=True)
    return (sel / tot * ROUTED_SCALE).reshape(N_EXPERTS, t)


def _mix_kernel(yt_ref, attn_ref, x_ref, mod_ref, wglu_ref, bglu_ref, gssm_ref, wouts_ref, gattn_ref, wouta_ref,
                nffn_ref, wrt_ref, rbias_ref, x1_ref, gates_ref, *ynat):
    nch = yt_ref.shape[3]
    part = pl.program_id(1)

    @pl.when(part == 0)
    def _():
        for t in range(CHUNK):
            y_t = yt_ref[:, t].reshape(SSM_WIDTH, nch).T
            for j, ref in enumerate(ynat):
                ref[pl.ds(t, nch, stride=CHUNK), :] = y_t[:, j * LANES:(j + 1) * LANES]

    base = part * x_ref.shape[0]
    g1 = mod_ref[2:3, :]
    zeros = jnp.zeros((LANES - N_EXPERTS, LANES), F32)

    def body(r, carry):
        r0 = pl.multiple_of(r * MIX_ROWS, MIX_ROWS)
        y = _rows(ynat, pl.ds(pl.multiple_of(base + r0, MIX_ROWS), MIX_ROWS))
        glu = jnp.dot(y.astype(MXU), wglu_ref[...], preferred_element_type=F32) + bglu_ref[...]
        z = y * _sigmoid(glu)
        zn = z * lax.rsqrt(jnp.mean(z * z, axis=-1, keepdims=True) + EPS) * gssm_ref[...]
        o_s = jnp.dot(zn.astype(MXU), wouts_ref[...], preferred_element_type=F32)
        a = attn_ref[pl.ds(r0, MIX_ROWS), :]
        an = a * lax.rsqrt(jnp.mean(a * a, axis=-1, keepdims=True) + EPS) * gattn_ref[...]
        o_a = jnp.dot(an.astype(MXU), wouta_ref[...], preferred_element_type=F32)
        x1 = x_ref[pl.ds(r0, MIX_ROWS), :] + g1 * (o_s + o_a)
        x1_ref[pl.ds(r0, MIX_ROWS), :] = x1
        h2 = _ffn_input(x1, mod_ref, nffn_ref)
        logits_t = lax.dot_general(wrt_ref[...], h2, (((1,), (1,)), ((), ())),
                                   precision=HIGHEST, preferred_element_type=F32)
        gates_t = _route(logits_t, rbias_ref[...])
        for i in range(MIX_ROWS // LANES):
            piece = jnp.concatenate([gates_t[:, i * LANES:(i + 1) * LANES], zeros], axis=0)
            gates_ref[pl.ds(r0 + i * LANES, LANES), :] = piece.T
        return carry

    lax.fori_loop(0, x_ref.shape[0] // MIX_ROWS, body, 0)


def _ffn_input(x1, mod_ref, nffn_ref):
    h2 = x1 * lax.rsqrt(jnp.mean(x1 * x1, axis=-1, keepdims=True) + EPS) * nffn_ref[...]
    return h2 * (1.0 + mod_ref[4:5, :]) + mod_ref[3:4, :]


def _mix(yt, attn, x, mod3, wglu, bglu, gssm, wouts, gattn, wouta, nffn, wrt, rbias):
    nb, seq, d = x.shape
    nch = seq // CHUNK
    yt4 = yt.reshape(SSM_GROUPS, CHUNK, SSM_GROUP, nb * nch)
    c2 = lambda b, p: (0, 0)
    tok = lambda width: pl.BlockSpec((None, MIX_TOKENS, width), lambda b, p: (b, p, 0))
    return pl.pallas_call(
        _mix_kernel,
        grid=(nb, seq // MIX_TOKENS),
        in_specs=[pl.BlockSpec((SSM_GROUPS, CHUNK, SSM_GROUP, nch), lambda b, p: (0, 0, 0, b)),
                  tok(ATTN_WIDTH), tok(d),
                  pl.BlockSpec((None, N_MOD, d), lambda b, p: (b, 0, 0)),
                  pl.BlockSpec(wglu.shape, c2), pl.BlockSpec(bglu.shape, c2), pl.BlockSpec(gssm.shape, c2),
                  pl.BlockSpec(wouts.shape, c2), pl.BlockSpec(gattn.shape, c2), pl.BlockSpec(wouta.shape, c2),
                  pl.BlockSpec(nffn.shape, c2), pl.BlockSpec(wrt.shape, c2), pl.BlockSpec(rbias.shape, c2)],
        out_specs=[tok(d), tok(LANES)],
        out_shape=[jax.ShapeDtypeStruct((nb, seq, d), F32), jax.ShapeDtypeStruct((nb, seq, LANES), F32)],
        scratch_shapes=[pltpu.VMEM((seq, LANES), F32)] * (SSM_WIDTH // LANES),
        compiler_params=_cparams(("arbitrary", "arbitrary")),
    )(yt4, attn, x, mod3, wglu, bglu, gssm, wouts, gattn, wouta, nffn, wrt, rbias)


def _moe_kernel(x1_ref, gates_ref, mod_ref, nffn_ref, wg_ref, wu_ref, wd_ref, wsg_ref, wsu_ref, wsd_ref, nfin_ref,
                o_ref, acc_ref, hid_ref, h2_ref):
    s = pl.program_id(1)
    f = EXPERT_DIM

    def glu(h, wg, wu):
        gu = jnp.dot(h, jnp.concatenate([wg.astype(MXU), wu.astype(MXU)], axis=1), preferred_element_type=F32)
        g = gu[:, :f]
        return g * _sigmoid(g) * gu[:, f:]

    @pl.when(s == 0)
    def _():
        h2_ref[...] = _ffn_input(x1_ref[...], mod_ref, nffn_ref).astype(MXU)
        hs = glu(h2_ref[...], wsg_ref[...], wsu_ref[...])
        acc_ref[...] = jnp.dot(hs.astype(MXU), wsd_ref[...].astype(MXU), preferred_element_type=F32)

    h2 = h2_ref[...]
    gsh = pltpu.roll(gates_ref[...], (LANES - EXPERTS_PER_STEP * s) % LANES, 1)
    for e in range(EXPERTS_PER_STEP):
        hid = glu(h2, wg_ref[e], wu_ref[e]) * gsh[:, e:e + 1]
        hid_ref[:, e * f:(e + 1) * f] = hid.astype(MXU)
    acc_ref[...] += jnp.dot(hid_ref[...], wd_ref[...].astype(MXU), preferred_element_type=F32)

    @pl.when(s == pl.num_programs(1) - 1)
    def _():
        g2 = mod_ref[5:6, :]
        x2 = x1_ref[...] + g2 * acc_ref[...]
        o_ref[...] = x2 * lax.rsqrt(jnp.mean(x2 * x2, axis=-1, keepdims=True) + EPS) * nfin_ref[...]


def _moe(x1, gates, mod3, nffn, wg, wu, wd, wsg, wsu, wsd, nfin):
    nb, seq, d = x1.shape
    tiles_per_b = seq // MOE_TILE
    nsteps = N_EXPERTS // EXPERTS_PER_STEP
    c2 = lambda i, s: (0, 0)
    tok = lambda width: pl.BlockSpec((None, MOE_TILE, width), lambda i, s: (i // tiles_per_b, i % tiles_per_b, 0))
    experts = pl.BlockSpec((EXPERTS_PER_STEP, d, EXPERT_DIM), lambda i, s: (s, 0, 0))
    return pl.pallas_call(
        _moe_kernel,
        grid=(nb * tiles_per_b, nsteps),
        in_specs=[tok(d), tok(LANES),
                  pl.BlockSpec((None, N_MOD, d), lambda i, s: (i // tiles_per_b, 0, 0)),
                  pl.BlockSpec(nffn.shape, c2), experts, experts,
                  pl.BlockSpec((EXPERTS_PER_STEP * EXPERT_DIM, d), lambda i, s: (s, 0)),
                  pl.BlockSpec(wsg.shape, c2), pl.BlockSpec(wsu.shape, c2), pl.BlockSpec(wsd.shape, c2),
                  pl.BlockSpec(nfin.shape, c2)],
        out_specs=tok(d),
        out_shape=jax.ShapeDtypeStruct((nb, seq, d), F32),
        scratch_shapes=[pltpu.VMEM((MOE_TILE, d), F32),
                        pltpu.VMEM((MOE_TILE, EXPERTS_PER_STEP * EXPERT_DIM), MXU),
                        pltpu.VMEM((MOE_TILE, d), MXU)],
        compiler_params=_cparams(("arbitrary", "arbitrary")),
    )(x1, gates, mod3, nffn, wg, wu, wd, wsg, wsu, wsd, nfin)


def _rope_tables(seq):
    pos = jnp.arange(seq)
    row = (pos // GRID_W).astype(F32)
    col = (pos % GRID_W).astype(F32)
    inv = ROPE_THETA ** (-jnp.arange(ROT_PAIRS, dtype=F32) / ROT_PAIRS)
    ar, ac = row[:, None] * inv, col[:, None] * inv
    zero = jnp.zeros_like(ar)
    rep = LANES // HEAD_DIM
    cos_t = jnp.tile(jnp.concatenate([jnp.cos(ar), jnp.cos(ar), jnp.cos(ac), jnp.cos(ac)], axis=1), (1, rep))
    s1_t = jnp.tile(jnp.concatenate([-jnp.sin(ar), zero, -jnp.sin(ac), zero], axis=1), (1, rep))
    s2_t = jnp.tile(jnp.concatenate([zero, jnp.sin(ar), zero, jnp.sin(ac)], axis=1), (1, rep))
    return cos_t, s1_t, s2_t


def kernel(x, c, ctx, c_ctx, w_ada, b_ada, norm_mix, norm_ffn, w_in, attn_sink, ssm_a_re, ssm_a_im, ssm_log_dt, ssm_b_re, ssm_b_im, ssm_c_re, ssm_c_im, ssm_d, w_glu, b_glu, norm_attn_out, norm_ssm_out, w_out, w_router, router_bias, w_gate_e, w_up_e, w_down_e, w_gate_s, w_up_s, w_down_s, norm_final):
    nb, seq, d = x.shape
    nctx = ctx.shape[1]
    layer = 0

    pad = jnp.zeros((16 - nb - 1, d), F32)
    c_all = jnp.concatenate([c, c_ctx[None, :], pad], axis=0)
    mod3 = _ada(c_all, w_ada[layer], b_ada[layer]).reshape(16, N_MOD, d)

    gq = ATTN_HEADS // ATTN_KV_HEADS
    heads = jnp.arange(ATTN_HEADS).reshape(ATTN_KV_HEADS, gq).T.reshape(-1)
    perm_q = (heads[:, None] * HEAD_DIM + jnp.arange(HEAD_DIM)[None, :]).reshape(-1)
    w_in0 = w_in[layer]
    win = jnp.concatenate([w_in0[:, :ATTN_WIDTH][:, perm_q], w_in0[:, ATTN_WIDTH:]], axis=1).astype(MXU)
    nw = norm_mix[layer].reshape(1, d)
    cos_t, s1_t, s2_t = _rope_tables(seq)

    q, k, vt, ul = _proj(x, mod3, 0, nw, win, cos_t, s1_t, s2_t, True)
    _, kc, vct, uc = _proj(ctx.reshape(1, nb * nctx, d), mod3, nb, nw, win, cos_t, s1_t, s2_t, False)
    kc = kc.reshape(nb, nctx, KV_WIDTH)

    attn = _attention(attn_sink[layer], q, k, vt, kc, vct)

    l1, vcat, qt = _ssm_prep(ssm_a_re[layer], ssm_a_im[layer], ssm_log_dt[layer], ssm_b_re[layer],
                             ssm_b_im[layer], ssm_c_re[layer], ssm_c_im[layer], ssm_d[layer])
    yt = _ssm(ul, uc, l1, vcat, qt, nb)

    w_out0 = w_out[layer]
    nffn = norm_ffn[layer].reshape(1, d)
    x1, gates = _mix(
        yt, attn, x, mod3,
        w_glu[layer].astype(MXU), b_glu[layer].reshape(1, SSM_WIDTH), norm_ssm_out[layer].reshape(1, SSM_WIDTH),
        w_out0[ATTN_WIDTH:].astype(MXU), norm_attn_out[layer][perm_q].reshape(1, ATTN_WIDTH),
        w_out0[:ATTN_WIDTH][perm_q].astype(MXU), nffn,
        w_router[layer].T, router_bias[layer].reshape(N_EXPERTS, 1))

    wd = w_down_e[layer].reshape(N_EXPERTS * EXPERT_DIM, d)
    return _moe(x1, gates, mod3, nffn, w_gate_e[layer], w_up_e[layer], wd,
                w_gate_s[layer], w_up_s[layer], w_down_s[layer], norm_final.reshape(1, d))
```

```python
import functools
import math

import jax
import jax.numpy as jnp
from jax import lax
from jax.experimental import pallas as pl
from jax.experimental.pallas import tpu as pltpu

D_MODEL = 1024
EPS = 1e-6
N_MOD = 6
HEAD_DIM = 64
ATTN_HEADS = 8
ATTN_KV_HEADS = 2
ATTN_WIDTH = ATTN_HEADS * HEAD_DIM
KV_WIDTH = ATTN_KV_HEADS * HEAD_DIM
WINDOW = 128
ATTN_SCALE = HEAD_DIM ** -0.5
LOG2E = math.log2(math.e)
ROPE_THETA = 10000.0
ROT_PAIRS = HEAD_DIM // 4
GRID_W = 64
SSM_WIDTH = D_MODEL - ATTN_WIDTH
SSM_GROUP = 16
SSM_GROUPS = SSM_WIDTH // SSM_GROUP
SSM_STATE = 64
N_EXPERTS = 64
EXPERT_DIM = 128
TOP_K = 8
N_EXPERT_GROUPS = 8
TOPK_GROUPS = 4
ROUTED_SCALE = 2.5

CHUNK = 16
CHUNK_W = CHUNK * SSM_GROUP
LANES = 128
ATTN_QBLOCKS = 4
MIX_TOKENS = 1024
MIX_ROWS = 512
MOE_TILE = 1024
EXPERTS_PER_STEP = 8
MOE_SUB = 128
MOE_CAP = 32
VMEM_LIMIT = 56 * 1024 * 1024

MXU = jnp.bfloat16
F32 = jnp.float32
HIGHEST = lax.Precision.HIGHEST


def _sigmoid(x):
    return 1.0 / (1.0 + jnp.exp(-x))


def _cparams(sem):
    return pltpu.CompilerParams(dimension_semantics=sem, vmem_limit_bytes=VMEM_LIMIT)


def _ada_kernel(c_ref, w_ref, b_ref, o_ref):
    cv = c_ref[...]
    s = cv * _sigmoid(cv)
    o_ref[...] = jnp.dot(s, w_ref[...], precision=HIGHEST, preferred_element_type=F32) + b_ref[...]


def _ada(c_all, w, b):
    rows, d = c_all.shape
    n = w.shape[1]
    tn = 1024
    return pl.pallas_call(
        _ada_kernel,
        grid=(n // tn,),
        in_specs=[pl.BlockSpec((rows, d), lambda j: (0, 0)),
                  pl.BlockSpec((d, tn), lambda j: (0, j)),
                  pl.BlockSpec((1, tn), lambda j: (0, j))],
        out_specs=pl.BlockSpec((rows, tn), lambda j: (0, j)),
        out_shape=jax.ShapeDtypeStruct((rows, n), F32),
        compiler_params=_cparams(("arbitrary",)),
    )(c_all, w, b.reshape(1, n))


def _ssm_prep_kernel(are_ref, aim_ref, ldt_ref, btr_ref, bti_ref, cr_ref, ci_ref, dsk_ref,
                     l1_ref, v_ref, qt_ref):
    p, n = SSM_GROUP, SSM_STATE
    ar, ai = are_ref[...], aim_ref[...]
    dt = jnp.exp(ldt_ref[...])
    mag = jnp.exp(dt * ar)
    abr = mag * jnp.cos(dt * ai)
    abi = mag * jnp.sin(dt * ai)
    den = ar * ar + ai * ai
    nr = abr - 1.0
    cor = (nr * ar + abi * ai) / den
    coi = (abi * ar - nr * ai) / den
    btr, bti = btr_ref[...], bti_ref[...]
    bbr = cor[:, None, :] * btr - coi[:, None, :] * bti
    bbi = cor[:, None, :] * bti + coi[:, None, :] * btr
    cr, ci = cr_ref[...], ci_ref[...]
    pr, pi = jnp.ones_like(abr), jnp.zeros_like(abr)
    cps, pbs = [], []
    for k in range(CHUNK + 1):
        cpr = cr * pr[:, None, :] - ci * pi[:, None, :]
        cpi = cr * pi[:, None, :] + ci * pr[:, None, :]
        cps.append(jnp.concatenate([cpr, -cpi], axis=-1))
        if k < CHUNK:
            pbs.append(jnp.concatenate([bbr * pr[:, None, :] - bbi * pi[:, None, :],
                                        bbr * pi[:, None, :] + bbi * pr[:, None, :]], axis=-1))
            pr, pi = pr * abr - pi * abi, pr * abi + pi * abr
    qr, qi = pr, pi

    zero = jnp.zeros((p, 2 * n), F32)
    rows = [jnp.concatenate([zero, cps[k][1]], axis=1) for k in range(CHUNK - 1, 0, -1)]
    rows.append(jnp.concatenate([cps[0][0], cps[0][1]], axis=1))
    rows += [jnp.concatenate([cps[k][0], zero], axis=1) for k in range(1, CHUNK)]
    rows.append(jnp.concatenate([zero, zero], axis=1))
    stack = jnp.concatenate(rows, axis=0)
    bbcat = jnp.concatenate([bbr[0], bbi[0], bbr[1], bbi[1]], axis=-1)
    kwt = lax.dot_general(bbcat, stack, (((1,), (1,)), ((), ())), precision=HIGHEST, preferred_element_type=F32)
    toep_t = jnp.concatenate(
        [kwt[:, (CHUNK - 1 - t) * p:(CHUNK - 1 - t) * p + CHUNK_W] for t in range(CHUNK)], axis=0)
    wst_f = jnp.concatenate([pbs[CHUNK - 1 - t][0] for t in range(CHUNK)], axis=0)
    wst_b = jnp.concatenate([pbs[t][1] for t in range(CHUNK)], axis=0)
    l1_ref[...] = jnp.concatenate([toep_t.T, wst_f.T, wst_b.T], axis=0).astype(MXU)
    v_f = jnp.concatenate([cps[t + 1][0] for t in range(CHUNK)], axis=0)
    v_b = jnp.concatenate([cps[CHUNK - t][1] for t in range(CHUNK)], axis=0)
    v_ref[...] = jnp.concatenate([v_f, v_b], axis=1).astype(MXU)

    qrows = []
    for j in range(8):
        qrows.append(jnp.concatenate([qr, qi], axis=-1))
        qr, qi = qr * qr - qi * qi, 2.0 * qr * qi
    qmat = jnp.concatenate([qrows[j][d:d + 1] for d in range(2) for j in range(8)] + [dsk_ref[...]], axis=0)
    qpad = jnp.concatenate([qmat, jnp.zeros((LANES - qmat.shape[0], 2 * n), F32)], axis=0)
    qt_ref[...] = qpad.T


def _ssm_prep(a_re, a_im, log_dt, b_re, b_im, c_re, c_im, d_skip):
    g, n, p = SSM_GROUPS, SSM_STATE, SSM_GROUP
    are = jnp.transpose(a_re, (1, 0, 2))
    aim = jnp.transpose(a_im, (1, 0, 2))
    ldt = jnp.transpose(log_dt, (1, 0))[..., None]
    btr = jnp.transpose(b_re, (1, 0, 3, 2))
    bti = jnp.transpose(b_im, (1, 0, 3, 2))
    cr = jnp.transpose(c_re, (1, 0, 2, 3))
    ci = jnp.transpose(c_im, (1, 0, 2, 3))
    dsk = jnp.tile(d_skip.reshape(g, 1, p), (1, 1, 2 * n // p))
    vec = pl.BlockSpec((None, 2, n), lambda i: (i, 0, 0))
    mat = pl.BlockSpec((None, 2, p, n), lambda i: (i, 0, 0, 0))
    return pl.pallas_call(
        _ssm_prep_kernel,
        grid=(g,),
        in_specs=[vec, vec, pl.BlockSpec((None, 2, 1), lambda i: (i, 0, 0)), mat, mat, mat, mat,
                  pl.BlockSpec((None, 1, 2 * n), lambda i: (i, 0, 0))],
        out_specs=[pl.BlockSpec((None, 2 * CHUNK_W, CHUNK_W), lambda i: (i, 0, 0)),
                   pl.BlockSpec((None, CHUNK_W, CHUNK_W), lambda i: (i, 0, 0)),
                   pl.BlockSpec((None, LANES, LANES), lambda i: (i, 0, 0))],
        out_shape=[jax.ShapeDtypeStruct((g, 2 * CHUNK_W, CHUNK_W), MXU),
                   jax.ShapeDtypeStruct((g, CHUNK_W, CHUNK_W), MXU),
                   jax.ShapeDtypeStruct((g, LANES, LANES), F32)],
        compiler_params=_cparams(("arbitrary",)),
    )(are, aim, ldt, btr, bti, cr, ci, dsk)


def _rows(slabs, idx):
    return jnp.concatenate([s[idx, :] for s in slabs], axis=1)


def _proj_kernel(x_ref, mod_ref, nw_ref, win_ref, cos_ref, s1_ref, s2_ref, q_ref, k_ref, vt_ref, ut_ref, *u_nat,
                 rope):
    ntok = x_ref.shape[0]
    nch = ntok // CHUNK
    sh = mod_ref[0:1, :]
    sc = mod_ref[1:2, :]
    nw = nw_ref[...]
    rb = 256

    def rot(tile, r0):
        if not rope:
            return tile
        cs = cos_ref[pl.ds(r0, rb), :]
        a1 = s1_ref[pl.ds(r0, rb), :]
        a2 = s2_ref[pl.ds(r0, rb), :]
        return tile * cs + pltpu.roll(tile, LANES - ROT_PAIRS, 1) * a1 + pltpu.roll(tile, ROT_PAIRS, 1) * a2

    def body(r, carry):
        r0 = pl.multiple_of(r * rb, rb)
        xb = x_ref[pl.ds(r0, rb), :]
        hn = xb * lax.rsqrt(jnp.mean(xb * xb, axis=-1, keepdims=True) + EPS) * nw
        hb = (hn * (1.0 + sc) + sh).astype(MXU)
        proj = jnp.dot(hb, win_ref[...], preferred_element_type=F32)
        for j in range(ATTN_WIDTH // LANES):
            qj = rot(proj[:, j * LANES:(j + 1) * LANES], r0) * (ATTN_SCALE * LOG2E)
            q_ref[pl.ds(r0, rb), j * LANES:(j + 1) * LANES] = qj.astype(MXU)
        k_ref[pl.ds(r0, rb), :] = rot(proj[:, ATTN_WIDTH:ATTN_WIDTH + KV_WIDTH], r0).astype(MXU)
        vt_ref[:, pl.ds(r0, rb)] = proj[:, ATTN_WIDTH + KV_WIDTH:ATTN_WIDTH + 2 * KV_WIDTH].T.astype(MXU)
        u0 = ATTN_WIDTH + 2 * KV_WIDTH
        for j, ref in enumerate(u_nat):
            ref[pl.ds(r0, rb), :] = proj[:, u0 + j * LANES:u0 + (j + 1) * LANES]
        return carry

    lax.fori_loop(0, ntok // rb, body, 0)

    for t in range(CHUNK):
        u_t = _rows(u_nat, pl.ds(t, nch, stride=CHUNK))
        ut_ref[:, t * SSM_GROUP:(t + 1) * SSM_GROUP, :] = u_t.T.reshape(SSM_GROUPS, SSM_GROUP, nch)


def _proj(x, mod3, mod_row0, nw, win, cos_t, s1_t, s2_t, rope):
    nb, ntok, d = x.shape
    nch = ntok // CHUNK
    const2 = lambda b: (0, 0)
    tok = lambda width: pl.BlockSpec((None, ntok, width), lambda b: (b, 0, 0))
    tables = (cos_t, s1_t, s2_t)
    return pl.pallas_call(
        functools.partial(_proj_kernel, rope=rope),
        grid=(nb,),
        in_specs=[tok(d),
                  pl.BlockSpec((None, N_MOD, d), lambda b: (b + mod_row0, 0, 0)),
                  pl.BlockSpec((1, d), const2),
                  pl.BlockSpec(win.shape, const2)] + [pl.BlockSpec(t.shape, const2) for t in tables],
        out_specs=[tok(ATTN_WIDTH), tok(KV_WIDTH), pl.BlockSpec((None, KV_WIDTH, ntok), lambda b: (b, 0, 0)),
                   pl.BlockSpec((SSM_GROUPS, CHUNK_W, nch), lambda b: (0, 0, b))],
        out_shape=[jax.ShapeDtypeStruct((nb, ntok, ATTN_WIDTH), MXU),
                   jax.ShapeDtypeStruct((nb, ntok, KV_WIDTH), MXU),
                   jax.ShapeDtypeStruct((nb, KV_WIDTH, ntok), MXU),
                   jax.ShapeDtypeStruct((SSM_GROUPS, CHUNK_W, nb * nch), F32)],
        scratch_shapes=[pltpu.VMEM((ntok, LANES), F32)] * (SSM_WIDTH // LANES),
        compiler_params=_cparams(("arbitrary",)),
    )(x, mod3, nw, win, *tables)


def _attn_kernel(sink_ref, q_ref, k_ref, vt_ref, kc_ref, vct_ref, bias_ref, o_ref):
    blk = WINDOW
    seq = k_ref.shape[0]
    nwin = 3 * blk
    gq = ATTN_HEADS // ATTN_KV_HEADS
    nsub = q_ref.shape[0] // blk
    kc = kc_ref[...]
    ones = jnp.ones((16, 1), MXU)
    vct = jnp.concatenate([vct_ref[...], jnp.broadcast_to(ones, (16, kc.shape[0]))], axis=0)
    lane_q = lax.broadcasted_iota(jnp.int32, (blk, LANES), 1)
    nt = (((1,), (1,)), ((), ()))
    chains = [(sub, kh) for sub in range(nsub) for kh in range(ATTN_KV_HEADS)]

    starts, scores = {}, {}
    for sub in range(nsub):
        i = pl.program_id(1) * nsub + sub
        starts[sub] = (i, pl.multiple_of(jnp.clip((i - 1) * blk, 0, seq - nwin), blk))
    for sub, kh in chains:
        i, start = starts[sub]
        q = q_ref[sub * blk:(sub + 1) * blk, :]
        qs = jnp.concatenate(
            [jnp.where((lane_q // HEAD_DIM) == kh, q[:, j * LANES:(j + 1) * LANES], jnp.zeros((), MXU))
             for j in range(gq)], axis=0)
        s_loc = lax.dot_general(k_ref[pl.ds(start, nwin), :], qs, nt, preferred_element_type=F32)
        s_loc = s_loc + bias_ref[(i * blk - start) // blk]
        s_ctx = lax.dot_general(kc, qs, nt, preferred_element_type=F32)
        scores[sub, kh] = (s_loc, s_ctx)

    probs = {}
    for sub, kh in chains:
        s_loc, s_ctx = scores[sub, kh]
        sink = jnp.concatenate(
            [jnp.full((1, blk), sink_ref[kh * gq + j] * LOG2E, F32) for j in range(gq)], axis=1)
        m = jnp.maximum(jnp.maximum(jnp.max(s_loc, axis=0, keepdims=True),
                                    jnp.max(s_ctx, axis=0, keepdims=True)), sink)
        probs[sub, kh] = (jnp.exp2(s_loc - m).astype(MXU), jnp.exp2(s_ctx - m).astype(MXU), jnp.exp2(sink - m))

    outs = {}
    for sub, kh in chains:
        p_loc, p_ctx, p_sink = probs[sub, kh]
        _, start = starts[sub]
        vtw = jnp.concatenate([vt_ref[:, pl.ds(start, nwin)], jnp.broadcast_to(ones, (16, nwin))], axis=0)
        acc = (jnp.dot(vtw, p_loc, preferred_element_type=F32)
               + jnp.dot(vct, p_ctx, preferred_element_type=F32))
        outs[sub, kh] = acc[:KV_WIDTH] / (acc[KV_WIDTH:KV_WIDTH + 1] + p_sink)

    top = lax.broadcasted_iota(jnp.int32, (KV_WIDTH, blk), 0) < HEAD_DIM
    for sub in range(nsub):
        for j in range(gq):
            both = jnp.where(top, outs[sub, 0][:, j * blk:(j + 1) * blk], outs[sub, 1][:, j * blk:(j + 1) * blk])
            o_ref[sub * blk:(sub + 1) * blk, j * LANES:(j + 1) * LANES] = both.T


def _band_bias():
    gq = ATTN_HEADS // ATTN_KV_HEADS
    key = jnp.arange(3 * WINDOW)[None, :, None]
    qry = (jnp.arange(gq * WINDOW) % WINDOW)[None, None, :]
    var = jnp.arange(3)[:, None, None]
    inside = jnp.abs(key - var * WINDOW - qry) <= WINDOW
    return jnp.where(inside, 0.0, -jnp.inf).astype(F32)


def _attention(sink, q, k, vt, kc, vct):
    nb, seq, _ = q.shape
    nctx = kc.shape[1]
    blk = ATTN_QBLOCKS * WINDOW
    bias = _band_bias()
    return pl.pallas_call(
        _attn_kernel,
        grid=(nb, seq // blk),
        in_specs=[pl.BlockSpec(memory_space=pltpu.SMEM),
                  pl.BlockSpec((None, blk, ATTN_WIDTH), lambda b, i: (b, i, 0)),
                  pl.BlockSpec((None, seq, KV_WIDTH), lambda b, i: (b, 0, 0)),
                  pl.BlockSpec((None, KV_WIDTH, seq), lambda b, i: (b, 0, 0)),
                  pl.BlockSpec((None, nctx, KV_WIDTH), lambda b, i: (b, 0, 0)),
                  pl.BlockSpec((None, KV_WIDTH, nctx), lambda b, i: (0, 0, b)),
                  pl.BlockSpec(bias.shape, lambda b, i: (0, 0, 0))],
        out_specs=pl.BlockSpec((None, blk, ATTN_WIDTH), lambda b, i: (b, i, 0)),
        out_shape=jax.ShapeDtypeStruct((nb, seq, ATTN_WIDTH), F32),
        compiler_params=_cparams(("arbitrary", "arbitrary")),
    )(sink, q, k, vt, kc, vct, bias)


def _chunk_scans(xs, qt_ref, nsteps, seg):
    width = xs[0][0].shape[1]
    pos = lax.broadcasted_iota(jnp.int32, xs[0][0].shape, 1) % seg
    for j in range(nsteps):
        s = 1 << j
        nxt = []
        for d, (xr, xi) in enumerate(xs):
            if d == 0:
                sr, si, ok = pltpu.roll(xr, s, 1), pltpu.roll(xi, s, 1), pos >= s
            else:
                sr, si, ok = pltpu.roll(xr, width - s, 1), pltpu.roll(xi, width - s, 1), pos < seg - s
            sr = jnp.where(ok, sr, 0.0)
            si = jnp.where(ok, si, 0.0)
            qr, qi = _chunk_multiplier(qt_ref, d, j)
            nxt.append((xr + qr * sr - qi * si, xi + qr * si + qi * sr))
        xs = nxt
    return xs


def _chunk_multiplier(qt_ref, d, j):
    col = d * 8 + j
    return qt_ref[0:SSM_STATE, col:col + 1], qt_ref[SSM_STATE:2 * SSM_STATE, col:col + 1]


def _ssm_kernel(ul_ref, uc_ref, l1_ref, v_ref, qt_ref, y_ref, *, nb):
    n = SSM_STATE
    ul = ul_ref[...]
    width = ul.shape[1]
    seg_l = width // nb
    seg_c = uc_ref.shape[1] // nb
    r = jnp.dot(l1_ref[...], ul.astype(MXU), preferred_element_type=F32)
    rc = jnp.dot(l1_ref[CHUNK_W:, :], uc_ref[...].astype(MXU), preferred_element_type=F32)
    lane = lax.broadcasted_iota(jnp.int32, (n, width), 1)
    ctx = _chunk_scans([(rc[2 * n * d:2 * n * d + n], rc[2 * n * d + n:2 * n * d + 2 * n]) for d in range(2)],
                       qt_ref, int(math.log2(seg_c)), seg_c)
    xs, inj = [], []
    for d, (cr, ci) in enumerate(ctx):
        fwd = d == 0
        base = CHUNK_W + 2 * n * d
        xr, xi = r[base:base + n], r[base + n:base + 2 * n]
        injr = jnp.zeros((n, width), F32)
        inji = jnp.zeros((n, width), F32)
        for b in range(nb):
            src = b * seg_c + (seg_c - 1 if fwd else 0)
            dst = b * seg_l + (0 if fwd else seg_l - 1)
            injr = jnp.where(lane == dst, cr[:, src:src + 1], injr)
            inji = jnp.where(lane == dst, ci[:, src:src + 1], inji)
        q0r, q0i = _chunk_multiplier(qt_ref, d, 0)
        xs.append((xr + q0r * injr - q0i * inji, xi + q0r * inji + q0i * injr))
        inj.append((injr, inji))
    hin = []
    for d, (sr, si) in enumerate(_chunk_scans(xs, qt_ref, int(math.log2(seg_l)), seg_l)):
        fwd = d == 0
        edge = (lane % seg_l) == (0 if fwd else seg_l - 1)
        shift = 1 if fwd else width - 1
        hin.append(jnp.where(edge, inj[d][0], pltpu.roll(sr, shift, 1)))
        hin.append(jnp.where(edge, inj[d][1], pltpu.roll(si, shift, 1)))
    hcat = jnp.concatenate(hin, axis=0).astype(MXU)
    dsk = jnp.concatenate([qt_ref[:, 16:17]] * (CHUNK_W // LANES), axis=0)
    y = r[:CHUNK_W] + jnp.dot(v_ref[...], hcat, preferred_element_type=F32) + dsk * ul
    y_ref[...] = 0.5 * y * (1.0 + jnp.tanh(math.sqrt(2.0 / math.pi) * (y + 0.044715 * (y * y * y))))


def _ssm(ul, uc, l1, vcat, qt, nb):
    g, _, width = ul.shape
    wc = uc.shape[2]
    return pl.pallas_call(
        functools.partial(_ssm_kernel, nb=nb),
        grid=(g,),
        in_specs=[pl.BlockSpec((None, CHUNK_W, width), lambda i: (i, 0, 0)),
                  pl.BlockSpec((None, CHUNK_W, wc), lambda i: (i, 0, 0)),
                  pl.BlockSpec((None, 2 * CHUNK_W, CHUNK_W), lambda i: (i, 0, 0)),
                  pl.BlockSpec((None, CHUNK_W, CHUNK_W), lambda i: (i, 0, 0)),
                  pl.BlockSpec((None, LANES, LANES), lambda i: (i, 0, 0))],
        out_specs=pl.BlockSpec((None, CHUNK_W, width), lambda i: (i, 0, 0)),
        out_shape=jax.ShapeDtypeStruct((g, CHUNK_W, width), F32),
        compiler_params=_cparams(("arbitrary",)),
    )(ul, uc, l1, vcat, qt)


def _route(logits_t, bias):
    ng, ne = N_EXPERT_GROUPS, N_EXPERTS // N_EXPERT_GROUPS
    t = logits_t.shape[1]
    scores = _sigmoid(logits_t).reshape(ng, ne, t)
    biased = scores + bias.reshape(ng, ne, 1)
    iw = lax.broadcasted_iota(jnp.int32, (ng, ne, t), 1)
    ig = lax.broadcasted_iota(jnp.int32, (ng, ne, t), 0)
    neg = -jnp.inf
    m1 = jnp.max(biased, axis=1, keepdims=True)
    i1 = jnp.min(jnp.where(biased == m1, iw, ne), axis=1, keepdims=True)
    m2 = jnp.max(jnp.where(iw == i1, neg, biased), axis=1, keepdims=True)
    gscore = jnp.broadcast_to(m1 + m2, (ng, ne, t))
    gsel = jnp.zeros((ng, ne, t), F32)
    cur = gscore
    for _ in range(TOPK_GROUPS):
        m = jnp.max(cur, axis=0, keepdims=True)
        gi = jnp.min(jnp.where(cur == m, ig, ng), axis=0, keepdims=True)
        pick = ig == gi
        gsel = jnp.where(pick, 1.0, gsel)
        cur = jnp.where(pick, neg, cur)
    cur = jnp.where(gsel > 0.0, biased, neg)
    flat = ig * ne + iw
    chosen = jnp.zeros((ng, ne, t), F32)
    for _ in range(TOP_K):
        m = jnp.max(jnp.max(cur, axis=1, keepdims=True), axis=0, keepdims=True)
        fi = jnp.min(jnp.min(jnp.where(cur == m, flat, N_EXPERTS), axis=1, keepdims=True), axis=0, keepdims=True)
        pick = flat == fi
        chosen = jnp.where(pick, 1.0, chosen)
        cur = jnp.where(pick, neg, cur)
    sel = jnp.where(chosen > 0.0, scores, 0.0)
    tot = jnp.sum(jnp.sum(sel, axis=1, keepdims=True), axis=0, keepdims=True)
    return (sel / tot * ROUTED_SCALE).reshape(N_EXPERTS, t)


def _mix_kernel(yt_ref, attn_ref, x_ref, mod_ref, wglu_ref, bglu_ref, gssm_ref, wouts_ref, gattn_ref, wouta_ref,
                nffn_ref, wrt_ref, rbias_ref, x1_ref, gates_ref, *ynat):
    nch = yt_ref.shape[3]
    part = pl.program_id(1)

    @pl.when(part == 0)
    def _():
        for t in range(CHUNK):
            y_t = yt_ref[:, t].reshape(SSM_WIDTH, nch).T
            for j, ref in enumerate(ynat):
                ref[pl.ds(t, nch, stride=CHUNK), :] = y_t[:, j * LANES:(j + 1) * LANES]

    base = part * x_ref.shape[0]
    g1 = mod_ref[2:3, :]
    def body(r, carry):
        r0 = pl.multiple_of(r * MIX_ROWS, MIX_ROWS)
        y = _rows(ynat, pl.ds(pl.multiple_of(base + r0, MIX_ROWS), MIX_ROWS))
        glu = jnp.dot(y.astype(MXU), wglu_ref[...], preferred_element_type=F32) + bglu_ref[...]
        z = y * _sigmoid(glu)
        zn = z * lax.rsqrt(jnp.mean(z * z, axis=-1, keepdims=True) + EPS) * gssm_ref[...]
        o_s = jnp.dot(zn.astype(MXU), wouts_ref[...], preferred_element_type=F32)
        a = attn_ref[pl.ds(r0, MIX_ROWS), :]
        an = a * lax.rsqrt(jnp.mean(a * a, axis=-1, keepdims=True) + EPS) * gattn_ref[...]
        o_a = jnp.dot(an.astype(MXU), wouta_ref[...], preferred_element_type=F32)
        x1 = x_ref[pl.ds(r0, MIX_ROWS), :] + g1 * (o_s + o_a)
        x1_ref[pl.ds(r0, MIX_ROWS), :] = x1
        h2 = _ffn_input(x1, mod_ref, nffn_ref)
        logits_t = lax.dot_general(wrt_ref[...], h2, (((1,), (1,)), ((), ())),
                                   precision=HIGHEST, preferred_element_type=F32)
        gates_t = _route(logits_t, rbias_ref[...])
        gates_ref[:, pl.ds(r0, MIX_ROWS)] = gates_t
        return carry

    lax.fori_loop(0, x_ref.shape[0] // MIX_ROWS, body, 0)


def _ffn_input(x1, mod_ref, nffn_ref):
    h2 = x1 * lax.rsqrt(jnp.mean(x1 * x1, axis=-1, keepdims=True) + EPS) * nffn_ref[...]
    return h2 * (1.0 + mod_ref[4:5, :]) + mod_ref[3:4, :]


def _mix(yt, attn, x, mod3, wglu, bglu, gssm, wouts, gattn, wouta, nffn, wrt, rbias):
    nb, seq, d = x.shape
    nch = seq // CHUNK
    yt4 = yt.reshape(SSM_GROUPS, CHUNK, SSM_GROUP, nb * nch)
    c2 = lambda b, p: (0, 0)
    tok = lambda width: pl.BlockSpec((None, MIX_TOKENS, width), lambda b, p: (b, p, 0))
    return pl.pallas_call(
        _mix_kernel,
        grid=(nb, seq // MIX_TOKENS),
        in_specs=[pl.BlockSpec((SSM_GROUPS, CHUNK, SSM_GROUP, nch), lambda b, p: (0, 0, 0, b)),
                  tok(ATTN_WIDTH), tok(d),
                  pl.BlockSpec((None, N_MOD, d), lambda b, p: (b, 0, 0)),
                  pl.BlockSpec(wglu.shape, c2), pl.BlockSpec(bglu.shape, c2), pl.BlockSpec(gssm.shape, c2),
                  pl.BlockSpec(wouts.shape, c2), pl.BlockSpec(gattn.shape, c2), pl.BlockSpec(wouta.shape, c2),
                  pl.BlockSpec(nffn.shape, c2), pl.BlockSpec(wrt.shape, c2), pl.BlockSpec(rbias.shape, c2)],
        out_specs=[tok(d), pl.BlockSpec((None, N_EXPERTS, MIX_TOKENS), lambda b, p: (b, 0, p))],
        out_shape=[jax.ShapeDtypeStruct((nb, seq, d), F32), jax.ShapeDtypeStruct((nb, N_EXPERTS, seq), F32)],
        scratch_shapes=[pltpu.VMEM((seq, LANES), F32)] * (SSM_WIDTH // LANES),
        compiler_params=_cparams(("arbitrary", "arbitrary")),
    )(yt4, attn, x, mod3, wglu, bglu, gssm, wouts, gattn, wouta, nffn, wrt, rbias)


def _moe_kernel(x1_ref, gates_ref, mod_ref, nffn_ref, wg_ref, wu_ref, wd_ref, wsg_ref, wsu_ref, wsd_ref, nfin_ref,
                o_ref, h2_ref, rank_ref, xg_ref, yg_ref, gsel_ref, gs_ref, rounds_ref):
    s = pl.program_id(1)
    f = EXPERT_DIM
    nsub = MOE_TILE // MOE_SUB
    ne = EXPERTS_PER_STEP
    tn = (((0,), (0,)), ((), ()))

    def glu(h, wg, wu):
        gu = jnp.dot(h, jnp.concatenate([wg.astype(MXU), wu.astype(MXU)], axis=1), preferred_element_type=F32)
        g = gu[:, :f]
        return g * _sigmoid(g) * gu[:, f:]

    @pl.when(s == 0)
    def _():
        h2_ref[...] = _ffn_input(x1_ref[...], mod_ref, nffn_ref).astype(MXU)
        hs = glu(h2_ref[...], wsg_ref[...], wsu_ref[...])
        o_ref[...] = jnp.dot(hs.astype(MXU), wsd_ref[...].astype(MXU), preferred_element_type=F32)
        earlier = (lax.broadcasted_iota(jnp.int32, (MOE_SUB, MOE_SUB), 0)
                   < lax.broadcasted_iota(jnp.int32, (MOE_SUB, MOE_SUB), 1))
        tri = jnp.where(earlier, 1.0, 0.0).astype(MXU)
        most = jnp.zeros((N_EXPERTS, MOE_SUB), F32)
        for sb in range(nsub):
            sel = gates_ref[:, sb * MOE_SUB:(sb + 1) * MOE_SUB] > 0.0
            self32 = jnp.where(sel, 1.0, 0.0)
            rank = jnp.dot(self32.astype(MXU), tri, preferred_element_type=F32)
            rank_ref[:, sb * MOE_SUB:(sb + 1) * MOE_SUB] = jnp.where(sel, rank, -1.0)
            most = jnp.maximum(most, rank + self32)
        rounds_ref[0] = (jnp.max(most).astype(jnp.int32) + (MOE_CAP - 1)) // MOE_CAP

    e0 = pl.multiple_of(s * ne, ne)
    rank8 = rank_ref[pl.ds(e0, ne), :]
    gate8 = gates_ref[pl.ds(e0, ne), :]
    slot = lax.broadcasted_iota(jnp.int32, (MOE_CAP, MOE_SUB), 0).astype(F32)

    def serve(r, carry):
        first = (r * MOE_CAP).astype(F32)
        for sb in range(nsub):
            rk = rank8[:, sb * MOE_SUB:(sb + 1) * MOE_SUB] - first
            gt = gate8[:, sb * MOE_SUB:(sb + 1) * MOE_SUB]
            onehots = []
            for e in range(ne):
                hit = jnp.where(rk[e:e + 1, :] == slot, 1.0, 0.0)
                weight = jnp.sum(hit * gt[e:e + 1, :], axis=1, keepdims=True)
                gs_ref[e, sb * MOE_CAP:(sb + 1) * MOE_CAP, :] = jnp.broadcast_to(weight, (MOE_CAP, f))
                onehots.append(hit.astype(MXU))
            gsel = jnp.concatenate(onehots, axis=0)
            gsel_ref[sb] = gsel
            rows = jnp.dot(gsel, h2_ref[sb * MOE_SUB:(sb + 1) * MOE_SUB, :], preferred_element_type=F32)
            for e in range(ne):
                xg_ref[e, sb * MOE_CAP:(sb + 1) * MOE_CAP, :] = rows[e * MOE_CAP:(e + 1) * MOE_CAP].astype(MXU)
        for e in range(ne):
            hid = glu(xg_ref[e], wg_ref[e], wu_ref[e]) * gs_ref[e]
            yg_ref[e] = jnp.dot(hid.astype(MXU), wd_ref[e * f:(e + 1) * f, :].astype(MXU),
                                preferred_element_type=F32).astype(MXU)
        for sb in range(nsub):
            y = jnp.concatenate([yg_ref[e, sb * MOE_CAP:(sb + 1) * MOE_CAP, :] for e in range(ne)], axis=0)
            o_ref[sb * MOE_SUB:(sb + 1) * MOE_SUB, :] += lax.dot_general(
                gsel_ref[sb], y, tn, preferred_element_type=F32)
        return carry

    lax.fori_loop(0, rounds_ref[0], serve, 0)

    @pl.when(s == pl.num_programs(1) - 1)
    def _():
        g2 = mod_ref[5:6, :]
        x2 = x1_ref[...] + g2 * o_ref[...]
        o_ref[...] = x2 * lax.rsqrt(jnp.mean(x2 * x2, axis=-1, keepdims=True) + EPS) * nfin_ref[...]


def _moe(x1, gates, mod3, nffn, wg, wu, wd, wsg, wsu, wsd, nfin):
    nb, seq, d = x1.shape
    tiles_per_b = seq // MOE_TILE
    nsteps = N_EXPERTS // EXPERTS_PER_STEP
    nsub = MOE_TILE // MOE_SUB
    c2 = lambda i, s: (0, 0)
    tok = pl.BlockSpec((None, MOE_TILE, d), lambda i, s: (i // tiles_per_b, i % tiles_per_b, 0))
    experts = pl.BlockSpec((EXPERTS_PER_STEP, d, EXPERT_DIM), lambda i, s: (s, 0, 0))
    return pl.pallas_call(
        _moe_kernel,
        grid=(nb * tiles_per_b, nsteps),
        in_specs=[tok,
                  pl.BlockSpec((None, N_EXPERTS, MOE_TILE), lambda i, s: (i // tiles_per_b, 0, i % tiles_per_b)),
                  pl.BlockSpec((None, N_MOD, d), lambda i, s: (i // tiles_per_b, 0, 0)),
                  pl.BlockSpec(nffn.shape, c2), experts, experts,
                  pl.BlockSpec((EXPERTS_PER_STEP * EXPERT_DIM, d), lambda i, s: (s, 0)),
                  pl.BlockSpec(wsg.shape, c2), pl.BlockSpec(wsu.shape, c2), pl.BlockSpec(wsd.shape, c2),
                  pl.BlockSpec(nfin.shape, c2)],
        out_specs=tok,
        out_shape=jax.ShapeDtypeStruct((nb, seq, d), F32),
        scratch_shapes=[pltpu.VMEM((MOE_TILE, d), MXU),
                        pltpu.VMEM((N_EXPERTS, MOE_TILE), F32),
                        pltpu.VMEM((EXPERTS_PER_STEP, nsub * MOE_CAP, d), MXU),
                        pltpu.VMEM((EXPERTS_PER_STEP, nsub * MOE_CAP, d), MXU),
                        pltpu.VMEM((nsub, EXPERTS_PER_STEP * MOE_CAP, MOE_SUB), MXU),
                        pltpu.VMEM((EXPERTS_PER_STEP, nsub * MOE_CAP, EXPERT_DIM), F32),
                        pltpu.SMEM((1,), jnp.int32)],
        compiler_params=_cparams(("arbitrary", "arbitrary")),
    )(x1, gates, mod3, nffn, wg, wu, wd, wsg, wsu, wsd, nfin)


def _rope_tables(seq):
    pos = jnp.arange(seq)
    row = (pos // GRID_W).astype(F32)
    col = (pos % GRID_W).astype(F32)
    inv = ROPE_THETA ** (-jnp.arange(ROT_PAIRS, dtype=F32) / ROT_PAIRS)
    ar, ac = row[:, None] * inv, col[:, None] * inv
    zero = jnp.zeros_like(ar)
    rep = LANES // HEAD_DIM
    cos_t = jnp.tile(jnp.concatenate([jnp.cos(ar), jnp.cos(ar), jnp.cos(ac), jnp.cos(ac)], axis=1), (1, rep))
    s1_t = jnp.tile(jnp.concatenate([-jnp.sin(ar), zero, -jnp.sin(ac), zero], axis=1), (1, rep))
    s2_t = jnp.tile(jnp.concatenate([zero, jnp.sin(ar), zero, jnp.sin(ac)], axis=1), (1, rep))
    return cos_t, s1_t, s2_t


def kernel(x, c, ctx, c_ctx, w_ada, b_ada, norm_mix, norm_ffn, w_in, attn_sink, ssm_a_re, ssm_a_im, ssm_log_dt, ssm_b_re, ssm_b_im, ssm_c_re, ssm_c_im, ssm_d, w_glu, b_glu, norm_attn_out, norm_ssm_out, w_out, w_router, router_bias, w_gate_e, w_up_e, w_down_e, w_gate_s, w_up_s, w_down_s, norm_final):
    nb, seq, d = x.shape
    nctx = ctx.shape[1]
    layer = 0

    pad = jnp.zeros((16 - nb - 1, d), F32)
    c_all = jnp.concatenate([c, c_ctx[None, :], pad], axis=0)
    mod3 = _ada(c_all, w_ada[layer], b_ada[layer]).reshape(16, N_MOD, d)

    gq = ATTN_HEADS // ATTN_KV_HEADS
    heads = jnp.arange(ATTN_HEADS).reshape(ATTN_KV_HEADS, gq).T.reshape(-1)
    perm_q = (heads[:, None] * HEAD_DIM + jnp.arange(HEAD_DIM)[None, :]).reshape(-1)
    w_in0 = w_in[layer]
    win = jnp.concatenate([w_in0[:, :ATTN_WIDTH][:, perm_q], w_in0[:, ATTN_WIDTH:]], axis=1).astype(MXU)
    nw = norm_mix[layer].reshape(1, d)
    cos_t, s1_t, s2_t = _rope_tables(seq)

    q, k, vt, ul = _proj(x, mod3, 0, nw, win, cos_t, s1_t, s2_t, True)
    _, kc, vct, uc = _proj(ctx.reshape(1, nb * nctx, d), mod3, nb, nw, win, cos_t, s1_t, s2_t, False)
    kc = kc.reshape(nb, nctx, KV_WIDTH)

    attn = _attention(attn_sink[layer], q, k, vt, kc, vct)

    l1, vcat, qt = _ssm_prep(ssm_a_re[layer], ssm_a_im[layer], ssm_log_dt[layer], ssm_b_re[layer],
                             ssm_b_im[layer], ssm_c_re[layer], ssm_c_im[layer], ssm_d[layer])
    yt = _ssm(ul, uc, l1, vcat, qt, nb)

    w_out0 = w_out[layer]
    nffn = norm_ffn[layer].reshape(1, d)
    x1, gates = _mix(
        yt, attn, x, mod3,
        w_glu[layer].astype(MXU), b_glu[layer].reshape(1, SSM_WIDTH), norm_ssm_out[layer].reshape(1, SSM_WIDTH),
        w_out0[ATTN_WIDTH:].astype(MXU), norm_attn_out[layer][perm_q].reshape(1, ATTN_WIDTH),
        w_out0[:ATTN_WIDTH][perm_q].astype(MXU), nffn,
        w_router[layer].T, router_bias[layer].reshape(N_EXPERTS, 1))

    wd = w_down_e[layer].reshape(N_EXPERTS * EXPERT_DIM, d)
    return _moe(x1, gates, mod3, nffn, w_gate_e[layer], w_up_e[layer], wd,
                w_gate_s[layer], w_up_s[layer], w_down_s[layer], norm_final.reshape(1, d))
```

```python
import functools
import math

import jax
import jax.numpy as jnp
from jax import lax
from jax.experimental import pallas as pl
from jax.experimental.pallas import tpu as pltpu

D_MODEL = 1024
EPS = 1e-6
N_MOD = 6
HEAD_DIM = 64
ATTN_HEADS = 8
ATTN_KV_HEADS = 2
ATTN_WIDTH = ATTN_HEADS * HEAD_DIM
KV_WIDTH = ATTN_KV_HEADS * HEAD_DIM
WINDOW = 128
ATTN_SCALE = HEAD_DIM ** -0.5
LOG2E = math.log2(math.e)
ROPE_THETA = 10000.0
ROT_PAIRS = HEAD_DIM // 4
GRID_W = 64
SSM_WIDTH = D_MODEL - ATTN_WIDTH
SSM_GROUP = 16
SSM_GROUPS = SSM_WIDTH // SSM_GROUP
SSM_STATE = 64
N_EXPERTS = 64
EXPERT_DIM = 128
TOP_K = 8
N_EXPERT_GROUPS = 8
TOPK_GROUPS = 4
ROUTED_SCALE = 2.5

CHUNK = 16
CHUNK_W = CHUNK * SSM_GROUP
LANES = 128
ATTN_QBLOCKS = 8
MIX_TOKENS = 1024
MIX_ROWS = 512
MOE_TILE = 1024
EXPERTS_PER_STEP = 8
VMEM_LIMIT = 56 * 1024 * 1024

MXU = jnp.bfloat16
F32 = jnp.float32
HIGHEST = lax.Precision.HIGHEST


def _sigmoid(x):
    return 0.5 * jnp.tanh(0.5 * x) + 0.5


def _cparams(sem):
    return pltpu.CompilerParams(dimension_semantics=sem, vmem_limit_bytes=VMEM_LIMIT)


def _ada_kernel(c_ref, w_ref, b_ref, o_ref):
    cv = c_ref[...]
    s = cv * _sigmoid(cv)
    o_ref[...] = jnp.dot(s, w_ref[...], precision=HIGHEST, preferred_element_type=F32) + b_ref[...]


def _ada(c_all, w, b):
    rows, d = c_all.shape
    n = w.shape[1]
    tn = 1024
    return pl.pallas_call(
        _ada_kernel,
        grid=(n // tn,),
        in_specs=[pl.BlockSpec((rows, d), lambda j: (0, 0)),
                  pl.BlockSpec((d, tn), lambda j: (0, j)),
                  pl.BlockSpec((1, tn), lambda j: (0, j))],
        out_specs=pl.BlockSpec((rows, tn), lambda j: (0, j)),
        out_shape=jax.ShapeDtypeStruct((rows, n), F32),
        compiler_params=_cparams(("arbitrary",)),
    )(c_all, w, b.reshape(1, n))


def _ssm_prep_kernel(are_ref, aim_ref, ldt_ref, btr_ref, bti_ref, cr_ref, ci_ref, dsk_ref,
                     l1_ref, v_ref, qt_ref):
    p, n = SSM_GROUP, SSM_STATE
    ar, ai = are_ref[...], aim_ref[...]
    dt = jnp.exp(ldt_ref[...])
    mag = jnp.exp(dt * ar)
    abr = mag * jnp.cos(dt * ai)
    abi = mag * jnp.sin(dt * ai)
    den = ar * ar + ai * ai
    nr = abr - 1.0
    cor = (nr * ar + abi * ai) / den
    coi = (abi * ar - nr * ai) / den
    btr, bti = btr_ref[...], bti_ref[...]
    bbr = cor[:, None, :] * btr - coi[:, None, :] * bti
    bbi = cor[:, None, :] * bti + coi[:, None, :] * btr
    cr, ci = cr_ref[...], ci_ref[...]
    pr, pi = jnp.ones_like(abr), jnp.zeros_like(abr)
    cps, pbs = [], []
    for k in range(CHUNK + 1):
        cpr = cr * pr[:, None, :] - ci * pi[:, None, :]
        cpi = cr * pi[:, None, :] + ci * pr[:, None, :]
        cps.append(jnp.concatenate([cpr, -cpi], axis=-1))
        if k < CHUNK:
            pbs.append(jnp.concatenate([bbr * pr[:, None, :] - bbi * pi[:, None, :],
                                        bbr * pi[:, None, :] + bbi * pr[:, None, :]], axis=-1))
            pr, pi = pr * abr - pi * abi, pr * abi + pi * abr
    qr, qi = pr, pi

    zero = jnp.zeros((p, 2 * n), F32)
    rows = [jnp.concatenate([zero, cps[k][1]], axis=1) for k in range(CHUNK - 1, 0, -1)]
    rows.append(jnp.concatenate([cps[0][0], cps[0][1]], axis=1))
    rows += [jnp.concatenate([cps[k][0], zero], axis=1) for k in range(1, CHUNK)]
    rows.append(jnp.concatenate([zero, zero], axis=1))
    stack = jnp.concatenate(rows, axis=0)
    bbcat = jnp.concatenate([bbr[0], bbi[0], bbr[1], bbi[1]], axis=-1)
    kwt = lax.dot_general(bbcat, stack, (((1,), (1,)), ((), ())), precision=HIGHEST, preferred_element_type=F32)
    toep_t = jnp.concatenate(
        [kwt[:, (CHUNK - 1 - t) * p:(CHUNK - 1 - t) * p + CHUNK_W] for t in range(CHUNK)], axis=0)
    wst_f = jnp.concatenate([pbs[CHUNK - 1 - t][0] for t in range(CHUNK)], axis=0)
    wst_b = jnp.concatenate([pbs[t][1] for t in range(CHUNK)], axis=0)
    l1_ref[...] = jnp.concatenate([toep_t.T, wst_f.T, wst_b.T], axis=0).astype(MXU)
    v_f = jnp.concatenate([cps[t + 1][0] for t in range(CHUNK)], axis=0)
    v_b = jnp.concatenate([cps[CHUNK - t][1] for t in range(CHUNK)], axis=0)
    v_ref[...] = jnp.concatenate([v_f, v_b], axis=1).astype(MXU)

    qrows = []
    for j in range(8):
        qrows.append(jnp.concatenate([qr, qi], axis=-1))
        qr, qi = qr * qr - qi * qi, 2.0 * qr * qi
    qmat = jnp.concatenate([qrows[j][d:d + 1] for d in range(2) for j in range(8)] + [dsk_ref[...]], axis=0)
    qpad = jnp.concatenate([qmat, jnp.zeros((LANES - qmat.shape[0], 2 * n), F32)], axis=0)
    qt_ref[...] = qpad.T


def _ssm_prep(a_re, a_im, log_dt, b_re, b_im, c_re, c_im, d_skip):
    g, n, p = SSM_GROUPS, SSM_STATE, SSM_GROUP
    are = jnp.transpose(a_re, (1, 0, 2))
    aim = jnp.transpose(a_im, (1, 0, 2))
    ldt = jnp.transpose(log_dt, (1, 0))[..., None]
    btr = jnp.transpose(b_re, (1, 0, 3, 2))
    bti = jnp.transpose(b_im, (1, 0, 3, 2))
    cr = jnp.transpose(c_re, (1, 0, 2, 3))
    ci = jnp.transpose(c_im, (1, 0, 2, 3))
    dsk = jnp.tile(d_skip.reshape(g, 1, p), (1, 1, 2 * n // p))
    vec = pl.BlockSpec((None, 2, n), lambda i: (i, 0, 0))
    mat = pl.BlockSpec((None, 2, p, n), lambda i: (i, 0, 0, 0))
    return pl.pallas_call(
        _ssm_prep_kernel,
        grid=(g,),
        in_specs=[vec, vec, pl.BlockSpec((None, 2, 1), lambda i: (i, 0, 0)), mat, mat, mat, mat,
                  pl.BlockSpec((None, 1, 2 * n), lambda i: (i, 0, 0))],
        out_specs=[pl.BlockSpec((None, 2 * CHUNK_W, CHUNK_W), lambda i: (i, 0, 0)),
                   pl.BlockSpec((None, CHUNK_W, CHUNK_W), lambda i: (i, 0, 0)),
                   pl.BlockSpec((None, LANES, LANES), lambda i: (i, 0, 0))],
        out_shape=[jax.ShapeDtypeStruct((g, 2 * CHUNK_W, CHUNK_W), MXU),
                   jax.ShapeDtypeStruct((g, CHUNK_W, CHUNK_W), MXU),
                   jax.ShapeDtypeStruct((g, LANES, LANES), F32)],
        compiler_params=_cparams(("arbitrary",)),
    )(are, aim, ldt, btr, bti, cr, ci, dsk)


def _rows(slabs, idx):
    return jnp.concatenate([s[idx, :] for s in slabs], axis=1)


def _proj_kernel(x_ref, mod_ref, nw_ref, win_ref, cos_ref, s1_ref, s2_ref, q_ref, k_ref, vt_ref, ut_ref, *u_nat,
                 rope):
    ntok = x_ref.shape[0]
    nch = ntok // CHUNK
    sh = mod_ref[0:1, :]
    sc = mod_ref[1:2, :]
    nw = nw_ref[...]
    rb = 256

    def rot(tile, r0):
        if not rope:
            return tile
        cs = cos_ref[pl.ds(r0, rb), :]
        a1 = s1_ref[pl.ds(r0, rb), :]
        a2 = s2_ref[pl.ds(r0, rb), :]
        return tile * cs + pltpu.roll(tile, LANES - ROT_PAIRS, 1) * a1 + pltpu.roll(tile, ROT_PAIRS, 1) * a2

    def body(r, carry):
        r0 = pl.multiple_of(r * rb, rb)
        xb = x_ref[pl.ds(r0, rb), :]
        hn = xb * lax.rsqrt(jnp.mean(xb * xb, axis=-1, keepdims=True) + EPS) * nw
        hb = (hn * (1.0 + sc) + sh).astype(MXU)
        proj = jnp.dot(hb, win_ref[...], preferred_element_type=F32)
        for j in range(ATTN_WIDTH // LANES):
            qj = rot(proj[:, j * LANES:(j + 1) * LANES], r0) * (ATTN_SCALE * LOG2E)
            q_ref[pl.ds(r0, rb), j * LANES:(j + 1) * LANES] = qj.astype(MXU)
        k_ref[pl.ds(r0, rb), :] = rot(proj[:, ATTN_WIDTH:ATTN_WIDTH + KV_WIDTH], r0).astype(MXU)
        vt_ref[:, pl.ds(r0, rb)] = proj[:, ATTN_WIDTH + KV_WIDTH:ATTN_WIDTH + 2 * KV_WIDTH].T.astype(MXU)
        u0 = ATTN_WIDTH + 2 * KV_WIDTH
        for j, ref in enumerate(u_nat):
            ref[pl.ds(r0, rb), :] = proj[:, u0 + j * LANES:u0 + (j + 1) * LANES]
        return carry

    lax.fori_loop(0, ntok // rb, body, 0)

    for t in range(CHUNK):
        u_t = _rows(u_nat, pl.ds(t, nch, stride=CHUNK))
        ut_ref[:, t * SSM_GROUP:(t + 1) * SSM_GROUP, :] = u_t.T.reshape(SSM_GROUPS, SSM_GROUP, nch)


def _proj(x, mod3, mod_row0, nw, win, cos_t, s1_t, s2_t, rope):
    nb, ntok, d = x.shape
    nch = ntok // CHUNK
    const2 = lambda b: (0, 0)
    tok = lambda width: pl.BlockSpec((None, ntok, width), lambda b: (b, 0, 0))
    tables = (cos_t, s1_t, s2_t)
    return pl.pallas_call(
        functools.partial(_proj_kernel, rope=rope),
        grid=(nb,),
        in_specs=[tok(d),
                  pl.BlockSpec((None, N_MOD, d), lambda b: (b + mod_row0, 0, 0)),
                  pl.BlockSpec((1, d), const2),
                  pl.BlockSpec(win.shape, const2)] + [pl.BlockSpec(t.shape, const2) for t in tables],
        out_specs=[tok(ATTN_WIDTH), tok(KV_WIDTH), pl.BlockSpec((None, KV_WIDTH, ntok), lambda b: (b, 0, 0)),
                   pl.BlockSpec((SSM_GROUPS, CHUNK_W, nch), lambda b: (0, 0, b))],
        out_shape=[jax.ShapeDtypeStruct((nb, ntok, ATTN_WIDTH), MXU),
                   jax.ShapeDtypeStruct((nb, ntok, KV_WIDTH), MXU),
                   jax.ShapeDtypeStruct((nb, KV_WIDTH, ntok), MXU),
                   jax.ShapeDtypeStruct((SSM_GROUPS, CHUNK_W, nb * nch), F32)],
        scratch_shapes=[pltpu.VMEM((ntok, LANES), F32)] * (SSM_WIDTH // LANES),
        compiler_params=_cparams(("arbitrary",)),
    )(x, mod3, nw, win, *tables)


def _attn_kernel(sink_ref, q_ref, k_ref, vt_ref, kc_ref, vct_ref, bias_ref, o_ref):
    blk = WINDOW
    seq = k_ref.shape[0]
    nwin = 3 * blk
    gq = ATTN_HEADS // ATTN_KV_HEADS
    nsub = q_ref.shape[0] // blk
    kc = kc_ref[...]
    ones = jnp.ones((16, 1), MXU)
    vct = jnp.concatenate([vct_ref[...], jnp.broadcast_to(ones, (16, kc.shape[0]))], axis=0)
    lane_q = lax.broadcasted_iota(jnp.int32, (blk, LANES), 1)
    nt = (((1,), (1,)), ((), ()))
    chains = [(sub, kh) for sub in range(nsub) for kh in range(ATTN_KV_HEADS)]

    starts, scores = {}, {}
    for sub in range(nsub):
        i = pl.program_id(1) * nsub + sub
        starts[sub] = (i, pl.multiple_of(jnp.clip((i - 1) * blk, 0, seq - nwin), blk))
    for sub, kh in chains:
        i, start = starts[sub]
        q = q_ref[sub * blk:(sub + 1) * blk, :]
        qs = jnp.concatenate(
            [jnp.where((lane_q // HEAD_DIM) == kh, q[:, j * LANES:(j + 1) * LANES], jnp.zeros((), MXU))
             for j in range(gq)], axis=0)
        s_loc = lax.dot_general(k_ref[pl.ds(start, nwin), :], qs, nt, preferred_element_type=F32)
        s_loc = s_loc + bias_ref[(i * blk - start) // blk]
        s_ctx = lax.dot_general(kc, qs, nt, preferred_element_type=F32)
        scores[sub, kh] = (s_loc, s_ctx)

    probs = {}
    for sub, kh in chains:
        s_loc, s_ctx = scores[sub, kh]
        sink = jnp.concatenate(
            [jnp.full((1, blk), sink_ref[kh * gq + j] * LOG2E, F32) for j in range(gq)], axis=1)
        m = jnp.maximum(jnp.maximum(jnp.max(s_loc, axis=0, keepdims=True),
                                    jnp.max(s_ctx, axis=0, keepdims=True)), sink)
        probs[sub, kh] = (jnp.exp2(s_loc - m).astype(MXU), jnp.exp2(s_ctx - m).astype(MXU), jnp.exp2(sink - m))

    outs = {}
    for sub, kh in chains:
        p_loc, p_ctx, p_sink = probs[sub, kh]
        _, start = starts[sub]
        vtw = jnp.concatenate([vt_ref[:, pl.ds(start, nwin)], jnp.broadcast_to(ones, (16, nwin))], axis=0)
        acc = (jnp.dot(vtw, p_loc, preferred_element_type=F32)
               + jnp.dot(vct, p_ctx, preferred_element_type=F32))
        outs[sub, kh] = acc[:KV_WIDTH] / (acc[KV_WIDTH:KV_WIDTH + 1] + p_sink)

    top = lax.broadcasted_iota(jnp.int32, (KV_WIDTH, blk), 0) < HEAD_DIM
    for sub in range(nsub):
        for j in range(gq):
            both = jnp.where(top, outs[sub, 0][:, j * blk:(j + 1) * blk], outs[sub, 1][:, j * blk:(j + 1) * blk])
            o_ref[sub * blk:(sub + 1) * blk, j * LANES:(j + 1) * LANES] = both.T


def _band_bias():
    gq = ATTN_HEADS // ATTN_KV_HEADS
    key = jnp.arange(3 * WINDOW)[None, :, None]
    qry = (jnp.arange(gq * WINDOW) % WINDOW)[None, None, :]
    var = jnp.arange(3)[:, None, None]
    inside = jnp.abs(key - var * WINDOW - qry) <= WINDOW
    return jnp.where(inside, 0.0, -jnp.inf).astype(F32)


def _attention(sink, q, k, vt, kc, vct):
    nb, seq, _ = q.shape
    nctx = kc.shape[1]
    blk = ATTN_QBLOCKS * WINDOW
    bias = _band_bias()
    return pl.pallas_call(
        _attn_kernel,
        grid=(nb, seq // blk),
        in_specs=[pl.BlockSpec(memory_space=pltpu.SMEM),
                  pl.BlockSpec((None, blk, ATTN_WIDTH), lambda b, i: (b, i, 0)),
                  pl.BlockSpec((None, seq, KV_WIDTH), lambda b, i: (b, 0, 0)),
                  pl.BlockSpec((None, KV_WIDTH, seq), lambda b, i: (b, 0, 0)),
                  pl.BlockSpec((None, nctx, KV_WIDTH), lambda b, i: (b, 0, 0)),
                  pl.BlockSpec((None, KV_WIDTH, nctx), lambda b, i: (0, 0, b)),
                  pl.BlockSpec(bias.shape, lambda b, i: (0, 0, 0))],
        out_specs=pl.BlockSpec((None, blk, ATTN_WIDTH), lambda b, i: (b, i, 0)),
        out_shape=jax.ShapeDtypeStruct((nb, seq, ATTN_WIDTH), F32),
        compiler_params=_cparams(("arbitrary", "arbitrary")),
    )(sink, q, k, vt, kc, vct, bias)


def _chunk_scans(xs, qt_ref, nsteps, seg):
    width = xs[0][0].shape[1]
    pos = lax.broadcasted_iota(jnp.int32, xs[0][0].shape, 1) % seg
    for j in range(nsteps):
        s = 1 << j
        nxt = []
        for d, (xr, xi) in enumerate(xs):
            if d == 0:
                sr, si, ok = pltpu.roll(xr, s, 1), pltpu.roll(xi, s, 1), pos >= s
            else:
                sr, si, ok = pltpu.roll(xr, width - s, 1), pltpu.roll(xi, width - s, 1), pos < seg - s
            sr = jnp.where(ok, sr, 0.0)
            si = jnp.where(ok, si, 0.0)
            qr, qi = _chunk_multiplier(qt_ref, d, j)
            nxt.append((xr + qr * sr - qi * si, xi + qr * si + qi * sr))
        xs = nxt
    return xs


def _chunk_multiplier(qt_ref, d, j):
    col = d * 8 + j
    return qt_ref[0:SSM_STATE, col:col + 1], qt_ref[SSM_STATE:2 * SSM_STATE, col:col + 1]


def _ssm_kernel(ul_ref, uc_ref, l1_ref, v_ref, qt_ref, y_ref, *, nb):
    n = SSM_STATE
    ul = ul_ref[...]
    width = ul.shape[1]
    seg_l = width // nb
    seg_c = uc_ref.shape[1] // nb
    r = jnp.dot(l1_ref[...], ul.astype(MXU), preferred_element_type=F32)
    rc = jnp.dot(l1_ref[CHUNK_W:, :], uc_ref[...].astype(MXU), preferred_element_type=F32)
    lane = lax.broadcasted_iota(jnp.int32, (n, width), 1)
    ctx = _chunk_scans([(rc[2 * n * d:2 * n * d + n], rc[2 * n * d + n:2 * n * d + 2 * n]) for d in range(2)],
                       qt_ref, int(math.log2(seg_c)), seg_c)
    xs, inj = [], []
    for d, (cr, ci) in enumerate(ctx):
        fwd = d == 0
        base = CHUNK_W + 2 * n * d
        xr, xi = r[base:base + n], r[base + n:base + 2 * n]
        injr = jnp.zeros((n, width), F32)
        inji = jnp.zeros((n, width), F32)
        for b in range(nb):
            src = b * seg_c + (seg_c - 1 if fwd else 0)
            dst = b * seg_l + (0 if fwd else seg_l - 1)
            injr = jnp.where(lane == dst, cr[:, src:src + 1], injr)
            inji = jnp.where(lane == dst, ci[:, src:src + 1], inji)
        q0r, q0i = _chunk_multiplier(qt_ref, d, 0)
        xs.append((xr + q0r * injr - q0i * inji, xi + q0r * inji + q0i * injr))
        inj.append((injr, inji))
    hin = []
    for d, (sr, si) in enumerate(_chunk_scans(xs, qt_ref, int(math.log2(seg_l)), seg_l)):
        fwd = d == 0
        edge = (lane % seg_l) == (0 if fwd else seg_l - 1)
        shift = 1 if fwd else width - 1
        hin.append(jnp.where(edge, inj[d][0], pltpu.roll(sr, shift, 1)))
        hin.append(jnp.where(edge, inj[d][1], pltpu.roll(si, shift, 1)))
    hcat = jnp.concatenate(hin, axis=0).astype(MXU)
    dsk = jnp.concatenate([qt_ref[:, 16:17]] * (CHUNK_W // LANES), axis=0)
    y = r[:CHUNK_W] + jnp.dot(v_ref[...], hcat, preferred_element_type=F32) + dsk * ul
    y_ref[...] = 0.5 * y * (1.0 + jnp.tanh(math.sqrt(2.0 / math.pi) * (y + 0.044715 * (y * y * y))))


def _ssm(ul, uc, l1, vcat, qt, nb):
    g, _, width = ul.shape
    wc = uc.shape[2]
    return pl.pallas_call(
        functools.partial(_ssm_kernel, nb=nb),
        grid=(g,),
        in_specs=[pl.BlockSpec((None, CHUNK_W, width), lambda i: (i, 0, 0)),
                  pl.BlockSpec((None, CHUNK_W, wc), lambda i: (i, 0, 0)),
                  pl.BlockSpec((None, 2 * CHUNK_W, CHUNK_W), lambda i: (i, 0, 0)),
                  pl.BlockSpec((None, CHUNK_W, CHUNK_W), lambda i: (i, 0, 0)),
                  pl.BlockSpec((None, LANES, LANES), lambda i: (i, 0, 0))],
        out_specs=pl.BlockSpec((None, CHUNK_W, width), lambda i: (i, 0, 0)),
        out_shape=jax.ShapeDtypeStruct((g, CHUNK_W, width), F32),
        compiler_params=_cparams(("arbitrary",)),
    )(ul, uc, l1, vcat, qt)


def _route(logits_t, bias):
    ng, ne = N_EXPERT_GROUPS, N_EXPERTS // N_EXPERT_GROUPS
    t = logits_t.shape[1]
    scores = _sigmoid(logits_t).reshape(ng, ne, t)
    biased = scores + bias.reshape(ng, ne, 1)
    iw = lax.broadcasted_iota(jnp.int32, (ng, ne, t), 1)
    ig = lax.broadcasted_iota(jnp.int32, (ng, ne, t), 0)
    neg = -jnp.inf
    m1 = jnp.max(biased, axis=1, keepdims=True)
    i1 = jnp.min(jnp.where(biased == m1, iw, ne), axis=1, keepdims=True)
    m2 = jnp.max(jnp.where(iw == i1, neg, biased), axis=1, keepdims=True)
    gscore = jnp.broadcast_to(m1 + m2, (ng, ne, t))
    gsel = jnp.zeros((ng, ne, t), F32)
    cur = gscore
    for _ in range(TOPK_GROUPS):
        m = jnp.max(cur, axis=0, keepdims=True)
        gi = jnp.min(jnp.where(cur == m, ig, ng), axis=0, keepdims=True)
        pick = ig == gi
        gsel = jnp.where(pick, 1.0, gsel)
        cur = jnp.where(pick, neg, cur)
    cur = jnp.where(gsel > 0.0, biased, neg)
    flat = ig * ne + iw
    chosen = jnp.zeros((ng, ne, t), F32)
    for _ in range(TOP_K):
        m = jnp.max(jnp.max(cur, axis=0, keepdims=True), axis=1, keepdims=True)
        fi = jnp.min(jnp.min(jnp.where(cur == m, flat, N_EXPERTS), axis=0, keepdims=True), axis=1, keepdims=True)
        pick = flat == fi
        chosen = jnp.where(pick, 1.0, chosen)
        cur = jnp.where(pick, neg, cur)
    sel = jnp.where(chosen > 0.0, scores, 0.0)
    tot = jnp.sum(jnp.sum(sel, axis=1, keepdims=True), axis=0, keepdims=True)
    return (sel / tot * ROUTED_SCALE).reshape(N_EXPERTS, t)


def _mix_kernel(yt_ref, attn_ref, x_ref, mod_ref, wglu_ref, bglu_ref, gssm_ref, wouts_ref, gattn_ref, wouta_ref,
                nffn_ref, wrt_ref, rbias_ref, x1_ref, gates_ref, *ynat):
    nch = yt_ref.shape[3]
    part = pl.program_id(1)

    @pl.when(part == 0)
    def _():
        for t in range(CHUNK):
            y_t = yt_ref[:, t].reshape(SSM_WIDTH, nch).T
            for j, ref in enumerate(ynat):
                ref[pl.ds(t, nch, stride=CHUNK), :] = y_t[:, j * LANES:(j + 1) * LANES]

    base = part * x_ref.shape[0]
    g1 = mod_ref[2:3, :]
    zeros = jnp.zeros((LANES - N_EXPERTS, LANES), F32)

    def body(r, carry):
        r0 = pl.multiple_of(r * MIX_ROWS, MIX_ROWS)
        y = _rows(ynat, pl.ds(pl.multiple_of(base + r0, MIX_ROWS), MIX_ROWS))
        glu = jnp.dot(y.astype(MXU), wglu_ref[...], preferred_element_type=F32) + bglu_ref[...]
        z = y * _sigmoid(glu)
        zn = z * lax.rsqrt(jnp.mean(z * z, axis=-1, keepdims=True) + EPS) * gssm_ref[...]
        o_s = jnp.dot(zn.astype(MXU), wouts_ref[...], preferred_element_type=F32)
        a = attn_ref[pl.ds(r0, MIX_ROWS), :]
        an = a * lax.rsqrt(jnp.mean(a * a, axis=-1, keepdims=True) + EPS) * gattn_ref[...]
        o_a = jnp.dot(an.astype(MXU), wouta_ref[...], preferred_element_type=F32)
        x1 = x_ref[pl.ds(r0, MIX_ROWS), :] + g1 * (o_s + o_a)
        x1_ref[pl.ds(r0, MIX_ROWS), :] = x1
        h2 = _ffn_input(x1, mod_ref, nffn_ref)
        logits_t = _router_logits(wrt_ref[...], h2)
        gates_t = _route(logits_t, rbias_ref[...])
        for i in range(MIX_ROWS // LANES):
            piece = jnp.concatenate([gates_t[:, i * LANES:(i + 1) * LANES], zeros], axis=0)
            gates_ref[pl.ds(r0 + i * LANES, LANES), :] = piece.T
        return carry

    lax.fori_loop(0, x_ref.shape[0] // MIX_ROWS, body, 0)


def _split_bf16(a):
    hi = a.astype(MXU)
    return hi, (a - hi.astype(F32)).astype(MXU)


def _router_logits(w_t, h):
    nt = (((1,), (1,)), ((), ()))
    w_hi, w_lo = _split_bf16(w_t)
    h_hi, h_lo = _split_bf16(h)
    both = lax.dot_general(jnp.concatenate([w_hi, w_lo], axis=0), h_hi, nt, preferred_element_type=F32)
    ne = w_t.shape[0]
    return both[:ne] + both[ne:] + lax.dot_general(w_hi, h_lo, nt, preferred_element_type=F32)


def _ffn_input(x1, mod_ref, nffn_ref):
    h2 = x1 * lax.rsqrt(jnp.mean(x1 * x1, axis=-1, keepdims=True) + EPS) * nffn_ref[...]
    return h2 * (1.0 + mod_ref[4:5, :]) + mod_ref[3:4, :]


def _mix(yt, attn, x, mod3, wglu, bglu, gssm, wouts, gattn, wouta, nffn, wrt, rbias):
    nb, seq, d = x.shape
    nch = seq // CHUNK
    yt4 = yt.reshape(SSM_GROUPS, CHUNK, SSM_GROUP, nb * nch)
    c2 = lambda b, p: (0, 0)
    tok = lambda width: pl.BlockSpec((None, MIX_TOKENS, width), lambda b, p: (b, p, 0))
    return pl.pallas_call(
        _mix_kernel,
        grid=(nb, seq // MIX_TOKENS),
        in_specs=[pl.BlockSpec((SSM_GROUPS, CHUNK, SSM_GROUP, nch), lambda b, p: (0, 0, 0, b)),
                  tok(ATTN_WIDTH), tok(d),
                  pl.BlockSpec((None, N_MOD, d), lambda b, p: (b, 0, 0)),
                  pl.BlockSpec(wglu.shape, c2), pl.BlockSpec(bglu.shape, c2), pl.BlockSpec(gssm.shape, c2),
                  pl.BlockSpec(wouts.shape, c2), pl.BlockSpec(gattn.shape, c2), pl.BlockSpec(wouta.shape, c2),
                  pl.BlockSpec(nffn.shape, c2), pl.BlockSpec(wrt.shape, c2), pl.BlockSpec(rbias.shape, c2)],
        out_specs=[tok(d), tok(LANES)],
        out_shape=[jax.ShapeDtypeStruct((nb, seq, d), F32), jax.ShapeDtypeStruct((nb, seq, LANES), F32)],
        scratch_shapes=[pltpu.VMEM((seq, LANES), F32)] * (SSM_WIDTH // LANES),
        compiler_params=_cparams(("arbitrary", "arbitrary")),
    )(yt4, attn, x, mod3, wglu, bglu, gssm, wouts, gattn, wouta, nffn, wrt, rbias)


def _moe_kernel(x1_ref, gates_ref, mod_ref, nffn_ref, wg_ref, wu_ref, wd_ref, wsg_ref, wsu_ref, wsd_ref, nfin_ref,
                o_ref, acc_ref, hid_ref, h2_ref):
    s = pl.program_id(1)
    f = EXPERT_DIM

    def glu(h, wg, wu):
        gu = jnp.dot(h, jnp.concatenate([wg.astype(MXU), wu.astype(MXU)], axis=1), preferred_element_type=F32)
        g = gu[:, :f]
        return g * _sigmoid(g) * gu[:, f:]

    @pl.when(s == 0)
    def _():
        h2_ref[...] = _ffn_input(x1_ref[...], mod_ref, nffn_ref).astype(MXU)
        hs = glu(h2_ref[...], wsg_ref[...], wsu_ref[...])
        acc_ref[...] = jnp.dot(hs.astype(MXU), wsd_ref[...].astype(MXU), preferred_element_type=F32)

    h2 = h2_ref[...]
    gsh = pltpu.roll(gates_ref[...], (LANES - EXPERTS_PER_STEP * s) % LANES, 1)
    for e in range(EXPERTS_PER_STEP):
        hid = glu(h2, wg_ref[e], wu_ref[e]) * gsh[:, e:e + 1]
        hid_ref[:, e * f:(e + 1) * f] = hid.astype(MXU)
    acc_ref[...] += jnp.dot(hid_ref[...], wd_ref[...].astype(MXU), preferred_element_type=F32)

    @pl.when(s == pl.num_programs(1) - 1)
    def _():
        g2 = mod_ref[5:6, :]
        x2 = x1_ref[...] + g2 * acc_ref[...]
        o_ref[...] = x2 * lax.rsqrt(jnp.mean(x2 * x2, axis=-1, keepdims=True) + EPS) * nfin_ref[...]


def _moe(x1, gates, mod3, nffn, wg, wu, wd, wsg, wsu, wsd, nfin):
    nb, seq, d = x1.shape
    tiles_per_b = seq // MOE_TILE
    nsteps = N_EXPERTS // EXPERTS_PER_STEP
    c2 = lambda i, s: (0, 0)
    tok = lambda width: pl.BlockSpec((None, MOE_TILE, width), lambda i, s: (i // tiles_per_b, i % tiles_per_b, 0))
    experts = pl.BlockSpec((EXPERTS_PER_STEP, d, EXPERT_DIM), lambda i, s: (s, 0, 0))
    return pl.pallas_call(
        _moe_kernel,
        grid=(nb * tiles_per_b, nsteps),
        in_specs=[tok(d), tok(LANES),
                  pl.BlockSpec((None, N_MOD, d), lambda i, s: (i // tiles_per_b, 0, 0)),
                  pl.BlockSpec(nffn.shape, c2), experts, experts,
                  pl.BlockSpec((EXPERTS_PER_STEP * EXPERT_DIM, d), lambda i, s: (s, 0)),
                  pl.BlockSpec(wsg.shape, c2), pl.BlockSpec(wsu.shape, c2), pl.BlockSpec(wsd.shape, c2),
                  pl.BlockSpec(nfin.shape, c2)],
        out_specs=tok(d),
        out_shape=jax.ShapeDtypeStruct((nb, seq, d), F32),
        scratch_shapes=[pltpu.VMEM((MOE_TILE, d), F32),
                        pltpu.VMEM((MOE_TILE, EXPERTS_PER_STEP * EXPERT_DIM), MXU),
                        pltpu.VMEM((MOE_TILE, d), MXU)],
        compiler_params=_cparams(("arbitrary", "arbitrary")),
    )(x1, gates, mod3, nffn, wg, wu, wd, wsg, wsu, wsd, nfin)


def _rope_tables(seq):
    pos = jnp.arange(seq)
    row = (pos // GRID_W).astype(F32)
    col = (pos % GRID_W).astype(F32)
    inv = ROPE_THETA ** (-jnp.arange(ROT_PAIRS, dtype=F32) / ROT_PAIRS)
    ar, ac = row[:, None] * inv, col[:, None] * inv
    zero = jnp.zeros_like(ar)
    rep = LANES // HEAD_DIM
    cos_t = jnp.tile(jnp.concatenate([jnp.cos(ar), jnp.cos(ar), jnp.cos(ac), jnp.cos(ac)], axis=1), (1, rep))
    s1_t = jnp.tile(jnp.concatenate([-jnp.sin(ar), zero, -jnp.sin(ac), zero], axis=1), (1, rep))
    s2_t = jnp.tile(jnp.concatenate([zero, jnp.sin(ar), zero, jnp.sin(ac)], axis=1), (1, rep))
    return cos_t, s1_t, s2_t


def kernel(x, c, ctx, c_ctx, w_ada, b_ada, norm_mix, norm_ffn, w_in, attn_sink, ssm_a_re, ssm_a_im, ssm_log_dt, ssm_b_re, ssm_b_im, ssm_c_re, ssm_c_im, ssm_d, w_glu, b_glu, norm_attn_out, norm_ssm_out, w_out, w_router, router_bias, w_gate_e, w_up_e, w_down_e, w_gate_s, w_up_s, w_down_s, norm_final):
    nb, seq, d = x.shape
    nctx = ctx.shape[1]
    layer = 0

    pad = jnp.zeros((16 - nb - 1, d), F32)
    c_all = jnp.concatenate([c, c_ctx[None, :], pad], axis=0)
    mod3 = _ada(c_all, w_ada[layer], b_ada[layer]).reshape(16, N_MOD, d)

    gq = ATTN_HEADS // ATTN_KV_HEADS
    heads = jnp.arange(ATTN_HEADS).reshape(ATTN_KV_HEADS, gq).T.reshape(-1)
    perm_q = (heads[:, None] * HEAD_DIM + jnp.arange(HEAD_DIM)[None, :]).reshape(-1)
    w_in0 = w_in[layer]
    win = jnp.concatenate([w_in0[:, :ATTN_WIDTH][:, perm_q], w_in0[:, ATTN_WIDTH:]], axis=1).astype(MXU)
    nw = norm_mix[layer].reshape(1, d)
    cos_t, s1_t, s2_t = _rope_tables(seq)

    q, k, vt, ul = _proj(x, mod3, 0, nw, win, cos_t, s1_t, s2_t, True)
    _, kc, vct, uc = _proj(ctx.reshape(1, nb * nctx, d), mod3, nb, nw, win, cos_t, s1_t, s2_t, False)
    kc = kc.reshape(nb, nctx, KV_WIDTH)

    attn = _attention(attn_sink[layer], q, k, vt, kc, vct)

    l1, vcat, qt = _ssm_prep(ssm_a_re[layer], ssm_a_im[layer], ssm_log_dt[layer], ssm_b_re[layer],
                             ssm_b_im[layer], ssm_c_re[layer], ssm_c_im[layer], ssm_d[layer])
    yt = _ssm(ul, uc, l1, vcat, qt, nb)

    w_out0 = w_out[layer]
    nffn = norm_ffn[layer].reshape(1, d)
    x1, gates = _mix(
        yt, attn, x, mod3,
        w_glu[layer].astype(MXU), b_glu[layer].reshape(1, SSM_WIDTH), norm_ssm_out[layer].reshape(1, SSM_WIDTH),
        w_out0[ATTN_WIDTH:].astype(MXU), norm_attn_out[layer][perm_q].reshape(1, ATTN_WIDTH),
        w_out0[:ATTN_WIDTH][perm_q].astype(MXU), nffn,
        w_router[layer].T, router_bias[layer].reshape(N_EXPERTS, 1))

    wd = w_down_e[layer].reshape(N_EXPERTS * EXPERT_DIM, d)
    return _moe(x1, gates, mod3, nffn, w_gate_e[layer], w_up_e[layer], wd,
                w_gate_s[layer], w_up_s[layer], w_down_s[layer], norm_final.reshape(1, d))
```

```python
import functools
import math

import jax
import jax.numpy as jnp
from jax import lax
from jax.experimental import pallas as pl
from jax.experimental.pallas import tpu as pltpu

D_MODEL = 1024
EPS = 1e-6
N_MOD = 6
HEAD_DIM = 64
ATTN_HEADS = 8
ATTN_KV_HEADS = 2
ATTN_WIDTH = ATTN_HEADS * HEAD_DIM
KV_WIDTH = ATTN_KV_HEADS * HEAD_DIM
WINDOW = 128
ATTN_SCALE = HEAD_DIM ** -0.5
LOG2E = math.log2(math.e)
ROPE_THETA = 10000.0
ROT_PAIRS = HEAD_DIM // 4
GRID_W = 64
SSM_WIDTH = D_MODEL - ATTN_WIDTH
SSM_GROUP = 16
SSM_GROUPS = SSM_WIDTH // SSM_GROUP
SSM_STATE = 64
N_EXPERTS = 64
EXPERT_DIM = 128
TOP_K = 8
N_EXPERT_GROUPS = 8
TOPK_GROUPS = 4
ROUTED_SCALE = 2.5

CHUNK = 16
CHUNK_W = CHUNK * SSM_GROUP
LANES = 128
ATTN_QBLOCKS = 8
MIX_TOKENS = 1024
MIX_ROWS = 1024
MOE_TILE = 1024
EXPERTS_PER_STEP = 8
PREP_GROUPS = 8
VMEM_LIMIT = 56 * 1024 * 1024

MXU = jnp.bfloat16
F32 = jnp.float32
HIGHEST = lax.Precision.HIGHEST


def _sigmoid(x):
    return 0.5 * jnp.tanh(0.5 * x) + 0.5


def _cparams(sem):
    return pltpu.CompilerParams(dimension_semantics=sem, vmem_limit_bytes=VMEM_LIMIT)


def _ada_kernel(c_ref, w_ref, b_ref, o_ref):
    cv = c_ref[...]
    s_hi, s_lo = _split_bf16(cv * _sigmoid(cv))
    w_hi, w_lo = _split_bf16(w_ref[...])
    rows = cv.shape[0]
    both = jnp.dot(jnp.concatenate([s_hi, s_lo], axis=0), w_hi, preferred_element_type=F32)
    o_ref[...] = both[:rows] + both[rows:] + jnp.dot(s_hi, w_lo, preferred_element_type=F32) + b_ref[...]


def _ada(c_all, w, b):
    rows, d = c_all.shape
    n = w.shape[1]
    tn = 1024
    return pl.pallas_call(
        _ada_kernel,
        grid=(n // tn,),
        in_specs=[pl.BlockSpec((rows, d), lambda j: (0, 0)),
                  pl.BlockSpec((d, tn), lambda j: (0, j)),
                  pl.BlockSpec((1, tn), lambda j: (0, j))],
        out_specs=pl.BlockSpec((rows, tn), lambda j: (0, j)),
        out_shape=jax.ShapeDtypeStruct((rows, n), F32),
        compiler_params=_cparams(("arbitrary",)),
    )(c_all, w, b.reshape(1, n))


def _ssm_prep_kernel(are_ref, aim_ref, ldt_ref, btr_ref, bti_ref, cr_ref, ci_ref, dsk_ref,
                     l1_ref, v_ref, qt_ref):
    p, n = SSM_GROUP, SSM_STATE
    gps = are_ref.shape[0]
    nd = 2 * gps
    ar, ai = are_ref[...].reshape(nd, n), aim_ref[...].reshape(nd, n)
    dt = jnp.exp(ldt_ref[...].reshape(nd, 1))
    mag = jnp.exp(dt * ar)
    abr = mag * jnp.cos(dt * ai)
    abi = mag * jnp.sin(dt * ai)
    den = ar * ar + ai * ai
    nr = abr - 1.0
    cor = (nr * ar + abi * ai) / den
    coi = (abi * ar - nr * ai) / den
    btr, bti = btr_ref[...].reshape(nd, p, n), bti_ref[...].reshape(nd, p, n)
    bbr = cor[:, None, :] * btr - coi[:, None, :] * bti
    bbi = cor[:, None, :] * bti + coi[:, None, :] * btr
    cr, ci = cr_ref[...].reshape(nd, p, n), ci_ref[...].reshape(nd, p, n)
    pr, pi = jnp.ones_like(abr), jnp.zeros_like(abr)
    cps, pbs = [], []
    for k in range(CHUNK + 1):
        cpr = cr * pr[:, None, :] - ci * pi[:, None, :]
        cpi = cr * pi[:, None, :] + ci * pr[:, None, :]
        cps.append(jnp.concatenate([cpr, -cpi], axis=-1))
        if k < CHUNK:
            pbs.append(jnp.concatenate([bbr * pr[:, None, :] - bbi * pi[:, None, :],
                                        bbr * pi[:, None, :] + bbi * pr[:, None, :]], axis=-1))
            pr, pi = pr * abr - pi * abi, pr * abi + pi * abr
    qr, qi = pr, pi
    qrows = []
    for j in range(8):
        qrows.append(jnp.concatenate([qr, qi], axis=-1))
        qr, qi = qr * qr - qi * qi, 2.0 * qr * qi

    zero = jnp.zeros((p, 2 * n), F32)
    kwts = []
    for g in range(gps):
        f, b = 2 * g, 2 * g + 1
        rows = [jnp.concatenate([zero, cps[k][b]], axis=1) for k in range(CHUNK - 1, 0, -1)]
        rows.append(jnp.concatenate([cps[0][f], cps[0][b]], axis=1))
        rows += [jnp.concatenate([cps[k][f], zero], axis=1) for k in range(1, CHUNK)]
        rows.append(jnp.concatenate([zero, zero], axis=1))
        stack = jnp.concatenate(rows, axis=0)
        bbcat = jnp.concatenate([bbr[f], bbi[f], bbr[b], bbi[b]], axis=-1)
        kwts.append(lax.dot_general(bbcat, stack, (((1,), (1,)), ((), ())), precision=HIGHEST,
                                    preferred_element_type=F32))
    for g, kwt in enumerate(kwts):
        f, b = 2 * g, 2 * g + 1
        toep_t = jnp.concatenate(
            [kwt[:, (CHUNK - 1 - t) * p:(CHUNK - 1 - t) * p + CHUNK_W] for t in range(CHUNK)], axis=0)
        wst_f = jnp.concatenate([pbs[CHUNK - 1 - t][f] for t in range(CHUNK)], axis=0)
        wst_b = jnp.concatenate([pbs[t][b] for t in range(CHUNK)], axis=0)
        l1_ref[g] = jnp.concatenate([toep_t.T, wst_f.T, wst_b.T], axis=0).astype(MXU)
        v_f = jnp.concatenate([cps[t + 1][f] for t in range(CHUNK)], axis=0)
        v_b = jnp.concatenate([cps[CHUNK - t][b] for t in range(CHUNK)], axis=0)
        v_ref[g] = jnp.concatenate([v_f, v_b], axis=1).astype(MXU)
        qmat = jnp.concatenate([qrows[j][2 * g + d:2 * g + d + 1] for d in range(2) for j in range(8)]
                               + [dsk_ref[g]], axis=0)
        qpad = jnp.concatenate([qmat, jnp.zeros((LANES - qmat.shape[0], 2 * n), F32)], axis=0)
        qt_ref[g] = qpad.T


def _ssm_prep(a_re, a_im, log_dt, b_re, b_im, c_re, c_im, d_skip):
    g, n, p = SSM_GROUPS, SSM_STATE, SSM_GROUP
    are = jnp.transpose(a_re, (1, 0, 2))
    aim = jnp.transpose(a_im, (1, 0, 2))
    ldt = jnp.transpose(log_dt, (1, 0))[..., None]
    btr = jnp.transpose(b_re, (1, 0, 3, 2))
    bti = jnp.transpose(b_im, (1, 0, 3, 2))
    cr = jnp.transpose(c_re, (1, 0, 2, 3))
    ci = jnp.transpose(c_im, (1, 0, 2, 3))
    dsk = jnp.tile(d_skip.reshape(g, 1, p), (1, 1, 2 * n // p))
    gps = PREP_GROUPS
    vec = pl.BlockSpec((gps, 2, n), lambda i: (i, 0, 0))
    mat = pl.BlockSpec((gps, 2, p, n), lambda i: (i, 0, 0, 0))
    return pl.pallas_call(
        _ssm_prep_kernel,
        grid=(g // gps,),
        in_specs=[vec, vec, pl.BlockSpec((gps, 2, 1), lambda i: (i, 0, 0)), mat, mat, mat, mat,
                  pl.BlockSpec((gps, 1, 2 * n), lambda i: (i, 0, 0))],
        out_specs=[pl.BlockSpec((gps, 2 * CHUNK_W, CHUNK_W), lambda i: (i, 0, 0)),
                   pl.BlockSpec((gps, CHUNK_W, CHUNK_W), lambda i: (i, 0, 0)),
                   pl.BlockSpec((gps, LANES, LANES), lambda i: (i, 0, 0))],
        out_shape=[jax.ShapeDtypeStruct((g, 2 * CHUNK_W, CHUNK_W), MXU),
                   jax.ShapeDtypeStruct((g, CHUNK_W, CHUNK_W), MXU),
                   jax.ShapeDtypeStruct((g, LANES, LANES), F32)],
        compiler_params=_cparams(("arbitrary",)),
    )(are, aim, ldt, btr, bti, cr, ci, dsk)


def _rows(slabs, idx):
    return jnp.concatenate([s[idx, :] for s in slabs], axis=1)


def _proj_kernel(x_ref, mod_ref, nw_ref, win_ref, cos_ref, s1_ref, s2_ref, q_ref, k_ref, vt_ref, ut_ref, *u_nat,
                 rope):
    ntok = x_ref.shape[0]
    nch = ntok // CHUNK
    sh = mod_ref[0:1, :]
    sc = mod_ref[1:2, :]
    nw = nw_ref[...]
    rb = 256

    def rot(tile, r0):
        if not rope:
            return tile
        cs = cos_ref[pl.ds(r0, rb), :]
        a1 = s1_ref[pl.ds(r0, rb), :]
        a2 = s2_ref[pl.ds(r0, rb), :]
        return tile * cs + pltpu.roll(tile, LANES - ROT_PAIRS, 1) * a1 + pltpu.roll(tile, ROT_PAIRS, 1) * a2

    def body(r, carry):
        r0 = pl.multiple_of(r * rb, rb)
        xb = x_ref[pl.ds(r0, rb), :]
        hn = xb * lax.rsqrt(jnp.mean(xb * xb, axis=-1, keepdims=True) + EPS) * nw
        hb = (hn * (1.0 + sc) + sh).astype(MXU)
        proj = jnp.dot(hb, win_ref[...], preferred_element_type=F32)
        for j in range(ATTN_WIDTH // LANES):
            qj = rot(proj[:, j * LANES:(j + 1) * LANES], r0) * (ATTN_SCALE * LOG2E)
            q_ref[pl.ds(r0, rb), j * LANES:(j + 1) * LANES] = qj.astype(MXU)
        k_ref[pl.ds(r0, rb), :] = rot(proj[:, ATTN_WIDTH:ATTN_WIDTH + KV_WIDTH], r0).astype(MXU)
        vt_ref[:, pl.ds(r0, rb)] = proj[:, ATTN_WIDTH + KV_WIDTH:ATTN_WIDTH + 2 * KV_WIDTH].T.astype(MXU)
        u0 = ATTN_WIDTH + 2 * KV_WIDTH
        for j, ref in enumerate(u_nat):
            ref[pl.ds(r0, rb), :] = proj[:, u0 + j * LANES:u0 + (j + 1) * LANES]
        return carry

    lax.fori_loop(0, ntok // rb, body, 0)

    for t in range(CHUNK):
        u_t = _rows(u_nat, pl.ds(t, nch, stride=CHUNK))
        ut_ref[:, t * SSM_GROUP:(t + 1) * SSM_GROUP, :] = u_t.T.reshape(SSM_GROUPS, SSM_GROUP, nch)


def _proj(x, mod3, mod_row0, nw, win, cos_t, s1_t, s2_t, rope):
    nb, ntok, d = x.shape
    nch = ntok // CHUNK
    const2 = lambda b: (0, 0)
    tok = lambda width: pl.BlockSpec((None, ntok, width), lambda b: (b, 0, 0))
    tables = (cos_t, s1_t, s2_t)
    return pl.pallas_call(
        functools.partial(_proj_kernel, rope=rope),
        grid=(nb,),
        in_specs=[tok(d),
                  pl.BlockSpec((None, N_MOD, d), lambda b: (b + mod_row0, 0, 0)),
                  pl.BlockSpec((1, d), const2),
                  pl.BlockSpec(win.shape, const2)] + [pl.BlockSpec(t.shape, const2) for t in tables],
        out_specs=[tok(ATTN_WIDTH), tok(KV_WIDTH), pl.BlockSpec((None, KV_WIDTH, ntok), lambda b: (b, 0, 0)),
                   pl.BlockSpec((SSM_GROUPS, CHUNK_W, nch), lambda b: (0, 0, b))],
        out_shape=[jax.ShapeDtypeStruct((nb, ntok, ATTN_WIDTH), MXU),
                   jax.ShapeDtypeStruct((nb, ntok, KV_WIDTH), MXU),
                   jax.ShapeDtypeStruct((nb, KV_WIDTH, ntok), MXU),
                   jax.ShapeDtypeStruct((SSM_GROUPS, CHUNK_W, nb * nch), F32)],
        scratch_shapes=[pltpu.VMEM((ntok, LANES), F32)] * (SSM_WIDTH // LANES),
        compiler_params=_cparams(("arbitrary",)),
    )(x, mod3, nw, win, *tables)


def _attn_kernel(sink_ref, q_ref, k_ref, vt_ref, kc_ref, vct_ref, bias_ref, o_ref):
    blk = WINDOW
    seq = k_ref.shape[0]
    nwin = 3 * blk
    gq = ATTN_HEADS // ATTN_KV_HEADS
    nsub = q_ref.shape[0] // blk
    kc = kc_ref[...]
    ones = jnp.ones((16, 1), MXU)
    vct = jnp.concatenate([vct_ref[...], jnp.broadcast_to(ones, (16, kc.shape[0]))], axis=0)
    lane_q = lax.broadcasted_iota(jnp.int32, (blk, LANES), 1)
    nt = (((1,), (1,)), ((), ()))
    chains = [(sub, kh) for sub in range(nsub) for kh in range(ATTN_KV_HEADS)]

    starts, scores = {}, {}
    for sub in range(nsub):
        i = pl.program_id(1) * nsub + sub
        starts[sub] = (i, pl.multiple_of(jnp.clip((i - 1) * blk, 0, seq - nwin), blk))
    for sub, kh in chains:
        i, start = starts[sub]
        q = q_ref[sub * blk:(sub + 1) * blk, :]
        qs = jnp.concatenate(
            [jnp.where((lane_q // HEAD_DIM) == kh, q[:, j * LANES:(j + 1) * LANES], jnp.zeros((), MXU))
             for j in range(gq)], axis=0)
        s_loc = lax.dot_general(k_ref[pl.ds(start, nwin), :], qs, nt, preferred_element_type=F32)
        s_loc = s_loc + bias_ref[(i * blk - start) // blk]
        s_ctx = lax.dot_general(kc, qs, nt, preferred_element_type=F32)
        scores[sub, kh] = (s_loc, s_ctx)

    probs = {}
    for sub, kh in chains:
        s_loc, s_ctx = scores[sub, kh]
        sink = jnp.concatenate(
            [jnp.full((1, blk), sink_ref[kh * gq + j] * LOG2E, F32) for j in range(gq)], axis=1)
        m = jnp.maximum(jnp.maximum(jnp.max(s_loc, axis=0, keepdims=True),
                                    jnp.max(s_ctx, axis=0, keepdims=True)), sink)
        probs[sub, kh] = (jnp.exp2(s_loc - m).astype(MXU), jnp.exp2(s_ctx - m).astype(MXU), jnp.exp2(sink - m))

    outs = {}
    for sub, kh in chains:
        p_loc, p_ctx, p_sink = probs[sub, kh]
        _, start = starts[sub]
        vtw = jnp.concatenate([vt_ref[:, pl.ds(start, nwin)], jnp.broadcast_to(ones, (16, nwin))], axis=0)
        acc = (jnp.dot(vtw, p_loc, preferred_element_type=F32)
               + jnp.dot(vct, p_ctx, preferred_element_type=F32))
        outs[sub, kh] = acc[:KV_WIDTH] / (acc[KV_WIDTH:KV_WIDTH + 1] + p_sink)

    top = lax.broadcasted_iota(jnp.int32, (KV_WIDTH, blk), 0) < HEAD_DIM
    for sub in range(nsub):
        for j in range(gq):
            both = jnp.where(top, outs[sub, 0][:, j * blk:(j + 1) * blk], outs[sub, 1][:, j * blk:(j + 1) * blk])
            o_ref[sub * blk:(sub + 1) * blk, j * LANES:(j + 1) * LANES] = both.T


def _band_bias():
    gq = ATTN_HEADS // ATTN_KV_HEADS
    key = jnp.arange(3 * WINDOW)[None, :, None]
    qry = (jnp.arange(gq * WINDOW) % WINDOW)[None, None, :]
    var = jnp.arange(3)[:, None, None]
    inside = jnp.abs(key - var * WINDOW - qry) <= WINDOW
    return jnp.where(inside, 0.0, -jnp.inf).astype(F32)


def _attention(sink, q, k, vt, kc, vct):
    nb, seq, _ = q.shape
    nctx = kc.shape[1]
    blk = ATTN_QBLOCKS * WINDOW
    bias = _band_bias()
    return pl.pallas_call(
        _attn_kernel,
        grid=(nb, seq // blk),
        in_specs=[pl.BlockSpec(memory_space=pltpu.SMEM),
                  pl.BlockSpec((None, blk, ATTN_WIDTH), lambda b, i: (b, i, 0)),
                  pl.BlockSpec((None, seq, KV_WIDTH), lambda b, i: (b, 0, 0)),
                  pl.BlockSpec((None, KV_WIDTH, seq), lambda b, i: (b, 0, 0)),
                  pl.BlockSpec((None, nctx, KV_WIDTH), lambda b, i: (b, 0, 0)),
                  pl.BlockSpec((None, KV_WIDTH, nctx), lambda b, i: (0, 0, b)),
                  pl.BlockSpec(bias.shape, lambda b, i: (0, 0, 0))],
        out_specs=pl.BlockSpec((None, blk, ATTN_WIDTH), lambda b, i: (b, i, 0)),
        out_shape=jax.ShapeDtypeStruct((nb, seq, ATTN_WIDTH), F32),
        compiler_params=_cparams(("arbitrary", "arbitrary")),
    )(sink, q, k, vt, kc, vct, bias)


def _chunk_scans(xs, qt_ref, nsteps, seg):
    width = xs[0][0].shape[1]
    pos = lax.broadcasted_iota(jnp.int32, xs[0][0].shape, 1) % seg
    for j in range(nsteps):
        s = 1 << j
        nxt = []
        for d, (xr, xi) in enumerate(xs):
            if d == 0:
                sr, si, ok = pltpu.roll(xr, s, 1), pltpu.roll(xi, s, 1), pos >= s
            else:
                sr, si, ok = pltpu.roll(xr, width - s, 1), pltpu.roll(xi, width - s, 1), pos < seg - s
            sr = jnp.where(ok, sr, 0.0)
            si = jnp.where(ok, si, 0.0)
            qr, qi = _chunk_multiplier(qt_ref, d, j)
            nxt.append((xr + qr * sr - qi * si, xi + qr * si + qi * sr))
        xs = nxt
    return xs


def _chunk_multiplier(qt_ref, d, j):
    col = d * 8 + j
    return qt_ref[0:SSM_STATE, col:col + 1], qt_ref[SSM_STATE:2 * SSM_STATE, col:col + 1]


def _ssm_kernel(ul_ref, uc_ref, l1_ref, v_ref, qt_ref, y_ref, *, nb):
    n = SSM_STATE
    ul = ul_ref[...]
    width = ul.shape[1]
    seg_l = width // nb
    seg_c = uc_ref.shape[1] // nb
    r = jnp.dot(l1_ref[...], ul.astype(MXU), preferred_element_type=F32)
    rc = jnp.dot(l1_ref[CHUNK_W:, :], uc_ref[...].astype(MXU), preferred_element_type=F32)
    lane = lax.broadcasted_iota(jnp.int32, (n, width), 1)
    ctx = _chunk_scans([(rc[2 * n * d:2 * n * d + n], rc[2 * n * d + n:2 * n * d + 2 * n]) for d in range(2)],
                       qt_ref, int(math.log2(seg_c)), seg_c)
    xs, inj = [], []
    for d, (cr, ci) in enumerate(ctx):
        fwd = d == 0
        base = CHUNK_W + 2 * n * d
        xr, xi = r[base:base + n], r[base + n:base + 2 * n]
        injr = jnp.zeros((n, width), F32)
        inji = jnp.zeros((n, width), F32)
        for b in range(nb):
            src = b * seg_c + (seg_c - 1 if fwd else 0)
            dst = b * seg_l + (0 if fwd else seg_l - 1)
            injr = jnp.where(lane == dst, cr[:, src:src + 1], injr)
            inji = jnp.where(lane == dst, ci[:, src:src + 1], inji)
        q0r, q0i = _chunk_multiplier(qt_ref, d, 0)
        xs.append((xr + q0r * injr - q0i * inji, xi + q0r * inji + q0i * injr))
        inj.append((injr, inji))
    hin = []
    for d, (sr, si) in enumerate(_chunk_scans(xs, qt_ref, int(math.log2(seg_l)), seg_l)):
        fwd = d == 0
        edge = (lane % seg_l) == (0 if fwd else seg_l - 1)
        shift = 1 if fwd else width - 1
        hin.append(jnp.where(edge, inj[d][0], pltpu.roll(sr, shift, 1)))
        hin.append(jnp.where(edge, inj[d][1], pltpu.roll(si, shift, 1)))
    hcat = jnp.concatenate(hin, axis=0).astype(MXU)
    dsk = jnp.concatenate([qt_ref[:, 16:17]] * (CHUNK_W // LANES), axis=0)
    y = r[:CHUNK_W] + jnp.dot(v_ref[...], hcat, preferred_element_type=F32) + dsk * ul
    y_ref[...] = 0.5 * y * (1.0 + jnp.tanh(math.sqrt(2.0 / math.pi) * (y + 0.044715 * (y * y * y))))


def _ssm(ul, uc, l1, vcat, qt, nb):
    g, _, width = ul.shape
    wc = uc.shape[2]
    return pl.pallas_call(
        functools.partial(_ssm_kernel, nb=nb),
        grid=(g,),
        in_specs=[pl.BlockSpec((None, CHUNK_W, width), lambda i: (i, 0, 0)),
                  pl.BlockSpec((None, CHUNK_W, wc), lambda i: (i, 0, 0)),
                  pl.BlockSpec((None, 2 * CHUNK_W, CHUNK_W), lambda i: (i, 0, 0)),
                  pl.BlockSpec((None, CHUNK_W, CHUNK_W), lambda i: (i, 0, 0)),
                  pl.BlockSpec((None, LANES, LANES), lambda i: (i, 0, 0))],
        out_specs=pl.BlockSpec((None, CHUNK_W, width), lambda i: (i, 0, 0)),
        out_shape=jax.ShapeDtypeStruct((g, CHUNK_W, width), F32),
        compiler_params=_cparams(("arbitrary",)),
    )(ul, uc, l1, vcat, qt)


def _route(logits_t, bias):
    ng, ne = N_EXPERT_GROUPS, N_EXPERTS // N_EXPERT_GROUPS
    t = logits_t.shape[1]
    scores = _sigmoid(logits_t).reshape(ng, ne, t)
    biased = scores + bias.reshape(ng, ne, 1)
    iw = lax.broadcasted_iota(jnp.int32, (ng, ne, t), 1)
    ig = lax.broadcasted_iota(jnp.int32, (ng, ne, t), 0)
    neg = -jnp.inf
    m1 = jnp.max(biased, axis=1, keepdims=True)
    i1 = jnp.min(jnp.where(biased == m1, iw, ne), axis=1, keepdims=True)
    m2 = jnp.max(jnp.where(iw == i1, neg, biased), axis=1, keepdims=True)
    gscore = jnp.broadcast_to(m1 + m2, (ng, ne, t))
    gsel = jnp.zeros((ng, ne, t), F32)
    cur = gscore
    for _ in range(TOPK_GROUPS):
        m = jnp.max(cur, axis=0, keepdims=True)
        gi = jnp.min(jnp.where(cur == m, ig, ng), axis=0, keepdims=True)
        pick = ig == gi
        gsel = jnp.where(pick, 1.0, gsel)
        cur = jnp.where(pick, neg, cur)
    cur = jnp.where(gsel > 0.0, biased, neg)
    flat = ig * ne + iw
    chosen = jnp.zeros((ng, ne, t), F32)
    for _ in range(TOP_K):
        m = jnp.max(jnp.max(cur, axis=0, keepdims=True), axis=1, keepdims=True)
        fi = jnp.min(jnp.min(jnp.where(cur == m, flat, N_EXPERTS), axis=0, keepdims=True), axis=1, keepdims=True)
        pick = flat == fi
        chosen = jnp.where(pick, 1.0, chosen)
        cur = jnp.where(pick, neg, cur)
    sel = jnp.where(chosen > 0.0, scores, 0.0)
    tot = jnp.sum(jnp.sum(sel, axis=1, keepdims=True), axis=0, keepdims=True)
    return (sel / tot * ROUTED_SCALE).reshape(N_EXPERTS, t)


def _mix_kernel(yt_ref, attn_ref, x_ref, mod_ref, wglu_ref, bglu_ref, gssm_ref, wouts_ref, gattn_ref, wouta_ref,
                nffn_ref, wrt_ref, rbias_ref, x1_ref, gates_ref, *ynat):
    nch = yt_ref.shape[3]
    part = pl.program_id(1)

    @pl.when(part == 0)
    def _():
        for t in range(CHUNK):
            y_t = yt_ref[:, t].reshape(SSM_WIDTH, nch).T
            for j, ref in enumerate(ynat):
                ref[pl.ds(t, nch, stride=CHUNK), :] = y_t[:, j * LANES:(j + 1) * LANES]

    base = part * x_ref.shape[0]
    g1 = mod_ref[2:3, :]
    zeros = jnp.zeros((LANES - N_EXPERTS, LANES), F32)

    def body(r, carry):
        r0 = pl.multiple_of(r * MIX_ROWS, MIX_ROWS)
        y = _rows(ynat, pl.ds(pl.multiple_of(base + r0, MIX_ROWS), MIX_ROWS))
        glu = jnp.dot(y.astype(MXU), wglu_ref[...], preferred_element_type=F32) + bglu_ref[...]
        z = y * _sigmoid(glu)
        zn = z * lax.rsqrt(jnp.mean(z * z, axis=-1, keepdims=True) + EPS) * gssm_ref[...]
        o_s = jnp.dot(zn.astype(MXU), wouts_ref[...], preferred_element_type=F32)
        a = attn_ref[pl.ds(r0, MIX_ROWS), :]
        an = a * lax.rsqrt(jnp.mean(a * a, axis=-1, keepdims=True) + EPS) * gattn_ref[...]
        o_a = jnp.dot(an.astype(MXU), wouta_ref[...], preferred_element_type=F32)
        x1 = x_ref[pl.ds(r0, MIX_ROWS), :] + g1 * (o_s + o_a)
        x1_ref[pl.ds(r0, MIX_ROWS), :] = x1
        h2 = _ffn_input(x1, mod_ref, nffn_ref)
        logits_t = _router_logits(wrt_ref[...], h2)
        gates_t = _route(logits_t, rbias_ref[...])
        for i in range(MIX_ROWS // LANES):
            piece = jnp.concatenate([gates_t[:, i * LANES:(i + 1) * LANES], zeros], axis=0)
            gates_ref[pl.ds(r0 + i * LANES, LANES), :] = piece.T
        return carry

    lax.fori_loop(0, x_ref.shape[0] // MIX_ROWS, body, 0)


def _split_bf16(a):
    hi = a.astype(MXU)
    return hi, (a - hi.astype(F32)).astype(MXU)


def _router_logits(w_t, h):
    nt = (((1,), (1,)), ((), ()))
    w_hi, w_lo = _split_bf16(w_t)
    h_hi, h_lo = _split_bf16(h)
    both = lax.dot_general(jnp.concatenate([w_hi, w_lo], axis=0), h_hi, nt, preferred_element_type=F32)
    ne = w_t.shape[0]
    return both[:ne] + both[ne:] + lax.dot_general(w_hi, h_lo, nt, preferred_element_type=F32)


def _ffn_input(x1, mod_ref, nffn_ref):
    h2 = x1 * lax.rsqrt(jnp.mean(x1 * x1, axis=-1, keepdims=True) + EPS) * nffn_ref[...]
    return h2 * (1.0 + mod_ref[4:5, :]) + mod_ref[3:4, :]


def _mix(yt, attn, x, mod3, wglu, bglu, gssm, wouts, gattn, wouta, nffn, wrt, rbias):
    nb, seq, d = x.shape
    nch = seq // CHUNK
    yt4 = yt.reshape(SSM_GROUPS, CHUNK, SSM_GROUP, nb * nch)
    c2 = lambda b, p: (0, 0)
    tok = lambda width: pl.BlockSpec((None, MIX_TOKENS, width), lambda b, p: (b, p, 0))
    return pl.pallas_call(
        _mix_kernel,
        grid=(nb, seq // MIX_TOKENS),
        in_specs=[pl.BlockSpec((SSM_GROUPS, CHUNK, SSM_GROUP, nch), lambda b, p: (0, 0, 0, b)),
                  tok(ATTN_WIDTH), tok(d),
                  pl.BlockSpec((None, N_MOD, d), lambda b, p: (b, 0, 0)),
                  pl.BlockSpec(wglu.shape, c2), pl.BlockSpec(bglu.shape, c2), pl.BlockSpec(gssm.shape, c2),
                  pl.BlockSpec(wouts.shape, c2), pl.BlockSpec(gattn.shape, c2), pl.BlockSpec(wouta.shape, c2),
                  pl.BlockSpec(nffn.shape, c2), pl.BlockSpec(wrt.shape, c2), pl.BlockSpec(rbias.shape, c2)],
        out_specs=[tok(d), tok(LANES)],
        out_shape=[jax.ShapeDtypeStruct((nb, seq, d), F32), jax.ShapeDtypeStruct((nb, seq, LANES), F32)],
        scratch_shapes=[pltpu.VMEM((seq, LANES), F32)] * (SSM_WIDTH // LANES),
        compiler_params=_cparams(("arbitrary", "arbitrary")),
    )(yt4, attn, x, mod3, wglu, bglu, gssm, wouts, gattn, wouta, nffn, wrt, rbias)


def _moe_kernel(x1_ref, gates_ref, mod_ref, nffn_ref, wg_ref, wu_ref, wd_ref, wsg_ref, wsu_ref, wsd_ref, nfin_ref,
                o_ref, acc_ref, hid_ref, h2_ref):
    s = pl.program_id(1)
    f = EXPERT_DIM

    def glu(h, wg, wu):
        gu = jnp.dot(h, jnp.concatenate([wg.astype(MXU), wu.astype(MXU)], axis=1), preferred_element_type=F32)
        g = gu[:, :f]
        return g * _sigmoid(g) * gu[:, f:]

    @pl.when(s == 0)
    def _():
        h2_ref[...] = _ffn_input(x1_ref[...], mod_ref, nffn_ref).astype(MXU)
        hs = glu(h2_ref[...], wsg_ref[...], wsu_ref[...])
        acc_ref[...] = jnp.dot(hs.astype(MXU), wsd_ref[...].astype(MXU), preferred_element_type=F32)

    h2 = h2_ref[...]
    gsh = pltpu.roll(gates_ref[...], (LANES - EXPERTS_PER_STEP * s) % LANES, 1)
    for e in range(EXPERTS_PER_STEP):
        hid = glu(h2, wg_ref[e], wu_ref[e]) * gsh[:, e:e + 1]
        hid_ref[:, e * f:(e + 1) * f] = hid.astype(MXU)
    acc_ref[...] += jnp.dot(hid_ref[...], wd_ref[...].astype(MXU), preferred_element_type=F32)

    @pl.when(s == pl.num_programs(1) - 1)
    def _():
        g2 = mod_ref[5:6, :]
        x2 = x1_ref[...] + g2 * acc_ref[...]
        o_ref[...] = x2 * lax.rsqrt(jnp.mean(x2 * x2, axis=-1, keepdims=True) + EPS) * nfin_ref[...]


def _moe(x1, gates, mod3, nffn, wg, wu, wd, wsg, wsu, wsd, nfin):
    nb, seq, d = x1.shape
    tiles_per_b = seq // MOE_TILE
    nsteps = N_EXPERTS // EXPERTS_PER_STEP
    c2 = lambda i, s: (0, 0)
    tok = lambda width: pl.BlockSpec((None, MOE_TILE, width), lambda i, s: (i // tiles_per_b, i % tiles_per_b, 0))
    experts = pl.BlockSpec((EXPERTS_PER_STEP, d, EXPERT_DIM), lambda i, s: (s, 0, 0))
    return pl.pallas_call(
        _moe_kernel,
        grid=(nb * tiles_per_b, nsteps),
        in_specs=[tok(d), tok(LANES),
                  pl.BlockSpec((None, N_MOD, d), lambda i, s: (i // tiles_per_b, 0, 0)),
                  pl.BlockSpec(nffn.shape, c2), experts, experts,
                  pl.BlockSpec((EXPERTS_PER_STEP * EXPERT_DIM, d), lambda i, s: (s, 0)),
                  pl.BlockSpec(wsg.shape, c2), pl.BlockSpec(wsu.shape, c2), pl.BlockSpec(wsd.shape, c2),
                  pl.BlockSpec(nfin.shape, c2)],
        out_specs=tok(d),
        out_shape=jax.ShapeDtypeStruct((nb, seq, d), F32),
        scratch_shapes=[pltpu.VMEM((MOE_TILE, d), F32),
                        pltpu.VMEM((MOE_TILE, EXPERTS_PER_STEP * EXPERT_DIM), MXU),
                        pltpu.VMEM((MOE_TILE, d), MXU)],
        compiler_params=_cparams(("arbitrary", "arbitrary")),
    )(x1, gates, mod3, nffn, wg, wu, wd, wsg, wsu, wsd, nfin)


def _rope_tables(seq):
    pos = jnp.arange(seq)
    row = (pos // GRID_W).astype(F32)
    col = (pos % GRID_W).astype(F32)
    inv = ROPE_THETA ** (-jnp.arange(ROT_PAIRS, dtype=F32) / ROT_PAIRS)
    ar, ac = row[:, None] * inv, col[:, None] * inv
    zero = jnp.zeros_like(ar)
    rep = LANES // HEAD_DIM
    cos_t = jnp.tile(jnp.concatenate([jnp.cos(ar), jnp.cos(ar), jnp.cos(ac), jnp.cos(ac)], axis=1), (1, rep))
    s1_t = jnp.tile(jnp.concatenate([-jnp.sin(ar), zero, -jnp.sin(ac), zero], axis=1), (1, rep))
    s2_t = jnp.tile(jnp.concatenate([zero, jnp.sin(ar), zero, jnp.sin(ac)], axis=1), (1, rep))
    return cos_t, s1_t, s2_t


def kernel(x, c, ctx, c_ctx, w_ada, b_ada, norm_mix, norm_ffn, w_in, attn_sink, ssm_a_re, ssm_a_im, ssm_log_dt, ssm_b_re, ssm_b_im, ssm_c_re, ssm_c_im, ssm_d, w_glu, b_glu, norm_attn_out, norm_ssm_out, w_out, w_router, router_bias, w_gate_e, w_up_e, w_down_e, w_gate_s, w_up_s, w_down_s, norm_final):
    nb, seq, d = x.shape
    nctx = ctx.shape[1]
    layer = 0

    pad = jnp.zeros((16 - nb - 1, d), F32)
    c_all = jnp.concatenate([c, c_ctx[None, :], pad], axis=0)
    mod3 = _ada(c_all, w_ada[layer], b_ada[layer]).reshape(16, N_MOD, d)

    gq = ATTN_HEADS // ATTN_KV_HEADS
    heads = jnp.arange(ATTN_HEADS).reshape(ATTN_KV_HEADS, gq).T.reshape(-1)
    perm_q = (heads[:, None] * HEAD_DIM + jnp.arange(HEAD_DIM)[None, :]).reshape(-1)
    w_in0 = w_in[layer]
    win = jnp.concatenate([w_in0[:, :ATTN_WIDTH][:, perm_q], w_in0[:, ATTN_WIDTH:]], axis=1).astype(MXU)
    nw = norm_mix[layer].reshape(1, d)
    cos_t, s1_t, s2_t = _rope_tables(seq)

    q, k, vt, ul = _proj(x, mod3, 0, nw, win, cos_t, s1_t, s2_t, True)
    _, kc, vct, uc = _proj(ctx.reshape(1, nb * nctx, d), mod3, nb, nw, win, cos_t, s1_t, s2_t, False)
    kc = kc.reshape(nb, nctx, KV_WIDTH)

    attn = _attention(attn_sink[layer], q, k, vt, kc, vct)

    l1, vcat, qt = _ssm_prep(ssm_a_re[layer], ssm_a_im[layer], ssm_log_dt[layer], ssm_b_re[layer],
                             ssm_b_im[layer], ssm_c_re[layer], ssm_c_im[layer], ssm_d[layer])
    yt = _ssm(ul, uc, l1, vcat, qt, nb)

    w_out0 = w_out[layer]
    nffn = norm_ffn[layer].reshape(1, d)
    x1, gates = _mix(
        yt, attn, x, mod3,
        w_glu[layer].astype(MXU), b_glu[layer].reshape(1, SSM_WIDTH), norm_ssm_out[layer].reshape(1, SSM_WIDTH),
        w_out0[ATTN_WIDTH:].astype(MXU), norm_attn_out[layer][perm_q].reshape(1, ATTN_WIDTH),
        w_out0[:ATTN_WIDTH][perm_q].astype(MXU), nffn,
        w_router[layer].T, router_bias[layer].reshape(N_EXPERTS, 1))

    wd = w_down_e[layer].reshape(N_EXPERTS * EXPERT_DIM, d)
    return _moe(x1, gates, mod3, nffn, w_gate_e[layer], w_up_e[layer], wd,
                w_gate_s[layer], w_up_s[layer], w_down_s[layer], norm_final.reshape(1, d))
```

```python
import functools
import math

import jax
import jax.numpy as jnp
import numpy as np
from jax import lax
from jax.experimental import pallas as pl
from jax.experimental.pallas import tpu as pltpu

D_MODEL = 1024
EPS = 1e-6
N_MOD = 6
HEAD_DIM = 64
ATTN_HEADS = 8
ATTN_KV_HEADS = 2
ATTN_WIDTH = ATTN_HEADS * HEAD_DIM
KV_WIDTH = ATTN_KV_HEADS * HEAD_DIM
WINDOW = 128
ATTN_SCALE = HEAD_DIM ** -0.5
LOG2E = math.log2(math.e)
ROPE_THETA = 10000.0
ROT_PAIRS = HEAD_DIM // 4
GRID_W = 64
SSM_WIDTH = D_MODEL - ATTN_WIDTH
SSM_GROUP = 16
SSM_GROUPS = SSM_WIDTH // SSM_GROUP
SSM_STATE = 64
N_EXPERTS = 64
EXPERT_DIM = 128
TOP_K = 8
N_EXPERT_GROUPS = 8
TOPK_GROUPS = 4
ROUTED_SCALE = 2.5

CHUNK = 16
CHUNK_W = CHUNK * SSM_GROUP
LANES = 128
ATTN_QBLOCKS = 8
MIX_TOKENS = 1024
MIX_ROWS = 1024
MOE_TILE = 1024
EXPERTS_PER_STEP = 8
PREP_GROUPS = 8
VMEM_LIMIT = 56 * 1024 * 1024

MXU = jnp.bfloat16
F32 = jnp.float32
HIGHEST = lax.Precision.HIGHEST


def _sigmoid(x):
    return 0.5 * jnp.tanh(0.5 * x) + 0.5


def _cparams(sem):
    return pltpu.CompilerParams(dimension_semantics=sem, vmem_limit_bytes=VMEM_LIMIT)


def _ada_kernel(c_ref, w_ref, b_ref, o_ref):
    cv = c_ref[...]
    s_hi, s_lo = _split_bf16(cv * _sigmoid(cv))
    w_hi, w_lo = _split_bf16(w_ref[...])
    rows = cv.shape[0]
    both = jnp.dot(jnp.concatenate([s_hi, s_lo], axis=0), w_hi, preferred_element_type=F32)
    o_ref[...] = both[:rows] + both[rows:] + jnp.dot(s_hi, w_lo, preferred_element_type=F32) + b_ref[...]


def _ada(c_all, w, b):
    rows, d = c_all.shape
    n = w.shape[1]
    tn = 1024
    return pl.pallas_call(
        _ada_kernel,
        grid=(n // tn,),
        in_specs=[pl.BlockSpec((rows, d), lambda j: (0, 0)),
                  pl.BlockSpec((d, tn), lambda j: (0, j)),
                  pl.BlockSpec((1, tn), lambda j: (0, j))],
        out_specs=pl.BlockSpec((rows, tn), lambda j: (0, j)),
        out_shape=jax.ShapeDtypeStruct((rows, n), F32),
        compiler_params=_cparams(("arbitrary",)),
    )(c_all, w, b.reshape(1, n))


def _ssm_prep_kernel(are_ref, aim_ref, ldt_ref, btr_ref, bti_ref, cr_ref, ci_ref, dsk_ref,
                     l1_ref, v_ref, qt_ref):
    p, n = SSM_GROUP, SSM_STATE
    gps = are_ref.shape[1]
    nd = 2 * gps
    ar, ai = are_ref[...].reshape(nd, n), aim_ref[...].reshape(nd, n)
    dt = jnp.exp(ldt_ref[...].reshape(nd, 1))
    mag = jnp.exp(dt * ar)
    abr = mag * jnp.cos(dt * ai)
    abi = mag * jnp.sin(dt * ai)
    den = ar * ar + ai * ai
    nr = abr - 1.0
    cor = (nr * ar + abi * ai) / den
    coi = (abi * ar - nr * ai) / den
    btr, bti = btr_ref[...].reshape(nd, p, n), bti_ref[...].reshape(nd, p, n)
    bbr = cor[:, None, :] * btr - coi[:, None, :] * bti
    bbi = cor[:, None, :] * bti + coi[:, None, :] * btr
    cr, ci = cr_ref[...].reshape(nd, p, n), ci_ref[...].reshape(nd, p, n)
    pr, pi = jnp.ones_like(abr), jnp.zeros_like(abr)
    cps, pbs = [], []
    for k in range(CHUNK + 1):
        cpr = cr * pr[:, None, :] - ci * pi[:, None, :]
        cpi = cr * pi[:, None, :] + ci * pr[:, None, :]
        cps.append(jnp.concatenate([cpr, -cpi], axis=-1))
        if k < CHUNK:
            pbs.append(jnp.concatenate([bbr * pr[:, None, :] - bbi * pi[:, None, :],
                                        bbr * pi[:, None, :] + bbi * pr[:, None, :]], axis=-1))
            pr, pi = pr * abr - pi * abi, pr * abi + pi * abr
    qr, qi = pr, pi
    qrows = []
    for j in range(8):
        qrows.append(jnp.concatenate([qr, qi], axis=-1))
        qr, qi = qr * qr - qi * qi, 2.0 * qr * qi

    zero = jnp.zeros((p, 2 * n), F32)
    kwts = []
    for g in range(gps):
        f, b = g, gps + g
        rows = [jnp.concatenate([zero, cps[k][b]], axis=1) for k in range(CHUNK - 1, 0, -1)]
        rows.append(jnp.concatenate([cps[0][f], cps[0][b]], axis=1))
        rows += [jnp.concatenate([cps[k][f], zero], axis=1) for k in range(1, CHUNK)]
        rows.append(jnp.concatenate([zero, zero], axis=1))
        stack = jnp.concatenate(rows, axis=0)
        bbcat = jnp.concatenate([bbr[f], bbi[f], bbr[b], bbi[b]], axis=-1)
        kwts.append(lax.dot_general(bbcat, stack, (((1,), (1,)), ((), ())), precision=HIGHEST,
                                    preferred_element_type=F32))
    for g, kwt in enumerate(kwts):
        f, b = g, gps + g
        toep_t = jnp.concatenate(
            [kwt[:, (CHUNK - 1 - t) * p:(CHUNK - 1 - t) * p + CHUNK_W] for t in range(CHUNK)], axis=0)
        wst_f = jnp.concatenate([pbs[CHUNK - 1 - t][f] for t in range(CHUNK)], axis=0)
        wst_b = jnp.concatenate([pbs[t][b] for t in range(CHUNK)], axis=0)
        l1_ref[g] = jnp.concatenate([toep_t.T, wst_f.T, wst_b.T], axis=0).astype(MXU)
        v_f = jnp.concatenate([cps[t + 1][f] for t in range(CHUNK)], axis=0)
        v_b = jnp.concatenate([cps[CHUNK - t][b] for t in range(CHUNK)], axis=0)
        v_ref[g] = jnp.concatenate([v_f, v_b], axis=1).astype(MXU)
        qmat = jnp.concatenate([qrows[j][d * gps + g:d * gps + g + 1] for d in range(2) for j in range(8)]
                               + [dsk_ref[g]], axis=0)
        qpad = jnp.concatenate([qmat, jnp.zeros((LANES - qmat.shape[0], 2 * n), F32)], axis=0)
        qt_ref[g] = qpad.T


def _ssm_prep(a_re, a_im, log_dt, b_re, b_im, c_re, c_im, d_skip):
    g, n, p = SSM_GROUPS, SSM_STATE, SSM_GROUP
    btr = jnp.transpose(b_re, (0, 1, 3, 2))
    bti = jnp.transpose(b_im, (0, 1, 3, 2))
    dsk = jnp.tile(d_skip.reshape(g, 1, p), (1, 1, 2 * n // p))
    gps = PREP_GROUPS
    vec = pl.BlockSpec((2, gps, n), lambda i: (0, i, 0))
    mat = pl.BlockSpec((2, gps, p, n), lambda i: (0, i, 0, 0))
    return pl.pallas_call(
        _ssm_prep_kernel,
        grid=(g // gps,),
        in_specs=[vec, vec, pl.BlockSpec((2, gps, 1), lambda i: (0, i, 0)), mat, mat, mat, mat,
                  pl.BlockSpec((gps, 1, 2 * n), lambda i: (i, 0, 0))],
        out_specs=[pl.BlockSpec((gps, 2 * CHUNK_W, CHUNK_W), lambda i: (i, 0, 0)),
                   pl.BlockSpec((gps, CHUNK_W, CHUNK_W), lambda i: (i, 0, 0)),
                   pl.BlockSpec((gps, LANES, LANES), lambda i: (i, 0, 0))],
        out_shape=[jax.ShapeDtypeStruct((g, 2 * CHUNK_W, CHUNK_W), MXU),
                   jax.ShapeDtypeStruct((g, CHUNK_W, CHUNK_W), MXU),
                   jax.ShapeDtypeStruct((g, LANES, LANES), F32)],
        compiler_params=_cparams(("arbitrary",)),
    )(a_re, a_im, log_dt[..., None], btr, bti, c_re, c_im, dsk)


def _rows(slabs, idx):
    return jnp.concatenate([s[idx, :] for s in slabs], axis=1)


def _proj_kernel(x_ref, mod_ref, nw_ref, win_ref, cos_ref, s1_ref, s2_ref, q_ref, k_ref, vt_ref, ut_ref, *u_nat,
                 rope):
    ntok = x_ref.shape[0]
    nch = ntok // CHUNK
    sh = mod_ref[0:1, :]
    sc = mod_ref[1:2, :]
    nw = nw_ref[...]
    rb = 512

    def rot(tile, r0):
        if not rope:
            return tile
        cs = cos_ref[pl.ds(r0, rb), :]
        a1 = s1_ref[pl.ds(r0, rb), :]
        a2 = s2_ref[pl.ds(r0, rb), :]
        return tile * cs + pltpu.roll(tile, LANES - ROT_PAIRS, 1) * a1 + pltpu.roll(tile, ROT_PAIRS, 1) * a2

    def body(r, carry):
        r0 = pl.multiple_of(r * rb, rb)
        xb = x_ref[pl.ds(r0, rb), :]
        hn = xb * lax.rsqrt(jnp.mean(xb * xb, axis=-1, keepdims=True) + EPS) * nw
        hb = (hn * (1.0 + sc) + sh).astype(MXU)
        proj = jnp.dot(hb, win_ref[...], preferred_element_type=F32)
        for j in range(ATTN_WIDTH // LANES):
            qj = rot(proj[:, j * LANES:(j + 1) * LANES], r0) * (ATTN_SCALE * LOG2E)
            q_ref[pl.ds(r0, rb), j * LANES:(j + 1) * LANES] = qj.astype(MXU)
        k_ref[pl.ds(r0, rb), :] = rot(proj[:, ATTN_WIDTH:ATTN_WIDTH + KV_WIDTH], r0).astype(MXU)
        vt_ref[:, pl.ds(r0, rb)] = proj[:, ATTN_WIDTH + KV_WIDTH:ATTN_WIDTH + 2 * KV_WIDTH].T.astype(MXU)
        u0 = ATTN_WIDTH + 2 * KV_WIDTH
        for j, ref in enumerate(u_nat):
            ref[pl.ds(r0, rb), :] = proj[:, u0 + j * LANES:u0 + (j + 1) * LANES]
        return carry

    lax.fori_loop(0, ntok // rb, body, 0)

    for t in range(CHUNK):
        u_t = _rows(u_nat, pl.ds(t, nch, stride=CHUNK))
        ut_ref[:, t * SSM_GROUP:(t + 1) * SSM_GROUP, :] = u_t.T.reshape(SSM_GROUPS, SSM_GROUP, nch)


def _proj(x, mod3, mod_row0, nw, win, cos_t, s1_t, s2_t, rope):
    nb, ntok, d = x.shape
    nch = ntok // CHUNK
    const2 = lambda b: (0, 0)
    tok = lambda width: pl.BlockSpec((None, ntok, width), lambda b: (b, 0, 0))
    tables = (cos_t, s1_t, s2_t)
    return pl.pallas_call(
        functools.partial(_proj_kernel, rope=rope),
        grid=(nb,),
        in_specs=[tok(d),
                  pl.BlockSpec((None, N_MOD, d), lambda b: (b + mod_row0, 0, 0)),
                  pl.BlockSpec((1, d), const2),
                  pl.BlockSpec(win.shape, const2)] + [pl.BlockSpec(t.shape, const2) for t in tables],
        out_specs=[tok(ATTN_WIDTH), tok(KV_WIDTH), pl.BlockSpec((None, KV_WIDTH, ntok), lambda b: (b, 0, 0)),
                   pl.BlockSpec((SSM_GROUPS, CHUNK_W, nch), lambda b: (0, 0, b))],
        out_shape=[jax.ShapeDtypeStruct((nb, ntok, ATTN_WIDTH), MXU),
                   jax.ShapeDtypeStruct((nb, ntok, KV_WIDTH), MXU),
                   jax.ShapeDtypeStruct((nb, KV_WIDTH, ntok), MXU),
                   jax.ShapeDtypeStruct((SSM_GROUPS, CHUNK_W, nb * nch), F32)],
        scratch_shapes=[pltpu.VMEM((ntok, LANES), F32)] * (SSM_WIDTH // LANES),
        compiler_params=_cparams(("arbitrary",)),
    )(x, mod3, nw, win, *tables)


def _attn_kernel(sink_ref, q_ref, k_ref, vt_ref, kc_ref, vct_ref, bias_ref, o_ref):
    blk = WINDOW
    seq = k_ref.shape[0]
    nwin = 3 * blk
    gq = ATTN_HEADS // ATTN_KV_HEADS
    nsub = q_ref.shape[0] // blk
    kc = kc_ref[...]
    ones = jnp.ones((16, 1), MXU)
    vct = jnp.concatenate([vct_ref[...], jnp.broadcast_to(ones, (16, kc.shape[0]))], axis=0)
    lane_q = lax.broadcasted_iota(jnp.int32, (blk, LANES), 1)
    nt = (((1,), (1,)), ((), ()))
    chains = [(sub, kh) for sub in range(nsub) for kh in range(ATTN_KV_HEADS)]

    starts, scores = {}, {}
    for sub in range(nsub):
        i = pl.program_id(1) * nsub + sub
        starts[sub] = (i, pl.multiple_of(jnp.clip((i - 1) * blk, 0, seq - nwin), blk))
    for sub, kh in chains:
        i, start = starts[sub]
        q = q_ref[sub * blk:(sub + 1) * blk, :]
        qs = jnp.concatenate(
            [jnp.where((lane_q // HEAD_DIM) == kh, q[:, j * LANES:(j + 1) * LANES], jnp.zeros((), MXU))
             for j in range(gq)], axis=0)
        s_loc = lax.dot_general(k_ref[pl.ds(start, nwin), :], qs, nt, preferred_element_type=F32)
        s_loc = s_loc + bias_ref[(i * blk - start) // blk]
        s_ctx = lax.dot_general(kc, qs, nt, preferred_element_type=F32)
        scores[sub, kh] = (s_loc, s_ctx)

    probs = {}
    for sub, kh in chains:
        s_loc, s_ctx = scores[sub, kh]
        sink = jnp.concatenate(
            [jnp.full((1, blk), sink_ref[kh * gq + j] * LOG2E, F32) for j in range(gq)], axis=1)
        m = jnp.maximum(jnp.maximum(jnp.max(s_loc, axis=0, keepdims=True),
                                    jnp.max(s_ctx, axis=0, keepdims=True)), sink)
        probs[sub, kh] = (jnp.exp2(s_loc - m).astype(MXU), jnp.exp2(s_ctx - m).astype(MXU), jnp.exp2(sink - m))

    outs = {}
    for sub, kh in chains:
        p_loc, p_ctx, p_sink = probs[sub, kh]
        _, start = starts[sub]
        vtw = jnp.concatenate([vt_ref[:, pl.ds(start, nwin)], jnp.broadcast_to(ones, (16, nwin))], axis=0)
        acc = (jnp.dot(vtw, p_loc, preferred_element_type=F32)
               + jnp.dot(vct, p_ctx, preferred_element_type=F32))
        outs[sub, kh] = acc[:KV_WIDTH] / (acc[KV_WIDTH:KV_WIDTH + 1] + p_sink)

    top = lax.broadcasted_iota(jnp.int32, (KV_WIDTH, blk), 0) < HEAD_DIM
    for sub in range(nsub):
        for j in range(gq):
            both = jnp.where(top, outs[sub, 0][:, j * blk:(j + 1) * blk], outs[sub, 1][:, j * blk:(j + 1) * blk])
            o_ref[sub * blk:(sub + 1) * blk, j * LANES:(j + 1) * LANES] = both.T


def _band_bias():
    gq = ATTN_HEADS // ATTN_KV_HEADS
    key = np.arange(3 * WINDOW)[None, :, None]
    qry = (np.arange(gq * WINDOW) % WINDOW)[None, None, :]
    var = np.arange(3)[:, None, None]
    inside = np.abs(key - var * WINDOW - qry) <= WINDOW
    return jnp.asarray(np.where(inside, 0.0, -np.inf), F32)


def _attention(sink, q, k, vt, kc, vct):
    nb, seq, _ = q.shape
    nctx = kc.shape[1]
    blk = ATTN_QBLOCKS * WINDOW
    bias = _band_bias()
    return pl.pallas_call(
        _attn_kernel,
        grid=(nb, seq // blk),
        in_specs=[pl.BlockSpec(memory_space=pltpu.SMEM),
                  pl.BlockSpec((None, blk, ATTN_WIDTH), lambda b, i: (b, i, 0)),
                  pl.BlockSpec((None, seq, KV_WIDTH), lambda b, i: (b, 0, 0)),
                  pl.BlockSpec((None, KV_WIDTH, seq), lambda b, i: (b, 0, 0)),
                  pl.BlockSpec((None, nctx, KV_WIDTH), lambda b, i: (b, 0, 0)),
                  pl.BlockSpec((None, KV_WIDTH, nctx), lambda b, i: (0, 0, b)),
                  pl.BlockSpec(bias.shape, lambda b, i: (0, 0, 0))],
        out_specs=pl.BlockSpec((None, blk, ATTN_WIDTH), lambda b, i: (b, i, 0)),
        out_shape=jax.ShapeDtypeStruct((nb, seq, ATTN_WIDTH), F32),
        compiler_params=_cparams(("arbitrary", "arbitrary")),
    )(sink, q, k, vt, kc, vct, bias)


def _chunk_scans(xs, qt_ref, nsteps, seg):
    width = xs[0][0].shape[1]
    pos = lax.broadcasted_iota(jnp.int32, xs[0][0].shape, 1) % seg
    for j in range(nsteps):
        s = 1 << j
        nxt = []
        for d, (xr, xi) in enumerate(xs):
            if d == 0:
                sr, si, ok = pltpu.roll(xr, s, 1), pltpu.roll(xi, s, 1), pos >= s
            else:
                sr, si, ok = pltpu.roll(xr, width - s, 1), pltpu.roll(xi, width - s, 1), pos < seg - s
            sr = jnp.where(ok, sr, 0.0)
            si = jnp.where(ok, si, 0.0)
            qr, qi = _chunk_multiplier(qt_ref, d, j)
            nxt.append((xr + qr * sr - qi * si, xi + qr * si + qi * sr))
        xs = nxt
    return xs


def _chunk_multiplier(qt_ref, d, j):
    col = d * 8 + j
    return qt_ref[0:SSM_STATE, col:col + 1], qt_ref[SSM_STATE:2 * SSM_STATE, col:col + 1]


def _ssm_kernel(ul_ref, uc_ref, l1_ref, v_ref, qt_ref, y_ref, *, nb):
    n = SSM_STATE
    ul = ul_ref[...]
    width = ul.shape[1]
    seg_l = width // nb
    seg_c = uc_ref.shape[1] // nb
    r = jnp.dot(l1_ref[...], ul.astype(MXU), preferred_element_type=F32)
    rc = jnp.dot(l1_ref[CHUNK_W:, :], uc_ref[...].astype(MXU), preferred_element_type=F32)
    lane = lax.broadcasted_iota(jnp.int32, (n, width), 1)
    ctx = _chunk_scans([(rc[2 * n * d:2 * n * d + n], rc[2 * n * d + n:2 * n * d + 2 * n]) for d in range(2)],
                       qt_ref, int(math.log2(seg_c)), seg_c)
    xs, inj = [], []
    for d, (cr, ci) in enumerate(ctx):
        fwd = d == 0
        base = CHUNK_W + 2 * n * d
        xr, xi = r[base:base + n], r[base + n:base + 2 * n]
        injr = jnp.zeros((n, width), F32)
        inji = jnp.zeros((n, width), F32)
        for b in range(nb):
            src = b * seg_c + (seg_c - 1 if fwd else 0)
            dst = b * seg_l + (0 if fwd else seg_l - 1)
            injr = jnp.where(lane == dst, cr[:, src:src + 1], injr)
            inji = jnp.where(lane == dst, ci[:, src:src + 1], inji)
        q0r, q0i = _chunk_multiplier(qt_ref, d, 0)
        xs.append((xr + q0r * injr - q0i * inji, xi + q0r * inji + q0i * injr))
        inj.append((injr, inji))
    hin = []
    for d, (sr, si) in enumerate(_chunk_scans(xs, qt_ref, int(math.log2(seg_l)), seg_l)):
        fwd = d == 0
        edge = (lane % seg_l) == (0 if fwd else seg_l - 1)
        shift = 1 if fwd else width - 1
        hin.append(jnp.where(edge, inj[d][0], pltpu.roll(sr, shift, 1)))
        hin.append(jnp.where(edge, inj[d][1], pltpu.roll(si, shift, 1)))
    hcat = jnp.concatenate(hin, axis=0).astype(MXU)
    dsk = jnp.concatenate([qt_ref[:, 16:17]] * (CHUNK_W // LANES), axis=0)
    y = r[:CHUNK_W] + jnp.dot(v_ref[...], hcat, preferred_element_type=F32) + dsk * ul
    y_ref[...] = 0.5 * y * (1.0 + jnp.tanh(math.sqrt(2.0 / math.pi) * (y + 0.044715 * (y * y * y))))


def _ssm(ul, uc, l1, vcat, qt, nb):
    g, _, width = ul.shape
    wc = uc.shape[2]
    return pl.pallas_call(
        functools.partial(_ssm_kernel, nb=nb),
        grid=(g,),
        in_specs=[pl.BlockSpec((None, CHUNK_W, width), lambda i: (i, 0, 0)),
                  pl.BlockSpec((None, CHUNK_W, wc), lambda i: (i, 0, 0)),
                  pl.BlockSpec((None, 2 * CHUNK_W, CHUNK_W), lambda i: (i, 0, 0)),
                  pl.BlockSpec((None, CHUNK_W, CHUNK_W), lambda i: (i, 0, 0)),
                  pl.BlockSpec((None, LANES, LANES), lambda i: (i, 0, 0))],
        out_specs=pl.BlockSpec((None, CHUNK_W, width), lambda i: (i, 0, 0)),
        out_shape=jax.ShapeDtypeStruct((g, CHUNK_W, width), F32),
        compiler_params=_cparams(("arbitrary",)),
    )(ul, uc, l1, vcat, qt)


def _route(logits_t, bias):
    ng, ne = N_EXPERT_GROUPS, N_EXPERTS // N_EXPERT_GROUPS
    t = logits_t.shape[1]
    scores = _sigmoid(logits_t).reshape(ng, ne, t)
    biased = scores + bias.reshape(ng, ne, 1)
    iw = lax.broadcasted_iota(jnp.int32, (ng, ne, t), 1)
    ig = lax.broadcasted_iota(jnp.int32, (ng, ne, t), 0)
    neg = -jnp.inf
    m1 = jnp.max(biased, axis=1, keepdims=True)
    i1 = jnp.min(jnp.where(biased == m1, iw, ne), axis=1, keepdims=True)
    m2 = jnp.max(jnp.where(iw == i1, neg, biased), axis=1, keepdims=True)
    gscore = jnp.broadcast_to(m1 + m2, (ng, ne, t))
    gsel = jnp.zeros((ng, ne, t), F32)
    cur = gscore
    for _ in range(TOPK_GROUPS):
        m = jnp.max(cur, axis=0, keepdims=True)
        gi = jnp.min(jnp.where(cur == m, ig, ng), axis=0, keepdims=True)
        pick = ig == gi
        gsel = jnp.where(pick, 1.0, gsel)
        cur = jnp.where(pick, neg, cur)
    cur = jnp.where(gsel > 0.0, biased, neg)
    flat = ig * ne + iw
    chosen = jnp.zeros((ng, ne, t), F32)
    for _ in range(TOP_K):
        m = jnp.max(jnp.max(cur, axis=0, keepdims=True), axis=1, keepdims=True)
        fi = jnp.min(jnp.min(jnp.where(cur == m, flat, N_EXPERTS), axis=0, keepdims=True), axis=1, keepdims=True)
        pick = flat == fi
        chosen = jnp.where(pick, 1.0, chosen)
        cur = jnp.where(pick, neg, cur)
    sel = jnp.where(chosen > 0.0, scores, 0.0)
    tot = jnp.sum(jnp.sum(sel, axis=1, keepdims=True), axis=0, keepdims=True)
    return (sel / tot * ROUTED_SCALE).reshape(N_EXPERTS, t)


def _mix_kernel(yt_ref, attn_ref, x_ref, mod_ref, wglu_ref, bglu_ref, gssm_ref, wouts_ref, gattn_ref, wouta_ref,
                nffn_ref, wrt_ref, rbias_ref, x1_ref, gates_ref, *ynat):
    nch = yt_ref.shape[3]
    part = pl.program_id(1)

    @pl.when(part == 0)
    def _():
        for t in range(CHUNK):
            y_t = yt_ref[:, t].reshape(SSM_WIDTH, nch).T
            for j, ref in enumerate(ynat):
                ref[pl.ds(t, nch, stride=CHUNK), :] = y_t[:, j * LANES:(j + 1) * LANES]

    base = part * x_ref.shape[0]
    g1 = mod_ref[2:3, :]
    zeros = jnp.zeros((LANES - N_EXPERTS, LANES), F32)

    def body(r, carry):
        r0 = pl.multiple_of(r * MIX_ROWS, MIX_ROWS)
        y = _rows(ynat, pl.ds(pl.multiple_of(base + r0, MIX_ROWS), MIX_ROWS))
        glu = jnp.dot(y.astype(MXU), wglu_ref[...], preferred_element_type=F32) + bglu_ref[...]
        z = y * _sigmoid(glu)
        zn = z * lax.rsqrt(jnp.mean(z * z, axis=-1, keepdims=True) + EPS) * gssm_ref[...]
        o_s = jnp.dot(zn.astype(MXU), wouts_ref[...], preferred_element_type=F32)
        a = attn_ref[pl.ds(r0, MIX_ROWS), :]
        an = a * lax.rsqrt(jnp.mean(a * a, axis=-1, keepdims=True) + EPS) * gattn_ref[...]
        o_a = jnp.dot(an.astype(MXU), wouta_ref[...], preferred_element_type=F32)
        x1 = x_ref[pl.ds(r0, MIX_ROWS), :] + g1 * (o_s + o_a)
        x1_ref[pl.ds(r0, MIX_ROWS), :] = x1
        h2 = _ffn_input(x1, mod_ref, nffn_ref)
        logits_t = _router_logits(wrt_ref[...], h2)
        gates_t = _route(logits_t, rbias_ref[...])
        for i in range(MIX_ROWS // LANES):
            piece = jnp.concatenate([gates_t[:, i * LANES:(i + 1) * LANES], zeros], axis=0)
            gates_ref[pl.ds(r0 + i * LANES, LANES), :] = piece.T
        return carry

    lax.fori_loop(0, x_ref.shape[0] // MIX_ROWS, body, 0)


def _split_bf16(a):
    hi = a.astype(MXU)
    return hi, (a - hi.astype(F32)).astype(MXU)


def _router_logits(w_t, h):
    nt = (((1,), (1,)), ((), ()))
    w_hi, w_lo = _split_bf16(w_t)
    h_hi, h_lo = _split_bf16(h)
    both = lax.dot_general(jnp.concatenate([w_hi, w_lo], axis=0), h_hi, nt, preferred_element_type=F32)
    ne = w_t.shape[0]
    return both[:ne] + both[ne:] + lax.dot_general(w_hi, h_lo, nt, preferred_element_type=F32)


def _ffn_input(x1, mod_ref, nffn_ref):
    h2 = x1 * lax.rsqrt(jnp.mean(x1 * x1, axis=-1, keepdims=True) + EPS) * nffn_ref[...]
    return h2 * (1.0 + mod_ref[4:5, :]) + mod_ref[3:4, :]


def _mix(yt, attn, x, mod3, wglu, bglu, gssm, wouts, gattn, wouta, nffn, wrt, rbias):
    nb, seq, d = x.shape
    nch = seq // CHUNK
    yt4 = yt.reshape(SSM_GROUPS, CHUNK, SSM_GROUP, nb * nch)
    c2 = lambda b, p: (0, 0)
    tok = lambda width: pl.BlockSpec((None, MIX_TOKENS, width), lambda b, p: (b, p, 0))
    return pl.pallas_call(
        _mix_kernel,
        grid=(nb, seq // MIX_TOKENS),
        in_specs=[pl.BlockSpec((SSM_GROUPS, CHUNK, SSM_GROUP, nch), lambda b, p: (0, 0, 0, b)),
                  tok(ATTN_WIDTH), tok(d),
                  pl.BlockSpec((None, N_MOD, d), lambda b, p: (b, 0, 0)),
                  pl.BlockSpec(wglu.shape, c2), pl.BlockSpec(bglu.shape, c2), pl.BlockSpec(gssm.shape, c2),
                  pl.BlockSpec(wouts.shape, c2), pl.BlockSpec(gattn.shape, c2), pl.BlockSpec(wouta.shape, c2),
                  pl.BlockSpec(nffn.shape, c2), pl.BlockSpec(wrt.shape, c2), pl.BlockSpec(rbias.shape, c2)],
        out_specs=[tok(d), tok(LANES)],
        out_shape=[jax.ShapeDtypeStruct((nb, seq, d), F32), jax.ShapeDtypeStruct((nb, seq, LANES), F32)],
        scratch_shapes=[pltpu.VMEM((seq, LANES), F32)] * (SSM_WIDTH // LANES),
        compiler_params=_cparams(("arbitrary", "arbitrary")),
    )(yt4, attn, x, mod3, wglu, bglu, gssm, wouts, gattn, wouta, nffn, wrt, rbias)


def _moe_kernel(x1_ref, gates_ref, mod_ref, nffn_ref, wg_ref, wu_ref, wd_ref, wsg_ref, wsu_ref, wsd_ref, nfin_ref,
                o_ref, acc_ref, hid_ref, h2_ref):
    s = pl.program_id(1)
    f = EXPERT_DIM

    def glu(h, wg, wu):
        gu = jnp.dot(h, jnp.concatenate([wg.astype(MXU), wu.astype(MXU)], axis=1), preferred_element_type=F32)
        g = gu[:, :f]
        return g * _sigmoid(g) * gu[:, f:]

    @pl.when(s == 0)
    def _():
        h2_ref[...] = _ffn_input(x1_ref[...], mod_ref, nffn_ref).astype(MXU)
        hs = glu(h2_ref[...], wsg_ref[...], wsu_ref[...])
        acc_ref[...] = jnp.dot(hs.astype(MXU), wsd_ref[...].astype(MXU), preferred_element_type=F32)

    h2 = h2_ref[...]
    gsh = pltpu.roll(gates_ref[...], (LANES - EXPERTS_PER_STEP * s) % LANES, 1)
    for e in range(EXPERTS_PER_STEP):
        hid = glu(h2, wg_ref[e], wu_ref[e]) * gsh[:, e:e + 1]
        hid_ref[:, e * f:(e + 1) * f] = hid.astype(MXU)
    acc_ref[...] += jnp.dot(hid_ref[...], wd_ref[...].astype(MXU), preferred_element_type=F32)

    @pl.when(s == pl.num_programs(1) - 1)
    def _():
        g2 = mod_ref[5:6, :]
        x2 = x1_ref[...] + g2 * acc_ref[...]
        o_ref[...] = x2 * lax.rsqrt(jnp.mean(x2 * x2, axis=-1, keepdims=True) + EPS) * nfin_ref[...]


def _moe(x1, gates, mod3, nffn, wg, wu, wd, wsg, wsu, wsd, nfin):
    nb, seq, d = x1.shape
    tiles_per_b = seq // MOE_TILE
    nsteps = N_EXPERTS // EXPERTS_PER_STEP
    c2 = lambda i, s: (0, 0)
    tok = lambda width: pl.BlockSpec((None, MOE_TILE, width), lambda i, s: (i // tiles_per_b, i % tiles_per_b, 0))
    experts = pl.BlockSpec((EXPERTS_PER_STEP, d, EXPERT_DIM), lambda i, s: (s, 0, 0))
    return pl.pallas_call(
        _moe_kernel,
        grid=(nb * tiles_per_b, nsteps),
        in_specs=[tok(d), tok(LANES),
                  pl.BlockSpec((None, N_MOD, d), lambda i, s: (i // tiles_per_b, 0, 0)),
                  pl.BlockSpec(nffn.shape, c2), experts, experts,
                  pl.BlockSpec((EXPERTS_PER_STEP * EXPERT_DIM, d), lambda i, s: (s, 0)),
                  pl.BlockSpec(wsg.shape, c2), pl.BlockSpec(wsu.shape, c2), pl.BlockSpec(wsd.shape, c2),
                  pl.BlockSpec(nfin.shape, c2)],
        out_specs=tok(d),
        out_shape=jax.ShapeDtypeStruct((nb, seq, d), F32),
        scratch_shapes=[pltpu.VMEM((MOE_TILE, d), F32),
                        pltpu.VMEM((MOE_TILE, EXPERTS_PER_STEP * EXPERT_DIM), MXU),
                        pltpu.VMEM((MOE_TILE, d), MXU)],
        compiler_params=_cparams(("arbitrary", "arbitrary")),
    )(x1, gates, mod3, nffn, wg, wu, wd, wsg, wsu, wsd, nfin)


def _rope_tables(seq):
    pos = np.arange(seq)
    inv = ROPE_THETA ** (-np.arange(ROT_PAIRS, dtype=np.float64) / ROT_PAIRS)
    ar, ac = (pos // GRID_W)[:, None] * inv, (pos % GRID_W)[:, None] * inv
    zero = np.zeros_like(ar)
    rep = LANES // HEAD_DIM
    cos_t = np.tile(np.concatenate([np.cos(ar), np.cos(ar), np.cos(ac), np.cos(ac)], axis=1), (1, rep))
    s1_t = np.tile(np.concatenate([-np.sin(ar), zero, -np.sin(ac), zero], axis=1), (1, rep))
    s2_t = np.tile(np.concatenate([zero, np.sin(ar), zero, np.sin(ac)], axis=1), (1, rep))
    return tuple(jnp.asarray(t, F32) for t in (cos_t, s1_t, s2_t))


def kernel(x, c, ctx, c_ctx, w_ada, b_ada, norm_mix, norm_ffn, w_in, attn_sink, ssm_a_re, ssm_a_im, ssm_log_dt, ssm_b_re, ssm_b_im, ssm_c_re, ssm_c_im, ssm_d, w_glu, b_glu, norm_attn_out, norm_ssm_out, w_out, w_router, router_bias, w_gate_e, w_up_e, w_down_e, w_gate_s, w_up_s, w_down_s, norm_final):
    nb, seq, d = x.shape
    nctx = ctx.shape[1]
    layer = 0

    pad = jnp.zeros((16 - nb - 1, d), F32)
    c_all = jnp.concatenate([c, c_ctx[None, :], pad], axis=0)
    mod3 = _ada(c_all, w_ada[layer], b_ada[layer]).reshape(16, N_MOD, d)

    gq = ATTN_HEADS // ATTN_KV_HEADS
    heads = np.arange(ATTN_HEADS).reshape(ATTN_KV_HEADS, gq).T.reshape(-1)
    perm_q = (heads[:, None] * HEAD_DIM + np.arange(HEAD_DIM)[None, :]).reshape(-1)
    w_in0 = w_in[layer]
    win = jnp.concatenate([w_in0[:, :ATTN_WIDTH][:, perm_q], w_in0[:, ATTN_WIDTH:]], axis=1).astype(MXU)
    nw = norm_mix[layer].reshape(1, d)
    cos_t, s1_t, s2_t = _rope_tables(seq)

    q, k, vt, ul = _proj(x, mod3, 0, nw, win, cos_t, s1_t, s2_t, True)
    _, kc, vct, uc = _proj(ctx.reshape(1, nb * nctx, d), mod3, nb, nw, win, cos_t, s1_t, s2_t, False)
    kc = kc.reshape(nb, nctx, KV_WIDTH)

    attn = _attention(attn_sink[layer], q, k, vt, kc, vct)

    l1, vcat, qt = _ssm_prep(ssm_a_re[layer], ssm_a_im[layer], ssm_log_dt[layer], ssm_b_re[layer],
                             ssm_b_im[layer], ssm_c_re[layer], ssm_c_im[layer], ssm_d[layer])
    yt = _ssm(ul, uc, l1, vcat, qt, nb)

    w_out0 = w_out[layer]
    nffn = norm_ffn[layer].reshape(1, d)
    x1, gates = _mix(
        yt, attn, x, mod3,
        w_glu[layer].astype(MXU), b_glu[layer].reshape(1, SSM_WIDTH), norm_ssm_out[layer].reshape(1, SSM_WIDTH),
        w_out0[ATTN_WIDTH:].astype(MXU), norm_attn_out[layer][perm_q].reshape(1, ATTN_WIDTH),
        w_out0[:ATTN_WIDTH][perm_q].astype(MXU), nffn,
        w_router[layer].T, router_bias[layer].reshape(N_EXPERTS, 1))

    wd = w_down_e[layer].reshape(N_EXPERTS * EXPERT_DIM, d)
    return _moe(x1, gates, mod3, nffn, w_gate_e[layer], w_up_e[layer], wd,
                w_gate_s[layer], w_up_s[layer], w_down_s[layer], norm_final.reshape(1, d))
```

```python
import functools
import math

import jax
import jax.numpy as jnp
import numpy as np
from jax import lax
from jax.experimental import pallas as pl
from jax.experimental.pallas import tpu as pltpu

D_MODEL = 1024
EPS = 1e-6
N_MOD = 6
HEAD_DIM = 64
ATTN_HEADS = 8
ATTN_KV_HEADS = 2
ATTN_WIDTH = ATTN_HEADS * HEAD_DIM
KV_WIDTH = ATTN_KV_HEADS * HEAD_DIM
WINDOW = 128
ATTN_SCALE = HEAD_DIM ** -0.5
LOG2E = math.log2(math.e)
ROPE_THETA = 10000.0
ROT_PAIRS = HEAD_DIM // 4
GRID_W = 64
SSM_WIDTH = D_MODEL - ATTN_WIDTH
SSM_GROUP = 16
SSM_GROUPS = SSM_WIDTH // SSM_GROUP
SSM_STATE = 64
N_EXPERTS = 64
EXPERT_DIM = 128
TOP_K = 8
N_EXPERT_GROUPS = 8
TOPK_GROUPS = 4
ROUTED_SCALE = 2.5

CHUNK = 16
CHUNK_W = CHUNK * SSM_GROUP
LANES = 128
ATTN_QBLOCKS = 8
MIX_TOKENS = 1024
MIX_ROWS = 1024
MOE_TILE = 1024
EXPERTS_PER_STEP = 8
SCAN_PAD = 4
SSM_GROUPS_PER_STEP = 4
PREP_GROUPS = 8
VMEM_LIMIT = 56 * 1024 * 1024

MXU = jnp.bfloat16
F32 = jnp.float32
HIGHEST = lax.Precision.HIGHEST


def _sigmoid(x):
    return 0.5 * jnp.tanh(0.5 * x) + 0.5


def _cparams(sem):
    return pltpu.CompilerParams(dimension_semantics=sem, vmem_limit_bytes=VMEM_LIMIT)


def _ada_kernel(c_ref, w_ref, b_ref, o_ref):
    cv = c_ref[...]
    s_hi, s_lo = _split_bf16(cv * _sigmoid(cv))
    w_hi, w_lo = _split_bf16(w_ref[...])
    rows = cv.shape[0]
    both = jnp.dot(jnp.concatenate([s_hi, s_lo], axis=0), w_hi, preferred_element_type=F32)
    o_ref[...] = both[:rows] + both[rows:] + jnp.dot(s_hi, w_lo, preferred_element_type=F32) + b_ref[...]


def _ada(c_all, w, b):
    rows, d = c_all.shape
    n = w.shape[1]
    tn = 1024
    return pl.pallas_call(
        _ada_kernel,
        grid=(n // tn,),
        in_specs=[pl.BlockSpec((rows, d), lambda j: (0, 0)),
                  pl.BlockSpec((d, tn), lambda j: (0, j)),
                  pl.BlockSpec((1, tn), lambda j: (0, j))],
        out_specs=pl.BlockSpec((rows, tn), lambda j: (0, j)),
        out_shape=jax.ShapeDtypeStruct((rows, n), F32),
        compiler_params=_cparams(("arbitrary",)),
    )(c_all, w, b.reshape(1, n))


def _ssm_prep_kernel(are_ref, aim_ref, ldt_ref, btr_ref, bti_ref, cr_ref, ci_ref, dsk_ref,
                     l1_ref, ws_ref, v_ref, qt_ref, am_ref):
    p, n = SSM_GROUP, SSM_STATE
    gps = are_ref.shape[1]
    nd = 2 * gps
    ar, ai = are_ref[...].reshape(nd, n), aim_ref[...].reshape(nd, n)
    dt = jnp.exp(ldt_ref[...].reshape(nd, 1))
    mag = jnp.exp(dt * ar)
    abr = mag * jnp.cos(dt * ai)
    abi = mag * jnp.sin(dt * ai)
    den = ar * ar + ai * ai
    nr = abr - 1.0
    cor = (nr * ar + abi * ai) / den
    coi = (abi * ar - nr * ai) / den
    btr, bti = btr_ref[...].reshape(nd, p, n), bti_ref[...].reshape(nd, p, n)
    bbr = cor[:, None, :] * btr - coi[:, None, :] * bti
    bbi = cor[:, None, :] * bti + coi[:, None, :] * btr
    cr, ci = cr_ref[...].reshape(nd, p, n), ci_ref[...].reshape(nd, p, n)
    pr, pi = jnp.ones_like(abr), jnp.zeros_like(abr)
    cps, pbs = [], []
    for k in range(CHUNK + 1):
        cpr = cr * pr[:, None, :] - ci * pi[:, None, :]
        cpi = cr * pi[:, None, :] + ci * pr[:, None, :]
        cps.append(jnp.concatenate([cpr, -cpi], axis=-1))
        if k < CHUNK:
            pbs.append(jnp.concatenate([bbr * pr[:, None, :] - bbi * pi[:, None, :],
                                        bbr * pi[:, None, :] + bbi * pr[:, None, :]], axis=-1))
            pr, pi = pr * abr - pi * abi, pr * abi + pi * abr
    qr, qi = pr, pi
    qrows = []
    for j in range(8):
        qrows.append(jnp.concatenate([qr, qi], axis=-1))
        qr, qi = qr * qr - qi * qi, 2.0 * qr * qi

    zero = jnp.zeros((p, 2 * n), F32)
    kwts = []
    for g in range(gps):
        f, b = g, gps + g
        rows = [jnp.concatenate([zero, cps[k][b]], axis=1) for k in range(CHUNK - 1, 0, -1)]
        rows.append(jnp.concatenate([cps[0][f], cps[0][b]], axis=1))
        rows += [jnp.concatenate([cps[k][f], zero], axis=1) for k in range(1, CHUNK)]
        rows.append(jnp.concatenate([zero, zero], axis=1))
        stack = jnp.concatenate(rows, axis=0)
        bbcat = jnp.concatenate([bbr[f], bbi[f], bbr[b], bbi[b]], axis=-1)
        kwts.append(lax.dot_general(bbcat, stack, (((1,), (1,)), ((), ())), precision=HIGHEST,
                                    preferred_element_type=F32))
    for g, kwt in enumerate(kwts):
        f, b = g, gps + g
        toep_t = jnp.concatenate(
            [kwt[:, (CHUNK - 1 - t) * p:(CHUNK - 1 - t) * p + CHUNK_W] for t in range(CHUNK)], axis=0)
        wst_f = jnp.concatenate([pbs[CHUNK - 1 - t][f] for t in range(CHUNK)], axis=0)
        wst_b = jnp.concatenate([pbs[t][b] for t in range(CHUNK)], axis=0)
        l1_ref[g] = toep_t.T.astype(MXU)
        low = lax.broadcasted_iota(jnp.int32, wst_f.shape, 1) < n
        tiles = [jnp.where(low, w, 0.0) for m in (wst_f, wst_b) for w in (m, pltpu.roll(m, n, 1))]
        ws_ref[g] = jnp.concatenate(tiles, axis=1).astype(MXU)
        v_f = jnp.concatenate([cps[t + 1][f] for t in range(CHUNK)], axis=0)
        v_b = jnp.concatenate([cps[CHUNK - t][b] for t in range(CHUNK)], axis=0)
        v_ref[g] = jnp.concatenate([v_f, v_b], axis=1).astype(MXU)
        qmat = jnp.concatenate([qrows[j][d * gps + g:d * gps + g + 1] for d in range(2) for j in range(8)]
                               + [dsk_ref[g]], axis=0)
        qpad = jnp.concatenate([qmat, jnp.zeros((LANES - qmat.shape[0], 2 * n), F32)], axis=0)
        qt_ref[g] = qpad.T
        a_f, a_b = qrows[0][f:f + 1], qrows[0][b:b + 1]
        zrow = jnp.zeros_like(a_f)
        am_ref[g] = jnp.concatenate([a_f, pltpu.roll(a_f, n, 1), a_b, pltpu.roll(a_b, n, 1)] + [zrow] * 4, axis=0)


def _ssm_prep(a_re, a_im, log_dt, b_re, b_im, c_re, c_im, d_skip):
    g, n, p = SSM_GROUPS, SSM_STATE, SSM_GROUP
    btr = jnp.transpose(b_re, (0, 1, 3, 2))
    bti = jnp.transpose(b_im, (0, 1, 3, 2))
    dsk = jnp.tile(d_skip.reshape(g, 1, p), (1, 1, 2 * n // p))
    gps = PREP_GROUPS
    vec = pl.BlockSpec((2, gps, n), lambda i: (0, i, 0))
    mat = pl.BlockSpec((2, gps, p, n), lambda i: (0, i, 0, 0))
    return pl.pallas_call(
        _ssm_prep_kernel,
        grid=(g // gps,),
        in_specs=[vec, vec, pl.BlockSpec((2, gps, 1), lambda i: (0, i, 0)), mat, mat, mat, mat,
                  pl.BlockSpec((gps, 1, 2 * n), lambda i: (i, 0, 0))],
        out_specs=[pl.BlockSpec((gps, CHUNK_W, CHUNK_W), lambda i: (i, 0, 0)),
                   pl.BlockSpec((gps, CHUNK_W, 4 * LANES), lambda i: (i, 0, 0)),
                   pl.BlockSpec((gps, CHUNK_W, CHUNK_W), lambda i: (i, 0, 0)),
                   pl.BlockSpec((gps, LANES, LANES), lambda i: (i, 0, 0)),
                   pl.BlockSpec((gps, 8, LANES), lambda i: (i, 0, 0))],
        out_shape=[jax.ShapeDtypeStruct((g, CHUNK_W, CHUNK_W), MXU),
                   jax.ShapeDtypeStruct((g, CHUNK_W, 4 * LANES), MXU),
                   jax.ShapeDtypeStruct((g, CHUNK_W, CHUNK_W), MXU),
                   jax.ShapeDtypeStruct((g, LANES, LANES), F32),
                   jax.ShapeDtypeStruct((g, 8, LANES), F32)],
        compiler_params=_cparams(("arbitrary",)),
    )(a_re, a_im, log_dt[..., None], btr, bti, c_re, c_im, dsk)


def _rows(slabs, idx):
    return jnp.concatenate([s[idx, :] for s in slabs], axis=1)


def _proj_kernel(x_ref, mod_ref, nw_ref, win_ref, cos_ref, s1_ref, s2_ref, q_ref, k_ref, vt_ref, ut_ref, *u_nat,
                 rope):
    ntok = x_ref.shape[0]
    nch = ntok // CHUNK
    sh = mod_ref[0:1, :]
    sc = mod_ref[1:2, :]
    nw = nw_ref[...]
    rb = 512

    def rot(tile, r0):
        if not rope:
            return tile
        cs = cos_ref[pl.ds(r0, rb), :]
        a1 = s1_ref[pl.ds(r0, rb), :]
        a2 = s2_ref[pl.ds(r0, rb), :]
        return tile * cs + pltpu.roll(tile, LANES - ROT_PAIRS, 1) * a1 + pltpu.roll(tile, ROT_PAIRS, 1) * a2

    def body(r, carry):
        r0 = pl.multiple_of(r * rb, rb)
        xb = x_ref[pl.ds(r0, rb), :]
        hn = xb * lax.rsqrt(jnp.mean(xb * xb, axis=-1, keepdims=True) + EPS) * nw
        hb = (hn * (1.0 + sc) + sh).astype(MXU)
        proj = jnp.dot(hb, win_ref[...], preferred_element_type=F32)
        for j in range(ATTN_WIDTH // LANES):
            qj = rot(proj[:, j * LANES:(j + 1) * LANES], r0) * (ATTN_SCALE * LOG2E)
            q_ref[pl.ds(r0, rb), j * LANES:(j + 1) * LANES] = qj.astype(MXU)
        k_ref[pl.ds(r0, rb), :] = rot(proj[:, ATTN_WIDTH:ATTN_WIDTH + KV_WIDTH], r0).astype(MXU)
        vt_ref[:, pl.ds(r0, rb)] = proj[:, ATTN_WIDTH + KV_WIDTH:ATTN_WIDTH + 2 * KV_WIDTH].T.astype(MXU)
        u0 = ATTN_WIDTH + 2 * KV_WIDTH
        for j, ref in enumerate(u_nat):
            ref[pl.ds(r0, rb), :] = proj[:, u0 + j * LANES:u0 + (j + 1) * LANES]
        return carry

    lax.fori_loop(0, ntok // rb, body, 0)

    for t in range(CHUNK):
        u_t = _rows(u_nat, pl.ds(t, nch, stride=CHUNK))
        ut_ref[:, t * SSM_GROUP:(t + 1) * SSM_GROUP, :] = u_t.T.reshape(SSM_GROUPS, SSM_GROUP, nch)


def _proj(x, mod3, mod_row0, nw, win, cos_t, s1_t, s2_t, rope):
    nb, ntok, d = x.shape
    nch = ntok // CHUNK
    const2 = lambda b: (0, 0)
    tok = lambda width: pl.BlockSpec((None, ntok, width), lambda b: (b, 0, 0))
    tables = (cos_t, s1_t, s2_t)
    return pl.pallas_call(
        functools.partial(_proj_kernel, rope=rope),
        grid=(nb,),
        in_specs=[tok(d),
                  pl.BlockSpec((None, N_MOD, d), lambda b: (b + mod_row0, 0, 0)),
                  pl.BlockSpec((1, d), const2),
                  pl.BlockSpec(win.shape, const2)] + [pl.BlockSpec(t.shape, const2) for t in tables],
        out_specs=[tok(ATTN_WIDTH), tok(KV_WIDTH), pl.BlockSpec((None, KV_WIDTH, ntok), lambda b: (b, 0, 0)),
                   pl.BlockSpec((SSM_GROUPS, CHUNK_W, nch), lambda b: (0, 0, b))],
        out_shape=[jax.ShapeDtypeStruct((nb, ntok, ATTN_WIDTH), MXU),
                   jax.ShapeDtypeStruct((nb, ntok, KV_WIDTH), MXU),
                   jax.ShapeDtypeStruct((nb, KV_WIDTH, ntok), MXU),
                   jax.ShapeDtypeStruct((SSM_GROUPS, CHUNK_W, nb * nch), F32)],
        scratch_shapes=[pltpu.VMEM((ntok, LANES), F32)] * (SSM_WIDTH // LANES),
        compiler_params=_cparams(("arbitrary",)),
    )(x, mod3, nw, win, *tables)


def _attn_kernel(sink_ref, q_ref, k_ref, vt_ref, kc_ref, vct_ref, bias_ref, o_ref):
    blk = WINDOW
    seq = k_ref.shape[0]
    nwin = 3 * blk
    gq = ATTN_HEADS // ATTN_KV_HEADS
    nsub = q_ref.shape[0] // blk
    kc = kc_ref[...]
    ones = jnp.ones((16, 1), MXU)
    vct = jnp.concatenate([vct_ref[...], jnp.broadcast_to(ones, (16, kc.shape[0]))], axis=0)
    lane_q = lax.broadcasted_iota(jnp.int32, (blk, LANES), 1)
    nt = (((1,), (1,)), ((), ()))
    chains = [(sub, kh) for sub in range(nsub) for kh in range(ATTN_KV_HEADS)]

    starts, scores = {}, {}
    for sub in range(nsub):
        i = pl.program_id(1) * nsub + sub
        starts[sub] = (i, pl.multiple_of(jnp.clip((i - 1) * blk, 0, seq - nwin), blk))
    for sub, kh in chains:
        i, start = starts[sub]
        q = q_ref[sub * blk:(sub + 1) * blk, :]
        qs = jnp.concatenate(
            [jnp.where((lane_q // HEAD_DIM) == kh, q[:, j * LANES:(j + 1) * LANES], jnp.zeros((), MXU))
             for j in range(gq)], axis=0)
        s_loc = lax.dot_general(k_ref[pl.ds(start, nwin), :], qs, nt, preferred_element_type=F32)
        s_loc = s_loc + bias_ref[(i * blk - start) // blk]
        s_ctx = lax.dot_general(kc, qs, nt, preferred_element_type=F32)
        scores[sub, kh] = (s_loc, s_ctx)

    probs = {}
    for sub, kh in chains:
        s_loc, s_ctx = scores[sub, kh]
        sink = jnp.concatenate(
            [jnp.full((1, blk), sink_ref[kh * gq + j] * LOG2E, F32) for j in range(gq)], axis=1)
        m = jnp.maximum(jnp.maximum(jnp.max(s_loc, axis=0, keepdims=True),
                                    jnp.max(s_ctx, axis=0, keepdims=True)), sink)
        probs[sub, kh] = (jnp.exp2(s_loc - m).astype(MXU), jnp.exp2(s_ctx - m).astype(MXU), jnp.exp2(sink - m))

    outs = {}
    for sub, kh in chains:
        p_loc, p_ctx, p_sink = probs[sub, kh]
        _, start = starts[sub]
        vtw = jnp.concatenate([vt_ref[:, pl.ds(start, nwin)], jnp.broadcast_to(ones, (16, nwin))], axis=0)
        acc = (jnp.dot(vtw, p_loc, preferred_element_type=F32)
               + jnp.dot(vct, p_ctx, preferred_element_type=F32))
        outs[sub, kh] = acc[:KV_WIDTH] / (acc[KV_WIDTH:KV_WIDTH + 1] + p_sink)

    top = lax.broadcasted_iota(jnp.int32, (KV_WIDTH, blk), 0) < HEAD_DIM
    for sub in range(nsub):
        for j in range(gq):
            both = jnp.where(top, outs[sub, 0][:, j * blk:(j + 1) * blk], outs[sub, 1][:, j * blk:(j + 1) * blk])
            o_ref[sub * blk:(sub + 1) * blk, j * LANES:(j + 1) * LANES] = both.T


def _band_bias():
    gq = ATTN_HEADS // ATTN_KV_HEADS
    key = np.arange(3 * WINDOW)[None, :, None]
    qry = (np.arange(gq * WINDOW) % WINDOW)[None, None, :]
    var = np.arange(3)[:, None, None]
    inside = np.abs(key - var * WINDOW - qry) <= WINDOW
    return jnp.asarray(np.where(inside, 0.0, -np.inf), F32)


def _attention(sink, q, k, vt, kc, vct):
    nb, seq, _ = q.shape
    nctx = kc.shape[1]
    blk = ATTN_QBLOCKS * WINDOW
    bias = _band_bias()
    return pl.pallas_call(
        _attn_kernel,
        grid=(nb, seq // blk),
        in_specs=[pl.BlockSpec(memory_space=pltpu.SMEM),
                  pl.BlockSpec((None, blk, ATTN_WIDTH), lambda b, i: (b, i, 0)),
                  pl.BlockSpec((None, seq, KV_WIDTH), lambda b, i: (b, 0, 0)),
                  pl.BlockSpec((None, KV_WIDTH, seq), lambda b, i: (b, 0, 0)),
                  pl.BlockSpec((None, nctx, KV_WIDTH), lambda b, i: (b, 0, 0)),
                  pl.BlockSpec((None, KV_WIDTH, nctx), lambda b, i: (0, 0, b)),
                  pl.BlockSpec(bias.shape, lambda b, i: (0, 0, 0))],
        out_specs=pl.BlockSpec((None, blk, ATTN_WIDTH), lambda b, i: (b, i, 0)),
        out_shape=jax.ShapeDtypeStruct((nb, seq, ATTN_WIDTH), F32),
        compiler_params=_cparams(("arbitrary", "arbitrary")),
    )(sink, q, k, vt, kc, vct, bias)


def _ssm_kernel(ul_ref, uc_ref, l1_ref, ws_ref, v_ref, qt_ref, am_ref, y_ref, *scratch, nb):
    ng = ul_ref.shape[0]
    xt = [scratch[10 * g:10 * g + 4] for g in range(ng)]
    xct = [scratch[10 * g + 4:10 * g + 8] for g in range(ng)]
    hin = [scratch[10 * g + 8:10 * g + 10] for g in range(ng)]
    n = SSM_STATE
    width = ul_ref.shape[2]
    seg_l = width // nb
    seg_c = uc_ref.shape[2] // nb
    tn = (((0,), (0,)), ((), ()))
    pitch_l, pitch_c = seg_l + SCAN_PAD, seg_c + SCAN_PAD
    for g in range(ng):
        states = lax.dot_general(ul_ref[g].astype(MXU), ws_ref[g], tn, preferred_element_type=F32)
        states_c = lax.dot_general(uc_ref[g].astype(MXU), ws_ref[g], tn, preferred_element_type=F32)
        for i in range(4):
            for b in range(nb):
                xt[g][i][b * pitch_l:b * pitch_l + seg_l, :] = states[b * seg_l:(b + 1) * seg_l,
                                                                      i * LANES:(i + 1) * LANES]
                xct[g][i][b * pitch_c:b * pitch_c + seg_c, :] = states_c[b * seg_c:(b + 1) * seg_c,
                                                                         i * LANES:(i + 1) * LANES]
    mult = [[(am_ref[g, 2 * d:2 * d + 1, :], am_ref[g, 2 * d + 1:2 * d + 2, :]) for d in range(2)]
            for g in range(ng)]

    def advance(g, d, s, x, rows):
        (ar, ai), (sr, si) = mult[g][d], s
        xr, xi = x[2 * d][rows, :], x[2 * d + 1][rows, :]
        return ar * sr - ai * si + xr, ar * si + ai * sr + xi

    def rows_of(k, d, seg):
        return pl.ds(k if d == 0 else seg - 1 - k, nb, stride=seg + SCAN_PAD)

    zero = jnp.zeros((nb, LANES), F32)
    carry = tuple(tuple((zero, zero) for d in range(2)) for g in range(ng))
    for k in range(seg_c):
        carry = tuple(tuple(advance(g, d, carry[g][d], xct[g], rows_of(k, d, seg_c)) for d in range(2))
                      for g in range(ng))

    first_half = lax.broadcasted_iota(jnp.int32, (nb, LANES), 1) < n

    def step(k, carry):
        nxt = []
        for g in range(ng):
            per_dir = []
            for d in range(2):
                rows = rows_of(k, d, seg_l)
                sr, si = carry[g][d]
                hin[g][d][rows, :] = jnp.where(first_half, sr, pltpu.roll(si, n, 1))
                per_dir.append(advance(g, d, carry[g][d], xt[g], rows))
            nxt.append(tuple(per_dir))
        return tuple(nxt)

    lax.fori_loop(0, seg_l, step, carry, unroll=8)

    nt = (((1,), (1,)), ((), ()))
    for g in range(ng):
        ul = ul_ref[g]
        hcat = jnp.concatenate(
            [jnp.concatenate([hin[g][d][b * pitch_l:b * pitch_l + seg_l, :] for b in range(nb)], axis=0)
             for d in range(2)], axis=1).astype(MXU)
        dsk = jnp.concatenate([qt_ref[g, :, 16:17]] * (CHUNK_W // LANES), axis=0)
        y = (jnp.dot(l1_ref[g], ul.astype(MXU), preferred_element_type=F32)
             + lax.dot_general(v_ref[g], hcat, nt, preferred_element_type=F32) + dsk * ul)
        y_ref[g] = 0.5 * y * (1.0 + jnp.tanh(math.sqrt(2.0 / math.pi) * (y + 0.044715 * (y * y * y))))


def _ssm(ul, uc, l1, ws, vcat, qt, am, nb):
    g, _, width = ul.shape
    wc = uc.shape[2]
    ng = SSM_GROUPS_PER_STEP
    blk = lambda *shape: pl.BlockSpec((ng,) + shape, lambda i: (i,) + (0,) * len(shape))
    rows_l, rows_c = width + nb * SCAN_PAD, wc + nb * SCAN_PAD
    per_group = [pltpu.VMEM((rows_l, LANES), F32)] * 4 + [pltpu.VMEM((rows_c, LANES), F32)] * 4 \
        + [pltpu.VMEM((rows_l, LANES), F32)] * 2
    return pl.pallas_call(
        functools.partial(_ssm_kernel, nb=nb),
        grid=(g // ng,),
        in_specs=[blk(CHUNK_W, width), blk(CHUNK_W, wc), blk(CHUNK_W, CHUNK_W), blk(CHUNK_W, 4 * LANES),
                  blk(CHUNK_W, CHUNK_W), blk(LANES, LANES), blk(8, LANES)],
        out_specs=blk(CHUNK_W, width),
        out_shape=jax.ShapeDtypeStruct((g, CHUNK_W, width), F32),
        scratch_shapes=per_group * ng,
        compiler_params=_cparams(("arbitrary",)),
    )(ul, uc, l1, ws, vcat, qt, am)


def _route(logits_t, bias):
    ng, ne = N_EXPERT_GROUPS, N_EXPERTS // N_EXPERT_GROUPS
    t = logits_t.shape[1]
    scores = _sigmoid(logits_t).reshape(ng, ne, t)
    biased = scores + bias.reshape(ng, ne, 1)
    iw = lax.broadcasted_iota(jnp.int32, (ng, ne, t), 1)
    ig = lax.broadcasted_iota(jnp.int32, (ng, ne, t), 0)
    neg = -jnp.inf
    m1 = jnp.max(biased, axis=1, keepdims=True)
    i1 = jnp.min(jnp.where(biased == m1, iw, ne), axis=1, keepdims=True)
    m2 = jnp.max(jnp.where(iw == i1, neg, biased), axis=1, keepdims=True)
    gscore = jnp.broadcast_to(m1 + m2, (ng, ne, t))
    gsel = jnp.zeros((ng, ne, t), F32)
    cur = gscore
    for _ in range(TOPK_GROUPS):
        m = jnp.max(cur, axis=0, keepdims=True)
        gi = jnp.min(jnp.where(cur == m, ig, ng), axis=0, keepdims=True)
        pick = ig == gi
        gsel = jnp.where(pick, 1.0, gsel)
        cur = jnp.where(pick, neg, cur)
    cur = jnp.where(gsel > 0.0, biased, neg)
    flat = ig * ne + iw
    chosen = jnp.zeros((ng, ne, t), F32)
    for _ in range(TOP_K):
        m = jnp.max(jnp.max(cur, axis=0, keepdims=True), axis=1, keepdims=True)
        fi = jnp.min(jnp.min(jnp.where(cur == m, flat, N_EXPERTS), axis=0, keepdims=True), axis=1, keepdims=True)
        pick = flat == fi
        chosen = jnp.where(pick, 1.0, chosen)
        cur = jnp.where(pick, neg, cur)
    sel = jnp.where(chosen > 0.0, scores, 0.0)
    tot = jnp.sum(jnp.sum(sel, axis=1, keepdims=True), axis=0, keepdims=True)
    return (sel / tot * ROUTED_SCALE).reshape(N_EXPERTS, t)


def _mix_kernel(yt_ref, attn_ref, x_ref, mod_ref, wglu_ref, bglu_ref, gssm_ref, wouts_ref, gattn_ref, wouta_ref,
                nffn_ref, wrt_ref, rbias_ref, x1_ref, gates_ref, *ynat):
    nch = yt_ref.shape[3]
    part = pl.program_id(1)

    @pl.when(part == 0)
    def _():
        for t in range(CHUNK):
            y_t = yt_ref[:, t].reshape(SSM_WIDTH, nch).T
            for j, ref in enumerate(ynat):
                ref[pl.ds(t, nch, stride=CHUNK), :] = y_t[:, j * LANES:(j + 1) * LANES]

    base = part * x_ref.shape[0]
    g1 = mod_ref[2:3, :]
    zeros = jnp.zeros((LANES - N_EXPERTS, LANES), F32)

    def body(r, carry):
        r0 = pl.multiple_of(r * MIX_ROWS, MIX_ROWS)
        y = _rows(ynat, pl.ds(pl.multiple_of(base + r0, MIX_ROWS), MIX_ROWS))
        glu = jnp.dot(y.astype(MXU), wglu_ref[...], preferred_element_type=F32) + bglu_ref[...]
        z = y * _sigmoid(glu)
        zn = z * lax.rsqrt(jnp.mean(z * z, axis=-1, keepdims=True) + EPS) * gssm_ref[...]
        o_s = jnp.dot(zn.astype(MXU), wouts_ref[...], preferred_element_type=F32)
        a = attn_ref[pl.ds(r0, MIX_ROWS), :]
        an = a * lax.rsqrt(jnp.mean(a * a, axis=-1, keepdims=True) + EPS) * gattn_ref[...]
        o_a = jnp.dot(an.astype(MXU), wouta_ref[...], preferred_element_type=F32)
        x1 = x_ref[pl.ds(r0, MIX_ROWS), :] + g1 * (o_s + o_a)
        x1_ref[pl.ds(r0, MIX_ROWS), :] = x1
        h2 = _ffn_input(x1, mod_ref, nffn_ref)
        logits_t = _router_logits(wrt_ref[...], h2)
        gates_t = _route(logits_t, rbias_ref[...])
        for i in range(MIX_ROWS // LANES):
            piece = jnp.concatenate([gates_t[:, i * LANES:(i + 1) * LANES], zeros], axis=0)
            gates_ref[pl.ds(r0 + i * LANES, LANES), :] = piece.T
        return carry

    lax.fori_loop(0, x_ref.shape[0] // MIX_ROWS, body, 0)


def _split_bf16(a):
    hi = a.astype(MXU)
    return hi, (a - hi.astype(F32)).astype(MXU)


def _router_logits(w_t, h):
    nt = (((1,), (1,)), ((), ()))
    w_hi, w_lo = _split_bf16(w_t)
    h_hi, h_lo = _split_bf16(h)
    both = lax.dot_general(jnp.concatenate([w_hi, w_lo], axis=0), h_hi, nt, preferred_element_type=F32)
    ne = w_t.shape[0]
    return both[:ne] + both[ne:] + lax.dot_general(w_hi, h_lo, nt, preferred_element_type=F32)


def _ffn_input(x1, mod_ref, nffn_ref):
    h2 = x1 * lax.rsqrt(jnp.mean(x1 * x1, axis=-1, keepdims=True) + EPS) * nffn_ref[...]
    return h2 * (1.0 + mod_ref[4:5, :]) + mod_ref[3:4, :]


def _mix(yt, attn, x, mod3, wglu, bglu, gssm, wouts, gattn, wouta, nffn, wrt, rbias):
    nb, seq, d = x.shape
    nch = seq // CHUNK
    yt4 = yt.reshape(SSM_GROUPS, CHUNK, SSM_GROUP, nb * nch)
    c2 = lambda b, p: (0, 0)
    tok = lambda width: pl.BlockSpec((None, MIX_TOKENS, width), lambda b, p: (b, p, 0))
    return pl.pallas_call(
        _mix_kernel,
        grid=(nb, seq // MIX_TOKENS),
        in_specs=[pl.BlockSpec((SSM_GROUPS, CHUNK, SSM_GROUP, nch), lambda b, p: (0, 0, 0, b)),
                  tok(ATTN_WIDTH), tok(d),
                  pl.BlockSpec((None, N_MOD, d), lambda b, p: (b, 0, 0)),
                  pl.BlockSpec(wglu.shape, c2), pl.BlockSpec(bglu.shape, c2), pl.BlockSpec(gssm.shape, c2),
                  pl.BlockSpec(wouts.shape, c2), pl.BlockSpec(gattn.shape, c2), pl.BlockSpec(wouta.shape, c2),
                  pl.BlockSpec(nffn.shape, c2), pl.BlockSpec(wrt.shape, c2), pl.BlockSpec(rbias.shape, c2)],
        out_specs=[tok(d), tok(LANES)],
        out_shape=[jax.ShapeDtypeStruct((nb, seq, d), F32), jax.ShapeDtypeStruct((nb, seq, LANES), F32)],
        scratch_shapes=[pltpu.VMEM((seq, LANES), F32)] * (SSM_WIDTH // LANES),
        compiler_params=_cparams(("arbitrary", "arbitrary")),
    )(yt4, attn, x, mod3, wglu, bglu, gssm, wouts, gattn, wouta, nffn, wrt, rbias)


def _moe_kernel(x1_ref, gates_ref, mod_ref, nffn_ref, wg_ref, wu_ref, wd_ref, wsg_ref, wsu_ref, wsd_ref, nfin_ref,
                o_ref, acc_ref, hid_ref, h2_ref):
    s = pl.program_id(1)
    f = EXPERT_DIM

    def glu(h, wg, wu):
        gu = jnp.dot(h, jnp.concatenate([wg.astype(MXU), wu.astype(MXU)], axis=1), preferred_element_type=F32)
        g = gu[:, :f]
        return g * _sigmoid(g) * gu[:, f:]

    @pl.when(s == 0)
    def _():
        h2_ref[...] = _ffn_input(x1_ref[...], mod_ref, nffn_ref).astype(MXU)
        hs = glu(h2_ref[...], wsg_ref[...], wsu_ref[...])
        acc_ref[...] = jnp.dot(hs.astype(MXU), wsd_ref[...].astype(MXU), preferred_element_type=F32)

    h2 = h2_ref[...]
    gsh = pltpu.roll(gates_ref[...], (LANES - EXPERTS_PER_STEP * s) % LANES, 1)
    for e in range(EXPERTS_PER_STEP):
        hid = glu(h2, wg_ref[e], wu_ref[e]) * gsh[:, e:e + 1]
        hid_ref[:, e * f:(e + 1) * f] = hid.astype(MXU)
    acc_ref[...] += jnp.dot(hid_ref[...], wd_ref[...].astype(MXU), preferred_element_type=F32)

    @pl.when(s == pl.num_programs(1) - 1)
    def _():
        g2 = mod_ref[5:6, :]
        x2 = x1_ref[...] + g2 * acc_ref[...]
        o_ref[...] = x2 * lax.rsqrt(jnp.mean(x2 * x2, axis=-1, keepdims=True) + EPS) * nfin_ref[...]


def _moe(x1, gates, mod3, nffn, wg, wu, wd, wsg, wsu, wsd, nfin):
    nb, seq, d = x1.shape
    tiles_per_b = seq // MOE_TILE
    nsteps = N_EXPERTS // EXPERTS_PER_STEP
    c2 = lambda i, s: (0, 0)
    tok = lambda width: pl.BlockSpec((None, MOE_TILE, width), lambda i, s: (i // tiles_per_b, i % tiles_per_b, 0))
    experts = pl.BlockSpec((EXPERTS_PER_STEP, d, EXPERT_DIM), lambda i, s: (s, 0, 0))
    return pl.pallas_call(
        _moe_kernel,
        grid=(nb * tiles_per_b, nsteps),
        in_specs=[tok(d), tok(LANES),
                  pl.BlockSpec((None, N_MOD, d), lambda i, s: (i // tiles_per_b, 0, 0)),
                  pl.BlockSpec(nffn.shape, c2), experts, experts,
                  pl.BlockSpec((EXPERTS_PER_STEP * EXPERT_DIM, d), lambda i, s: (s, 0)),
                  pl.BlockSpec(wsg.shape, c2), pl.BlockSpec(wsu.shape, c2), pl.BlockSpec(wsd.shape, c2),
                  pl.BlockSpec(nfin.shape, c2)],
        out_specs=tok(d),
        out_shape=jax.ShapeDtypeStruct((nb, seq, d), F32),
        scratch_shapes=[pltpu.VMEM((MOE_TILE, d), F32),
                        pltpu.VMEM((MOE_TILE, EXPERTS_PER_STEP * EXPERT_DIM), MXU),
                        pltpu.VMEM((MOE_TILE, d), MXU)],
        compiler_params=_cparams(("arbitrary", "arbitrary")),
    )(x1, gates, mod3, nffn, wg, wu, wd, wsg, wsu, wsd, nfin)


def _rope_tables(seq):
    pos = np.arange(seq)
    inv = ROPE_THETA ** (-np.arange(ROT_PAIRS, dtype=np.float64) / ROT_PAIRS)
    ar, ac = (pos // GRID_W)[:, None] * inv, (pos % GRID_W)[:, None] * inv
    zero = np.zeros_like(ar)
    rep = LANES // HEAD_DIM
    cos_t = np.tile(np.concatenate([np.cos(ar), np.cos(ar), np.cos(ac), np.cos(ac)], axis=1), (1, rep))
    s1_t = np.tile(np.concatenate([-np.sin(ar), zero, -np.sin(ac), zero], axis=1), (1, rep))
    s2_t = np.tile(np.concatenate([zero, np.sin(ar), zero, np.sin(ac)], axis=1), (1, rep))
    return tuple(jnp.asarray(t, F32) for t in (cos_t, s1_t, s2_t))


def kernel(x, c, ctx, c_ctx, w_ada, b_ada, norm_mix, norm_ffn, w_in, attn_sink, ssm_a_re, ssm_a_im, ssm_log_dt, ssm_b_re, ssm_b_im, ssm_c_re, ssm_c_im, ssm_d, w_glu, b_glu, norm_attn_out, norm_ssm_out, w_out, w_router, router_bias, w_gate_e, w_up_e, w_down_e, w_gate_s, w_up_s, w_down_s, norm_final):
    nb, seq, d = x.shape
    nctx = ctx.shape[1]
    layer = 0

    pad = jnp.zeros((16 - nb - 1, d), F32)
    c_all = jnp.concatenate([c, c_ctx[None, :], pad], axis=0)
    mod3 = _ada(c_all, w_ada[layer], b_ada[layer]).reshape(16, N_MOD, d)

    gq = ATTN_HEADS // ATTN_KV_HEADS
    heads = np.arange(ATTN_HEADS).reshape(ATTN_KV_HEADS, gq).T.reshape(-1)
    perm_q = (heads[:, None] * HEAD_DIM + np.arange(HEAD_DIM)[None, :]).reshape(-1)
    w_in0 = w_in[layer]
    win = jnp.concatenate([w_in0[:, :ATTN_WIDTH][:, perm_q], w_in0[:, ATTN_WIDTH:]], axis=1).astype(MXU)
    nw = norm_mix[layer].reshape(1, d)
    cos_t, s1_t, s2_t = _rope_tables(seq)

    q, k, vt, ul = _proj(x, mod3, 0, nw, win, cos_t, s1_t, s2_t, True)
    _, kc, vct, uc = _proj(ctx.reshape(1, nb * nctx, d), mod3, nb, nw, win, cos_t, s1_t, s2_t, False)
    kc = kc.reshape(nb, nctx, KV_WIDTH)

    attn = _attention(attn_sink[layer], q, k, vt, kc, vct)

    l1, ws, vcat, qt, am = _ssm_prep(ssm_a_re[layer], ssm_a_im[layer], ssm_log_dt[layer], ssm_b_re[layer],
                                     ssm_b_im[layer], ssm_c_re[layer], ssm_c_im[layer], ssm_d[layer])
    yt = _ssm(ul, uc, l1, ws, vcat, qt, am, nb)

    w_out0 = w_out[layer]
    nffn = norm_ffn[layer].reshape(1, d)
    x1, gates = _mix(
        yt, attn, x, mod3,
        w_glu[layer].astype(MXU), b_glu[layer].reshape(1, SSM_WIDTH), norm_ssm_out[layer].reshape(1, SSM_WIDTH),
        w_out0[ATTN_WIDTH:].astype(MXU), norm_attn_out[layer][perm_q].reshape(1, ATTN_WIDTH),
        w_out0[:ATTN_WIDTH][perm_q].astype(MXU), nffn,
        w_router[layer].T, router_bias[layer].reshape(N_EXPERTS, 1))

    wd = w_down_e[layer].reshape(N_EXPERTS * EXPERT_DIM, d)
    return _moe(x1, gates, mod3, nffn, w_gate_e[layer], w_up_e[layer], wd,
                w_gate_s[layer], w_up_s[layer], w_down_s[layer], norm_final.reshape(1, d))
```

```python
import functools
import math

import jax
import jax.numpy as jnp
import numpy as np
from jax import lax
from jax.experimental import pallas as pl
from jax.experimental.pallas import tpu as pltpu

D_MODEL = 1024
EPS = 1e-6
N_MOD = 6
HEAD_DIM = 64
ATTN_HEADS = 8
ATTN_KV_HEADS = 2
ATTN_WIDTH = ATTN_HEADS * HEAD_DIM
KV_WIDTH = ATTN_KV_HEADS * HEAD_DIM
WINDOW = 128
ATTN_SCALE = HEAD_DIM ** -0.5
LOG2E = math.log2(math.e)
ROPE_THETA = 10000.0
ROT_PAIRS = HEAD_DIM // 4
GRID_W = 64
SSM_WIDTH = D_MODEL - ATTN_WIDTH
SSM_GROUP = 16
SSM_GROUPS = SSM_WIDTH // SSM_GROUP
SSM_STATE = 64
N_EXPERTS = 64
EXPERT_DIM = 128
TOP_K = 8
N_EXPERT_GROUPS = 8
TOPK_GROUPS = 4
ROUTED_SCALE = 2.5

CHUNK = 16
CHUNK_W = CHUNK * SSM_GROUP
LANES = 128
ATTN_QBLOCKS = 8
MIX_TOKENS = 1024
MIX_ROWS = 1024
MOE_TILE = 1024
EXPERTS_PER_STEP = 8
SCAN_PAD = 4
SSM_GROUPS_PER_STEP = 4
PREP_GROUPS = 8
VMEM_LIMIT = 56 * 1024 * 1024

MXU = jnp.bfloat16
F32 = jnp.float32
HIGHEST = lax.Precision.HIGHEST


def _sigmoid(x):
    return 0.5 * jnp.tanh(0.5 * x) + 0.5


def _cparams(sem):
    return pltpu.CompilerParams(dimension_semantics=sem, vmem_limit_bytes=VMEM_LIMIT)


def _ada_kernel(c_ref, w_ref, b_ref, o_ref):
    cv = c_ref[...]
    s_hi, s_lo = _split_bf16(cv * _sigmoid(cv))
    w_hi, w_lo = _split_bf16(w_ref[...])
    rows = cv.shape[0]
    both = jnp.dot(jnp.concatenate([s_hi, s_lo], axis=0), w_hi, preferred_element_type=F32)
    o_ref[...] = both[:rows] + both[rows:] + jnp.dot(s_hi, w_lo, preferred_element_type=F32) + b_ref[...]


def _ada(c_all, w, b):
    rows, d = c_all.shape
    n = w.shape[1]
    tn = 1024
    return pl.pallas_call(
        _ada_kernel,
        grid=(n // tn,),
        in_specs=[pl.BlockSpec((rows, d), lambda j: (0, 0)),
                  pl.BlockSpec((d, tn), lambda j: (0, j)),
                  pl.BlockSpec((1, tn), lambda j: (0, j))],
        out_specs=pl.BlockSpec((rows, tn), lambda j: (0, j)),
        out_shape=jax.ShapeDtypeStruct((rows, n), F32),
        compiler_params=_cparams(("arbitrary",)),
    )(c_all, w, b.reshape(1, n))


def _ssm_prep_kernel(are_ref, aim_ref, ldt_ref, btr_ref, bti_ref, cr_ref, ci_ref, dsk_ref,
                     l1_ref, ws_ref, v_ref, qt_ref, am_ref):
    p, n = SSM_GROUP, SSM_STATE
    gps = are_ref.shape[1]
    nd = 2 * gps
    ar, ai = are_ref[...].reshape(nd, n), aim_ref[...].reshape(nd, n)
    dt = jnp.exp(ldt_ref[...].reshape(nd, 1))
    mag = jnp.exp(dt * ar)
    abr = mag * jnp.cos(dt * ai)
    abi = mag * jnp.sin(dt * ai)
    den = ar * ar + ai * ai
    nr = abr - 1.0
    cor = (nr * ar + abi * ai) / den
    coi = (abi * ar - nr * ai) / den
    btr, bti = btr_ref[...].reshape(nd, p, n), bti_ref[...].reshape(nd, p, n)
    bbr = cor[:, None, :] * btr - coi[:, None, :] * bti
    bbi = cor[:, None, :] * bti + coi[:, None, :] * btr
    cr, ci = cr_ref[...].reshape(nd, p, n), ci_ref[...].reshape(nd, p, n)
    pr, pi = jnp.ones_like(abr), jnp.zeros_like(abr)
    cps, pbs = [], []
    for k in range(CHUNK + 1):
        cpr = cr * pr[:, None, :] - ci * pi[:, None, :]
        cpi = cr * pi[:, None, :] + ci * pr[:, None, :]
        cps.append(jnp.concatenate([cpr, -cpi], axis=-1))
        if k < CHUNK:
            pbs.append(jnp.concatenate([bbr * pr[:, None, :] - bbi * pi[:, None, :],
                                        bbr * pi[:, None, :] + bbi * pr[:, None, :]], axis=-1))
            pr, pi = pr * abr - pi * abi, pr * abi + pi * abr
    qr, qi = pr, pi
    qrows = []
    for j in range(8):
        qrows.append(jnp.concatenate([qr, qi], axis=-1))
        qr, qi = qr * qr - qi * qi, 2.0 * qr * qi

    zero = jnp.zeros((p, 2 * n), F32)
    kwts = []
    for g in range(gps):
        f, b = g, gps + g
        rows = [jnp.concatenate([zero, cps[k][b]], axis=1) for k in range(CHUNK - 1, 0, -1)]
        rows.append(jnp.concatenate([cps[0][f], cps[0][b]], axis=1))
        rows += [jnp.concatenate([cps[k][f], zero], axis=1) for k in range(1, CHUNK)]
        rows.append(jnp.concatenate([zero, zero], axis=1))
        stack = jnp.concatenate(rows, axis=0)
        bbcat = jnp.concatenate([bbr[f], bbi[f], bbr[b], bbi[b]], axis=-1)
        kwts.append(lax.dot_general(bbcat, stack, (((1,), (1,)), ((), ())), precision=HIGHEST,
                                    preferred_element_type=F32))
    for g, kwt in enumerate(kwts):
        f, b = g, gps + g
        toep_t = jnp.concatenate(
            [kwt[:, (CHUNK - 1 - t) * p:(CHUNK - 1 - t) * p + CHUNK_W] for t in range(CHUNK)], axis=0)
        wst_f = jnp.concatenate([pbs[CHUNK - 1 - t][f] for t in range(CHUNK)], axis=0)
        wst_b = jnp.concatenate([pbs[t][b] for t in range(CHUNK)], axis=0)
        l1_ref[g] = toep_t.T.astype(MXU)
        low = lax.broadcasted_iota(jnp.int32, wst_f.shape, 1) < n
        tiles = [jnp.where(low, w, 0.0) for m in (wst_f, wst_b) for w in (m, pltpu.roll(m, n, 1))]
        ws_ref[g] = jnp.concatenate(tiles, axis=1).astype(MXU)
        v_f = jnp.concatenate([cps[t + 1][f] for t in range(CHUNK)], axis=0)
        v_b = jnp.concatenate([cps[CHUNK - t][b] for t in range(CHUNK)], axis=0)
        v_ref[g] = jnp.concatenate([v_f, v_b], axis=1).astype(MXU)
        qmat = jnp.concatenate([qrows[j][d * gps + g:d * gps + g + 1] for d in range(2) for j in range(8)]
                               + [dsk_ref[g]], axis=0)
        qpad = jnp.concatenate([qmat, jnp.zeros((LANES - qmat.shape[0], 2 * n), F32)], axis=0)
        qt_ref[g] = qpad.T
        a_f, a_b = qrows[0][f:f + 1], qrows[0][b:b + 1]
        zrow = jnp.zeros_like(a_f)
        am_ref[g] = jnp.concatenate([a_f, pltpu.roll(a_f, n, 1), a_b, pltpu.roll(a_b, n, 1)] + [zrow] * 4, axis=0)


def _ssm_prep(a_re, a_im, log_dt, b_re, b_im, c_re, c_im, d_skip):
    g, n, p = SSM_GROUPS, SSM_STATE, SSM_GROUP
    btr = jnp.transpose(b_re, (0, 1, 3, 2))
    bti = jnp.transpose(b_im, (0, 1, 3, 2))
    dsk = jnp.tile(d_skip.reshape(g, 1, p), (1, 1, 2 * n // p))
    gps = PREP_GROUPS
    vec = pl.BlockSpec((2, gps, n), lambda i: (0, i, 0))
    mat = pl.BlockSpec((2, gps, p, n), lambda i: (0, i, 0, 0))
    return pl.pallas_call(
        _ssm_prep_kernel,
        grid=(g // gps,),
        in_specs=[vec, vec, pl.BlockSpec((2, gps, 1), lambda i: (0, i, 0)), mat, mat, mat, mat,
                  pl.BlockSpec((gps, 1, 2 * n), lambda i: (i, 0, 0))],
        out_specs=[pl.BlockSpec((gps, CHUNK_W, CHUNK_W), lambda i: (i, 0, 0)),
                   pl.BlockSpec((gps, CHUNK_W, 4 * LANES), lambda i: (i, 0, 0)),
                   pl.BlockSpec((gps, CHUNK_W, CHUNK_W), lambda i: (i, 0, 0)),
                   pl.BlockSpec((gps, LANES, LANES), lambda i: (i, 0, 0)),
                   pl.BlockSpec((gps, 8, LANES), lambda i: (i, 0, 0))],
        out_shape=[jax.ShapeDtypeStruct((g, CHUNK_W, CHUNK_W), MXU),
                   jax.ShapeDtypeStruct((g, CHUNK_W, 4 * LANES), MXU),
                   jax.ShapeDtypeStruct((g, CHUNK_W, CHUNK_W), MXU),
                   jax.ShapeDtypeStruct((g, LANES, LANES), F32),
                   jax.ShapeDtypeStruct((g, 8, LANES), F32)],
        compiler_params=_cparams(("arbitrary",)),
    )(a_re, a_im, log_dt[..., None], btr, bti, c_re, c_im, dsk)


def _rows(slabs, idx):
    return jnp.concatenate([s[idx, :] for s in slabs], axis=1)


def _proj_kernel(x_ref, mod_ref, nw_ref, win_ref, cos_ref, s1_ref, s2_ref, q_ref, k_ref, vt_ref, ut_ref, *u_nat,
                 rope):
    ntok = x_ref.shape[0]
    nch = ntok // CHUNK
    sh = mod_ref[0:1, :]
    sc = mod_ref[1:2, :]
    nw = nw_ref[...]
    rb = 512

    def rot(tile, r0):
        if not rope:
            return tile
        cs = cos_ref[pl.ds(r0, rb), :]
        a1 = s1_ref[pl.ds(r0, rb), :]
        a2 = s2_ref[pl.ds(r0, rb), :]
        return tile * cs + pltpu.roll(tile, LANES - ROT_PAIRS, 1) * a1 + pltpu.roll(tile, ROT_PAIRS, 1) * a2

    def body(r, carry):
        r0 = pl.multiple_of(r * rb, rb)
        xb = x_ref[pl.ds(r0, rb), :]
        hn = xb * lax.rsqrt(jnp.mean(xb * xb, axis=-1, keepdims=True) + EPS) * nw
        hb = (hn * (1.0 + sc) + sh).astype(MXU)
        proj = jnp.dot(hb, win_ref[...], preferred_element_type=F32)
        gq = ATTN_HEADS // ATTN_KV_HEADS
        tiles = [rot(proj[:, j * LANES:(j + 1) * LANES], r0) * (ATTN_SCALE * LOG2E) for j in range(gq)]
        low = lax.broadcasted_iota(jnp.int32, tiles[0].shape, 1) < HEAD_DIM
        half = gq // 2
        for j in range(half):
            a, b = tiles[j], tiles[half + j]
            q_ref[pl.ds(r0, rb), (2 * j) * LANES:(2 * j + 1) * LANES] = jnp.where(
                low, a, pltpu.roll(b, HEAD_DIM, 1)).astype(MXU)
            q_ref[pl.ds(r0, rb), (2 * j + 1) * LANES:(2 * j + 2) * LANES] = jnp.where(
                low, pltpu.roll(a, HEAD_DIM, 1), b).astype(MXU)
        k_ref[pl.ds(r0, rb), :] = rot(proj[:, ATTN_WIDTH:ATTN_WIDTH + KV_WIDTH], r0).astype(MXU)
        vt_ref[:, pl.ds(r0, rb)] = proj[:, ATTN_WIDTH + KV_WIDTH:ATTN_WIDTH + 2 * KV_WIDTH].T.astype(MXU)
        u0 = ATTN_WIDTH + 2 * KV_WIDTH
        for j, ref in enumerate(u_nat):
            ref[pl.ds(r0, rb), :] = proj[:, u0 + j * LANES:u0 + (j + 1) * LANES]
        return carry

    lax.fori_loop(0, ntok // rb, body, 0)

    for t in range(CHUNK):
        u_t = _rows(u_nat, pl.ds(t, nch, stride=CHUNK))
        ut_ref[:, t * SSM_GROUP:(t + 1) * SSM_GROUP, :] = u_t.T.reshape(SSM_GROUPS, SSM_GROUP, nch)


def _proj(x, mod3, mod_row0, nw, win, cos_t, s1_t, s2_t, rope):
    nb, ntok, d = x.shape
    nch = ntok // CHUNK
    const2 = lambda b: (0, 0)
    tok = lambda width: pl.BlockSpec((None, ntok, width), lambda b: (b, 0, 0))
    tables = (cos_t, s1_t, s2_t)
    return pl.pallas_call(
        functools.partial(_proj_kernel, rope=rope),
        grid=(nb,),
        in_specs=[tok(d),
                  pl.BlockSpec((None, N_MOD, d), lambda b: (b + mod_row0, 0, 0)),
                  pl.BlockSpec((1, d), const2),
                  pl.BlockSpec(win.shape, const2)] + [pl.BlockSpec(t.shape, const2) for t in tables],
        out_specs=[tok(ATTN_WIDTH), tok(KV_WIDTH), pl.BlockSpec((None, KV_WIDTH, ntok), lambda b: (b, 0, 0)),
                   pl.BlockSpec((SSM_GROUPS, CHUNK_W, nch), lambda b: (0, 0, b))],
        out_shape=[jax.ShapeDtypeStruct((nb, ntok, ATTN_WIDTH), MXU),
                   jax.ShapeDtypeStruct((nb, ntok, KV_WIDTH), MXU),
                   jax.ShapeDtypeStruct((nb, KV_WIDTH, ntok), MXU),
                   jax.ShapeDtypeStruct((SSM_GROUPS, CHUNK_W, nb * nch), F32)],
        scratch_shapes=[pltpu.VMEM((ntok, LANES), F32)] * (SSM_WIDTH // LANES),
        compiler_params=_cparams(("arbitrary",)),
    )(x, mod3, nw, win, *tables)


def _attn_kernel(sink_ref, q_ref, k_ref, vt_ref, kc_ref, vct_ref, bias_ref, o_ref):
    blk = WINDOW
    seq = k_ref.shape[0]
    nwin = 3 * blk
    gq = ATTN_HEADS // ATTN_KV_HEADS
    nsub = q_ref.shape[0] // blk
    kc = kc_ref[...]
    ones = jnp.ones((16, 1), MXU)
    vct = jnp.concatenate([vct_ref[...], jnp.broadcast_to(ones, (16, kc.shape[0]))], axis=0)
    lane_q = lax.broadcasted_iota(jnp.int32, (blk, LANES), 1)
    nt = (((1,), (1,)), ((), ()))
    chains = [(sub, kh) for sub in range(nsub) for kh in range(ATTN_KV_HEADS)]

    starts, scores = {}, {}
    for sub in range(nsub):
        i = pl.program_id(1) * nsub + sub
        starts[sub] = (i, pl.multiple_of(jnp.clip((i - 1) * blk, 0, seq - nwin), blk))
    for sub, kh in chains:
        i, start = starts[sub]
        q = q_ref[sub * blk:(sub + 1) * blk, :]
        qs = jnp.concatenate(
            [jnp.where((lane_q // HEAD_DIM) == kh, q[:, j * LANES:(j + 1) * LANES], jnp.zeros((), MXU))
             for j in range(gq)], axis=0)
        s_loc = lax.dot_general(k_ref[pl.ds(start, nwin), :], qs, nt, preferred_element_type=F32)
        s_loc = s_loc + bias_ref[(i * blk - start) // blk]
        s_ctx = lax.dot_general(kc, qs, nt, preferred_element_type=F32)
        scores[sub, kh] = (s_loc, s_ctx)

    probs = {}
    for sub, kh in chains:
        s_loc, s_ctx = scores[sub, kh]
        sink = jnp.concatenate(
            [jnp.full((1, blk), sink_ref[kh * gq + j] * LOG2E, F32) for j in range(gq)], axis=1)
        m = jnp.maximum(jnp.maximum(jnp.max(s_loc, axis=0, keepdims=True),
                                    jnp.max(s_ctx, axis=0, keepdims=True)), sink)
        probs[sub, kh] = (jnp.exp2(s_loc - m).astype(MXU), jnp.exp2(s_ctx - m).astype(MXU), jnp.exp2(sink - m))

    outs = {}
    for sub, kh in chains:
        p_loc, p_ctx, p_sink = probs[sub, kh]
        _, start = starts[sub]
        vtw = jnp.concatenate([vt_ref[:, pl.ds(start, nwin)], jnp.broadcast_to(ones, (16, nwin))], axis=0)
        acc = (jnp.dot(vtw, p_loc, preferred_element_type=F32)
               + jnp.dot(vct, p_ctx, preferred_element_type=F32))
        outs[sub, kh] = acc[:KV_WIDTH] / (acc[KV_WIDTH:KV_WIDTH + 1] + p_sink)

    for sub in range(nsub):
        for pair in range(ATTN_HEADS // 2):
            kh, g0 = (2 * pair) // gq, (2 * pair) % gq
            rows = slice(kh * HEAD_DIM, (kh + 1) * HEAD_DIM)
            both = jnp.concatenate([outs[sub, kh][rows, g0 * blk:(g0 + 1) * blk],
                                    outs[sub, kh][rows, (g0 + 1) * blk:(g0 + 2) * blk]], axis=0)
            o_ref[sub * blk:(sub + 1) * blk, pair * LANES:(pair + 1) * LANES] = both.T


def _band_bias():
    gq = ATTN_HEADS // ATTN_KV_HEADS
    key = np.arange(3 * WINDOW)[None, :, None]
    qry = (np.arange(gq * WINDOW) % WINDOW)[None, None, :]
    var = np.arange(3)[:, None, None]
    inside = np.abs(key - var * WINDOW - qry) <= WINDOW
    return jnp.asarray(np.where(inside, 0.0, -np.inf), F32)


def _attention(sink, q, k, vt, kc, vct):
    nb, seq, _ = q.shape
    nctx = kc.shape[1]
    blk = ATTN_QBLOCKS * WINDOW
    bias = _band_bias()
    return pl.pallas_call(
        _attn_kernel,
        grid=(nb, seq // blk),
        in_specs=[pl.BlockSpec(memory_space=pltpu.SMEM),
                  pl.BlockSpec((None, blk, ATTN_WIDTH), lambda b, i: (b, i, 0)),
                  pl.BlockSpec((None, seq, KV_WIDTH), lambda b, i: (b, 0, 0)),
                  pl.BlockSpec((None, KV_WIDTH, seq), lambda b, i: (b, 0, 0)),
                  pl.BlockSpec((None, nctx, KV_WIDTH), lambda b, i: (b, 0, 0)),
                  pl.BlockSpec((None, KV_WIDTH, nctx), lambda b, i: (0, 0, b)),
                  pl.BlockSpec(bias.shape, lambda b, i: (0, 0, 0))],
        out_specs=pl.BlockSpec((None, blk, ATTN_WIDTH), lambda b, i: (b, i, 0)),
        out_shape=jax.ShapeDtypeStruct((nb, seq, ATTN_WIDTH), F32),
        compiler_params=_cparams(("arbitrary", "arbitrary")),
    )(sink, q, k, vt, kc, vct, bias)


def _ssm_kernel(ul_ref, uc_ref, l1_ref, ws_ref, v_ref, qt_ref, am_ref, y_ref, *scratch, nb):
    ng = ul_ref.shape[0]
    xt = [scratch[10 * g:10 * g + 4] for g in range(ng)]
    xct = [scratch[10 * g + 4:10 * g + 8] for g in range(ng)]
    hin = [scratch[10 * g + 8:10 * g + 10] for g in range(ng)]
    n = SSM_STATE
    width = ul_ref.shape[2]
    seg_l = width // nb
    seg_c = uc_ref.shape[2] // nb
    tn = (((0,), (0,)), ((), ()))
    pitch_l, pitch_c = seg_l + SCAN_PAD, seg_c + SCAN_PAD
    for g in range(ng):
        states = lax.dot_general(ul_ref[g].astype(MXU), ws_ref[g], tn, preferred_element_type=F32)
        states_c = lax.dot_general(uc_ref[g].astype(MXU), ws_ref[g], tn, preferred_element_type=F32)
        for i in range(4):
            for b in range(nb):
                xt[g][i][b * pitch_l:b * pitch_l + seg_l, :] = states[b * seg_l:(b + 1) * seg_l,
                                                                      i * LANES:(i + 1) * LANES]
                xct[g][i][b * pitch_c:b * pitch_c + seg_c, :] = states_c[b * seg_c:(b + 1) * seg_c,
                                                                         i * LANES:(i + 1) * LANES]
    mult = [[(am_ref[g, 2 * d:2 * d + 1, :], am_ref[g, 2 * d + 1:2 * d + 2, :]) for d in range(2)]
            for g in range(ng)]

    def advance(g, d, s, x, rows):
        (ar, ai), (sr, si) = mult[g][d], s
        xr, xi = x[2 * d][rows, :], x[2 * d + 1][rows, :]
        return ar * sr - ai * si + xr, ar * si + ai * sr + xi

    def rows_of(k, d, seg):
        return pl.ds(k if d == 0 else seg - 1 - k, nb, stride=seg + SCAN_PAD)

    zero = jnp.zeros((nb, LANES), F32)
    carry = tuple(tuple((zero, zero) for d in range(2)) for g in range(ng))
    for k in range(seg_c):
        carry = tuple(tuple(advance(g, d, carry[g][d], xct[g], rows_of(k, d, seg_c)) for d in range(2))
                      for g in range(ng))

    first_half = lax.broadcasted_iota(jnp.int32, (nb, LANES), 1) < n

    def step(k, carry):
        nxt = []
        for g in range(ng):
            per_dir = []
            for d in range(2):
                rows = rows_of(k, d, seg_l)
                sr, si = carry[g][d]
                hin[g][d][rows, :] = jnp.where(first_half, sr, pltpu.roll(si, n, 1))
                per_dir.append(advance(g, d, carry[g][d], xt[g], rows))
            nxt.append(tuple(per_dir))
        return tuple(nxt)

    lax.fori_loop(0, seg_l, step, carry, unroll=8)

    nt = (((1,), (1,)), ((), ()))
    for g in range(ng):
        ul = ul_ref[g]
        hcat = jnp.concatenate(
            [jnp.concatenate([hin[g][d][b * pitch_l:b * pitch_l + seg_l, :] for b in range(nb)], axis=0)
             for d in range(2)], axis=1).astype(MXU)
        dsk = jnp.concatenate([qt_ref[g, :, 16:17]] * (CHUNK_W // LANES), axis=0)
        y = (jnp.dot(l1_ref[g], ul.astype(MXU), preferred_element_type=F32)
             + lax.dot_general(v_ref[g], hcat, nt, preferred_element_type=F32) + dsk * ul)
        y_ref[g] = 0.5 * y * (1.0 + jnp.tanh(math.sqrt(2.0 / math.pi) * (y + 0.044715 * (y * y * y))))


def _ssm(ul, uc, l1, ws, vcat, qt, am, nb):
    g, _, width = ul.shape
    wc = uc.shape[2]
    ng = SSM_GROUPS_PER_STEP
    blk = lambda *shape: pl.BlockSpec((ng,) + shape, lambda i: (i,) + (0,) * len(shape))
    rows_l, rows_c = width + nb * SCAN_PAD, wc + nb * SCAN_PAD
    per_group = [pltpu.VMEM((rows_l, LANES), F32)] * 4 + [pltpu.VMEM((rows_c, LANES), F32)] * 4 \
        + [pltpu.VMEM((rows_l, LANES), F32)] * 2
    return pl.pallas_call(
        functools.partial(_ssm_kernel, nb=nb),
        grid=(g // ng,),
        in_specs=[blk(CHUNK_W, width), blk(CHUNK_W, wc), blk(CHUNK_W, CHUNK_W), blk(CHUNK_W, 4 * LANES),
                  blk(CHUNK_W, CHUNK_W), blk(LANES, LANES), blk(8, LANES)],
        out_specs=blk(CHUNK_W, width),
        out_shape=jax.ShapeDtypeStruct((g, CHUNK_W, width), F32),
        scratch_shapes=per_group * ng,
        compiler_params=_cparams(("arbitrary",)),
    )(ul, uc, l1, ws, vcat, qt, am)


def _route(logits_t, bias):
    ng, ne = N_EXPERT_GROUPS, N_EXPERTS // N_EXPERT_GROUPS
    t = logits_t.shape[1]
    scores = _sigmoid(logits_t).reshape(ng, ne, t)
    biased = scores + bias.reshape(ng, ne, 1)
    iw = lax.broadcasted_iota(jnp.int32, (ng, ne, t), 1)
    ig = lax.broadcasted_iota(jnp.int32, (ng, ne, t), 0)
    neg = -jnp.inf
    m1 = jnp.max(biased, axis=1, keepdims=True)
    i1 = jnp.min(jnp.where(biased == m1, iw, ne), axis=1, keepdims=True)
    m2 = jnp.max(jnp.where(iw == i1, neg, biased), axis=1, keepdims=True)
    gscore = jnp.broadcast_to(m1 + m2, (ng, ne, t))
    gsel = jnp.zeros((ng, ne, t), F32)
    cur = gscore
    for _ in range(TOPK_GROUPS):
        m = jnp.max(cur, axis=0, keepdims=True)
        gi = jnp.min(jnp.where(cur == m, ig, ng), axis=0, keepdims=True)
        pick = ig == gi
        gsel = jnp.where(pick, 1.0, gsel)
        cur = jnp.where(pick, neg, cur)
    cur = jnp.where(gsel > 0.0, biased, neg)
    flat = ig * ne + iw
    chosen = jnp.zeros((ng, ne, t), F32)
    for _ in range(TOP_K):
        m = jnp.max(jnp.max(cur, axis=0, keepdims=True), axis=1, keepdims=True)
        fi = jnp.min(jnp.min(jnp.where(cur == m, flat, N_EXPERTS), axis=0, keepdims=True), axis=1, keepdims=True)
        pick = flat == fi
        chosen = jnp.where(pick, 1.0, chosen)
        cur = jnp.where(pick, neg, cur)
    sel = jnp.where(chosen > 0.0, scores, 0.0)
    tot = jnp.sum(jnp.sum(sel, axis=1, keepdims=True), axis=0, keepdims=True)
    return (sel / tot * ROUTED_SCALE).reshape(N_EXPERTS, t)


def _mix_kernel(yt_ref, attn_ref, x_ref, mod_ref, wglu_ref, bglu_ref, gssm_ref, wouts_ref, gattn_ref, wouta_ref,
                nffn_ref, wrt_ref, rbias_ref, x1_ref, gates_ref, *ynat):
    nch = yt_ref.shape[3]
    part = pl.program_id(1)

    @pl.when(part == 0)
    def _():
        for t in range(CHUNK):
            y_t = yt_ref[:, t].reshape(SSM_WIDTH, nch).T
            for j, ref in enumerate(ynat):
                ref[pl.ds(t, nch, stride=CHUNK), :] = y_t[:, j * LANES:(j + 1) * LANES]

    base = part * x_ref.shape[0]
    g1 = mod_ref[2:3, :]
    zeros = jnp.zeros((LANES - N_EXPERTS, LANES), F32)

    def body(r, carry):
        r0 = pl.multiple_of(r * MIX_ROWS, MIX_ROWS)
        y = _rows(ynat, pl.ds(pl.multiple_of(base + r0, MIX_ROWS), MIX_ROWS))
        glu = jnp.dot(y.astype(MXU), wglu_ref[...], preferred_element_type=F32) + bglu_ref[...]
        z = y * _sigmoid(glu)
        zn = z * lax.rsqrt(jnp.mean(z * z, axis=-1, keepdims=True) + EPS) * gssm_ref[...]
        o_s = jnp.dot(zn.astype(MXU), wouts_ref[...], preferred_element_type=F32)
        a = attn_ref[pl.ds(r0, MIX_ROWS), :]
        an = a * lax.rsqrt(jnp.mean(a * a, axis=-1, keepdims=True) + EPS) * gattn_ref[...]
        o_a = jnp.dot(an.astype(MXU), wouta_ref[...], preferred_element_type=F32)
        x1 = x_ref[pl.ds(r0, MIX_ROWS), :] + g1 * (o_s + o_a)
        x1_ref[pl.ds(r0, MIX_ROWS), :] = x1
        h2 = _ffn_input(x1, mod_ref, nffn_ref)
        logits_t = _router_logits(wrt_ref[...], h2)
        gates_t = _route(logits_t, rbias_ref[...])
        for i in range(MIX_ROWS // LANES):
            piece = jnp.concatenate([gates_t[:, i * LANES:(i + 1) * LANES], zeros], axis=0)
            gates_ref[pl.ds(r0 + i * LANES, LANES), :] = piece.T
        return carry

    lax.fori_loop(0, x_ref.shape[0] // MIX_ROWS, body, 0)


def _split_bf16(a):
    hi = a.astype(MXU)
    return hi, (a - hi.astype(F32)).astype(MXU)


def _router_logits(w_t, h):
    nt = (((1,), (1,)), ((), ()))
    w_hi, w_lo = _split_bf16(w_t)
    h_hi, h_lo = _split_bf16(h)
    both = lax.dot_general(jnp.concatenate([w_hi, w_lo], axis=0), h_hi, nt, preferred_element_type=F32)
    ne = w_t.shape[0]
    return both[:ne] + both[ne:] + lax.dot_general(w_hi, h_lo, nt, preferred_element_type=F32)


def _ffn_input(x1, mod_ref, nffn_ref):
    h2 = x1 * lax.rsqrt(jnp.mean(x1 * x1, axis=-1, keepdims=True) + EPS) * nffn_ref[...]
    return h2 * (1.0 + mod_ref[4:5, :]) + mod_ref[3:4, :]


def _mix(yt, attn, x, mod3, wglu, bglu, gssm, wouts, gattn, wouta, nffn, wrt, rbias):
    nb, seq, d = x.shape
    nch = seq // CHUNK
    yt4 = yt.reshape(SSM_GROUPS, CHUNK, SSM_GROUP, nb * nch)
    c2 = lambda b, p: (0, 0)
    tok = lambda width: pl.BlockSpec((None, MIX_TOKENS, width), lambda b, p: (b, p, 0))
    return pl.pallas_call(
        _mix_kernel,
        grid=(nb, seq // MIX_TOKENS),
        in_specs=[pl.BlockSpec((SSM_GROUPS, CHUNK, SSM_GROUP, nch), lambda b, p: (0, 0, 0, b)),
                  tok(ATTN_WIDTH), tok(d),
                  pl.BlockSpec((None, N_MOD, d), lambda b, p: (b, 0, 0)),
                  pl.BlockSpec(wglu.shape, c2), pl.BlockSpec(bglu.shape, c2), pl.BlockSpec(gssm.shape, c2),
                  pl.BlockSpec(wouts.shape, c2), pl.BlockSpec(gattn.shape, c2), pl.BlockSpec(wouta.shape, c2),
                  pl.BlockSpec(nffn.shape, c2), pl.BlockSpec(wrt.shape, c2), pl.BlockSpec(rbias.shape, c2)],
        out_specs=[tok(d), tok(LANES)],
        out_shape=[jax.ShapeDtypeStruct((nb, seq, d), F32), jax.ShapeDtypeStruct((nb, seq, LANES), F32)],
        scratch_shapes=[pltpu.VMEM((seq, LANES), F32)] * (SSM_WIDTH // LANES),
        compiler_params=_cparams(("arbitrary", "arbitrary")),
    )(yt4, attn, x, mod3, wglu, bglu, gssm, wouts, gattn, wouta, nffn, wrt, rbias)


def _moe_kernel(x1_ref, gates_ref, mod_ref, nffn_ref, wg_ref, wu_ref, wd_ref, wsg_ref, wsu_ref, wsd_ref, nfin_ref,
                o_ref, acc_ref, hid_ref, h2_ref):
    s = pl.program_id(1)
    f = EXPERT_DIM

    def glu(h, wg, wu):
        gu = jnp.dot(h, jnp.concatenate([wg.astype(MXU), wu.astype(MXU)], axis=1), preferred_element_type=F32)
        g = gu[:, :f]
        return g * _sigmoid(g) * gu[:, f:]

    @pl.when(s == 0)
    def _():
        h2_ref[...] = _ffn_input(x1_ref[...], mod_ref, nffn_ref).astype(MXU)
        hs = glu(h2_ref[...], wsg_ref[...], wsu_ref[...])
        acc_ref[...] = jnp.dot(hs.astype(MXU), wsd_ref[...].astype(MXU), preferred_element_type=F32)

    h2 = h2_ref[...]
    gsh = pltpu.roll(gates_ref[...], (LANES - EXPERTS_PER_STEP * s) % LANES, 1)
    for e in range(EXPERTS_PER_STEP):
        hid = glu(h2, wg_ref[e], wu_ref[e]) * gsh[:, e:e + 1]
        hid_ref[:, e * f:(e + 1) * f] = hid.astype(MXU)
    acc_ref[...] += jnp.dot(hid_ref[...], wd_ref[...].astype(MXU), preferred_element_type=F32)

    @pl.when(s == pl.num_programs(1) - 1)
    def _():
        g2 = mod_ref[5:6, :]
        x2 = x1_ref[...] + g2 * acc_ref[...]
        o_ref[...] = x2 * lax.rsqrt(jnp.mean(x2 * x2, axis=-1, keepdims=True) + EPS) * nfin_ref[...]


def _moe(x1, gates, mod3, nffn, wg, wu, wd, wsg, wsu, wsd, nfin):
    nb, seq, d = x1.shape
    tiles_per_b = seq // MOE_TILE
    nsteps = N_EXPERTS // EXPERTS_PER_STEP
    c2 = lambda i, s: (0, 0)
    tok = lambda width: pl.BlockSpec((None, MOE_TILE, width), lambda i, s: (i // tiles_per_b, i % tiles_per_b, 0))
    experts = pl.BlockSpec((EXPERTS_PER_STEP, d, EXPERT_DIM), lambda i, s: (s, 0, 0))
    return pl.pallas_call(
        _moe_kernel,
        grid=(nb * tiles_per_b, nsteps),
        in_specs=[tok(d), tok(LANES),
                  pl.BlockSpec((None, N_MOD, d), lambda i, s: (i // tiles_per_b, 0, 0)),
                  pl.BlockSpec(nffn.shape, c2), experts, experts,
                  pl.BlockSpec((EXPERTS_PER_STEP * EXPERT_DIM, d), lambda i, s: (s, 0)),
                  pl.BlockSpec(wsg.shape, c2), pl.BlockSpec(wsu.shape, c2), pl.BlockSpec(wsd.shape, c2),
                  pl.BlockSpec(nfin.shape, c2)],
        out_specs=tok(d),
        out_shape=jax.ShapeDtypeStruct((nb, seq, d), F32),
        scratch_shapes=[pltpu.VMEM((MOE_TILE, d), F32),
                        pltpu.VMEM((MOE_TILE, EXPERTS_PER_STEP * EXPERT_DIM), MXU),
                        pltpu.VMEM((MOE_TILE, d), MXU)],
        compiler_params=_cparams(("arbitrary", "arbitrary")),
    )(x1, gates, mod3, nffn, wg, wu, wd, wsg, wsu, wsd, nfin)


def _rope_tables(seq):
    pos = np.arange(seq)
    inv = ROPE_THETA ** (-np.arange(ROT_PAIRS, dtype=np.float64) / ROT_PAIRS)
    ar, ac = (pos // GRID_W)[:, None] * inv, (pos % GRID_W)[:, None] * inv
    zero = np.zeros_like(ar)
    rep = LANES // HEAD_DIM
    cos_t = np.tile(np.concatenate([np.cos(ar), np.cos(ar), np.cos(ac), np.cos(ac)], axis=1), (1, rep))
    s1_t = np.tile(np.concatenate([-np.sin(ar), zero, -np.sin(ac), zero], axis=1), (1, rep))
    s2_t = np.tile(np.concatenate([zero, np.sin(ar), zero, np.sin(ac)], axis=1), (1, rep))
    return tuple(jnp.asarray(t, F32) for t in (cos_t, s1_t, s2_t))


def kernel(x, c, ctx, c_ctx, w_ada, b_ada, norm_mix, norm_ffn, w_in, attn_sink, ssm_a_re, ssm_a_im, ssm_log_dt, ssm_b_re, ssm_b_im, ssm_c_re, ssm_c_im, ssm_d, w_glu, b_glu, norm_attn_out, norm_ssm_out, w_out, w_router, router_bias, w_gate_e, w_up_e, w_down_e, w_gate_s, w_up_s, w_down_s, norm_final):
    nb, seq, d = x.shape
    nctx = ctx.shape[1]
    layer = 0

    pad = jnp.zeros((16 - nb - 1, d), F32)
    c_all = jnp.concatenate([c, c_ctx[None, :], pad], axis=0)
    mod3 = _ada(c_all, w_ada[layer], b_ada[layer]).reshape(16, N_MOD, d)

    win = w_in[layer].astype(MXU)
    nw = norm_mix[layer].reshape(1, d)
    cos_t, s1_t, s2_t = _rope_tables(seq)

    q, k, vt, ul = _proj(x, mod3, 0, nw, win, cos_t, s1_t, s2_t, True)
    _, kc, vct, uc = _proj(ctx.reshape(1, nb * nctx, d), mod3, nb, nw, win, cos_t, s1_t, s2_t, False)
    kc = kc.reshape(nb, nctx, KV_WIDTH)

    attn = _attention(attn_sink[layer], q, k, vt, kc, vct)

    l1, ws, vcat, qt, am = _ssm_prep(ssm_a_re[layer], ssm_a_im[layer], ssm_log_dt[layer], ssm_b_re[layer],
                                     ssm_b_im[layer], ssm_c_re[layer], ssm_c_im[layer], ssm_d[layer])
    yt = _ssm(ul, uc, l1, ws, vcat, qt, am, nb)

    w_out0 = w_out[layer]
    nffn = norm_ffn[layer].reshape(1, d)
    x1, gates = _mix(
        yt, attn, x, mod3,
        w_glu[layer].astype(MXU), b_glu[layer].reshape(1, SSM_WIDTH), norm_ssm_out[layer].reshape(1, SSM_WIDTH),
        w_out0[ATTN_WIDTH:].astype(MXU), norm_attn_out[layer].reshape(1, ATTN_WIDTH),
        w_out0[:ATTN_WIDTH].astype(MXU), nffn,
        w_router[layer].T, router_bias[layer].reshape(N_EXPERTS, 1))

    wd = w_down_e[layer].reshape(N_EXPERTS * EXPERT_DIM, d)
    return _moe(x1, gates, mod3, nffn, w_gate_e[layer], w_up_e[layer], wd,
                w_gate_s[layer], w_up_s[layer], w_down_s[layer], norm_final.reshape(1, d))
```

```python
import functools
import math

import jax
import jax.numpy as jnp
import numpy as np
from jax import lax
from jax.experimental import pallas as pl
from jax.experimental.pallas import tpu as pltpu

D_MODEL = 1024
EPS = 1e-6
N_MOD = 6
HEAD_DIM = 64
ATTN_HEADS = 8
ATTN_KV_HEADS = 2
ATTN_WIDTH = ATTN_HEADS * HEAD_DIM
KV_WIDTH = ATTN_KV_HEADS * HEAD_DIM
WINDOW = 128
ATTN_SCALE = HEAD_DIM ** -0.5
LOG2E = math.log2(math.e)
ROPE_THETA = 10000.0
ROT_PAIRS = HEAD_DIM // 4
GRID_W = 64
SSM_WIDTH = D_MODEL - ATTN_WIDTH
SSM_GROUP = 16
SSM_GROUPS = SSM_WIDTH // SSM_GROUP
SSM_STATE = 64
N_EXPERTS = 64
EXPERT_DIM = 128
TOP_K = 8
N_EXPERT_GROUPS = 8
TOPK_GROUPS = 4
ROUTED_SCALE = 2.5

CHUNK = 16
CHUNK_W = CHUNK * SSM_GROUP
LANES = 128
BF16_ROWS = 16
MOD_ROWS = 16
ATTN_QBLOCKS = 8
MIX_TOKENS = 1024
MIX_ROWS = 1024
MOE_TILE = 1024
EXPERTS_PER_STEP = 8
SCAN_PAD = 4
SSM_GROUPS_PER_STEP = 4
PREP_GROUPS = 8
VMEM_LIMIT = 56 * 1024 * 1024

MXU = jnp.bfloat16
F32 = jnp.float32
HIGHEST = lax.Precision.HIGHEST


def _sigmoid(x):
    return 0.5 * jnp.tanh(0.5 * x) + 0.5


def _cparams(sem):
    return pltpu.CompilerParams(dimension_semantics=sem, vmem_limit_bytes=VMEM_LIMIT)


def _ada_kernel(c_ref, w_ref, b_ref, o_ref):
    cv = c_ref[...]
    s_hi, s_lo = _split_bf16(cv * _sigmoid(cv))
    w_hi, w_lo = _split_bf16(w_ref[...])
    rows = cv.shape[0]
    both = jnp.dot(jnp.concatenate([s_hi, s_lo], axis=0), w_hi, preferred_element_type=F32)
    o_ref[...] = both[:rows] + both[rows:] + jnp.dot(s_hi, w_lo, preferred_element_type=F32) + b_ref[...]


def _ada(c_all, w, b):
    rows, d = c_all.shape
    n = w.shape[1]
    tn = 1024
    return pl.pallas_call(
        _ada_kernel,
        grid=(n // tn,),
        in_specs=[pl.BlockSpec((rows, d), lambda j: (0, 0)),
                  pl.BlockSpec((d, tn), lambda j: (0, j)),
                  pl.BlockSpec((1, tn), lambda j: (0, j))],
        out_specs=pl.BlockSpec((rows, tn), lambda j: (0, j)),
        out_shape=jax.ShapeDtypeStruct((rows, n), F32),
        compiler_params=_cparams(("arbitrary",)),
    )(c_all, w, b.reshape(1, n))


def _ssm_prep_kernel(are_ref, aim_ref, ldt_ref, btr_ref, bti_ref, cr_ref, ci_ref, dsk_ref,
                     l1_ref, ws_ref, v_ref, qt_ref, am_ref):
    p, n = SSM_GROUP, SSM_STATE
    gps = are_ref.shape[1]
    nd = 2 * gps
    ar, ai = are_ref[...].reshape(nd, n), aim_ref[...].reshape(nd, n)
    dt = jnp.exp(ldt_ref[...].reshape(nd, 1))
    mag = jnp.exp(dt * ar)
    abr = mag * jnp.cos(dt * ai)
    abi = mag * jnp.sin(dt * ai)
    den = ar * ar + ai * ai
    nr = abr - 1.0
    cor = (nr * ar + abi * ai) / den
    coi = (abi * ar - nr * ai) / den
    btr, bti = btr_ref[...].reshape(nd, p, n), bti_ref[...].reshape(nd, p, n)
    bbr = cor[:, None, :] * btr - coi[:, None, :] * bti
    bbi = cor[:, None, :] * bti + coi[:, None, :] * btr
    cr, ci = cr_ref[...].reshape(nd, p, n), ci_ref[...].reshape(nd, p, n)
    pr, pi = jnp.ones_like(abr), jnp.zeros_like(abr)
    cps, pbs = [], []
    for k in range(CHUNK + 1):
        cpr = cr * pr[:, None, :] - ci * pi[:, None, :]
        cpi = cr * pi[:, None, :] + ci * pr[:, None, :]
        cps.append(jnp.concatenate([cpr, -cpi], axis=-1))
        if k < CHUNK:
            pbs.append(jnp.concatenate([bbr * pr[:, None, :] - bbi * pi[:, None, :],
                                        bbr * pi[:, None, :] + bbi * pr[:, None, :]], axis=-1))
            pr, pi = pr * abr - pi * abi, pr * abi + pi * abr
    amult = jnp.concatenate([pr, pi], axis=-1)

    zero = jnp.zeros((p, 2 * n), F32)
    kwts = []
    for g in range(gps):
        f, b = g, gps + g
        rows = [jnp.concatenate([zero, cps[k][b]], axis=1) for k in range(CHUNK - 1, 0, -1)]
        rows.append(jnp.concatenate([cps[0][f], cps[0][b]], axis=1))
        rows += [jnp.concatenate([cps[k][f], zero], axis=1) for k in range(1, CHUNK)]
        rows.append(jnp.concatenate([zero, zero], axis=1))
        stack = jnp.concatenate(rows, axis=0)
        bbcat = jnp.concatenate([bbr[f], bbi[f], bbr[b], bbi[b]], axis=-1)
        kwts.append(lax.dot_general(bbcat, stack, (((1,), (1,)), ((), ())), precision=HIGHEST,
                                    preferred_element_type=F32))
    for g, kwt in enumerate(kwts):
        f, b = g, gps + g
        toep_t = jnp.concatenate(
            [kwt[:, (CHUNK - 1 - t) * p:(CHUNK - 1 - t) * p + CHUNK_W] for t in range(CHUNK)], axis=0)
        wst_f = jnp.concatenate([pbs[CHUNK - 1 - t][f] for t in range(CHUNK)], axis=0)
        wst_b = jnp.concatenate([pbs[t][b] for t in range(CHUNK)], axis=0)
        l1_ref[g] = toep_t.T.astype(MXU)
        low = lax.broadcasted_iota(jnp.int32, wst_f.shape, 1) < n
        tiles = [jnp.where(low, w, 0.0) for m in (wst_f, wst_b) for w in (m, pltpu.roll(m, n, 1))]
        ws_ref[g] = jnp.concatenate(tiles, axis=1).astype(MXU)
        v_f = jnp.concatenate([cps[t + 1][f] for t in range(CHUNK)], axis=0)
        v_b = jnp.concatenate([cps[CHUNK - t][b] for t in range(CHUNK)], axis=0)
        v_ref[g] = jnp.concatenate([v_f, v_b], axis=1).astype(MXU)
        skip = dsk_ref[g]
        qt_ref[g] = jnp.concatenate([skip, jnp.zeros((LANES - 1, 2 * n), F32)], axis=0).T
        a_f, a_b = amult[f:f + 1], amult[b:b + 1]
        zrow = jnp.zeros_like(a_f)
        am_ref[g] = jnp.concatenate([a_f, pltpu.roll(a_f, n, 1), a_b, pltpu.roll(a_b, n, 1)] + [zrow] * 4, axis=0)


def _ssm_prep(a_re, a_im, log_dt, b_re, b_im, c_re, c_im, d_skip):
    g, n, p = SSM_GROUPS, SSM_STATE, SSM_GROUP
    btr = jnp.transpose(b_re, (0, 1, 3, 2))
    bti = jnp.transpose(b_im, (0, 1, 3, 2))
    dsk = jnp.tile(d_skip.reshape(g, 1, p), (1, 1, 2 * n // p))
    gps = PREP_GROUPS
    vec = pl.BlockSpec((2, gps, n), lambda i: (0, i, 0))
    mat = pl.BlockSpec((2, gps, p, n), lambda i: (0, i, 0, 0))
    return pl.pallas_call(
        _ssm_prep_kernel,
        grid=(g // gps,),
        in_specs=[vec, vec, pl.BlockSpec((2, gps, 1), lambda i: (0, i, 0)), mat, mat, mat, mat,
                  pl.BlockSpec((gps, 1, 2 * n), lambda i: (i, 0, 0))],
        out_specs=[pl.BlockSpec((gps, CHUNK_W, CHUNK_W), lambda i: (i, 0, 0)),
                   pl.BlockSpec((gps, CHUNK_W, 4 * LANES), lambda i: (i, 0, 0)),
                   pl.BlockSpec((gps, CHUNK_W, CHUNK_W), lambda i: (i, 0, 0)),
                   pl.BlockSpec((gps, LANES, LANES), lambda i: (i, 0, 0)),
                   pl.BlockSpec((gps, 8, LANES), lambda i: (i, 0, 0))],
        out_shape=[jax.ShapeDtypeStruct((g, CHUNK_W, CHUNK_W), MXU),
                   jax.ShapeDtypeStruct((g, CHUNK_W, 4 * LANES), MXU),
                   jax.ShapeDtypeStruct((g, CHUNK_W, CHUNK_W), MXU),
                   jax.ShapeDtypeStruct((g, LANES, LANES), F32),
                   jax.ShapeDtypeStruct((g, 8, LANES), F32)],
        compiler_params=_cparams(("arbitrary",)),
    )(a_re, a_im, log_dt[..., None], btr, bti, c_re, c_im, dsk)


def _rows(slabs, idx):
    return jnp.concatenate([s[idx, :] for s in slabs], axis=1)


def _proj_kernel(x_ref, mod_ref, nw_ref, win_ref, cos_ref, s1_ref, s2_ref, q_ref, k_ref, vt_ref, ut_ref, *u_nat,
                 rope):
    ntok = x_ref.shape[0]
    nch = ntok // CHUNK
    sh = mod_ref[0:1, :]
    sc = mod_ref[1:2, :]
    nw = nw_ref[...]
    rb = 512

    def rot(tile, r0):
        if not rope:
            return tile
        cs = cos_ref[pl.ds(r0, rb), :]
        a1 = s1_ref[pl.ds(r0, rb), :]
        a2 = s2_ref[pl.ds(r0, rb), :]
        return tile * cs + pltpu.roll(tile, LANES - ROT_PAIRS, 1) * a1 + pltpu.roll(tile, ROT_PAIRS, 1) * a2

    pitch = CHUNK + SCAN_PAD

    def body(r):
        r0 = r * rb
        xb = x_ref[pl.ds(r0, rb), :]
        hn = xb * lax.rsqrt(jnp.mean(xb * xb, axis=-1, keepdims=True) + EPS) * nw
        hb = (hn * (1.0 + sc) + sh).astype(MXU)
        proj = jnp.dot(hb, win_ref[...], preferred_element_type=F32)
        gq = ATTN_HEADS // ATTN_KV_HEADS
        tiles = [rot(proj[:, j * LANES:(j + 1) * LANES], r0) * (ATTN_SCALE * LOG2E) for j in range(gq)]
        low = lax.broadcasted_iota(jnp.int32, tiles[0].shape, 1) < HEAD_DIM
        half = gq // 2
        for j in range(half):
            a, b = tiles[j], tiles[half + j]
            q_ref[pl.ds(r0, rb), (2 * j) * LANES:(2 * j + 1) * LANES] = jnp.where(
                low, a, pltpu.roll(b, HEAD_DIM, 1)).astype(MXU)
            q_ref[pl.ds(r0, rb), (2 * j + 1) * LANES:(2 * j + 2) * LANES] = jnp.where(
                low, pltpu.roll(a, HEAD_DIM, 1), b).astype(MXU)
        k_ref[pl.ds(r0, rb), :] = rot(proj[:, ATTN_WIDTH:ATTN_WIDTH + KV_WIDTH], r0).astype(MXU)
        vt_ref[:, pl.ds(r0, rb)] = proj[:, ATTN_WIDTH + KV_WIDTH:ATTN_WIDTH + 2 * KV_WIDTH].T.astype(MXU)
        u0 = ATTN_WIDTH + 2 * KV_WIDTH
        for j, ref in enumerate(u_nat):
            for c in range(rb // CHUNK):
                row = (r0 // CHUNK + c) * pitch
                ref[row:row + CHUNK, :] = proj[c * CHUNK:(c + 1) * CHUNK, u0 + j * LANES:u0 + (j + 1) * LANES]

    for r in range(ntok // rb):
        body(r)

    for t in range(CHUNK):
        u_t = _rows(u_nat, pl.ds(t, nch, stride=pitch))
        ut_ref[:, t * SSM_GROUP:(t + 1) * SSM_GROUP, :] = u_t.T.reshape(SSM_GROUPS, SSM_GROUP, nch)


def _proj(x, mod3, mod_row0, nw, win, cos_t, s1_t, s2_t, rope):
    nb, ntok, d = x.shape
    nch = ntok // CHUNK
    const2 = lambda b: (0, 0)
    tok = lambda width: pl.BlockSpec((None, ntok, width), lambda b: (b, 0, 0))
    tables = (cos_t, s1_t, s2_t)
    return pl.pallas_call(
        functools.partial(_proj_kernel, rope=rope),
        grid=(nb,),
        in_specs=[tok(d),
                  pl.BlockSpec((None, N_MOD, d), lambda b: (b + mod_row0, 0, 0)),
                  pl.BlockSpec((1, d), const2),
                  pl.BlockSpec(win.shape, const2)] + [pl.BlockSpec(t.shape, const2) for t in tables],
        out_specs=[tok(ATTN_WIDTH), tok(KV_WIDTH), pl.BlockSpec((None, KV_WIDTH, ntok), lambda b: (b, 0, 0)),
                   pl.BlockSpec((SSM_GROUPS, CHUNK_W, nch), lambda b: (0, 0, b))],
        out_shape=[jax.ShapeDtypeStruct((nb, ntok, ATTN_WIDTH), MXU),
                   jax.ShapeDtypeStruct((nb, ntok, KV_WIDTH), MXU),
                   jax.ShapeDtypeStruct((nb, KV_WIDTH, ntok), MXU),
                   jax.ShapeDtypeStruct((SSM_GROUPS, CHUNK_W, nb * nch), F32)],
        scratch_shapes=[pltpu.VMEM((nch * (CHUNK + SCAN_PAD), LANES), F32)] * (SSM_WIDTH // LANES),
        compiler_params=_cparams(("arbitrary",)),
    )(x, mod3, nw, win, *tables)


def _attn_kernel(sink_ref, q_ref, k_ref, vt_ref, kc_ref, vct_ref, bias_ref, o_ref):
    blk = WINDOW
    seq = k_ref.shape[0]
    nwin = 3 * blk
    gq = ATTN_HEADS // ATTN_KV_HEADS
    nsub = q_ref.shape[0] // blk
    kc = kc_ref[...]
    ones = jnp.ones((BF16_ROWS, 1), MXU)
    vct = [jnp.concatenate([vct_ref[kh * HEAD_DIM:(kh + 1) * HEAD_DIM, :],
                            jnp.broadcast_to(ones, (BF16_ROWS, kc.shape[0]))], axis=0) for kh in range(ATTN_KV_HEADS)]
    lane_q = lax.broadcasted_iota(jnp.int32, (blk, LANES), 1)
    nt = (((1,), (1,)), ((), ()))
    chains = [(sub, kh) for sub in range(nsub) for kh in range(ATTN_KV_HEADS)]

    starts, scores = {}, {}
    for sub in range(nsub):
        i = pl.program_id(1) * nsub + sub
        starts[sub] = (i, pl.multiple_of(jnp.clip((i - 1) * blk, 0, seq - nwin), blk))
    for sub, kh in chains:
        i, start = starts[sub]
        q = q_ref[sub * blk:(sub + 1) * blk, :]
        qs = jnp.concatenate(
            [jnp.where((lane_q // HEAD_DIM) == kh, q[:, j * LANES:(j + 1) * LANES], jnp.zeros((), MXU))
             for j in range(gq)], axis=0)
        s_loc = lax.dot_general(k_ref[pl.ds(start, nwin), :], qs, nt, preferred_element_type=F32)
        s_loc = s_loc + bias_ref[(i * blk - start) // blk]
        s_ctx = lax.dot_general(kc, qs, nt, preferred_element_type=F32)
        scores[sub, kh] = (s_loc, s_ctx)

    probs = {}
    for sub, kh in chains:
        s_loc, s_ctx = scores[sub, kh]
        sink = jnp.concatenate(
            [jnp.full((1, blk), sink_ref[kh * gq + j] * LOG2E, F32) for j in range(gq)], axis=1)
        m = jnp.maximum(jnp.maximum(jnp.max(s_loc, axis=0, keepdims=True),
                                    jnp.max(s_ctx, axis=0, keepdims=True)), sink)
        probs[sub, kh] = (jnp.exp2(s_loc - m).astype(MXU), jnp.exp2(s_ctx - m).astype(MXU), jnp.exp2(sink - m))

    outs = {}
    for sub, kh in chains:
        p_loc, p_ctx, p_sink = probs[sub, kh]
        _, start = starts[sub]
        dims = slice(kh * HEAD_DIM, (kh + 1) * HEAD_DIM)
        vtw = jnp.concatenate([vt_ref[dims, pl.ds(start, nwin)], jnp.broadcast_to(ones, (BF16_ROWS, nwin))], axis=0)
        acc = (jnp.dot(vtw, p_loc, preferred_element_type=F32)
               + jnp.dot(vct[kh], p_ctx, preferred_element_type=F32))
        outs[sub, kh] = acc[:HEAD_DIM] / (acc[HEAD_DIM:HEAD_DIM + 1] + p_sink)

    for sub in range(nsub):
        for pair in range(ATTN_HEADS // 2):
            kh, g0 = (2 * pair) // gq, (2 * pair) % gq
            both = jnp.concatenate([outs[sub, kh][:, g0 * blk:(g0 + 1) * blk],
                                    outs[sub, kh][:, (g0 + 1) * blk:(g0 + 2) * blk]], axis=0)
            o_ref[sub * blk:(sub + 1) * blk, pair * LANES:(pair + 1) * LANES] = both.T


def _band_bias():
    gq = ATTN_HEADS // ATTN_KV_HEADS
    key = np.arange(3 * WINDOW)[None, :, None]
    qry = (np.arange(gq * WINDOW) % WINDOW)[None, None, :]
    var = np.arange(3)[:, None, None]
    inside = np.abs(key - var * WINDOW - qry) <= WINDOW
    return jnp.asarray(np.where(inside, 0.0, -np.inf), F32)


def _attention(sink, q, k, vt, kc, vct):
    nb, seq, _ = q.shape
    nctx = kc.shape[1]
    blk = ATTN_QBLOCKS * WINDOW
    bias = _band_bias()
    return pl.pallas_call(
        _attn_kernel,
        grid=(nb, seq // blk),
        in_specs=[pl.BlockSpec(memory_space=pltpu.SMEM),
                  pl.BlockSpec((None, blk, ATTN_WIDTH), lambda b, i: (b, i, 0)),
                  pl.BlockSpec((None, seq, KV_WIDTH), lambda b, i: (b, 0, 0)),
                  pl.BlockSpec((None, KV_WIDTH, seq), lambda b, i: (b, 0, 0)),
                  pl.BlockSpec((None, nctx, KV_WIDTH), lambda b, i: (b, 0, 0)),
                  pl.BlockSpec((None, KV_WIDTH, nctx), lambda b, i: (0, 0, b)),
                  pl.BlockSpec(bias.shape, lambda b, i: (0, 0, 0))],
        out_specs=pl.BlockSpec((None, blk, ATTN_WIDTH), lambda b, i: (b, i, 0)),
        out_shape=jax.ShapeDtypeStruct((nb, seq, ATTN_WIDTH), F32),
        compiler_params=_cparams(("arbitrary", "arbitrary")),
    )(sink, q, k, vt, kc, vct, bias)


def _ssm_kernel(ul_ref, uc_ref, l1_ref, ws_ref, v_ref, qt_ref, am_ref, y_ref, *scratch, nb):
    ng = ul_ref.shape[0]
    xt = [scratch[10 * g:10 * g + 4] for g in range(ng)]
    xct = [scratch[10 * g + 4:10 * g + 8] for g in range(ng)]
    hin = [scratch[10 * g + 8:10 * g + 10] for g in range(ng)]
    n = SSM_STATE
    width = ul_ref.shape[2]
    seg_l = width // nb
    seg_c = uc_ref.shape[2] // nb
    tn = (((0,), (0,)), ((), ()))
    pitch_l, pitch_c = seg_l + SCAN_PAD, seg_c + SCAN_PAD
    for g in range(ng):
        states = lax.dot_general(ul_ref[g].astype(MXU), ws_ref[g], tn, preferred_element_type=F32)
        states_c = lax.dot_general(uc_ref[g].astype(MXU), ws_ref[g], tn, preferred_element_type=F32)
        for i in range(4):
            for b in range(nb):
                xt[g][i][b * pitch_l:b * pitch_l + seg_l, :] = states[b * seg_l:(b + 1) * seg_l,
                                                                      i * LANES:(i + 1) * LANES]
                xct[g][i][b * pitch_c:b * pitch_c + seg_c, :] = states_c[b * seg_c:(b + 1) * seg_c,
                                                                         i * LANES:(i + 1) * LANES]
    mult = [[(am_ref[g, 2 * d:2 * d + 1, :], am_ref[g, 2 * d + 1:2 * d + 2, :]) for d in range(2)]
            for g in range(ng)]

    def advance(g, d, s, x, rows):
        (ar, ai), (sr, si) = mult[g][d], s
        xr, xi = x[2 * d][rows, :], x[2 * d + 1][rows, :]
        return ar * sr - ai * si + xr, ar * si + ai * sr + xi

    def rows_of(k, d, seg):
        return pl.ds(k if d == 0 else seg - 1 - k, nb, stride=seg + SCAN_PAD)

    zero = jnp.zeros((nb, LANES), F32)
    carry = tuple(tuple((zero, zero) for d in range(2)) for g in range(ng))
    for k in range(seg_c):
        carry = tuple(tuple(advance(g, d, carry[g][d], xct[g], rows_of(k, d, seg_c)) for d in range(2))
                      for g in range(ng))

    first_half = lax.broadcasted_iota(jnp.int32, (nb, LANES), 1) < n

    def step(k, carry):
        nxt = []
        for g in range(ng):
            per_dir = []
            for d in range(2):
                rows = rows_of(k, d, seg_l)
                sr, si = carry[g][d]
                hin[g][d][rows, :] = jnp.where(first_half, sr, pltpu.roll(si, n, 1))
                per_dir.append(advance(g, d, carry[g][d], xt[g], rows))
            nxt.append(tuple(per_dir))
        return tuple(nxt)

    lax.fori_loop(0, seg_l, step, carry, unroll=8)

    nt = (((1,), (1,)), ((), ()))
    for g in range(ng):
        ul = ul_ref[g]
        hcat = jnp.concatenate(
            [jnp.concatenate([hin[g][d][b * pitch_l:b * pitch_l + seg_l, :] for b in range(nb)], axis=0)
             for d in range(2)], axis=1).astype(MXU)
        dsk = jnp.concatenate([qt_ref[g, :, 0:1]] * (CHUNK_W // LANES), axis=0)
        y = (jnp.dot(l1_ref[g], ul.astype(MXU), preferred_element_type=F32)
             + lax.dot_general(v_ref[g], hcat, nt, preferred_element_type=F32) + dsk * ul)
        y_ref[g] = 0.5 * y * (1.0 + jnp.tanh(math.sqrt(2.0 / math.pi) * (y + 0.044715 * (y * y * y))))


def _ssm(ul, uc, l1, ws, vcat, qt, am, nb):
    g, _, width = ul.shape
    wc = uc.shape[2]
    ng = SSM_GROUPS_PER_STEP
    blk = lambda *shape: pl.BlockSpec((ng,) + shape, lambda i: (i,) + (0,) * len(shape))
    rows_l, rows_c = width + nb * SCAN_PAD, wc + nb * SCAN_PAD
    per_group = [pltpu.VMEM((rows_l, LANES), F32)] * 4 + [pltpu.VMEM((rows_c, LANES), F32)] * 4 \
        + [pltpu.VMEM((rows_l, LANES), F32)] * 2
    return pl.pallas_call(
        functools.partial(_ssm_kernel, nb=nb),
        grid=(g // ng,),
        in_specs=[blk(CHUNK_W, width), blk(CHUNK_W, wc), blk(CHUNK_W, CHUNK_W), blk(CHUNK_W, 4 * LANES),
                  blk(CHUNK_W, CHUNK_W), blk(LANES, LANES), blk(8, LANES)],
        out_specs=blk(CHUNK_W, width),
        out_shape=jax.ShapeDtypeStruct((g, CHUNK_W, width), F32),
        scratch_shapes=per_group * ng,
        compiler_params=_cparams(("arbitrary",)),
    )(ul, uc, l1, ws, vcat, qt, am)


def _route(logits_t, bias):
    ng, ne = N_EXPERT_GROUPS, N_EXPERTS // N_EXPERT_GROUPS
    t = logits_t.shape[1]
    scores = _sigmoid(logits_t).reshape(ng, ne, t)
    biased = scores + bias.reshape(ng, ne, 1)
    iw = lax.broadcasted_iota(jnp.int32, (ng, ne, t), 1)
    ig = lax.broadcasted_iota(jnp.int32, (ng, ne, t), 0)
    neg = -jnp.inf
    m1 = jnp.max(biased, axis=1, keepdims=True)
    i1 = jnp.min(jnp.where(biased == m1, iw, ne), axis=1, keepdims=True)
    m2 = jnp.max(jnp.where(iw == i1, neg, biased), axis=1, keepdims=True)
    gscore = jnp.broadcast_to(m1 + m2, (ng, ne, t))
    gsel = jnp.zeros((ng, ne, t), F32)
    cur = gscore
    for _ in range(TOPK_GROUPS):
        m = jnp.max(cur, axis=0, keepdims=True)
        gi = jnp.min(jnp.where(cur == m, ig, ng), axis=0, keepdims=True)
        pick = ig == gi
        gsel = jnp.where(pick, 1.0, gsel)
        cur = jnp.where(pick, neg, cur)
    cur = jnp.where(gsel > 0.0, biased, neg)
    flat = ig * ne + iw
    chosen = jnp.zeros((ng, ne, t), F32)
    for _ in range(TOP_K):
        m = jnp.max(jnp.max(cur, axis=0, keepdims=True), axis=1, keepdims=True)
        fi = jnp.min(jnp.min(jnp.where(cur == m, flat, N_EXPERTS), axis=0, keepdims=True), axis=1, keepdims=True)
        pick = flat == fi
        chosen = jnp.where(pick, 1.0, chosen)
        cur = jnp.where(pick, neg, cur)
    sel = jnp.where(chosen > 0.0, scores, 0.0)
    tot = jnp.sum(jnp.sum(sel, axis=1, keepdims=True), axis=0, keepdims=True)
    return (sel / tot * ROUTED_SCALE).reshape(N_EXPERTS, t)


def _mix_kernel(yt_ref, attn_ref, x_ref, mod_ref, wglu_ref, bglu_ref, gssm_ref, wouts_ref, gattn_ref, wouta_ref,
                nffn_ref, wrt_ref, rbias_ref, x1_ref, gates_ref, *ynat):
    nch = yt_ref.shape[3]
    part = pl.program_id(1)

    @pl.when(part == 0)
    def _():
        for t in range(CHUNK):
            y_t = yt_ref[:, t].reshape(SSM_WIDTH, nch).T
            for j, ref in enumerate(ynat):
                ref[pl.ds(t, nch, stride=CHUNK), :] = y_t[:, j * LANES:(j + 1) * LANES]

    base = part * x_ref.shape[0]
    g1 = mod_ref[2:3, :]
    zeros = jnp.zeros((LANES - N_EXPERTS, LANES), F32)

    def body(r, carry):
        r0 = pl.multiple_of(r * MIX_ROWS, MIX_ROWS)
        y = _rows(ynat, pl.ds(pl.multiple_of(base + r0, MIX_ROWS), MIX_ROWS))
        glu = jnp.dot(y.astype(MXU), wglu_ref[...], preferred_element_type=F32) + bglu_ref[...]
        z = y * _sigmoid(glu)
        zn = z * lax.rsqrt(jnp.mean(z * z, axis=-1, keepdims=True) + EPS) * gssm_ref[...]
        o_s = jnp.dot(zn.astype(MXU), wouts_ref[...], preferred_element_type=F32)
        a = attn_ref[pl.ds(r0, MIX_ROWS), :]
        an = a * lax.rsqrt(jnp.mean(a * a, axis=-1, keepdims=True) + EPS) * gattn_ref[...]
        o_a = jnp.dot(an.astype(MXU), wouta_ref[...], preferred_element_type=F32)
        x1 = x_ref[pl.ds(r0, MIX_ROWS), :] + g1 * (o_s + o_a)
        x1_ref[pl.ds(r0, MIX_ROWS), :] = x1
        h2 = _ffn_input(x1, mod_ref, nffn_ref)
        logits_t = _router_logits(wrt_ref[...], h2)
        gates_t = _route(logits_t, rbias_ref[...])
        for i in range(MIX_ROWS // LANES):
            piece = jnp.concatenate([gates_t[:, i * LANES:(i + 1) * LANES], zeros], axis=0)
            gates_ref[pl.ds(r0 + i * LANES, LANES), :] = piece.T
        return carry

    lax.fori_loop(0, x_ref.shape[0] // MIX_ROWS, body, 0)


def _split_bf16(a):
    hi = a.astype(MXU)
    return hi, (a - hi.astype(F32)).astype(MXU)


def _router_logits(w_t, h):
    nt = (((1,), (1,)), ((), ()))
    w_hi, w_lo = _split_bf16(w_t)
    h_hi, h_lo = _split_bf16(h)
    both = lax.dot_general(jnp.concatenate([w_hi, w_lo], axis=0), h_hi, nt, preferred_element_type=F32)
    ne = w_t.shape[0]
    return both[:ne] + both[ne:] + lax.dot_general(w_hi, h_lo, nt, preferred_element_type=F32)


def _ffn_input(x1, mod_ref, nffn_ref):
    h2 = x1 * lax.rsqrt(jnp.mean(x1 * x1, axis=-1, keepdims=True) + EPS) * nffn_ref[...]
    return h2 * (1.0 + mod_ref[4:5, :]) + mod_ref[3:4, :]


def _mix(yt, attn, x, mod3, wglu, bglu, gssm, wouts, gattn, wouta, nffn, wrt, rbias):
    nb, seq, d = x.shape
    nch = seq // CHUNK
    yt4 = yt.reshape(SSM_GROUPS, CHUNK, SSM_GROUP, nb * nch)
    c2 = lambda b, p: (0, 0)
    tok = lambda width: pl.BlockSpec((None, MIX_TOKENS, width), lambda b, p: (b, p, 0))
    return pl.pallas_call(
        _mix_kernel,
        grid=(nb, seq // MIX_TOKENS),
        in_specs=[pl.BlockSpec((SSM_GROUPS, CHUNK, SSM_GROUP, nch), lambda b, p: (0, 0, 0, b)),
                  tok(ATTN_WIDTH), tok(d),
                  pl.BlockSpec((None, N_MOD, d), lambda b, p: (b, 0, 0)),
                  pl.BlockSpec(wglu.shape, c2), pl.BlockSpec(bglu.shape, c2), pl.BlockSpec(gssm.shape, c2),
                  pl.BlockSpec(wouts.shape, c2), pl.BlockSpec(gattn.shape, c2), pl.BlockSpec(wouta.shape, c2),
                  pl.BlockSpec(nffn.shape, c2), pl.BlockSpec(wrt.shape, c2), pl.BlockSpec(rbias.shape, c2)],
        out_specs=[tok(d), tok(LANES)],
        out_shape=[jax.ShapeDtypeStruct((nb, seq, d), F32), jax.ShapeDtypeStruct((nb, seq, LANES), F32)],
        scratch_shapes=[pltpu.VMEM((seq, LANES), F32)] * (SSM_WIDTH // LANES),
        compiler_params=_cparams(("arbitrary", "arbitrary")),
    )(yt4, attn, x, mod3, wglu, bglu, gssm, wouts, gattn, wouta, nffn, wrt, rbias)


def _moe_kernel(x1_ref, gates_ref, mod_ref, nffn_ref, wg_ref, wu_ref, wd_ref, wsg_ref, wsu_ref, wsd_ref, nfin_ref,
                o_ref, acc_ref, hid_ref, h2_ref):
    s = pl.program_id(1)
    f = EXPERT_DIM

    def glu(h, wg, wu):
        gu = jnp.dot(h, jnp.concatenate([wg.astype(MXU), wu.astype(MXU)], axis=1), preferred_element_type=F32)
        g = gu[:, :f]
        return g * _sigmoid(g) * gu[:, f:]

    @pl.when(s == 0)
    def _():
        h2_ref[...] = _ffn_input(x1_ref[...], mod_ref, nffn_ref).astype(MXU)
        hs = glu(h2_ref[...], wsg_ref[...], wsu_ref[...])
        acc_ref[...] = jnp.dot(hs.astype(MXU), wsd_ref[...].astype(MXU), preferred_element_type=F32)

    h2 = h2_ref[...]
    gsh = pltpu.roll(gates_ref[...], (LANES - EXPERTS_PER_STEP * s) % LANES, 1)
    for e in range(EXPERTS_PER_STEP):
        hid = glu(h2, wg_ref[e], wu_ref[e]) * gsh[:, e:e + 1]
        hid_ref[:, e * f:(e + 1) * f] = hid.astype(MXU)
    acc_ref[...] += jnp.dot(hid_ref[...], wd_ref[...].astype(MXU), preferred_element_type=F32)

    @pl.when(s == pl.num_programs(1) - 1)
    def _():
        g2 = mod_ref[5:6, :]
        x2 = x1_ref[...] + g2 * acc_ref[...]
        o_ref[...] = x2 * lax.rsqrt(jnp.mean(x2 * x2, axis=-1, keepdims=True) + EPS) * nfin_ref[...]


def _moe(x1, gates, mod3, nffn, wg, wu, wd, wsg, wsu, wsd, nfin):
    nb, seq, d = x1.shape
    tiles_per_b = seq // MOE_TILE
    nsteps = N_EXPERTS // EXPERTS_PER_STEP
    c2 = lambda i, s: (0, 0)
    tok = lambda width: pl.BlockSpec((None, MOE_TILE, width), lambda i, s: (i // tiles_per_b, i % tiles_per_b, 0))
    experts = pl.BlockSpec((EXPERTS_PER_STEP, d, EXPERT_DIM), lambda i, s: (s, 0, 0))
    return pl.pallas_call(
        _moe_kernel,
        grid=(nb * tiles_per_b, nsteps),
        in_specs=[tok(d), tok(LANES),
                  pl.BlockSpec((None, N_MOD, d), lambda i, s: (i // tiles_per_b, 0, 0)),
                  pl.BlockSpec(nffn.shape, c2), experts, experts,
                  pl.BlockSpec((EXPERTS_PER_STEP * EXPERT_DIM, d), lambda i, s: (s, 0)),
                  pl.BlockSpec(wsg.shape, c2), pl.BlockSpec(wsu.shape, c2), pl.BlockSpec(wsd.shape, c2),
                  pl.BlockSpec(nfin.shape, c2)],
        out_specs=tok(d),
        out_shape=jax.ShapeDtypeStruct((nb, seq, d), F32),
        scratch_shapes=[pltpu.VMEM((MOE_TILE, d), F32),
                        pltpu.VMEM((MOE_TILE, EXPERTS_PER_STEP * EXPERT_DIM), MXU),
                        pltpu.VMEM((MOE_TILE, d), MXU)],
        compiler_params=_cparams(("arbitrary", "arbitrary")),
    )(x1, gates, mod3, nffn, wg, wu, wd, wsg, wsu, wsd, nfin)


def _rope_tables(seq):
    pos = np.arange(seq)
    inv = ROPE_THETA ** (-np.arange(ROT_PAIRS, dtype=np.float64) / ROT_PAIRS)
    ar, ac = (pos // GRID_W)[:, None] * inv, (pos % GRID_W)[:, None] * inv
    zero = np.zeros_like(ar)
    rep = LANES // HEAD_DIM
    cos_t = np.tile(np.concatenate([np.cos(ar), np.cos(ar), np.cos(ac), np.cos(ac)], axis=1), (1, rep))
    s1_t = np.tile(np.concatenate([-np.sin(ar), zero, -np.sin(ac), zero], axis=1), (1, rep))
    s2_t = np.tile(np.concatenate([zero, np.sin(ar), zero, np.sin(ac)], axis=1), (1, rep))
    return tuple(jnp.asarray(t, F32) for t in (cos_t, s1_t, s2_t))


def kernel(x, c, ctx, c_ctx, w_ada, b_ada, norm_mix, norm_ffn, w_in, attn_sink, ssm_a_re, ssm_a_im, ssm_log_dt, ssm_b_re, ssm_b_im, ssm_c_re, ssm_c_im, ssm_d, w_glu, b_glu, norm_attn_out, norm_ssm_out, w_out, w_router, router_bias, w_gate_e, w_up_e, w_down_e, w_gate_s, w_up_s, w_down_s, norm_final):
    nb, seq, d = x.shape
    nctx = ctx.shape[1]
    layer = 0

    pad = jnp.zeros((MOD_ROWS - nb - 1, d), F32)
    c_all = jnp.concatenate([c, c_ctx[None, :], pad], axis=0)
    mod3 = _ada(c_all, w_ada[layer], b_ada[layer]).reshape(MOD_ROWS, N_MOD, d)

    win = w_in[layer].astype(MXU)
    nw = norm_mix[layer].reshape(1, d)
    cos_t, s1_t, s2_t = _rope_tables(seq)

    q, k, vt, ul = _proj(x, mod3, 0, nw, win, cos_t, s1_t, s2_t, True)
    _, kc, vct, uc = _proj(ctx.reshape(1, nb * nctx, d), mod3, nb, nw, win, cos_t, s1_t, s2_t, False)
    kc = kc.reshape(nb, nctx, KV_WIDTH)

    attn = _attention(attn_sink[layer], q, k, vt, kc, vct)

    l1, ws, vcat, qt, am = _ssm_prep(ssm_a_re[layer], ssm_a_im[layer], ssm_log_dt[layer], ssm_b_re[layer],
                                     ssm_b_im[layer], ssm_c_re[layer], ssm_c_im[layer], ssm_d[layer])
    yt = _ssm(ul, uc, l1, ws, vcat, qt, am, nb)

    w_out0 = w_out[layer]
    nffn = norm_ffn[layer].reshape(1, d)
    x1, gates = _mix(
        yt, attn, x, mod3,
        w_glu[layer].astype(MXU), b_glu[layer].reshape(1, SSM_WIDTH), norm_ssm_out[layer].reshape(1, SSM_WIDTH),
        w_out0[ATTN_WIDTH:].astype(MXU), norm_attn_out[layer].reshape(1, ATTN_WIDTH),
        w_out0[:ATTN_WIDTH].astype(MXU), nffn,
        w_router[layer].T, router_bias[layer].reshape(N_EXPERTS, 1))

    wd = w_down_e[layer].reshape(N_EXPERTS * EXPERT_DIM, d)
    return _moe(x1, gates, mod3, nffn, w_gate_e[layer], w_up_e[layer], wd,
                w_gate_s[layer], w_up_s[layer], w_down_s[layer], norm_final.reshape(1, d))
```

```python
import functools
import math

import jax
import jax.numpy as jnp
import numpy as np
from jax import lax
from jax.experimental import pallas as pl
from jax.experimental.pallas import tpu as pltpu

D_MODEL = 1024
EPS = 1e-6
N_MOD = 6
HEAD_DIM = 64
ATTN_HEADS = 8
ATTN_KV_HEADS = 2
ATTN_WIDTH = ATTN_HEADS * HEAD_DIM
KV_WIDTH = ATTN_KV_HEADS * HEAD_DIM
WINDOW = 128
ATTN_SCALE = HEAD_DIM ** -0.5
LOG2E = math.log2(math.e)
ROPE_THETA = 10000.0
ROT_PAIRS = HEAD_DIM // 4
GRID_W = 64
SSM_WIDTH = D_MODEL - ATTN_WIDTH
SSM_GROUP = 16
SSM_GROUPS = SSM_WIDTH // SSM_GROUP
SSM_STATE = 64
N_EXPERTS = 64
EXPERT_DIM = 128
TOP_K = 8
N_EXPERT_GROUPS = 8
TOPK_GROUPS = 4
ROUTED_SCALE = 2.5

CHUNK = 16
CHUNK_W = CHUNK * SSM_GROUP
LANES = 128
BF16_ROWS = 16
MOD_ROWS = 16
ATTN_QBLOCKS = 8
MIX_TOKENS = 1024
MIX_ROWS = 1024
MOE_TILE = 1024
EXPERTS_PER_STEP = 8
SCAN_PAD = 4
SSM_GROUPS_PER_STEP = 4
PREP_GROUPS = 8
VMEM_LIMIT = 56 * 1024 * 1024

MXU = jnp.bfloat16
F32 = jnp.float32
HIGHEST = lax.Precision.HIGHEST


def _sigmoid(x):
    return 0.5 * jnp.tanh(0.5 * x) + 0.5


def _cparams(sem):
    return pltpu.CompilerParams(dimension_semantics=sem, vmem_limit_bytes=VMEM_LIMIT)


def _ada_kernel(c_ref, w_ref, b_ref, o_ref):
    cv = c_ref[...]
    s_hi, s_lo = _split_bf16(cv * _sigmoid(cv))
    w_hi, w_lo = _split_bf16(w_ref[...])
    rows = cv.shape[0]
    both = jnp.dot(jnp.concatenate([s_hi, s_lo], axis=0), w_hi, preferred_element_type=F32)
    o_ref[...] = both[:rows] + both[rows:] + jnp.dot(s_hi, w_lo, preferred_element_type=F32) + b_ref[...]


def _ada(c_all, w, b):
    rows, d = c_all.shape
    n = w.shape[1]
    tn = 1024
    return pl.pallas_call(
        _ada_kernel,
        grid=(n // tn,),
        in_specs=[pl.BlockSpec((rows, d), lambda j: (0, 0)),
                  pl.BlockSpec((d, tn), lambda j: (0, j)),
                  pl.BlockSpec((1, tn), lambda j: (0, j))],
        out_specs=pl.BlockSpec((rows, tn), lambda j: (0, j)),
        out_shape=jax.ShapeDtypeStruct((rows, n), F32),
        compiler_params=_cparams(("arbitrary",)),
    )(c_all, w, b.reshape(1, n))


def _ssm_prep_kernel(are_ref, aim_ref, ldt_ref, btr_ref, bti_ref, cr_ref, ci_ref, dsk_ref,
                     l1_ref, ws_ref, v_ref, qt_ref, am_ref):
    p, n = SSM_GROUP, SSM_STATE
    gps = are_ref.shape[1]
    nd = 2 * gps
    ar, ai = are_ref[...].reshape(nd, n), aim_ref[...].reshape(nd, n)
    dt = jnp.exp(ldt_ref[...].reshape(nd, 1))
    mag = jnp.exp(dt * ar)
    abr = mag * jnp.cos(dt * ai)
    abi = mag * jnp.sin(dt * ai)
    den = ar * ar + ai * ai
    nr = abr - 1.0
    cor = (nr * ar + abi * ai) / den
    coi = (abi * ar - nr * ai) / den
    btr, bti = btr_ref[...].reshape(nd, p, n), bti_ref[...].reshape(nd, p, n)
    bbr = cor[:, None, :] * btr - coi[:, None, :] * bti
    bbi = cor[:, None, :] * bti + coi[:, None, :] * btr
    cr, ci = cr_ref[...].reshape(nd, p, n), ci_ref[...].reshape(nd, p, n)
    pr, pi = jnp.ones_like(abr), jnp.zeros_like(abr)
    cps, pbs = [], []
    for k in range(CHUNK + 1):
        cpr = cr * pr[:, None, :] - ci * pi[:, None, :]
        cpi = cr * pi[:, None, :] + ci * pr[:, None, :]
        cps.append(jnp.concatenate([cpr, -cpi], axis=-1))
        if k < CHUNK:
            pbs.append(jnp.concatenate([bbr * pr[:, None, :] - bbi * pi[:, None, :],
                                        bbr * pi[:, None, :] + bbi * pr[:, None, :]], axis=-1))
            pr, pi = pr * abr - pi * abi, pr * abi + pi * abr
    amult = jnp.concatenate([pr, pi], axis=-1)

    zero = jnp.zeros((p, 2 * n), F32)
    kwts = []
    for g in range(gps):
        f, b = g, gps + g
        rows = [jnp.concatenate([zero, cps[k][b]], axis=1) for k in range(CHUNK - 1, 0, -1)]
        rows.append(jnp.concatenate([cps[0][f], cps[0][b]], axis=1))
        rows += [jnp.concatenate([cps[k][f], zero], axis=1) for k in range(1, CHUNK)]
        rows.append(jnp.concatenate([zero, zero], axis=1))
        stack = jnp.concatenate(rows, axis=0)
        bbcat = jnp.concatenate([bbr[f], bbi[f], bbr[b], bbi[b]], axis=-1)
        kwts.append(lax.dot_general(bbcat, stack, (((1,), (1,)), ((), ())), precision=HIGHEST,
                                    preferred_element_type=F32))
    for g, kwt in enumerate(kwts):
        f, b = g, gps + g
        toep_t = jnp.concatenate(
            [kwt[:, (CHUNK - 1 - t) * p:(CHUNK - 1 - t) * p + CHUNK_W] for t in range(CHUNK)], axis=0)
        wst_f = jnp.concatenate([pbs[CHUNK - 1 - t][f] for t in range(CHUNK)], axis=0)
        wst_b = jnp.concatenate([pbs[t][b] for t in range(CHUNK)], axis=0)
        l1_ref[g] = toep_t.T.astype(MXU)
        low = lax.broadcasted_iota(jnp.int32, wst_f.shape, 1) < n
        tiles = [jnp.where(low, w, 0.0) for m in (wst_f, wst_b) for w in (m, pltpu.roll(m, n, 1))]
        ws_ref[g] = jnp.concatenate(tiles, axis=1).astype(MXU)
        v_f = jnp.concatenate([cps[t + 1][f] for t in range(CHUNK)], axis=0)
        v_b = jnp.concatenate([cps[CHUNK - t][b] for t in range(CHUNK)], axis=0)
        v_ref[g] = jnp.concatenate([v_f, v_b], axis=1).astype(MXU)
        skip = dsk_ref[g]
        qt_ref[g] = jnp.concatenate([skip, jnp.zeros((LANES - 1, 2 * n), F32)], axis=0).T
        a_f, a_b = amult[f:f + 1], amult[b:b + 1]
        zrow = jnp.zeros_like(a_f)
        am_ref[g] = jnp.concatenate([a_f, pltpu.roll(a_f, n, 1), a_b, pltpu.roll(a_b, n, 1)] + [zrow] * 4, axis=0)


def _ssm_prep(a_re, a_im, log_dt, b_re, b_im, c_re, c_im, d_skip):
    g, n, p = SSM_GROUPS, SSM_STATE, SSM_GROUP
    btr = jnp.transpose(b_re, (0, 1, 3, 2))
    bti = jnp.transpose(b_im, (0, 1, 3, 2))
    dsk = jnp.tile(d_skip.reshape(g, 1, p), (1, 1, 2 * n // p))
    gps = PREP_GROUPS
    vec = pl.BlockSpec((2, gps, n), lambda i: (0, i, 0))
    mat = pl.BlockSpec((2, gps, p, n), lambda i: (0, i, 0, 0))
    return pl.pallas_call(
        _ssm_prep_kernel,
        grid=(g // gps,),
        in_specs=[vec, vec, pl.BlockSpec((2, gps, 1), lambda i: (0, i, 0)), mat, mat, mat, mat,
                  pl.BlockSpec((gps, 1, 2 * n), lambda i: (i, 0, 0))],
        out_specs=[pl.BlockSpec((gps, CHUNK_W, CHUNK_W), lambda i: (i, 0, 0)),
                   pl.BlockSpec((gps, CHUNK_W, 4 * LANES), lambda i: (i, 0, 0)),
                   pl.BlockSpec((gps, CHUNK_W, CHUNK_W), lambda i: (i, 0, 0)),
                   pl.BlockSpec((gps, LANES, LANES), lambda i: (i, 0, 0)),
                   pl.BlockSpec((gps, 8, LANES), lambda i: (i, 0, 0))],
        out_shape=[jax.ShapeDtypeStruct((g, CHUNK_W, CHUNK_W), MXU),
                   jax.ShapeDtypeStruct((g, CHUNK_W, 4 * LANES), MXU),
                   jax.ShapeDtypeStruct((g, CHUNK_W, CHUNK_W), MXU),
                   jax.ShapeDtypeStruct((g, LANES, LANES), F32),
                   jax.ShapeDtypeStruct((g, 8, LANES), F32)],
        compiler_params=_cparams(("arbitrary",)),
    )(a_re, a_im, log_dt[..., None], btr, bti, c_re, c_im, dsk)


def _rows(slabs, idx):
    return jnp.concatenate([s[idx, :] for s in slabs], axis=1)


def _proj_kernel(x_ref, mod_ref, nw_ref, win_ref, cos_ref, s1_ref, s2_ref, q_ref, k_ref, vt_ref, ut_ref, *u_nat,
                 rope):
    ntok = x_ref.shape[0]
    nch = ntok // CHUNK
    sh = mod_ref[0:1, :]
    sc = mod_ref[1:2, :]
    nw = nw_ref[...]
    rb = 512

    def rot(tile, r0):
        if not rope:
            return tile
        cs = cos_ref[pl.ds(r0, rb), :]
        a1 = s1_ref[pl.ds(r0, rb), :]
        a2 = s2_ref[pl.ds(r0, rb), :]
        return tile * cs + pltpu.roll(tile, LANES - ROT_PAIRS, 1) * a1 + pltpu.roll(tile, ROT_PAIRS, 1) * a2

    pitch = CHUNK + SCAN_PAD

    def body(r):
        r0 = r * rb
        xb = x_ref[pl.ds(r0, rb), :]
        hn = xb * lax.rsqrt(jnp.mean(xb * xb, axis=-1, keepdims=True) + EPS) * nw
        hb = (hn * (1.0 + sc) + sh).astype(MXU)
        proj = jnp.dot(hb, win_ref[...], preferred_element_type=F32)
        gq = ATTN_HEADS // ATTN_KV_HEADS
        tiles = [rot(proj[:, j * LANES:(j + 1) * LANES], r0) * (ATTN_SCALE * LOG2E) for j in range(gq)]
        low = lax.broadcasted_iota(jnp.int32, tiles[0].shape, 1) < HEAD_DIM
        half = gq // 2
        for j in range(half):
            a, b = tiles[j], tiles[half + j]
            q_ref[pl.ds(r0, rb), (2 * j) * LANES:(2 * j + 1) * LANES] = jnp.where(
                low, a, pltpu.roll(b, HEAD_DIM, 1)).astype(MXU)
            q_ref[pl.ds(r0, rb), (2 * j + 1) * LANES:(2 * j + 2) * LANES] = jnp.where(
                low, pltpu.roll(a, HEAD_DIM, 1), b).astype(MXU)
        k_ref[pl.ds(r0, rb), :] = rot(proj[:, ATTN_WIDTH:ATTN_WIDTH + KV_WIDTH], r0).astype(MXU)
        vt_ref[:, pl.ds(r0, rb)] = proj[:, ATTN_WIDTH + KV_WIDTH:ATTN_WIDTH + 2 * KV_WIDTH].T.astype(MXU)
        u0 = ATTN_WIDTH + 2 * KV_WIDTH
        for j, ref in enumerate(u_nat):
            for c in range(rb // CHUNK):
                row = (r0 // CHUNK + c) * pitch
                ref[row:row + CHUNK, :] = proj[c * CHUNK:(c + 1) * CHUNK, u0 + j * LANES:u0 + (j + 1) * LANES]

    for r in range(ntok // rb):
        body(r)

    for t in range(CHUNK):
        u_t = _rows(u_nat, pl.ds(t, nch, stride=pitch))
        ut_ref[:, t * SSM_GROUP:(t + 1) * SSM_GROUP, :] = u_t.T.reshape(SSM_GROUPS, SSM_GROUP, nch)


def _proj(x, mod3, mod_row0, nw, win, cos_t, s1_t, s2_t, rope):
    nb, ntok, d = x.shape
    nch = ntok // CHUNK
    const2 = lambda b: (0, 0)
    tok = lambda width: pl.BlockSpec((None, ntok, width), lambda b: (b, 0, 0))
    tables = (cos_t, s1_t, s2_t)
    return pl.pallas_call(
        functools.partial(_proj_kernel, rope=rope),
        grid=(nb,),
        in_specs=[tok(d),
                  pl.BlockSpec((None, N_MOD, d), lambda b: (b + mod_row0, 0, 0)),
                  pl.BlockSpec((1, d), const2),
                  pl.BlockSpec(win.shape, const2)] + [pl.BlockSpec(t.shape, const2) for t in tables],
        out_specs=[tok(ATTN_WIDTH), tok(KV_WIDTH), pl.BlockSpec((None, KV_WIDTH, ntok), lambda b: (b, 0, 0)),
                   pl.BlockSpec((SSM_GROUPS, CHUNK_W, nch), lambda b: (0, 0, b))],
        out_shape=[jax.ShapeDtypeStruct((nb, ntok, ATTN_WIDTH), MXU),
                   jax.ShapeDtypeStruct((nb, ntok, KV_WIDTH), MXU),
                   jax.ShapeDtypeStruct((nb, KV_WIDTH, ntok), MXU),
                   jax.ShapeDtypeStruct((SSM_GROUPS, CHUNK_W, nb * nch), F32)],
        scratch_shapes=[pltpu.VMEM((nch * (CHUNK + SCAN_PAD), LANES), F32)] * (SSM_WIDTH // LANES),
        compiler_params=_cparams(("arbitrary",)),
    )(x, mod3, nw, win, *tables)


def _attn_kernel(sink_ref, q_ref, k_ref, vt_ref, kc_ref, vct_ref, bias_ref, o_ref):
    blk = WINDOW
    seq = k_ref.shape[0]
    nwin = 3 * blk
    gq = ATTN_HEADS // ATTN_KV_HEADS
    nsub = q_ref.shape[0] // blk
    kc = kc_ref[...]
    ones = jnp.ones((BF16_ROWS, 1), MXU)
    vct = [jnp.concatenate([vct_ref[kh * HEAD_DIM:(kh + 1) * HEAD_DIM, :],
                            jnp.broadcast_to(ones, (BF16_ROWS, kc.shape[0]))], axis=0) for kh in range(ATTN_KV_HEADS)]
    lane_q = lax.broadcasted_iota(jnp.int32, (blk, LANES), 1)
    nt = (((1,), (1,)), ((), ()))
    chains = [(sub, kh) for sub in range(nsub) for kh in range(ATTN_KV_HEADS)]

    starts, scores = {}, {}
    for sub in range(nsub):
        i = pl.program_id(1) * nsub + sub
        starts[sub] = (i, pl.multiple_of(jnp.clip((i - 1) * blk, 0, seq - nwin), blk))
    for sub, kh in chains:
        i, start = starts[sub]
        q = q_ref[sub * blk:(sub + 1) * blk, :]
        qs = jnp.concatenate(
            [jnp.where((lane_q // HEAD_DIM) == kh, q[:, j * LANES:(j + 1) * LANES], jnp.zeros((), MXU))
             for j in range(gq)], axis=0)
        s_loc = lax.dot_general(k_ref[pl.ds(start, nwin), :], qs, nt, preferred_element_type=F32)
        s_loc = s_loc + bias_ref[(i * blk - start) // blk]
        s_ctx = lax.dot_general(kc, qs, nt, preferred_element_type=F32)
        scores[sub, kh] = (s_loc, s_ctx)

    probs = {}
    for sub, kh in chains:
        s_loc, s_ctx = scores[sub, kh]
        sink = jnp.concatenate(
            [jnp.full((1, blk), sink_ref[kh * gq + j] * LOG2E, F32) for j in range(gq)], axis=1)
        m = jnp.maximum(jnp.maximum(jnp.max(s_loc, axis=0, keepdims=True),
                                    jnp.max(s_ctx, axis=0, keepdims=True)), sink)
        probs[sub, kh] = (jnp.exp2(s_loc - m).astype(MXU), jnp.exp2(s_ctx - m).astype(MXU), jnp.exp2(sink - m))

    outs = {}
    for sub, kh in chains:
        p_loc, p_ctx, p_sink = probs[sub, kh]
        _, start = starts[sub]
        dims = slice(kh * HEAD_DIM, (kh + 1) * HEAD_DIM)
        vtw = jnp.concatenate([vt_ref[dims, pl.ds(start, nwin)], jnp.broadcast_to(ones, (BF16_ROWS, nwin))], axis=0)
        acc = (jnp.dot(vtw, p_loc, preferred_element_type=F32)
               + jnp.dot(vct[kh], p_ctx, preferred_element_type=F32))
        outs[sub, kh] = acc[:HEAD_DIM] / (acc[HEAD_DIM:HEAD_DIM + 1] + p_sink)

    for sub in range(nsub):
        for pair in range(ATTN_HEADS // 2):
            kh, g0 = (2 * pair) // gq, (2 * pair) % gq
            both = jnp.concatenate([outs[sub, kh][:, g0 * blk:(g0 + 1) * blk],
                                    outs[sub, kh][:, (g0 + 1) * blk:(g0 + 2) * blk]], axis=0)
            o_ref[sub * blk:(sub + 1) * blk, pair * LANES:(pair + 1) * LANES] = both.T


def _band_bias():
    gq = ATTN_HEADS // ATTN_KV_HEADS
    key = np.arange(3 * WINDOW)[None, :, None]
    qry = (np.arange(gq * WINDOW) % WINDOW)[None, None, :]
    var = np.arange(3)[:, None, None]
    inside = np.abs(key - var * WINDOW - qry) <= WINDOW
    return jnp.asarray(np.where(inside, 0.0, -np.inf), F32)


def _attention(sink, q, k, vt, kc, vct):
    nb, seq, _ = q.shape
    nctx = kc.shape[1]
    blk = ATTN_QBLOCKS * WINDOW
    bias = _band_bias()
    return pl.pallas_call(
        _attn_kernel,
        grid=(nb, seq // blk),
        in_specs=[pl.BlockSpec(memory_space=pltpu.SMEM),
                  pl.BlockSpec((None, blk, ATTN_WIDTH), lambda b, i: (b, i, 0)),
                  pl.BlockSpec((None, seq, KV_WIDTH), lambda b, i: (b, 0, 0)),
                  pl.BlockSpec((None, KV_WIDTH, seq), lambda b, i: (b, 0, 0)),
                  pl.BlockSpec((None, nctx, KV_WIDTH), lambda b, i: (b, 0, 0)),
                  pl.BlockSpec((None, KV_WIDTH, nctx), lambda b, i: (0, 0, b)),
                  pl.BlockSpec(bias.shape, lambda b, i: (0, 0, 0))],
        out_specs=pl.BlockSpec((None, blk, ATTN_WIDTH), lambda b, i: (b, i, 0)),
        out_shape=jax.ShapeDtypeStruct((nb, seq, ATTN_WIDTH), F32),
        compiler_params=_cparams(("arbitrary", "arbitrary")),
    )(sink, q, k, vt, kc, vct, bias)


def _ssm_kernel(ul_ref, uc_ref, l1_ref, ws_ref, v_ref, qt_ref, am_ref, y_ref, *scratch, nb):
    ng = ul_ref.shape[0]
    xt = [scratch[10 * g:10 * g + 4] for g in range(ng)]
    xct = [scratch[10 * g + 4:10 * g + 8] for g in range(ng)]
    hin = [scratch[10 * g + 8:10 * g + 10] for g in range(ng)]
    n = SSM_STATE
    width = ul_ref.shape[2]
    seg_l = width // nb
    seg_c = uc_ref.shape[2] // nb
    tn = (((0,), (0,)), ((), ()))
    pitch_l, pitch_c = seg_l + SCAN_PAD, seg_c + SCAN_PAD
    for g in range(ng):
        states = lax.dot_general(ul_ref[g].astype(MXU), ws_ref[g], tn, preferred_element_type=F32)
        states_c = lax.dot_general(uc_ref[g].astype(MXU), ws_ref[g], tn, preferred_element_type=F32)
        for i in range(4):
            for b in range(nb):
                xt[g][i][b * pitch_l:b * pitch_l + seg_l, :] = states[b * seg_l:(b + 1) * seg_l,
                                                                      i * LANES:(i + 1) * LANES]
                xct[g][i][b * pitch_c:b * pitch_c + seg_c, :] = states_c[b * seg_c:(b + 1) * seg_c,
                                                                         i * LANES:(i + 1) * LANES]
    mult = [[(am_ref[g, 2 * d:2 * d + 1, :], am_ref[g, 2 * d + 1:2 * d + 2, :]) for d in range(2)]
            for g in range(ng)]

    def advance(g, d, s, x, rows):
        (ar, ai), (sr, si) = mult[g][d], s
        xr, xi = x[2 * d][rows, :], x[2 * d + 1][rows, :]
        return ar * sr - ai * si + xr, ar * si + ai * sr + xi

    def rows_of(k, d, seg):
        return pl.ds(k if d == 0 else seg - 1 - k, nb, stride=seg + SCAN_PAD)

    zero = jnp.zeros((nb, LANES), F32)
    carry = tuple(tuple((zero, zero) for d in range(2)) for g in range(ng))
    for k in range(seg_c):
        carry = tuple(tuple(advance(g, d, carry[g][d], xct[g], rows_of(k, d, seg_c)) for d in range(2))
                      for g in range(ng))

    first_half = lax.broadcasted_iota(jnp.int32, (nb, LANES), 1) < n

    def step(k, carry):
        nxt = []
        for g in range(ng):
            per_dir = []
            for d in range(2):
                rows = rows_of(k, d, seg_l)
                sr, si = carry[g][d]
                hin[g][d][rows, :] = jnp.where(first_half, sr, pltpu.roll(si, n, 1))
                per_dir.append(advance(g, d, carry[g][d], xt[g], rows))
            nxt.append(tuple(per_dir))
        return tuple(nxt)

    lax.fori_loop(0, seg_l, step, carry, unroll=16)

    nt = (((1,), (1,)), ((), ()))
    for g in range(ng):
        ul = ul_ref[g]
        hcat = jnp.concatenate(
            [jnp.concatenate([hin[g][d][b * pitch_l:b * pitch_l + seg_l, :] for b in range(nb)], axis=0)
             for d in range(2)], axis=1).astype(MXU)
        dsk = jnp.concatenate([qt_ref[g, :, 0:1]] * (CHUNK_W // LANES), axis=0)
        y = (jnp.dot(l1_ref[g], ul.astype(MXU), preferred_element_type=F32)
             + lax.dot_general(v_ref[g], hcat, nt, preferred_element_type=F32) + dsk * ul)
        y_ref[g] = 0.5 * y * (1.0 + jnp.tanh(math.sqrt(2.0 / math.pi) * (y + 0.044715 * (y * y * y))))


def _ssm(ul, uc, l1, ws, vcat, qt, am, nb):
    g, _, width = ul.shape
    wc = uc.shape[2]
    ng = SSM_GROUPS_PER_STEP
    blk = lambda *shape: pl.BlockSpec((ng,) + shape, lambda i: (i,) + (0,) * len(shape))
    rows_l, rows_c = width + nb * SCAN_PAD, wc + nb * SCAN_PAD
    per_group = [pltpu.VMEM((rows_l, LANES), F32)] * 4 + [pltpu.VMEM((rows_c, LANES), F32)] * 4 \
        + [pltpu.VMEM((rows_l, LANES), F32)] * 2
    return pl.pallas_call(
        functools.partial(_ssm_kernel, nb=nb),
        grid=(g // ng,),
        in_specs=[blk(CHUNK_W, width), blk(CHUNK_W, wc), blk(CHUNK_W, CHUNK_W), blk(CHUNK_W, 4 * LANES),
                  blk(CHUNK_W, CHUNK_W), blk(LANES, LANES), blk(8, LANES)],
        out_specs=blk(CHUNK_W, width),
        out_shape=jax.ShapeDtypeStruct((g, CHUNK_W, width), F32),
        scratch_shapes=per_group * ng,
        compiler_params=_cparams(("arbitrary",)),
    )(ul, uc, l1, ws, vcat, qt, am)


def _route(logits_t, bias):
    ng, ne = N_EXPERT_GROUPS, N_EXPERTS // N_EXPERT_GROUPS
    t = logits_t.shape[1]
    scores = _sigmoid(logits_t).reshape(ng, ne, t)
    biased = scores + bias.reshape(ng, ne, 1)
    iw = lax.broadcasted_iota(jnp.int32, (ng, ne, t), 1)
    ig = lax.broadcasted_iota(jnp.int32, (ng, ne, t), 0)
    neg = -jnp.inf
    m1 = jnp.max(biased, axis=1, keepdims=True)
    i1 = jnp.min(jnp.where(biased == m1, iw, ne), axis=1, keepdims=True)
    m2 = jnp.max(jnp.where(iw == i1, neg, biased), axis=1, keepdims=True)
    gscore = jnp.broadcast_to(m1 + m2, (ng, ne, t))
    gsel = jnp.zeros((ng, ne, t), F32)
    cur = gscore
    for _ in range(TOPK_GROUPS):
        m = jnp.max(cur, axis=0, keepdims=True)
        gi = jnp.min(jnp.where(cur == m, ig, ng), axis=0, keepdims=True)
        pick = ig == gi
        gsel = jnp.where(pick, 1.0, gsel)
        cur = jnp.where(pick, neg, cur)
    cur = jnp.where(gsel > 0.0, biased, neg)
    flat = ig * ne + iw
    chosen = jnp.zeros((ng, ne, t), F32)
    for _ in range(TOP_K):
        m = jnp.max(jnp.max(cur, axis=0, keepdims=True), axis=1, keepdims=True)
        fi = jnp.min(jnp.min(jnp.where(cur == m, flat, N_EXPERTS), axis=0, keepdims=True), axis=1, keepdims=True)
        pick = flat == fi
        chosen = jnp.where(pick, 1.0, chosen)
        cur = jnp.where(pick, neg, cur)
    sel = jnp.where(chosen > 0.0, scores, 0.0)
    tot = jnp.sum(jnp.sum(sel, axis=1, keepdims=True), axis=0, keepdims=True)
    return (sel / tot * ROUTED_SCALE).reshape(N_EXPERTS, t)


def _mix_kernel(yt_ref, attn_ref, x_ref, mod_ref, wglu_ref, bglu_ref, gssm_ref, wouts_ref, gattn_ref, wouta_ref,
                nffn_ref, wrt_ref, rbias_ref, x1_ref, gates_ref, *ynat):
    nch = yt_ref.shape[3]
    part = pl.program_id(1)

    @pl.when(part == 0)
    def _():
        for t in range(CHUNK):
            y_t = yt_ref[:, t].reshape(SSM_WIDTH, nch).T
            for j, ref in enumerate(ynat):
                ref[pl.ds(t, nch, stride=CHUNK), :] = y_t[:, j * LANES:(j + 1) * LANES]

    base = part * x_ref.shape[0]
    g1 = mod_ref[2:3, :]
    zeros = jnp.zeros((LANES - N_EXPERTS, LANES), F32)

    def body(r, carry):
        r0 = pl.multiple_of(r * MIX_ROWS, MIX_ROWS)
        y = _rows(ynat, pl.ds(pl.multiple_of(base + r0, MIX_ROWS), MIX_ROWS))
        glu = jnp.dot(y.astype(MXU), wglu_ref[...], preferred_element_type=F32) + bglu_ref[...]
        z = y * _sigmoid(glu)
        zn = z * lax.rsqrt(jnp.mean(z * z, axis=-1, keepdims=True) + EPS) * gssm_ref[...]
        o_s = jnp.dot(zn.astype(MXU), wouts_ref[...], preferred_element_type=F32)
        a = attn_ref[pl.ds(r0, MIX_ROWS), :]
        an = a * lax.rsqrt(jnp.mean(a * a, axis=-1, keepdims=True) + EPS) * gattn_ref[...]
        o_a = jnp.dot(an.astype(MXU), wouta_ref[...], preferred_element_type=F32)
        x1 = x_ref[pl.ds(r0, MIX_ROWS), :] + g1 * (o_s + o_a)
        x1_ref[pl.ds(r0, MIX_ROWS), :] = x1
        h2 = _ffn_input(x1, mod_ref, nffn_ref)
        logits_t = _router_logits(wrt_ref[...], h2)
        gates_t = _route(logits_t, rbias_ref[...])
        for i in range(MIX_ROWS // LANES):
            piece = jnp.concatenate([gates_t[:, i * LANES:(i + 1) * LANES], zeros], axis=0)
            gates_ref[pl.ds(r0 + i * LANES, LANES), :] = piece.T
        return carry

    lax.fori_loop(0, x_ref.shape[0] // MIX_ROWS, body, 0)


def _split_bf16(a):
    hi = a.astype(MXU)
    return hi, (a - hi.astype(F32)).astype(MXU)


def _router_logits(w_t, h):
    nt = (((1,), (1,)), ((), ()))
    w_hi, w_lo = _split_bf16(w_t)
    h_hi, h_lo = _split_bf16(h)
    both = lax.dot_general(jnp.concatenate([w_hi, w_lo], axis=0), h_hi, nt, preferred_element_type=F32)
    ne = w_t.shape[0]
    return both[:ne] + both[ne:] + lax.dot_general(w_hi, h_lo, nt, preferred_element_type=F32)


def _ffn_input(x1, mod_ref, nffn_ref):
    h2 = x1 * lax.rsqrt(jnp.mean(x1 * x1, axis=-1, keepdims=True) + EPS) * nffn_ref[...]
    return h2 * (1.0 + mod_ref[4:5, :]) + mod_ref[3:4, :]


def _mix(yt, attn, x, mod3, wglu, bglu, gssm, wouts, gattn, wouta, nffn, wrt, rbias):
    nb, seq, d = x.shape
    nch = seq // CHUNK
    yt4 = yt.reshape(SSM_GROUPS, CHUNK, SSM_GROUP, nb * nch)
    c2 = lambda b, p: (0, 0)
    tok = lambda width: pl.BlockSpec((None, MIX_TOKENS, width), lambda b, p: (b, p, 0))
    return pl.pallas_call(
        _mix_kernel,
        grid=(nb, seq // MIX_TOKENS),
        in_specs=[pl.BlockSpec((SSM_GROUPS, CHUNK, SSM_GROUP, nch), lambda b, p: (0, 0, 0, b)),
                  tok(ATTN_WIDTH), tok(d),
                  pl.BlockSpec((None, N_MOD, d), lambda b, p: (b, 0, 0)),
                  pl.BlockSpec(wglu.shape, c2), pl.BlockSpec(bglu.shape, c2), pl.BlockSpec(gssm.shape, c2),
                  pl.BlockSpec(wouts.shape, c2), pl.BlockSpec(gattn.shape, c2), pl.BlockSpec(wouta.shape, c2),
                  pl.BlockSpec(nffn.shape, c2), pl.BlockSpec(wrt.shape, c2), pl.BlockSpec(rbias.shape, c2)],
        out_specs=[tok(d), tok(LANES)],
        out_shape=[jax.ShapeDtypeStruct((nb, seq, d), F32), jax.ShapeDtypeStruct((nb, seq, LANES), F32)],
        scratch_shapes=[pltpu.VMEM((seq, LANES), F32)] * (SSM_WIDTH // LANES),
        compiler_params=_cparams(("arbitrary", "arbitrary")),
    )(yt4, attn, x, mod3, wglu, bglu, gssm, wouts, gattn, wouta, nffn, wrt, rbias)


def _moe_kernel(x1_ref, gates_ref, mod_ref, nffn_ref, wg_ref, wu_ref, wd_ref, wsg_ref, wsu_ref, wsd_ref, nfin_ref,
                o_ref, acc_ref, hid_ref, h2_ref):
    s = pl.program_id(1)
    f = EXPERT_DIM

    def glu(h, wg, wu):
        gu = jnp.dot(h, jnp.concatenate([wg.astype(MXU), wu.astype(MXU)], axis=1), preferred_element_type=F32)
        g = gu[:, :f]
        return g * _sigmoid(g) * gu[:, f:]

    @pl.when(s == 0)
    def _():
        h2_ref[...] = _ffn_input(x1_ref[...], mod_ref, nffn_ref).astype(MXU)
        hs = glu(h2_ref[...], wsg_ref[...], wsu_ref[...])
        acc_ref[...] = jnp.dot(hs.astype(MXU), wsd_ref[...].astype(MXU), preferred_element_type=F32)

    h2 = h2_ref[...]
    gsh = pltpu.roll(gates_ref[...], (LANES - EXPERTS_PER_STEP * s) % LANES, 1)
    for e in range(EXPERTS_PER_STEP):
        hid = glu(h2, wg_ref[e], wu_ref[e]) * gsh[:, e:e + 1]
        hid_ref[:, e * f:(e + 1) * f] = hid.astype(MXU)
    acc_ref[...] += jnp.dot(hid_ref[...], wd_ref[...].astype(MXU), preferred_element_type=F32)

    @pl.when(s == pl.num_programs(1) - 1)
    def _():
        g2 = mod_ref[5:6, :]
        x2 = x1_ref[...] + g2 * acc_ref[...]
        o_ref[...] = x2 * lax.rsqrt(jnp.mean(x2 * x2, axis=-1, keepdims=True) + EPS) * nfin_ref[...]


def _moe(x1, gates, mod3, nffn, wg, wu, wd, wsg, wsu, wsd, nfin):
    nb, seq, d = x1.shape
    tiles_per_b = seq // MOE_TILE
    nsteps = N_EXPERTS // EXPERTS_PER_STEP
    c2 = lambda i, s: (0, 0)
    tok = lambda width: pl.BlockSpec((None, MOE_TILE, width), lambda i, s: (i // tiles_per_b, i % tiles_per_b, 0))
    experts = pl.BlockSpec((EXPERTS_PER_STEP, d, EXPERT_DIM), lambda i, s: (s, 0, 0))
    return pl.pallas_call(
        _moe_kernel,
        grid=(nb * tiles_per_b, nsteps),
        in_specs=[tok(d), tok(LANES),
                  pl.BlockSpec((None, N_MOD, d), lambda i, s: (i // tiles_per_b, 0, 0)),
                  pl.BlockSpec(nffn.shape, c2), experts, experts,
                  pl.BlockSpec((EXPERTS_PER_STEP * EXPERT_DIM, d), lambda i, s: (s, 0)),
                  pl.BlockSpec(wsg.shape, c2), pl.BlockSpec(wsu.shape, c2), pl.BlockSpec(wsd.shape, c2),
                  pl.BlockSpec(nfin.shape, c2)],
        out_specs=tok(d),
        out_shape=jax.ShapeDtypeStruct((nb, seq, d), F32),
        scratch_shapes=[pltpu.VMEM((MOE_TILE, d), F32),
                        pltpu.VMEM((MOE_TILE, EXPERTS_PER_STEP * EXPERT_DIM), MXU),
                        pltpu.VMEM((MOE_TILE, d), MXU)],
        compiler_params=_cparams(("arbitrary", "arbitrary")),
    )(x1, gates, mod3, nffn, wg, wu, wd, wsg, wsu, wsd, nfin)


def _rope_tables(seq):
    pos = np.arange(seq)
    inv = ROPE_THETA ** (-np.arange(ROT_PAIRS, dtype=np.float64) / ROT_PAIRS)
    ar, ac = (pos // GRID_W)[:, None] * inv, (pos % GRID_W)[:, None] * inv
    zero = np.zeros_like(ar)
    rep = LANES // HEAD_DIM
    cos_t = np.tile(np.concatenate([np.cos(ar), np.cos(ar), np.cos(ac), np.cos(ac)], axis=1), (1, rep))
    s1_t = np.tile(np.concatenate([-np.sin(ar), zero, -np.sin(ac), zero], axis=1), (1, rep))
    s2_t = np.tile(np.concatenate([zero, np.sin(ar), zero, np.sin(ac)], axis=1), (1, rep))
    return tuple(jnp.asarray(t, F32) for t in (cos_t, s1_t, s2_t))


def kernel(x, c, ctx, c_ctx, w_ada, b_ada, norm_mix, norm_ffn, w_in, attn_sink, ssm_a_re, ssm_a_im, ssm_log_dt, ssm_b_re, ssm_b_im, ssm_c_re, ssm_c_im, ssm_d, w_glu, b_glu, norm_attn_out, norm_ssm_out, w_out, w_router, router_bias, w_gate_e, w_up_e, w_down_e, w_gate_s, w_up_s, w_down_s, norm_final):
    nb, seq, d = x.shape
    nctx = ctx.shape[1]
    layer = 0

    pad = jnp.zeros((MOD_ROWS - nb - 1, d), F32)
    c_all = jnp.concatenate([c, c_ctx[None, :], pad], axis=0)
    mod3 = _ada(c_all, w_ada[layer], b_ada[layer]).reshape(MOD_ROWS, N_MOD, d)

    win = w_in[layer].astype(MXU)
    nw = norm_mix[layer].reshape(1, d)
    cos_t, s1_t, s2_t = _rope_tables(seq)

    q, k, vt, ul = _proj(x, mod3, 0, nw, win, cos_t, s1_t, s2_t, True)
    _, kc, vct, uc = _proj(ctx.reshape(1, nb * nctx, d), mod3, nb, nw, win, cos_t, s1_t, s2_t, False)
    kc = kc.reshape(nb, nctx, KV_WIDTH)

    attn = _attention(attn_sink[layer], q, k, vt, kc, vct)

    l1, ws, vcat, qt, am = _ssm_prep(ssm_a_re[layer], ssm_a_im[layer], ssm_log_dt[layer], ssm_b_re[layer],
                                     ssm_b_im[layer], ssm_c_re[layer], ssm_c_im[layer], ssm_d[layer])
    yt = _ssm(ul, uc, l1, ws, vcat, qt, am, nb)

    w_out0 = w_out[layer]
    nffn = norm_ffn[layer].reshape(1, d)
    x1, gates = _mix(
        yt, attn, x, mod3,
        w_glu[layer].astype(MXU), b_glu[layer].reshape(1, SSM_WIDTH), norm_ssm_out[layer].reshape(1, SSM_WIDTH),
        w_out0[ATTN_WIDTH:].astype(MXU), norm_attn_out[layer].reshape(1, ATTN_WIDTH),
        w_out0[:ATTN_WIDTH].astype(MXU), nffn,
        w_router[layer].T, router_bias[layer].reshape(N_EXPERTS, 1))

    wd = w_down_e[layer].reshape(N_EXPERTS * EXPERT_DIM, d)
    return _moe(x1, gates, mod3, nffn, w_gate_e[layer], w_up_e[layer], wd,
                w_gate_s[layer], w_up_s[layer], w_down_s[layer], norm_final.reshape(1, d))
```

```python
import functools
import math

import jax
import jax.numpy as jnp
import numpy as np
from jax import lax
from jax.experimental import pallas as pl
from jax.experimental.pallas import tpu as pltpu

D_MODEL = 1024
EPS = 1e-6
N_MOD = 6
HEAD_DIM = 64
ATTN_HEADS = 8
ATTN_KV_HEADS = 2
ATTN_WIDTH = ATTN_HEADS * HEAD_DIM
KV_WIDTH = ATTN_KV_HEADS * HEAD_DIM
WINDOW = 128
ATTN_SCALE = HEAD_DIM ** -0.5
LOG2E = math.log2(math.e)
ROPE_THETA = 10000.0
ROT_PAIRS = HEAD_DIM // 4
GRID_W = 64
SSM_WIDTH = D_MODEL - ATTN_WIDTH
SSM_GROUP = 16
SSM_GROUPS = SSM_WIDTH // SSM_GROUP
SSM_STATE = 64
N_EXPERTS = 64
EXPERT_DIM = 128
TOP_K = 8
N_EXPERT_GROUPS = 8
TOPK_GROUPS = 4
ROUTED_SCALE = 2.5

CHUNK = 16
CHUNK_W = CHUNK * SSM_GROUP
LANES = 128
BF16_ROWS = 16
MOD_ROWS = 16
ATTN_QBLOCKS = 8
MIX_TOKENS = 1024
MIX_ROWS = 1024
MOE_TILE = 1024
EXPERTS_PER_STEP = 8
SCAN_PAD = 4
SSM_GROUPS_PER_STEP = 4
PREP_GROUPS = 8
VMEM_LIMIT = 56 * 1024 * 1024

MXU = jnp.bfloat16
F32 = jnp.float32
HIGHEST = lax.Precision.HIGHEST


def _sigmoid(x):
    return 0.5 * jnp.tanh(0.5 * x) + 0.5


def _cparams(sem):
    return pltpu.CompilerParams(dimension_semantics=sem, vmem_limit_bytes=VMEM_LIMIT)


def _ada_kernel(c_ref, w_ref, b_ref, o_ref):
    cv = c_ref[...]
    s_hi, s_lo = _split_bf16(cv * _sigmoid(cv))
    w_hi, w_lo = _split_bf16(w_ref[...])
    rows = cv.shape[0]
    both = jnp.dot(jnp.concatenate([s_hi, s_lo], axis=0), w_hi, preferred_element_type=F32)
    o_ref[...] = both[:rows] + both[rows:] + jnp.dot(s_hi, w_lo, preferred_element_type=F32) + b_ref[...]


def _ada(c_all, w, b):
    rows, d = c_all.shape
    n = w.shape[1]
    tn = 1024
    return pl.pallas_call(
        _ada_kernel,
        grid=(n // tn,),
        in_specs=[pl.BlockSpec((rows, d), lambda j: (0, 0)),
                  pl.BlockSpec((d, tn), lambda j: (0, j)),
                  pl.BlockSpec((1, tn), lambda j: (0, j))],
        out_specs=pl.BlockSpec((rows, tn), lambda j: (0, j)),
        out_shape=jax.ShapeDtypeStruct((rows, n), F32),
        compiler_params=_cparams(("arbitrary",)),
    )(c_all, w, b.reshape(1, n))


def _ssm_prep_kernel(are_ref, aim_ref, ldt_ref, btr_ref, bti_ref, cr_ref, ci_ref, dsk_ref,
                     l1_ref, ws_ref, v_ref, qt_ref, am_ref):
    p, n = SSM_GROUP, SSM_STATE
    gps = are_ref.shape[1]
    nd = 2 * gps
    ar, ai = are_ref[...].reshape(nd, n), aim_ref[...].reshape(nd, n)
    dt = jnp.exp(ldt_ref[...].reshape(nd, 1))
    mag = jnp.exp(dt * ar)
    abr = mag * jnp.cos(dt * ai)
    abi = mag * jnp.sin(dt * ai)
    den = ar * ar + ai * ai
    nr = abr - 1.0
    cor = (nr * ar + abi * ai) / den
    coi = (abi * ar - nr * ai) / den
    btr, bti = btr_ref[...].reshape(nd, p, n), bti_ref[...].reshape(nd, p, n)
    bbr = cor[:, None, :] * btr - coi[:, None, :] * bti
    bbi = cor[:, None, :] * bti + coi[:, None, :] * btr
    cr, ci = cr_ref[...].reshape(nd, p, n), ci_ref[...].reshape(nd, p, n)
    pr, pi = jnp.ones_like(abr), jnp.zeros_like(abr)
    cps, pbs = [], []
    for k in range(CHUNK + 1):
        cpr = cr * pr[:, None, :] - ci * pi[:, None, :]
        cpi = cr * pi[:, None, :] + ci * pr[:, None, :]
        cps.append(jnp.concatenate([cpr, -cpi], axis=-1))
        if k < CHUNK:
            pbs.append(jnp.concatenate([bbr * pr[:, None, :] - bbi * pi[:, None, :],
                                        bbr * pi[:, None, :] + bbi * pr[:, None, :]], axis=-1))
            pr, pi = pr * abr - pi * abi, pr * abi + pi * abr
    amult = jnp.concatenate([pr, pi], axis=-1)

    zero = jnp.zeros((p, 2 * n), F32)
    kwts = []
    for g in range(gps):
        f, b = g, gps + g
        rows = [jnp.concatenate([zero, cps[k][b]], axis=1) for k in range(CHUNK - 1, 0, -1)]
        rows.append(jnp.concatenate([cps[0][f], cps[0][b]], axis=1))
        rows += [jnp.concatenate([cps[k][f], zero], axis=1) for k in range(1, CHUNK)]
        rows.append(jnp.concatenate([zero, zero], axis=1))
        stack = jnp.concatenate(rows, axis=0)
        bbcat = jnp.concatenate([bbr[f], bbi[f], bbr[b], bbi[b]], axis=-1)
        kwts.append(lax.dot_general(bbcat, stack, (((1,), (1,)), ((), ())), precision=HIGHEST,
                                    preferred_element_type=F32))
    for g, kwt in enumerate(kwts):
        f, b = g, gps + g
        toep_t = jnp.concatenate(
            [kwt[:, (CHUNK - 1 - t) * p:(CHUNK - 1 - t) * p + CHUNK_W] for t in range(CHUNK)], axis=0)
        wst_f = jnp.concatenate([pbs[CHUNK - 1 - t][f] for t in range(CHUNK)], axis=0)
        wst_b = jnp.concatenate([pbs[t][b] for t in range(CHUNK)], axis=0)
        l1_ref[g] = toep_t.T.astype(MXU)
        low = lax.broadcasted_iota(jnp.int32, wst_f.shape, 1) < n
        tiles = [jnp.where(low, w, 0.0) for m in (wst_f, wst_b) for w in (m, pltpu.roll(m, n, 1))]
        ws_ref[g] = jnp.concatenate(tiles, axis=1).astype(MXU)
        v_f = jnp.concatenate([cps[t + 1][f] for t in range(CHUNK)], axis=0)
        v_b = jnp.concatenate([cps[CHUNK - t][b] for t in range(CHUNK)], axis=0)
        v_ref[g] = jnp.concatenate([v_f, v_b], axis=1).astype(MXU)
        skip = dsk_ref[g]
        qt_ref[g] = jnp.concatenate([skip, jnp.zeros((LANES - 1, 2 * n), F32)], axis=0).T
        a_f, a_b = amult[f:f + 1], amult[b:b + 1]
        zrow = jnp.zeros_like(a_f)
        am_ref[g] = jnp.concatenate([a_f, pltpu.roll(a_f, n, 1), a_b, pltpu.roll(a_b, n, 1)] + [zrow] * 4, axis=0)


def _ssm_prep(a_re, a_im, log_dt, b_re, b_im, c_re, c_im, d_skip):
    g, n, p = SSM_GROUPS, SSM_STATE, SSM_GROUP
    btr = jnp.transpose(b_re, (0, 1, 3, 2))
    bti = jnp.transpose(b_im, (0, 1, 3, 2))
    dsk = jnp.tile(d_skip.reshape(g, 1, p), (1, 1, 2 * n // p))
    gps = PREP_GROUPS
    vec = pl.BlockSpec((2, gps, n), lambda i: (0, i, 0))
    mat = pl.BlockSpec((2, gps, p, n), lambda i: (0, i, 0, 0))
    return pl.pallas_call(
        _ssm_prep_kernel,
        grid=(g // gps,),
        in_specs=[vec, vec, pl.BlockSpec((2, gps, 1), lambda i: (0, i, 0)), mat, mat, mat, mat,
                  pl.BlockSpec((gps, 1, 2 * n), lambda i: (i, 0, 0))],
        out_specs=[pl.BlockSpec((gps, CHUNK_W, CHUNK_W), lambda i: (i, 0, 0)),
                   pl.BlockSpec((gps, CHUNK_W, 4 * LANES), lambda i: (i, 0, 0)),
                   pl.BlockSpec((gps, CHUNK_W, CHUNK_W), lambda i: (i, 0, 0)),
                   pl.BlockSpec((gps, LANES, LANES), lambda i: (i, 0, 0)),
                   pl.BlockSpec((gps, 8, LANES), lambda i: (i, 0, 0))],
        out_shape=[jax.ShapeDtypeStruct((g, CHUNK_W, CHUNK_W), MXU),
                   jax.ShapeDtypeStruct((g, CHUNK_W, 4 * LANES), MXU),
                   jax.ShapeDtypeStruct((g, CHUNK_W, CHUNK_W), MXU),
                   jax.ShapeDtypeStruct((g, LANES, LANES), F32),
                   jax.ShapeDtypeStruct((g, 8, LANES), F32)],
        compiler_params=_cparams(("arbitrary",)),
    )(a_re, a_im, log_dt[..., None], btr, bti, c_re, c_im, dsk)


def _rows(slabs, idx):
    return jnp.concatenate([s[idx, :] for s in slabs], axis=1)


def _proj_kernel(x_ref, mod_ref, nw_ref, win_ref, cos_ref, s1_ref, s2_ref, q_ref, k_ref, vt_ref, ut_ref, *u_nat,
                 rope):
    ntok = x_ref.shape[0]
    nch = ntok // CHUNK
    sh = mod_ref[0:1, :]
    sc = mod_ref[1:2, :]
    nw = nw_ref[...]
    rb = 512

    def rot(tile, r0):
        if not rope:
            return tile
        cs = cos_ref[pl.ds(r0, rb), :]
        a1 = s1_ref[pl.ds(r0, rb), :]
        a2 = s2_ref[pl.ds(r0, rb), :]
        return tile * cs + pltpu.roll(tile, LANES - ROT_PAIRS, 1) * a1 + pltpu.roll(tile, ROT_PAIRS, 1) * a2

    pitch = CHUNK + SCAN_PAD

    def body(r):
        r0 = r * rb
        xb = x_ref[pl.ds(r0, rb), :]
        hn = xb * lax.rsqrt(jnp.mean(xb * xb, axis=-1, keepdims=True) + EPS) * nw
        hb = (hn * (1.0 + sc) + sh).astype(MXU)
        proj = jnp.dot(hb, win_ref[...], preferred_element_type=F32)
        gq = ATTN_HEADS // ATTN_KV_HEADS
        tiles = [rot(proj[:, j * LANES:(j + 1) * LANES], r0) * (ATTN_SCALE * LOG2E) for j in range(gq)]
        low = lax.broadcasted_iota(jnp.int32, tiles[0].shape, 1) < HEAD_DIM
        half = gq // 2
        for j in range(half):
            a, b = tiles[j], tiles[half + j]
            q_ref[pl.ds(r0, rb), (2 * j) * LANES:(2 * j + 1) * LANES] = jnp.where(
                low, a, pltpu.roll(b, HEAD_DIM, 1)).astype(MXU)
            q_ref[pl.ds(r0, rb), (2 * j + 1) * LANES:(2 * j + 2) * LANES] = jnp.where(
                low, pltpu.roll(a, HEAD_DIM, 1), b).astype(MXU)
        k_ref[pl.ds(r0, rb), :] = rot(proj[:, ATTN_WIDTH:ATTN_WIDTH + KV_WIDTH], r0).astype(MXU)
        vt_ref[:, pl.ds(r0, rb)] = proj[:, ATTN_WIDTH + KV_WIDTH:ATTN_WIDTH + 2 * KV_WIDTH].T.astype(MXU)
        u0 = ATTN_WIDTH + 2 * KV_WIDTH
        for j, ref in enumerate(u_nat):
            for c in range(rb // CHUNK):
                row = (r0 // CHUNK + c) * pitch
                ref[row:row + CHUNK, :] = proj[c * CHUNK:(c + 1) * CHUNK, u0 + j * LANES:u0 + (j + 1) * LANES]

    for r in range(ntok // rb):
        body(r)

    for t in range(CHUNK):
        u_t = _rows(u_nat, pl.ds(t, nch, stride=pitch))
        ut_ref[:, t * SSM_GROUP:(t + 1) * SSM_GROUP, :] = u_t.T.reshape(SSM_GROUPS, SSM_GROUP, nch)


def _proj(x, mod3, mod_row0, nw, win, cos_t, s1_t, s2_t, rope):
    nb, ntok, d = x.shape
    nch = ntok // CHUNK
    const2 = lambda b: (0, 0)
    tok = lambda width: pl.BlockSpec((None, ntok, width), lambda b: (b, 0, 0))
    tables = (cos_t, s1_t, s2_t)
    return pl.pallas_call(
        functools.partial(_proj_kernel, rope=rope),
        grid=(nb,),
        in_specs=[tok(d),
                  pl.BlockSpec((None, N_MOD, d), lambda b: (b + mod_row0, 0, 0)),
                  pl.BlockSpec((1, d), const2),
                  pl.BlockSpec(win.shape, const2)] + [pl.BlockSpec(t.shape, const2) for t in tables],
        out_specs=[tok(ATTN_WIDTH), tok(KV_WIDTH), pl.BlockSpec((None, KV_WIDTH, ntok), lambda b: (b, 0, 0)),
                   pl.BlockSpec((SSM_GROUPS, CHUNK_W, nch), lambda b: (0, 0, b))],
        out_shape=[jax.ShapeDtypeStruct((nb, ntok, ATTN_WIDTH), MXU),
                   jax.ShapeDtypeStruct((nb, ntok, KV_WIDTH), MXU),
                   jax.ShapeDtypeStruct((nb, KV_WIDTH, ntok), MXU),
                   jax.ShapeDtypeStruct((SSM_GROUPS, CHUNK_W, nb * nch), F32)],
        scratch_shapes=[pltpu.VMEM((nch * (CHUNK + SCAN_PAD), LANES), F32)] * (SSM_WIDTH // LANES),
        compiler_params=_cparams(("arbitrary",)),
    )(x, mod3, nw, win, *tables)


def _attn_kernel(sink_ref, q_ref, k_ref, vt_ref, kc_ref, vct_ref, bias_ref, o_ref):
    blk = WINDOW
    seq = k_ref.shape[0]
    nwin = 3 * blk
    gq = ATTN_HEADS // ATTN_KV_HEADS
    nsub = q_ref.shape[0] // blk
    kc = kc_ref[...]
    ones = jnp.ones((BF16_ROWS, 1), MXU)
    vct = [jnp.concatenate([vct_ref[kh * HEAD_DIM:(kh + 1) * HEAD_DIM, :],
                            jnp.broadcast_to(ones, (BF16_ROWS, kc.shape[0]))], axis=0) for kh in range(ATTN_KV_HEADS)]
    lane_q = lax.broadcasted_iota(jnp.int32, (blk, LANES), 1)
    nt = (((1,), (1,)), ((), ()))
    chains = [(sub, kh) for sub in range(nsub) for kh in range(ATTN_KV_HEADS)]

    starts, scores = {}, {}
    for sub in range(nsub):
        i = pl.program_id(1) * nsub + sub
        starts[sub] = (i, pl.multiple_of(jnp.clip((i - 1) * blk, 0, seq - nwin), blk))
    for sub, kh in chains:
        i, start = starts[sub]
        q = q_ref[sub * blk:(sub + 1) * blk, :]
        qs = jnp.concatenate(
            [jnp.where((lane_q // HEAD_DIM) == kh, q[:, j * LANES:(j + 1) * LANES], jnp.zeros((), MXU))
             for j in range(gq)], axis=0)
        s_loc = lax.dot_general(k_ref[pl.ds(start, nwin), :], qs, nt, preferred_element_type=F32)
        s_loc = s_loc + bias_ref[(i * blk - start) // blk]
        s_ctx = lax.dot_general(kc, qs, nt, preferred_element_type=F32)
        scores[sub, kh] = (s_loc, s_ctx)

    probs = {}
    for sub, kh in chains:
        s_loc, s_ctx = scores[sub, kh]
        sink = jnp.concatenate(
            [jnp.full((1, blk), sink_ref[kh * gq + j] * LOG2E, F32) for j in range(gq)], axis=1)
        m = jnp.maximum(jnp.maximum(jnp.max(s_loc, axis=0, keepdims=True),
                                    jnp.max(s_ctx, axis=0, keepdims=True)), sink)
        probs[sub, kh] = (jnp.exp2(s_loc - m).astype(MXU), jnp.exp2(s_ctx - m).astype(MXU), jnp.exp2(sink - m))

    outs = {}
    for sub, kh in chains:
        p_loc, p_ctx, p_sink = probs[sub, kh]
        _, start = starts[sub]
        dims = slice(kh * HEAD_DIM, (kh + 1) * HEAD_DIM)
        vtw = jnp.concatenate([vt_ref[dims, pl.ds(start, nwin)], jnp.broadcast_to(ones, (BF16_ROWS, nwin))], axis=0)
        acc = (jnp.dot(vtw, p_loc, preferred_element_type=F32)
               + jnp.dot(vct[kh], p_ctx, preferred_element_type=F32))
        outs[sub, kh] = acc[:HEAD_DIM] / (acc[HEAD_DIM:HEAD_DIM + 1] + p_sink)

    for sub in range(nsub):
        for pair in range(ATTN_HEADS // 2):
            kh, g0 = (2 * pair) // gq, (2 * pair) % gq
            both = jnp.concatenate([outs[sub, kh][:, g0 * blk:(g0 + 1) * blk],
                                    outs[sub, kh][:, (g0 + 1) * blk:(g0 + 2) * blk]], axis=0)
            o_ref[sub * blk:(sub + 1) * blk, pair * LANES:(pair + 1) * LANES] = both.T


def _band_bias():
    gq = ATTN_HEADS // ATTN_KV_HEADS
    key = np.arange(3 * WINDOW)[None, :, None]
    qry = (np.arange(gq * WINDOW) % WINDOW)[None, None, :]
    var = np.arange(3)[:, None, None]
    inside = np.abs(key - var * WINDOW - qry) <= WINDOW
    return jnp.asarray(np.where(inside, 0.0, -np.inf), F32)


def _attention(sink, q, k, vt, kc, vct):
    nb, seq, _ = q.shape
    nctx = kc.shape[1]
    blk = ATTN_QBLOCKS * WINDOW
    bias = _band_bias()
    return pl.pallas_call(
        _attn_kernel,
        grid=(nb, seq // blk),
        in_specs=[pl.BlockSpec(memory_space=pltpu.SMEM),
                  pl.BlockSpec((None, blk, ATTN_WIDTH), lambda b, i: (b, i, 0)),
                  pl.BlockSpec((None, seq, KV_WIDTH), lambda b, i: (b, 0, 0)),
                  pl.BlockSpec((None, KV_WIDTH, seq), lambda b, i: (b, 0, 0)),
                  pl.BlockSpec((None, nctx, KV_WIDTH), lambda b, i: (b, 0, 0)),
                  pl.BlockSpec((None, KV_WIDTH, nctx), lambda b, i: (0, 0, b)),
                  pl.BlockSpec(bias.shape, lambda b, i: (0, 0, 0))],
        out_specs=pl.BlockSpec((None, blk, ATTN_WIDTH), lambda b, i: (b, i, 0)),
        out_shape=jax.ShapeDtypeStruct((nb, seq, ATTN_WIDTH), F32),
        compiler_params=_cparams(("arbitrary", "arbitrary")),
    )(sink, q, k, vt, kc, vct, bias)


def _ssm_kernel(ul_ref, uc_ref, l1_ref, ws_ref, v_ref, qt_ref, am_ref, y_ref, *scratch, nb):
    ng = ul_ref.shape[0]
    xt = [scratch[10 * g:10 * g + 4] for g in range(ng)]
    xct = [scratch[10 * g + 4:10 * g + 8] for g in range(ng)]
    hin = [scratch[10 * g + 8:10 * g + 10] for g in range(ng)]
    n = SSM_STATE
    width = ul_ref.shape[2]
    seg_l = width // nb
    seg_c = uc_ref.shape[2] // nb
    tn = (((0,), (0,)), ((), ()))
    pitch_l, pitch_c = seg_l + SCAN_PAD, seg_c + SCAN_PAD
    for g in range(ng):
        states = lax.dot_general(ul_ref[g].astype(MXU), ws_ref[g], tn, preferred_element_type=F32)
        states_c = lax.dot_general(uc_ref[g].astype(MXU), ws_ref[g], tn, preferred_element_type=F32)
        for i in range(4):
            for b in range(nb):
                xt[g][i][b * pitch_l:b * pitch_l + seg_l, :] = states[b * seg_l:(b + 1) * seg_l,
                                                                      i * LANES:(i + 1) * LANES]
                xct[g][i][b * pitch_c:b * pitch_c + seg_c, :] = states_c[b * seg_c:(b + 1) * seg_c,
                                                                         i * LANES:(i + 1) * LANES]
    mult = [[(am_ref[g, 2 * d:2 * d + 1, :], am_ref[g, 2 * d + 1:2 * d + 2, :]) for d in range(2)]
            for g in range(ng)]

    def advance(g, d, s, x, rows):
        (ar, ai), (sr, si) = mult[g][d], s
        xr, xi = x[2 * d][rows, :], x[2 * d + 1][rows, :]
        return ar * sr - ai * si + xr, ar * si + ai * sr + xi

    def rows_of(k, d, seg):
        return pl.ds(k if d == 0 else seg - 1 - k, nb, stride=seg + SCAN_PAD)

    zero = jnp.zeros((nb, LANES), F32)
    carry = tuple(tuple((zero, zero) for d in range(2)) for g in range(ng))
    for k in range(seg_c):
        carry = tuple(tuple(advance(g, d, carry[g][d], xct[g], rows_of(k, d, seg_c)) for d in range(2))
                      for g in range(ng))

    first_half = lax.broadcasted_iota(jnp.int32, (nb, LANES), 1) < n

    def step(k, carry):
        nxt = []
        for g in range(ng):
            per_dir = []
            for d in range(2):
                rows = rows_of(k, d, seg_l)
                sr, si = carry[g][d]
                hin[g][d][rows, :] = jnp.where(first_half, sr, pltpu.roll(si, n, 1))
                per_dir.append(advance(g, d, carry[g][d], xt[g], rows))
            nxt.append(tuple(per_dir))
        return tuple(nxt)

    lax.fori_loop(0, seg_l, step, carry, unroll=16)

    nt = (((1,), (1,)), ((), ()))
    for g in range(ng):
        ul = ul_ref[g]
        hcat = jnp.concatenate(
            [jnp.concatenate([hin[g][d][b * pitch_l:b * pitch_l + seg_l, :] for b in range(nb)], axis=0)
             for d in range(2)], axis=1).astype(MXU)
        dsk = jnp.concatenate([qt_ref[g, :, 0:1]] * (CHUNK_W // LANES), axis=0)
        y = (jnp.dot(l1_ref[g], ul.astype(MXU), preferred_element_type=F32)
             + lax.dot_general(v_ref[g], hcat, nt, preferred_element_type=F32) + dsk * ul)
        y_ref[g] = 0.5 * y * (1.0 + jnp.tanh(math.sqrt(2.0 / math.pi) * (y + 0.044715 * (y * y * y))))


def _ssm(ul, uc, l1, ws, vcat, qt, am, nb):
    g, _, width = ul.shape
    wc = uc.shape[2]
    ng = SSM_GROUPS_PER_STEP
    blk = lambda *shape: pl.BlockSpec((ng,) + shape, lambda i: (i,) + (0,) * len(shape))
    rows_l, rows_c = width + nb * SCAN_PAD, wc + nb * SCAN_PAD
    per_group = [pltpu.VMEM((rows_l, LANES), F32)] * 4 + [pltpu.VMEM((rows_c, LANES), F32)] * 4 \
        + [pltpu.VMEM((rows_l, LANES), F32)] * 2
    return pl.pallas_call(
        functools.partial(_ssm_kernel, nb=nb),
        grid=(g // ng,),
        in_specs=[blk(CHUNK_W, width), blk(CHUNK_W, wc), blk(CHUNK_W, CHUNK_W), blk(CHUNK_W, 4 * LANES),
                  blk(CHUNK_W, CHUNK_W), blk(LANES, LANES), blk(8, LANES)],
        out_specs=blk(CHUNK_W, width),
        out_shape=jax.ShapeDtypeStruct((g, CHUNK_W, width), F32),
        scratch_shapes=per_group * ng,
        compiler_params=_cparams(("arbitrary",)),
    )(ul, uc, l1, ws, vcat, qt, am)


def _route(logits_t, bias):
    ng, ne = N_EXPERT_GROUPS, N_EXPERTS // N_EXPERT_GROUPS
    t = logits_t.shape[1]
    scores = _sigmoid(logits_t).reshape(ng, ne, t)
    biased = scores + bias.reshape(ng, ne, 1)
    iw = lax.broadcasted_iota(jnp.int32, (ng, ne, t), 1)
    ig = lax.broadcasted_iota(jnp.int32, (ng, ne, t), 0)
    neg = -jnp.inf
    m1 = jnp.max(biased, axis=1, keepdims=True)
    i1 = jnp.min(jnp.where(biased == m1, iw, ne), axis=1, keepdims=True)
    m2 = jnp.max(jnp.where(iw == i1, neg, biased), axis=1, keepdims=True)
    gscore = jnp.broadcast_to(m1 + m2, (ng, ne, t))
    gsel = jnp.zeros((ng, ne, t), F32)
    cur = gscore
    for _ in range(TOPK_GROUPS):
        m = jnp.max(cur, axis=0, keepdims=True)
        gi = jnp.min(jnp.where(cur == m, ig, ng), axis=0, keepdims=True)
        pick = ig == gi
        gsel = jnp.where(pick, 1.0, gsel)
        cur = jnp.where(pick, neg, cur)
    cur = jnp.where(gsel > 0.0, biased, neg)
    flat = ig * ne + iw
    chosen = jnp.zeros((ng, ne, t), F32)
    for _ in range(TOP_K):
        m = jnp.max(jnp.max(cur, axis=0, keepdims=True), axis=1, keepdims=True)
        fi = jnp.min(jnp.min(jnp.where(cur == m, flat, N_EXPERTS), axis=0, keepdims=True), axis=1, keepdims=True)
        pick = flat == fi
        chosen = jnp.where(pick, 1.0, chosen)
        cur = jnp.where(pick, neg, cur)
    sel = jnp.where(chosen > 0.0, scores, 0.0)
    tot = jnp.sum(jnp.sum(sel, axis=1, keepdims=True), axis=0, keepdims=True)
    return (sel / tot * ROUTED_SCALE).reshape(N_EXPERTS, t)


def _mix_kernel(yt_ref, attn_ref, x_ref, mod_ref, wglu_ref, bglu_ref, gssm_ref, wouts_ref, gattn_ref, wouta_ref,
                nffn_ref, wrt_ref, rbias_ref, x1_ref, gates_ref, *ynat):
    nch = yt_ref.shape[3]
    part = pl.program_id(1)

    @pl.when(part == 0)
    def _():
        for t in range(CHUNK):
            y_t = yt_ref[:, t].reshape(SSM_WIDTH, nch).T
            for j, ref in enumerate(ynat):
                ref[pl.ds(t, nch, stride=CHUNK), :] = y_t[:, j * LANES:(j + 1) * LANES]

    base = part * x_ref.shape[0]
    g1 = mod_ref[2:3, :]
    zeros = jnp.zeros((LANES - N_EXPERTS, LANES), F32)

    def body(r, carry):
        r0 = pl.multiple_of(r * MIX_ROWS, MIX_ROWS)
        half = MIX_ROWS // 2
        offs = [pl.multiple_of(r0 + h * half, half) for h in range(2)]
        ys = [_rows(ynat, pl.ds(pl.multiple_of(base + o, half), half)) for o in offs]
        glus = [jnp.dot(y.astype(MXU), wglu_ref[...], preferred_element_type=F32) + bglu_ref[...] for y in ys]
        avs = [attn_ref[pl.ds(o, half), :] for o in offs]
        ans = [a * lax.rsqrt(jnp.mean(a * a, axis=-1, keepdims=True) + EPS) * gattn_ref[...] for a in avs]
        o_as = [jnp.dot(an.astype(MXU), wouta_ref[...], preferred_element_type=F32) for an in ans]
        zs = [y * _sigmoid(g) for y, g in zip(ys, glus)]
        zns = [z * lax.rsqrt(jnp.mean(z * z, axis=-1, keepdims=True) + EPS) * gssm_ref[...] for z in zs]
        o_ss = [jnp.dot(zn.astype(MXU), wouts_ref[...], preferred_element_type=F32) for zn in zns]
        x1s = [x_ref[pl.ds(o, half), :] + g1 * (o_s + o_a) for o, o_s, o_a in zip(offs, o_ss, o_as)]
        for o, x1 in zip(offs, x1s):
            x1_ref[pl.ds(o, half), :] = x1
        h2s = [_ffn_input(x1, mod_ref, nffn_ref) for x1 in x1s]
        logits = [_router_logits(wrt_ref[...], h2) for h2 in h2s]
        gates = [_route(lg, rbias_ref[...]) for lg in logits]
        for o, gates_t in zip(offs, gates):
            for i in range(half // LANES):
                piece = jnp.concatenate([gates_t[:, i * LANES:(i + 1) * LANES], zeros], axis=0)
                gates_ref[pl.ds(o + i * LANES, LANES), :] = piece.T
        return carry

    lax.fori_loop(0, x_ref.shape[0] // MIX_ROWS, body, 0)


def _split_bf16(a):
    hi = a.astype(MXU)
    return hi, (a - hi.astype(F32)).astype(MXU)


def _router_logits(w_t, h):
    nt = (((1,), (1,)), ((), ()))
    w_hi, w_lo = _split_bf16(w_t)
    h_hi, h_lo = _split_bf16(h)
    both = lax.dot_general(jnp.concatenate([w_hi, w_lo], axis=0), h_hi, nt, preferred_element_type=F32)
    ne = w_t.shape[0]
    return both[:ne] + both[ne:] + lax.dot_general(w_hi, h_lo, nt, preferred_element_type=F32)


def _ffn_input(x1, mod_ref, nffn_ref):
    h2 = x1 * lax.rsqrt(jnp.mean(x1 * x1, axis=-1, keepdims=True) + EPS) * nffn_ref[...]
    return h2 * (1.0 + mod_ref[4:5, :]) + mod_ref[3:4, :]


def _mix(yt, attn, x, mod3, wglu, bglu, gssm, wouts, gattn, wouta, nffn, wrt, rbias):
    nb, seq, d = x.shape
    nch = seq // CHUNK
    yt4 = yt.reshape(SSM_GROUPS, CHUNK, SSM_GROUP, nb * nch)
    c2 = lambda b, p: (0, 0)
    tok = lambda width: pl.BlockSpec((None, MIX_TOKENS, width), lambda b, p: (b, p, 0))
    return pl.pallas_call(
        _mix_kernel,
        grid=(nb, seq // MIX_TOKENS),
        in_specs=[pl.BlockSpec((SSM_GROUPS, CHUNK, SSM_GROUP, nch), lambda b, p: (0, 0, 0, b)),
                  tok(ATTN_WIDTH), tok(d),
                  pl.BlockSpec((None, N_MOD, d), lambda b, p: (b, 0, 0)),
                  pl.BlockSpec(wglu.shape, c2), pl.BlockSpec(bglu.shape, c2), pl.BlockSpec(gssm.shape, c2),
                  pl.BlockSpec(wouts.shape, c2), pl.BlockSpec(gattn.shape, c2), pl.BlockSpec(wouta.shape, c2),
                  pl.BlockSpec(nffn.shape, c2), pl.BlockSpec(wrt.shape, c2), pl.BlockSpec(rbias.shape, c2)],
        out_specs=[tok(d), tok(LANES)],
        out_shape=[jax.ShapeDtypeStruct((nb, seq, d), F32), jax.ShapeDtypeStruct((nb, seq, LANES), F32)],
        scratch_shapes=[pltpu.VMEM((seq, LANES), F32)] * (SSM_WIDTH // LANES),
        compiler_params=_cparams(("arbitrary", "arbitrary")),
    )(yt4, attn, x, mod3, wglu, bglu, gssm, wouts, gattn, wouta, nffn, wrt, rbias)


def _moe_kernel(x1_ref, gates_ref, mod_ref, nffn_ref, wg_ref, wu_ref, wd_ref, wsg_ref, wsu_ref, wsd_ref, nfin_ref,
                o_ref, acc_ref, hid_ref, h2_ref):
    s = pl.program_id(1)
    f = EXPERT_DIM

    def glu(h, wg, wu):
        gu = jnp.dot(h, jnp.concatenate([wg.astype(MXU), wu.astype(MXU)], axis=1), preferred_element_type=F32)
        g = gu[:, :f]
        return g * _sigmoid(g) * gu[:, f:]

    @pl.when(s == 0)
    def _():
        h2_ref[...] = _ffn_input(x1_ref[...], mod_ref, nffn_ref).astype(MXU)
        hs = glu(h2_ref[...], wsg_ref[...], wsu_ref[...])
        acc_ref[...] = jnp.dot(hs.astype(MXU), wsd_ref[...].astype(MXU), preferred_element_type=F32)

    h2 = h2_ref[...]
    gsh = pltpu.roll(gates_ref[...], (LANES - EXPERTS_PER_STEP * s) % LANES, 1)
    for e in range(EXPERTS_PER_STEP):
        hid = glu(h2, wg_ref[e], wu_ref[e]) * gsh[:, e:e + 1]
        hid_ref[:, e * f:(e + 1) * f] = hid.astype(MXU)
    acc_ref[...] += jnp.dot(hid_ref[...], wd_ref[...].astype(MXU), preferred_element_type=F32)

    @pl.when(s == pl.num_programs(1) - 1)
    def _():
        g2 = mod_ref[5:6, :]
        x2 = x1_ref[...] + g2 * acc_ref[...]
        o_ref[...] = x2 * lax.rsqrt(jnp.mean(x2 * x2, axis=-1, keepdims=True) + EPS) * nfin_ref[...]


def _moe(x1, gates, mod3, nffn, wg, wu, wd, wsg, wsu, wsd, nfin):
    nb, seq, d = x1.shape
    tiles_per_b = seq // MOE_TILE
    nsteps = N_EXPERTS // EXPERTS_PER_STEP
    c2 = lambda i, s: (0, 0)
    tok = lambda width: pl.BlockSpec((None, MOE_TILE, width), lambda i, s: (i // tiles_per_b, i % tiles_per_b, 0))
    experts = pl.BlockSpec((EXPERTS_PER_STEP, d, EXPERT_DIM), lambda i, s: (s, 0, 0))
    return pl.pallas_call(
        _moe_kernel,
        grid=(nb * tiles_per_b, nsteps),
        in_specs=[tok(d), tok(LANES),
                  pl.BlockSpec((None, N_MOD, d), lambda i, s: (i // tiles_per_b, 0, 0)),
                  pl.BlockSpec(nffn.shape, c2), experts, experts,
                  pl.BlockSpec((EXPERTS_PER_STEP * EXPERT_DIM, d), lambda i, s: (s, 0)),
                  pl.BlockSpec(wsg.shape, c2), pl.BlockSpec(wsu.shape, c2), pl.BlockSpec(wsd.shape, c2),
                  pl.BlockSpec(nfin.shape, c2)],
        out_specs=tok(d),
        out_shape=jax.ShapeDtypeStruct((nb, seq, d), F32),
        scratch_shapes=[pltpu.VMEM((MOE_TILE, d), F32),
                        pltpu.VMEM((MOE_TILE, EXPERTS_PER_STEP * EXPERT_DIM), MXU),
                        pltpu.VMEM((MOE_TILE, d), MXU)],
        compiler_params=_cparams(("arbitrary", "arbitrary")),
    )(x1, gates, mod3, nffn, wg, wu, wd, wsg, wsu, wsd, nfin)


def _rope_tables(seq):
    pos = np.arange(seq)
    inv = ROPE_THETA ** (-np.arange(ROT_PAIRS, dtype=np.float64) / ROT_PAIRS)
    ar, ac = (pos // GRID_W)[:, None] * inv, (pos % GRID_W)[:, None] * inv
    zero = np.zeros_like(ar)
    rep = LANES // HEAD_DIM
    cos_t = np.tile(np.concatenate([np.cos(ar), np.cos(ar), np.cos(ac), np.cos(ac)], axis=1), (1, rep))
    s1_t = np.tile(np.concatenate([-np.sin(ar), zero, -np.sin(ac), zero], axis=1), (1, rep))
    s2_t = np.tile(np.concatenate([zero, np.sin(ar), zero, np.sin(ac)], axis=1), (1, rep))
    return tuple(jnp.asarray(t, F32) for t in (cos_t, s1_t, s2_t))


def kernel(x, c, ctx, c_ctx, w_ada, b_ada, norm_mix, norm_ffn, w_in, attn_sink, ssm_a_re, ssm_a_im, ssm_log_dt, ssm_b_re, ssm_b_im, ssm_c_re, ssm_c_im, ssm_d, w_glu, b_glu, norm_attn_out, norm_ssm_out, w_out, w_router, router_bias, w_gate_e, w_up_e, w_down_e, w_gate_s, w_up_s, w_down_s, norm_final):
    nb, seq, d = x.shape
    nctx = ctx.shape[1]
    layer = 0

    pad = jnp.zeros((MOD_ROWS - nb - 1, d), F32)
    c_all = jnp.concatenate([c, c_ctx[None, :], pad], axis=0)
    mod3 = _ada(c_all, w_ada[layer], b_ada[layer]).reshape(MOD_ROWS, N_MOD, d)

    win = w_in[layer].astype(MXU)
    nw = norm_mix[layer].reshape(1, d)
    cos_t, s1_t, s2_t = _rope_tables(seq)

    q, k, vt, ul = _proj(x, mod3, 0, nw, win, cos_t, s1_t, s2_t, True)
    _, kc, vct, uc = _proj(ctx.reshape(1, nb * nctx, d), mod3, nb, nw, win, cos_t, s1_t, s2_t, False)
    kc = kc.reshape(nb, nctx, KV_WIDTH)

    attn = _attention(attn_sink[layer], q, k, vt, kc, vct)

    l1, ws, vcat, qt, am = _ssm_prep(ssm_a_re[layer], ssm_a_im[layer], ssm_log_dt[layer], ssm_b_re[layer],
                                     ssm_b_im[layer], ssm_c_re[layer], ssm_c_im[layer], ssm_d[layer])
    yt = _ssm(ul, uc, l1, ws, vcat, qt, am, nb)

    w_out0 = w_out[layer]
    nffn = norm_ffn[layer].reshape(1, d)
    x1, gates = _mix(
        yt, attn, x, mod3,
        w_glu[layer].astype(MXU), b_glu[layer].reshape(1, SSM_WIDTH), norm_ssm_out[layer].reshape(1, SSM_WIDTH),
        w_out0[ATTN_WIDTH:].astype(MXU), norm_attn_out[layer].reshape(1, ATTN_WIDTH),
        w_out0[:ATTN_WIDTH].astype(MXU), nffn,
        w_router[layer].T, router_bias[layer].reshape(N_EXPERTS, 1))

    wd = w_down_e[layer].reshape(N_EXPERTS * EXPERT_DIM, d)
    return _moe(x1, gates, mod3, nffn, w_gate_e[layer], w_up_e[layer], wd,
                w_gate_s[layer], w_up_s[layer], w_down_s[layer], norm_final.reshape(1, d))
```

```python
import functools
import math

import jax
import jax.numpy as jnp
import numpy as np
from jax import lax
from jax.experimental import pallas as pl
from jax.experimental.pallas import tpu as pltpu

D_MODEL = 1024
EPS = 1e-6
N_MOD = 6
HEAD_DIM = 64
ATTN_HEADS = 8
ATTN_KV_HEADS = 2
ATTN_WIDTH = ATTN_HEADS * HEAD_DIM
KV_WIDTH = ATTN_KV_HEADS * HEAD_DIM
WINDOW = 128
ATTN_SCALE = HEAD_DIM ** -0.5
LOG2E = math.log2(math.e)
ROPE_THETA = 10000.0
ROT_PAIRS = HEAD_DIM // 4
GRID_W = 64
SSM_WIDTH = D_MODEL - ATTN_WIDTH
SSM_GROUP = 16
SSM_GROUPS = SSM_WIDTH // SSM_GROUP
SSM_STATE = 64
N_EXPERTS = 64
EXPERT_DIM = 128
TOP_K = 8
N_EXPERT_GROUPS = 8
TOPK_GROUPS = 4
ROUTED_SCALE = 2.5

CHUNK = 16
CHUNK_W = CHUNK * SSM_GROUP
LANES = 128
BF16_ROWS = 16
MOD_ROWS = 16
ATTN_QBLOCKS = 16
MIX_TOKENS = 1024
MIX_ROWS = 1024
MOE_TILE = 1024
EXPERTS_PER_STEP = 8
SCAN_PAD = 4
SSM_GROUPS_PER_STEP = 4
PREP_GROUPS = 8
VMEM_LIMIT = 56 * 1024 * 1024

MXU = jnp.bfloat16
F32 = jnp.float32
HIGHEST = lax.Precision.HIGHEST


def _sigmoid(x):
    return 0.5 * jnp.tanh(0.5 * x) + 0.5


def _cparams(sem):
    return pltpu.CompilerParams(dimension_semantics=sem, vmem_limit_bytes=VMEM_LIMIT)


def _ada_kernel(c_ref, w_ref, b_ref, o_ref):
    cv = c_ref[...]
    s_hi, s_lo = _split_bf16(cv * _sigmoid(cv))
    w_hi, w_lo = _split_bf16(w_ref[...])
    rows = cv.shape[0]
    both = jnp.dot(jnp.concatenate([s_hi, s_lo], axis=0), w_hi, preferred_element_type=F32)
    o_ref[...] = both[:rows] + both[rows:] + jnp.dot(s_hi, w_lo, preferred_element_type=F32) + b_ref[...]


def _ada(c_all, w, b):
    rows, d = c_all.shape
    n = w.shape[1]
    tn = 1024
    return pl.pallas_call(
        _ada_kernel,
        grid=(n // tn,),
        in_specs=[pl.BlockSpec((rows, d), lambda j: (0, 0)),
                  pl.BlockSpec((d, tn), lambda j: (0, j)),
                  pl.BlockSpec((1, tn), lambda j: (0, j))],
        out_specs=pl.BlockSpec((rows, tn), lambda j: (0, j)),
        out_shape=jax.ShapeDtypeStruct((rows, n), F32),
        compiler_params=_cparams(("arbitrary",)),
    )(c_all, w, b.reshape(1, n))


def _ssm_prep_kernel(are_ref, aim_ref, ldt_ref, btr_ref, bti_ref, cr_ref, ci_ref, dsk_ref,
                     l1_ref, ws_ref, v_ref, qt_ref, am_ref):
    p, n = SSM_GROUP, SSM_STATE
    gps = are_ref.shape[1]
    nd = 2 * gps
    ar, ai = are_ref[...].reshape(nd, n), aim_ref[...].reshape(nd, n)
    dt = jnp.exp(ldt_ref[...].reshape(nd, 1))
    mag = jnp.exp(dt * ar)
    abr = mag * jnp.cos(dt * ai)
    abi = mag * jnp.sin(dt * ai)
    den = ar * ar + ai * ai
    nr = abr - 1.0
    cor = (nr * ar + abi * ai) / den
    coi = (abi * ar - nr * ai) / den
    btr, bti = btr_ref[...].reshape(nd, p, n), bti_ref[...].reshape(nd, p, n)
    bbr = cor[:, None, :] * btr - coi[:, None, :] * bti
    bbi = cor[:, None, :] * bti + coi[:, None, :] * btr
    cr, ci = cr_ref[...].reshape(nd, p, n), ci_ref[...].reshape(nd, p, n)
    pr, pi = jnp.ones_like(abr), jnp.zeros_like(abr)
    cps, pbs = [], []
    for k in range(CHUNK + 1):
        cpr = cr * pr[:, None, :] - ci * pi[:, None, :]
        cpi = cr * pi[:, None, :] + ci * pr[:, None, :]
        cps.append(jnp.concatenate([cpr, -cpi], axis=-1))
        if k < CHUNK:
            pbs.append(jnp.concatenate([bbr * pr[:, None, :] - bbi * pi[:, None, :],
                                        bbr * pi[:, None, :] + bbi * pr[:, None, :]], axis=-1))
            pr, pi = pr * abr - pi * abi, pr * abi + pi * abr
    amult = jnp.concatenate([pr, pi], axis=-1)

    zero = jnp.zeros((p, 2 * n), F32)
    kwts = []
    for g in range(gps):
        f, b = g, gps + g
        rows = [jnp.concatenate([zero, cps[k][b]], axis=1) for k in range(CHUNK - 1, 0, -1)]
        rows.append(jnp.concatenate([cps[0][f], cps[0][b]], axis=1))
        rows += [jnp.concatenate([cps[k][f], zero], axis=1) for k in range(1, CHUNK)]
        rows.append(jnp.concatenate([zero, zero], axis=1))
        stack = jnp.concatenate(rows, axis=0)
        bbcat = jnp.concatenate([bbr[f], bbi[f], bbr[b], bbi[b]], axis=-1)
        kwts.append(lax.dot_general(bbcat, stack, (((1,), (1,)), ((), ())), precision=HIGHEST,
                                    preferred_element_type=F32))
    for g, kwt in enumerate(kwts):
        f, b = g, gps + g
        toep_t = jnp.concatenate(
            [kwt[:, (CHUNK - 1 - t) * p:(CHUNK - 1 - t) * p + CHUNK_W] for t in range(CHUNK)], axis=0)
        wst_f = jnp.concatenate([pbs[CHUNK - 1 - t][f] for t in range(CHUNK)], axis=0)
        wst_b = jnp.concatenate([pbs[t][b] for t in range(CHUNK)], axis=0)
        l1_ref[g] = toep_t.T.astype(MXU)
        low = lax.broadcasted_iota(jnp.int32, wst_f.shape, 1) < n
        tiles = [jnp.where(low, w, 0.0) for m in (wst_f, wst_b) for w in (m, pltpu.roll(m, n, 1))]
        ws_ref[g] = jnp.concatenate(tiles, axis=1).astype(MXU)
        v_f = jnp.concatenate([cps[t + 1][f] for t in range(CHUNK)], axis=0)
        v_b = jnp.concatenate([cps[CHUNK - t][b] for t in range(CHUNK)], axis=0)
        v_ref[g] = jnp.concatenate([v_f, v_b], axis=1).astype(MXU)
        skip = dsk_ref[g]
        qt_ref[g] = jnp.concatenate([skip, jnp.zeros((LANES - 1, 2 * n), F32)], axis=0).T
        a_f, a_b = amult[f:f + 1], amult[b:b + 1]
        zrow = jnp.zeros_like(a_f)
        am_ref[g] = jnp.concatenate([a_f, pltpu.roll(a_f, n, 1), a_b, pltpu.roll(a_b, n, 1)] + [zrow] * 4, axis=0)


def _ssm_prep(a_re, a_im, log_dt, b_re, b_im, c_re, c_im, d_skip):
    g, n, p = SSM_GROUPS, SSM_STATE, SSM_GROUP
    btr = jnp.transpose(b_re, (0, 1, 3, 2))
    bti = jnp.transpose(b_im, (0, 1, 3, 2))
    dsk = jnp.tile(d_skip.reshape(g, 1, p), (1, 1, 2 * n // p))
    gps = PREP_GROUPS
    vec = pl.BlockSpec((2, gps, n), lambda i: (0, i, 0))
    mat = pl.BlockSpec((2, gps, p, n), lambda i: (0, i, 0, 0))
    return pl.pallas_call(
        _ssm_prep_kernel,
        grid=(g // gps,),
        in_specs=[vec, vec, pl.BlockSpec((2, gps, 1), lambda i: (0, i, 0)), mat, mat, mat, mat,
                  pl.BlockSpec((gps, 1, 2 * n), lambda i: (i, 0, 0))],
        out_specs=[pl.BlockSpec((gps, CHUNK_W, CHUNK_W), lambda i: (i, 0, 0)),
                   pl.BlockSpec((gps, CHUNK_W, 4 * LANES), lambda i: (i, 0, 0)),
                   pl.BlockSpec((gps, CHUNK_W, CHUNK_W), lambda i: (i, 0, 0)),
                   pl.BlockSpec((gps, LANES, LANES), lambda i: (i, 0, 0)),
                   pl.BlockSpec((gps, 8, LANES), lambda i: (i, 0, 0))],
        out_shape=[jax.ShapeDtypeStruct((g, CHUNK_W, CHUNK_W), MXU),
                   jax.ShapeDtypeStruct((g, CHUNK_W, 4 * LANES), MXU),
                   jax.ShapeDtypeStruct((g, CHUNK_W, CHUNK_W), MXU),
                   jax.ShapeDtypeStruct((g, LANES, LANES), F32),
                   jax.ShapeDtypeStruct((g, 8, LANES), F32)],
        compiler_params=_cparams(("arbitrary",)),
    )(a_re, a_im, log_dt[..., None], btr, bti, c_re, c_im, dsk)


def _rows(slabs, idx):
    return jnp.concatenate([s[idx, :] for s in slabs], axis=1)


def _proj_kernel(x_ref, mod_ref, nw_ref, win_ref, cos_ref, s1_ref, s2_ref, q_ref, k_ref, vt_ref, ut_ref, *u_nat,
                 rope):
    ntok = x_ref.shape[0]
    nch = ntok // CHUNK
    sh = mod_ref[0:1, :]
    sc = mod_ref[1:2, :]
    nw = nw_ref[...]
    rb = 512

    def rot(tile, r0):
        if not rope:
            return tile
        cs = cos_ref[pl.ds(r0, rb), :]
        a1 = s1_ref[pl.ds(r0, rb), :]
        a2 = s2_ref[pl.ds(r0, rb), :]
        return tile * cs + pltpu.roll(tile, LANES - ROT_PAIRS, 1) * a1 + pltpu.roll(tile, ROT_PAIRS, 1) * a2

    pitch = CHUNK + SCAN_PAD

    def body(r):
        r0 = r * rb
        xb = x_ref[pl.ds(r0, rb), :]
        hn = xb * lax.rsqrt(jnp.mean(xb * xb, axis=-1, keepdims=True) + EPS) * nw
        hb = (hn * (1.0 + sc) + sh).astype(MXU)
        proj = jnp.dot(hb, win_ref[...], preferred_element_type=F32)
        gq = ATTN_HEADS // ATTN_KV_HEADS
        tiles = [rot(proj[:, j * LANES:(j + 1) * LANES], r0) * (ATTN_SCALE * LOG2E) for j in range(gq)]
        low = lax.broadcasted_iota(jnp.int32, tiles[0].shape, 1) < HEAD_DIM
        half = gq // 2
        for j in range(half):
            a, b = tiles[j], tiles[half + j]
            q_ref[pl.ds(r0, rb), (2 * j) * LANES:(2 * j + 1) * LANES] = jnp.where(
                low, a, pltpu.roll(b, HEAD_DIM, 1)).astype(MXU)
            q_ref[pl.ds(r0, rb), (2 * j + 1) * LANES:(2 * j + 2) * LANES] = jnp.where(
                low, pltpu.roll(a, HEAD_DIM, 1), b).astype(MXU)
        k_ref[pl.ds(r0, rb), :] = rot(proj[:, ATTN_WIDTH:ATTN_WIDTH + KV_WIDTH], r0).astype(MXU)
        vt_ref[:, pl.ds(r0, rb)] = proj[:, ATTN_WIDTH + KV_WIDTH:ATTN_WIDTH + 2 * KV_WIDTH].T.astype(MXU)
        u0 = ATTN_WIDTH + 2 * KV_WIDTH
        for j, ref in enumerate(u_nat):
            for c in range(rb // CHUNK):
                row = (r0 // CHUNK + c) * pitch
                ref[row:row + CHUNK, :] = proj[c * CHUNK:(c + 1) * CHUNK, u0 + j * LANES:u0 + (j + 1) * LANES]

    for r in range(ntok // rb):
        body(r)

    for t in range(CHUNK):
        u_t = _rows(u_nat, pl.ds(t, nch, stride=pitch))
        ut_ref[:, t * SSM_GROUP:(t + 1) * SSM_GROUP, :] = u_t.T.reshape(SSM_GROUPS, SSM_GROUP, nch)


def _proj(x, mod3, mod_row0, nw, win, cos_t, s1_t, s2_t, rope):
    nb, ntok, d = x.shape
    nch = ntok // CHUNK
    const2 = lambda b: (0, 0)
    tok = lambda width: pl.BlockSpec((None, ntok, width), lambda b: (b, 0, 0))
    tables = (cos_t, s1_t, s2_t)
    return pl.pallas_call(
        functools.partial(_proj_kernel, rope=rope),
        grid=(nb,),
        in_specs=[tok(d),
                  pl.BlockSpec((None, N_MOD, d), lambda b: (b + mod_row0, 0, 0)),
                  pl.BlockSpec((1, d), const2),
                  pl.BlockSpec(win.shape, const2)] + [pl.BlockSpec(t.shape, const2) for t in tables],
        out_specs=[tok(ATTN_WIDTH), tok(KV_WIDTH), pl.BlockSpec((None, KV_WIDTH, ntok), lambda b: (b, 0, 0)),
                   pl.BlockSpec((SSM_GROUPS, CHUNK_W, nch), lambda b: (0, 0, b))],
        out_shape=[jax.ShapeDtypeStruct((nb, ntok, ATTN_WIDTH), MXU),
                   jax.ShapeDtypeStruct((nb, ntok, KV_WIDTH), MXU),
                   jax.ShapeDtypeStruct((nb, KV_WIDTH, ntok), MXU),
                   jax.ShapeDtypeStruct((SSM_GROUPS, CHUNK_W, nb * nch), F32)],
        scratch_shapes=[pltpu.VMEM((nch * (CHUNK + SCAN_PAD), LANES), F32)] * (SSM_WIDTH // LANES),
        compiler_params=_cparams(("arbitrary",)),
    )(x, mod3, nw, win, *tables)


def _attn_kernel(sink_ref, q_ref, k_ref, vt_ref, kc_ref, vct_ref, bias_ref, o_ref):
    blk = WINDOW
    seq = k_ref.shape[0]
    nwin = 3 * blk
    gq = ATTN_HEADS // ATTN_KV_HEADS
    nsub = q_ref.shape[0] // blk
    kc = kc_ref[...]
    ones = jnp.ones((BF16_ROWS, 1), MXU)
    vct = [jnp.concatenate([vct_ref[kh * HEAD_DIM:(kh + 1) * HEAD_DIM, :],
                            jnp.broadcast_to(ones, (BF16_ROWS, kc.shape[0]))], axis=0) for kh in range(ATTN_KV_HEADS)]
    lane_q = lax.broadcasted_iota(jnp.int32, (blk, LANES), 1)
    nt = (((1,), (1,)), ((), ()))
    chains = [(sub, kh) for sub in range(nsub) for kh in range(ATTN_KV_HEADS)]

    starts, scores = {}, {}
    for sub in range(nsub):
        i = pl.program_id(1) * nsub + sub
        starts[sub] = (i, pl.multiple_of(jnp.clip((i - 1) * blk, 0, seq - nwin), blk))
    for sub, kh in chains:
        i, start = starts[sub]
        q = q_ref[sub * blk:(sub + 1) * blk, :]
        qs = jnp.concatenate(
            [jnp.where((lane_q // HEAD_DIM) == kh, q[:, j * LANES:(j + 1) * LANES], jnp.zeros((), MXU))
             for j in range(gq)], axis=0)
        s_loc = lax.dot_general(k_ref[pl.ds(start, nwin), :], qs, nt, preferred_element_type=F32)
        s_loc = s_loc + bias_ref[(i * blk - start) // blk]
        s_ctx = lax.dot_general(kc, qs, nt, preferred_element_type=F32)
        scores[sub, kh] = (s_loc, s_ctx)

    probs = {}
    for sub, kh in chains:
        s_loc, s_ctx = scores[sub, kh]
        sink = jnp.concatenate(
            [jnp.full((1, blk), sink_ref[kh * gq + j] * LOG2E, F32) for j in range(gq)], axis=1)
        m = jnp.maximum(jnp.maximum(jnp.max(s_loc, axis=0, keepdims=True),
                                    jnp.max(s_ctx, axis=0, keepdims=True)), sink)
        probs[sub, kh] = (jnp.exp2(s_loc - m).astype(MXU), jnp.exp2(s_ctx - m).astype(MXU), jnp.exp2(sink - m))

    outs = {}
    for sub, kh in chains:
        p_loc, p_ctx, p_sink = probs[sub, kh]
        _, start = starts[sub]
        dims = slice(kh * HEAD_DIM, (kh + 1) * HEAD_DIM)
        vtw = jnp.concatenate([vt_ref[dims, pl.ds(start, nwin)], jnp.broadcast_to(ones, (BF16_ROWS, nwin))], axis=0)
        acc = (jnp.dot(vtw, p_loc, preferred_element_type=F32)
               + jnp.dot(vct[kh], p_ctx, preferred_element_type=F32))
        outs[sub, kh] = acc[:HEAD_DIM] / (acc[HEAD_DIM:HEAD_DIM + 1] + p_sink)

    for sub in range(nsub):
        for pair in range(ATTN_HEADS // 2):
            kh, g0 = (2 * pair) // gq, (2 * pair) % gq
            both = jnp.concatenate([outs[sub, kh][:, g0 * blk:(g0 + 1) * blk],
                                    outs[sub, kh][:, (g0 + 1) * blk:(g0 + 2) * blk]], axis=0)
            o_ref[sub * blk:(sub + 1) * blk, pair * LANES:(pair + 1) * LANES] = both.T


def _band_bias():
    gq = ATTN_HEADS // ATTN_KV_HEADS
    key = np.arange(3 * WINDOW)[None, :, None]
    qry = (np.arange(gq * WINDOW) % WINDOW)[None, None, :]
    var = np.arange(3)[:, None, None]
    inside = np.abs(key - var * WINDOW - qry) <= WINDOW
    return jnp.asarray(np.where(inside, 0.0, -np.inf), F32)


def _attention(sink, q, k, vt, kc, vct):
    nb, seq, _ = q.shape
    nctx = kc.shape[1]
    blk = ATTN_QBLOCKS * WINDOW
    bias = _band_bias()
    return pl.pallas_call(
        _attn_kernel,
        grid=(nb, seq // blk),
        in_specs=[pl.BlockSpec(memory_space=pltpu.SMEM),
                  pl.BlockSpec((None, blk, ATTN_WIDTH), lambda b, i: (b, i, 0)),
                  pl.BlockSpec((None, seq, KV_WIDTH), lambda b, i: (b, 0, 0)),
                  pl.BlockSpec((None, KV_WIDTH, seq), lambda b, i: (b, 0, 0)),
                  pl.BlockSpec((None, nctx, KV_WIDTH), lambda b, i: (b, 0, 0)),
                  pl.BlockSpec((None, KV_WIDTH, nctx), lambda b, i: (0, 0, b)),
                  pl.BlockSpec(bias.shape, lambda b, i: (0, 0, 0))],
        out_specs=pl.BlockSpec((None, blk, ATTN_WIDTH), lambda b, i: (b, i, 0)),
        out_shape=jax.ShapeDtypeStruct((nb, seq, ATTN_WIDTH), F32),
        compiler_params=_cparams(("arbitrary", "arbitrary")),
    )(sink, q, k, vt, kc, vct, bias)


def _ssm_kernel(ul_ref, uc_ref, l1_ref, ws_ref, v_ref, qt_ref, am_ref, y_ref, *scratch, nb):
    ng = ul_ref.shape[0]
    xt = [scratch[10 * g:10 * g + 4] for g in range(ng)]
    xct = [scratch[10 * g + 4:10 * g + 8] for g in range(ng)]
    hin = [scratch[10 * g + 8:10 * g + 10] for g in range(ng)]
    n = SSM_STATE
    width = ul_ref.shape[2]
    seg_l = width // nb
    seg_c = uc_ref.shape[2] // nb
    tn = (((0,), (0,)), ((), ()))
    pitch_l, pitch_c = seg_l + SCAN_PAD, seg_c + SCAN_PAD
    for g in range(ng):
        states = lax.dot_general(ul_ref[g].astype(MXU), ws_ref[g], tn, preferred_element_type=F32)
        states_c = lax.dot_general(uc_ref[g].astype(MXU), ws_ref[g], tn, preferred_element_type=F32)
        for i in range(4):
            for b in range(nb):
                xt[g][i][b * pitch_l:b * pitch_l + seg_l, :] = states[b * seg_l:(b + 1) * seg_l,
                                                                      i * LANES:(i + 1) * LANES]
                xct[g][i][b * pitch_c:b * pitch_c + seg_c, :] = states_c[b * seg_c:(b + 1) * seg_c,
                                                                         i * LANES:(i + 1) * LANES]
    mult = [[(am_ref[g, 2 * d:2 * d + 1, :], am_ref[g, 2 * d + 1:2 * d + 2, :]) for d in range(2)]
            for g in range(ng)]

    def advance(g, d, s, x, rows):
        (ar, ai), (sr, si) = mult[g][d], s
        xr, xi = x[2 * d][rows, :], x[2 * d + 1][rows, :]
        return ar * sr - ai * si + xr, ar * si + ai * sr + xi

    def rows_of(k, d, seg):
        return pl.ds(k if d == 0 else seg - 1 - k, nb, stride=seg + SCAN_PAD)

    zero = jnp.zeros((nb, LANES), F32)
    carry = tuple(tuple((zero, zero) for d in range(2)) for g in range(ng))
    for k in range(seg_c):
        carry = tuple(tuple(advance(g, d, carry[g][d], xct[g], rows_of(k, d, seg_c)) for d in range(2))
                      for g in range(ng))

    first_half = lax.broadcasted_iota(jnp.int32, (nb, LANES), 1) < n

    def step(k, carry):
        nxt = []
        for g in range(ng):
            per_dir = []
            for d in range(2):
                rows = rows_of(k, d, seg_l)
                sr, si = carry[g][d]
                hin[g][d][rows, :] = jnp.where(first_half, sr, pltpu.roll(si, n, 1))
                per_dir.append(advance(g, d, carry[g][d], xt[g], rows))
            nxt.append(tuple(per_dir))
        return tuple(nxt)

    lax.fori_loop(0, seg_l, step, carry, unroll=16)

    nt = (((1,), (1,)), ((), ()))
    for g in range(ng):
        ul = ul_ref[g]
        hcat = jnp.concatenate(
            [jnp.concatenate([hin[g][d][b * pitch_l:b * pitch_l + seg_l, :] for b in range(nb)], axis=0)
             for d in range(2)], axis=1).astype(MXU)
        dsk = jnp.concatenate([qt_ref[g, :, 0:1]] * (CHUNK_W // LANES), axis=0)
        y = (jnp.dot(l1_ref[g], ul.astype(MXU), preferred_element_type=F32)
             + lax.dot_general(v_ref[g], hcat, nt, preferred_element_type=F32) + dsk * ul)
        y_ref[g] = 0.5 * y * (1.0 + jnp.tanh(math.sqrt(2.0 / math.pi) * (y + 0.044715 * (y * y * y))))


def _ssm(ul, uc, l1, ws, vcat, qt, am, nb):
    g, _, width = ul.shape
    wc = uc.shape[2]
    ng = SSM_GROUPS_PER_STEP
    blk = lambda *shape: pl.BlockSpec((ng,) + shape, lambda i: (i,) + (0,) * len(shape))
    rows_l, rows_c = width + nb * SCAN_PAD, wc + nb * SCAN_PAD
    per_group = [pltpu.VMEM((rows_l, LANES), F32)] * 4 + [pltpu.VMEM((rows_c, LANES), F32)] * 4 \
        + [pltpu.VMEM((rows_l, LANES), F32)] * 2
    return pl.pallas_call(
        functools.partial(_ssm_kernel, nb=nb),
        grid=(g // ng,),
        in_specs=[blk(CHUNK_W, width), blk(CHUNK_W, wc), blk(CHUNK_W, CHUNK_W), blk(CHUNK_W, 4 * LANES),
                  blk(CHUNK_W, CHUNK_W), blk(LANES, LANES), blk(8, LANES)],
        out_specs=blk(CHUNK_W, width),
        out_shape=jax.ShapeDtypeStruct((g, CHUNK_W, width), F32),
        scratch_shapes=per_group * ng,
        compiler_params=_cparams(("arbitrary",)),
    )(ul, uc, l1, ws, vcat, qt, am)


def _route(logits_t, bias):
    ng, ne = N_EXPERT_GROUPS, N_EXPERTS // N_EXPERT_GROUPS
    t = logits_t.shape[1]
    scores = _sigmoid(logits_t).reshape(ng, ne, t)
    biased = scores + bias.reshape(ng, ne, 1)
    iw = lax.broadcasted_iota(jnp.int32, (ng, ne, t), 1)
    ig = lax.broadcasted_iota(jnp.int32, (ng, ne, t), 0)
    neg = -jnp.inf
    m1 = jnp.max(biased, axis=1, keepdims=True)
    i1 = jnp.min(jnp.where(biased == m1, iw, ne), axis=1, keepdims=True)
    m2 = jnp.max(jnp.where(iw == i1, neg, biased), axis=1, keepdims=True)
    gscore = jnp.broadcast_to(m1 + m2, (ng, ne, t))
    gsel = jnp.zeros((ng, ne, t), F32)
    cur = gscore
    for _ in range(TOPK_GROUPS):
        m = jnp.max(cur, axis=0, keepdims=True)
        gi = jnp.min(jnp.where(cur == m, ig, ng), axis=0, keepdims=True)
        pick = ig == gi
        gsel = jnp.where(pick, 1.0, gsel)
        cur = jnp.where(pick, neg, cur)
    cur = jnp.where(gsel > 0.0, biased, neg)
    flat = ig * ne + iw
    chosen = jnp.zeros((ng, ne, t), F32)
    for _ in range(TOP_K):
        m = jnp.max(jnp.max(cur, axis=0, keepdims=True), axis=1, keepdims=True)
        fi = jnp.min(jnp.min(jnp.where(cur == m, flat, N_EXPERTS), axis=0, keepdims=True), axis=1, keepdims=True)
        pick = flat == fi
        chosen = jnp.where(pick, 1.0, chosen)
        cur = jnp.where(pick, neg, cur)
    sel = jnp.where(chosen > 0.0, scores, 0.0)
    tot = jnp.sum(jnp.sum(sel, axis=1, keepdims=True), axis=0, keepdims=True)
    return (sel / tot * ROUTED_SCALE).reshape(N_EXPERTS, t)


def _mix_kernel(yt_ref, attn_ref, x_ref, mod_ref, wglu_ref, bglu_ref, gssm_ref, wouts_ref, gattn_ref, wouta_ref,
                nffn_ref, wrt_ref, rbias_ref, x1_ref, gates_ref, *ynat):
    nch = yt_ref.shape[3]
    part = pl.program_id(1)

    @pl.when(part == 0)
    def _():
        for t in range(CHUNK):
            y_t = yt_ref[:, t].reshape(SSM_WIDTH, nch).T
            for j, ref in enumerate(ynat):
                ref[pl.ds(t, nch, stride=CHUNK), :] = y_t[:, j * LANES:(j + 1) * LANES]

    base = part * x_ref.shape[0]
    g1 = mod_ref[2:3, :]
    zeros = jnp.zeros((LANES - N_EXPERTS, LANES), F32)

    def body(r, carry):
        r0 = pl.multiple_of(r * MIX_ROWS, MIX_ROWS)
        half = MIX_ROWS // 2
        offs = [pl.multiple_of(r0 + h * half, half) for h in range(2)]
        ys = [_rows(ynat, pl.ds(pl.multiple_of(base + o, half), half)) for o in offs]
        glus = [jnp.dot(y.astype(MXU), wglu_ref[...], preferred_element_type=F32) + bglu_ref[...] for y in ys]
        avs = [attn_ref[pl.ds(o, half), :] for o in offs]
        ans = [a * lax.rsqrt(jnp.mean(a * a, axis=-1, keepdims=True) + EPS) * gattn_ref[...] for a in avs]
        o_as = [jnp.dot(an.astype(MXU), wouta_ref[...], preferred_element_type=F32) for an in ans]
        zs = [y * _sigmoid(g) for y, g in zip(ys, glus)]
        zns = [z * lax.rsqrt(jnp.mean(z * z, axis=-1, keepdims=True) + EPS) * gssm_ref[...] for z in zs]
        o_ss = [jnp.dot(zn.astype(MXU), wouts_ref[...], preferred_element_type=F32) for zn in zns]
        x1s = [x_ref[pl.ds(o, half), :] + g1 * (o_s + o_a) for o, o_s, o_a in zip(offs, o_ss, o_as)]
        for o, x1 in zip(offs, x1s):
            x1_ref[pl.ds(o, half), :] = x1
        h2s = [_ffn_input(x1, mod_ref, nffn_ref) for x1 in x1s]
        logits = [_router_logits(wrt_ref[...], h2) for h2 in h2s]
        gates = [_route(lg, rbias_ref[...]) for lg in logits]
        for o, gates_t in zip(offs, gates):
            for i in range(half // LANES):
                piece = jnp.concatenate([gates_t[:, i * LANES:(i + 1) * LANES], zeros], axis=0)
                gates_ref[pl.ds(o + i * LANES, LANES), :] = piece.T
        return carry

    lax.fori_loop(0, x_ref.shape[0] // MIX_ROWS, body, 0)


def _split_bf16(a):
    hi = a.astype(MXU)
    return hi, (a - hi.astype(F32)).astype(MXU)


def _router_logits(w_t, h):
    nt = (((1,), (1,)), ((), ()))
    w_hi, w_lo = _split_bf16(w_t)
    h_hi, h_lo = _split_bf16(h)
    both = lax.dot_general(jnp.concatenate([w_hi, w_lo], axis=0), h_hi, nt, preferred_element_type=F32)
    ne = w_t.shape[0]
    return both[:ne] + both[ne:] + lax.dot_general(w_hi, h_lo, nt, preferred_element_type=F32)


def _ffn_input(x1, mod_ref, nffn_ref):
    h2 = x1 * lax.rsqrt(jnp.mean(x1 * x1, axis=-1, keepdims=True) + EPS) * nffn_ref[...]
    return h2 * (1.0 + mod_ref[4:5, :]) + mod_ref[3:4, :]


def _mix(yt, attn, x, mod3, wglu, bglu, gssm, wouts, gattn, wouta, nffn, wrt, rbias):
    nb, seq, d = x.shape
    nch = seq // CHUNK
    yt4 = yt.reshape(SSM_GROUPS, CHUNK, SSM_GROUP, nb * nch)
    c2 = lambda b, p: (0, 0)
    tok = lambda width: pl.BlockSpec((None, MIX_TOKENS, width), lambda b, p: (b, p, 0))
    return pl.pallas_call(
        _mix_kernel,
        grid=(nb, seq // MIX_TOKENS),
        in_specs=[pl.BlockSpec((SSM_GROUPS, CHUNK, SSM_GROUP, nch), lambda b, p: (0, 0, 0, b)),
                  tok(ATTN_WIDTH), tok(d),
                  pl.BlockSpec((None, N_MOD, d), lambda b, p: (b, 0, 0)),
                  pl.BlockSpec(wglu.shape, c2), pl.BlockSpec(bglu.shape, c2), pl.BlockSpec(gssm.shape, c2),
                  pl.BlockSpec(wouts.shape, c2), pl.BlockSpec(gattn.shape, c2), pl.BlockSpec(wouta.shape, c2),
                  pl.BlockSpec(nffn.shape, c2), pl.BlockSpec(wrt.shape, c2), pl.BlockSpec(rbias.shape, c2)],
        out_specs=[tok(d), tok(LANES)],
        out_shape=[jax.ShapeDtypeStruct((nb, seq, d), F32), jax.ShapeDtypeStruct((nb, seq, LANES), F32)],
        scratch_shapes=[pltpu.VMEM((seq, LANES), F32)] * (SSM_WIDTH // LANES),
        compiler_params=_cparams(("arbitrary", "arbitrary")),
    )(yt4, attn, x, mod3, wglu, bglu, gssm, wouts, gattn, wouta, nffn, wrt, rbias)


def _moe_kernel(x1_ref, gates_ref, mod_ref, nffn_ref, wg_ref, wu_ref, wd_ref, wsg_ref, wsu_ref, wsd_ref, nfin_ref,
                o_ref, acc_ref, hid_ref, h2_ref):
    s = pl.program_id(1)
    f = EXPERT_DIM

    def glu(h, wg, wu):
        gu = jnp.dot(h, jnp.concatenate([wg.astype(MXU), wu.astype(MXU)], axis=1), preferred_element_type=F32)
        g = gu[:, :f]
        return g * _sigmoid(g) * gu[:, f:]

    @pl.when(s == 0)
    def _():
        h2_ref[...] = _ffn_input(x1_ref[...], mod_ref, nffn_ref).astype(MXU)
        hs = glu(h2_ref[...], wsg_ref[...], wsu_ref[...])
        acc_ref[...] = jnp.dot(hs.astype(MXU), wsd_ref[...].astype(MXU), preferred_element_type=F32)

    h2 = h2_ref[...]
    gsh = pltpu.roll(gates_ref[...], (LANES - EXPERTS_PER_STEP * s) % LANES, 1)
    for e in range(EXPERTS_PER_STEP):
        hid = glu(h2, wg_ref[e], wu_ref[e]) * gsh[:, e:e + 1]
        hid_ref[:, e * f:(e + 1) * f] = hid.astype(MXU)
    acc_ref[...] += jnp.dot(hid_ref[...], wd_ref[...].astype(MXU), preferred_element_type=F32)

    @pl.when(s == pl.num_programs(1) - 1)
    def _():
        g2 = mod_ref[5:6, :]
        x2 = x1_ref[...] + g2 * acc_ref[...]
        o_ref[...] = x2 * lax.rsqrt(jnp.mean(x2 * x2, axis=-1, keepdims=True) + EPS) * nfin_ref[...]


def _moe(x1, gates, mod3, nffn, wg, wu, wd, wsg, wsu, wsd, nfin):
    nb, seq, d = x1.shape
    tiles_per_b = seq // MOE_TILE
    nsteps = N_EXPERTS // EXPERTS_PER_STEP
    c2 = lambda i, s: (0, 0)
    tok = lambda width: pl.BlockSpec((None, MOE_TILE, width), lambda i, s: (i // tiles_per_b, i % tiles_per_b, 0))
    experts = pl.BlockSpec((EXPERTS_PER_STEP, d, EXPERT_DIM), lambda i, s: (s, 0, 0))
    return pl.pallas_call(
        _moe_kernel,
        grid=(nb * tiles_per_b, nsteps),
        in_specs=[tok(d), tok(LANES),
                  pl.BlockSpec((None, N_MOD, d), lambda i, s: (i // tiles_per_b, 0, 0)),
                  pl.BlockSpec(nffn.shape, c2), experts, experts,
                  pl.BlockSpec((EXPERTS_PER_STEP * EXPERT_DIM, d), lambda i, s: (s, 0)),
                  pl.BlockSpec(wsg.shape, c2), pl.BlockSpec(wsu.shape, c2), pl.BlockSpec(wsd.shape, c2),
                  pl.BlockSpec(nfin.shape, c2)],
        out_specs=tok(d),
        out_shape=jax.ShapeDtypeStruct((nb, seq, d), F32),
        scratch_shapes=[pltpu.VMEM((MOE_TILE, d), F32),
                        pltpu.VMEM((MOE_TILE, EXPERTS_PER_STEP * EXPERT_DIM), MXU),
                        pltpu.VMEM((MOE_TILE, d), MXU)],
        compiler_params=_cparams(("arbitrary", "arbitrary")),
    )(x1, gates, mod3, nffn, wg, wu, wd, wsg, wsu, wsd, nfin)


def _rope_tables(seq):
    pos = np.arange(seq)
    inv = ROPE_THETA ** (-np.arange(ROT_PAIRS, dtype=np.float64) / ROT_PAIRS)
    ar, ac = (pos // GRID_W)[:, None] * inv, (pos % GRID_W)[:, None] * inv
    zero = np.zeros_like(ar)
    rep = LANES // HEAD_DIM
    cos_t = np.tile(np.concatenate([np.cos(ar), np.cos(ar), np.cos(ac), np.cos(ac)], axis=1), (1, rep))
    s1_t = np.tile(np.concatenate([-np.sin(ar), zero, -np.sin(ac), zero], axis=1), (1, rep))
    s2_t = np.tile(np.concatenate([zero, np.sin(ar), zero, np.sin(ac)], axis=1), (1, rep))
    return tuple(jnp.asarray(t, F32) for t in (cos_t, s1_t, s2_t))


def kernel(x, c, ctx, c_ctx, w_ada, b_ada, norm_mix, norm_ffn, w_in, attn_sink, ssm_a_re, ssm_a_im, ssm_log_dt, ssm_b_re, ssm_b_im, ssm_c_re, ssm_c_im, ssm_d, w_glu, b_glu, norm_attn_out, norm_ssm_out, w_out, w_router, router_bias, w_gate_e, w_up_e, w_down_e, w_gate_s, w_up_s, w_down_s, norm_final):
    nb, seq, d = x.shape
    nctx = ctx.shape[1]
    layer = 0

    pad = jnp.zeros((MOD_ROWS - nb - 1, d), F32)
    c_all = jnp.concatenate([c, c_ctx[None, :], pad], axis=0)
    mod3 = _ada(c_all, w_ada[layer], b_ada[layer]).reshape(MOD_ROWS, N_MOD, d)

    win = w_in[layer].astype(MXU)
    nw = norm_mix[layer].reshape(1, d)
    cos_t, s1_t, s2_t = _rope_tables(seq)

    q, k, vt, ul = _proj(x, mod3, 0, nw, win, cos_t, s1_t, s2_t, True)
    _, kc, vct, uc = _proj(ctx.reshape(1, nb * nctx, d), mod3, nb, nw, win, cos_t, s1_t, s2_t, False)
    kc = kc.reshape(nb, nctx, KV_WIDTH)

    attn = _attention(attn_sink[layer], q, k, vt, kc, vct)

    l1, ws, vcat, qt, am = _ssm_prep(ssm_a_re[layer], ssm_a_im[layer], ssm_log_dt[layer], ssm_b_re[layer],
                                     ssm_b_im[layer], ssm_c_re[layer], ssm_c_im[layer], ssm_d[layer])
    yt = _ssm(ul, uc, l1, ws, vcat, qt, am, nb)

    w_out0 = w_out[layer]
    nffn = norm_ffn[layer].reshape(1, d)
    x1, gates = _mix(
        yt, attn, x, mod3,
        w_glu[layer].astype(MXU), b_glu[layer].reshape(1, SSM_WIDTH), norm_ssm_out[layer].reshape(1, SSM_WIDTH),
        w_out0[ATTN_WIDTH:].astype(MXU), norm_attn_out[layer].reshape(1, ATTN_WIDTH),
        w_out0[:ATTN_WIDTH].astype(MXU), nffn,
        w_router[layer].T, router_bias[layer].reshape(N_EXPERTS, 1))

    wd = w_down_e[layer].reshape(N_EXPERTS * EXPERT_DIM, d)
    return _moe(x1, gates, mod3, nffn, w_gate_e[layer], w_up_e[layer], wd,
                w_gate_s[layer], w_up_s[layer], w_down_s[layer], norm_final.reshape(1, d))
```
